```python
import math
import jax
import jax.numpy as jnp
from jax import lax
import numpy as np

D_MODEL = 1024
BATCH = 8
SEQ = 4096
DEPTH = 2
DEC_BATCH = 32
DEC_SEQ = 1
PAST_LEN = 16384
PAGE_SIZE = 128

N_META = 16
N_MIXERS = 2
N_ATTN_LAYERS = (DEPTH + 1) // 2
N_SSD_LAYERS = DEPTH // 2
N_HEADS = 16
HEAD_DIM = D_MODEL // N_HEADS
N_KV_HEADS = 4
N_IDX_HEADS = 8
IDX_DIM = 64
TOPK_MAX = 256
Q_BLOCK = 128
Q_DIM = N_HEADS * HEAD_DIM
KV_DIM = N_KV_HEADS * HEAD_DIM
QI_DIM = N_IDX_HEADS * IDX_DIM
ATTN_IN_DIM = Q_DIM + 2 * KV_DIM + QI_DIM + IDX_DIM + N_IDX_HEADS
D_INNER = 2 * D_MODEL
SSD_HEAD_DIM = 64
SSD_HEADS = D_INNER // SSD_HEAD_DIM
SSD_GROUPS = 4
HEADS_PER_GROUP = SSD_HEADS // SSD_GROUPS
D_STATE = 128
CONV_W = 4
CONV_DIM = D_INNER + 2 * SSD_GROUPS * D_STATE
SSD_IN_DIM = D_INNER + CONV_DIM + SSD_HEADS
CHUNK = 128
FFN_HIDDEN = 2816
EPS = 1e-6

kernel_name = 'dsa_ssd_macaron_meta_hybrid_step'


def rms_norm(x, g):
    xf = x.astype(jnp.float32)
    y = xf * lax.rsqrt(jnp.mean(xf * xf, axis=-1, keepdims=True) + EPS)
    return (y * g.astype(jnp.float32)).astype(x.dtype)


def swiglu_ffn(x, g, w_in, w_out):
    h = rms_norm(x, g) @ w_in
    a, b = jnp.split(h, 2, axis=-1)
    return (jax.nn.silu(a) * b) @ w_out


def attn_project(x, g, w_in, q_norm, k_norm, kidx_norm):
    B, L = x.shape[:2]
    h = rms_norm(x, g) @ w_in
    splits = [Q_DIM, Q_DIM + KV_DIM, Q_DIM + 2 * KV_DIM, Q_DIM + 2 * KV_DIM + QI_DIM,
              Q_DIM + 2 * KV_DIM + QI_DIM + IDX_DIM]
    q, k, v, qi, ki, wi = jnp.split(h, splits, axis=-1)
    q = rms_norm(q.reshape(B, L, N_HEADS, HEAD_DIM), q_norm)
    k = rms_norm(k.reshape(B, L, N_KV_HEADS, HEAD_DIM), k_norm)
    v = v.reshape(B, L, N_KV_HEADS, HEAD_DIM)
    qi = qi.reshape(B, L, N_IDX_HEADS, IDX_DIM)
    ki = rms_norm(ki, kidx_norm)
    wi = wi * (N_IDX_HEADS ** -0.5)
    return q, k, v, qi, ki, wi


def indexer_scores(qi, wi, ki, q_pos, k_pos):
    dots = jnp.einsum('bqhd,bld->bqhl', qi, ki).astype(jnp.float32) * (IDX_DIM ** -0.5)
    s = jnp.einsum('bqh,bqhl->bql', wi.astype(jnp.float32), jax.nn.relu(dots))
    return jnp.where(k_pos[None, None, :] <= q_pos[:, :, None], s, -jnp.inf)


def sparse_attend(q, k_sel, v_sel, valid):
    B, Q = q.shape[:2]
    qg = q.reshape(B, Q, N_KV_HEADS, N_HEADS // N_KV_HEADS, HEAD_DIM)
    s = jnp.einsum('bqgrd,bqkgd->bqgrk', qg, k_sel).astype(jnp.float32) * (HEAD_DIM ** -0.5)
    s = jnp.where(valid[:, :, None, None, :], s, -jnp.inf)
    p = jax.nn.softmax(s, axis=-1).astype(v_sel.dtype)
    o = jnp.einsum('bqgrk,bqkgd->bqgrd', p, v_sel)
    return o.reshape(B, Q, Q_DIM)


def dsa_prompt(q, k, v, qi, ki, wi, topk):
    B, T = q.shape[:2]
    n_blk = -(-T // Q_BLOCK)
    pad = n_blk * Q_BLOCK - T

    def blocks(a):
        a = jnp.pad(a, [(0, 0), (0, pad)] + [(0, 0)] * (a.ndim - 2))
        return a.reshape((B, n_blk, Q_BLOCK) + a.shape[2:]).swapaxes(0, 1)

    k_pos = jnp.arange(T)
    q_pos = jnp.arange(n_blk * Q_BLOCK).reshape(n_blk, Q_BLOCK)

    def one_block(args):
        qb, qib, wib, pos = args
        sc = indexer_scores(qib, wib, ki, pos[None], k_pos)
        _, idx = lax.top_k(sc, topk)
        k_sel = jax.vmap(lambda kb, ib: kb[ib])(k, idx)
        v_sel = jax.vmap(lambda vb, ib: vb[ib])(v, idx)
        return sparse_attend(qb, k_sel, v_sel, idx <= pos[None, :, None])

    o = lax.map(one_block, (blocks(q), blocks(qi), blocks(wi), q_pos))
    return o.swapaxes(0, 1).reshape(B, n_blk * Q_BLOCK, Q_DIM)[:, :T]


def dsa_sample(q, k_new, v_new, qi, ki_new, wi, cache_k, cache_v, cache_kidx, page_table, topk):
    B, S = q.shape[:2]
    ki_past = cache_kidx[page_table].reshape(B, PAST_LEN, IDX_DIM)
    ki_all = jnp.concatenate([ki_past, ki_new.astype(ki_past.dtype)], axis=1)
    k_pos = jnp.arange(PAST_LEN + S)
    q_pos = PAST_LEN + jnp.arange(S)
    sc = indexer_scores(qi, wi, ki_all, q_pos[None], k_pos)
    _, idx = lax.top_k(sc, topk)
    is_new = idx >= PAST_LEN
    pidx = jnp.minimum(idx, PAST_LEN - 1)
    phys = jax.vmap(lambda pt, i: pt[i])(page_table, pidx // PAGE_SIZE)
    off = pidx % PAGE_SIZE
    nidx = jnp.clip(idx - PAST_LEN, 0, S - 1)

    def select(cache, new):
        past = cache[phys, off]
        cur = jax.vmap(lambda nb, ib: nb[ib])(new.astype(past.dtype), nidx)
        return jnp.where(is_new[..., None, None], cur, past)

    return sparse_attend(q, select(cache_k, k_new), select(cache_v, v_new), idx <= q_pos[None, :, None])


def causal_conv(xbc, conv_state, w, b):
    L = xbc.shape[1]
    xp = jnp.concatenate([conv_state.astype(xbc.dtype), xbc], axis=1)
    y = sum(xp[:, j:j + L] * w[j] for j in range(CONV_W)) + b
    return jax.nn.silu(y), xp[:, -(CONV_W - 1):]


def ssd_chunked(x, dt, a, bm, cm, h0, chunk):
    B, L = x.shape[:2]
    c = L // chunk
    G, Hg, P, N = SSD_GROUPS, HEADS_PER_GROUP, SSD_HEAD_DIM, D_STATE
    xr = (x.astype(jnp.float32) * dt[..., None]).reshape(B, c, chunk, G, Hg, P)
    br = bm.astype(jnp.float32).reshape(B, c, chunk, G, N)
    cr = cm.astype(jnp.float32).reshape(B, c, chunk, G, N)
    a_cum = jnp.cumsum((dt * a).reshape(B, c, chunk, G, Hg), axis=2)
    seg = a_cum[:, :, :, None] - a_cum[:, :, None, :]
    causal = jnp.tril(jnp.ones((chunk, chunk), dtype=bool))[None, None, :, :, None, None]
    decay = jnp.exp(jnp.where(causal, seg, -jnp.inf))
    cb = jnp.einsum('bclgn,bcsgn->bclsg', cr, br)
    y_diag = jnp.einsum('bclsg,bclsgh,bcsghp->bclghp', cb, decay, xr)
    decay_s = jnp.exp(a_cum[:, :, -1:] - a_cum)
    states = jnp.einsum('bcsgn,bcsgh,bcsghp->bcghpn', br, decay_s, xr)
    chunk_decay = jnp.exp(a_cum[:, :, -1])

    def step(h, inp):
        st, dec = inp
        return dec[..., None, None] * h + st, h

    h_final, h_prev = lax.scan(step, h0.reshape(B, G, Hg, P, N),
                               (states.swapaxes(0, 1), chunk_decay.swapaxes(0, 1)))
    h_prev = h_prev.swapaxes(0, 1)
    y_off = jnp.einsum('bclgn,bcghpn,bclgh->bclghp', cr, h_prev, jnp.exp(a_cum))
    y = (y_diag + y_off).reshape(B, L, SSD_HEADS, P)
    return y, h_final.reshape(B, SSD_HEADS, P, N)


def ssd_mixer(x, g, w_in, conv_w, conv_b, dt_bias, a_log, d_skip, gate_norm, w_out,
              conv_state, h0, segments):
    B, L = x.shape[:2]
    h = rms_norm(x, g) @ w_in
    z, xbc, dt_raw = jnp.split(h, [D_INNER, D_INNER + CONV_DIM], axis=-1)
    xbc, new_conv = causal_conv(xbc, conv_state, conv_w, conv_b)
    xh, bm, cm = jnp.split(xbc, [D_INNER, D_INNER + SSD_GROUPS * D_STATE], axis=-1)
    xh = xh.reshape(B, L, SSD_HEADS, SSD_HEAD_DIM)
    bm = bm.reshape(B, L, SSD_GROUPS, D_STATE)
    cm = cm.reshape(B, L, SSD_GROUPS, D_STATE)
    dt = jax.nn.softplus(dt_raw.astype(jnp.float32) + dt_bias.astype(jnp.float32))
    a = -jnp.exp(a_log.astype(jnp.float32))
    state = h0.astype(jnp.float32)
    ys = []
    for start, length, chunk in segments:
        sl = slice(start, start + length)
        y_seg, state = ssd_chunked(xh[:, sl], dt[:, sl], a, bm[:, sl], cm[:, sl], state, chunk)
        ys.append(y_seg)
    y = jnp.concatenate(ys, axis=1)
    y = y + d_skip.astype(jnp.float32)[:, None] * xh.astype(jnp.float32)
    y = y.reshape(B, L, D_INNER) * jax.nn.silu(z.astype(jnp.float32))
    y = rms_norm(y.reshape(B, L, SSD_GROUPS, D_INNER // SSD_GROUPS),
                 gate_norm.reshape(SSD_GROUPS, D_INNER // SSD_GROUPS)).reshape(B, L, D_INNER)
    return y.astype(x.dtype) @ w_out, state, new_conv


def setup_inputs(seed: int = 0) -> dict:
    key = jax.random.key(seed)
    ks = jax.random.split(key, 40)
    f32 = jnp.float32

    def nrm(k, shape, fan_in):
        return jax.random.normal(k, shape, f32) * (fan_in ** -0.5)

    def gain(k, shape):
        return 1.0 + 0.05 * jax.random.normal(k, shape, f32)

    n_pages = PAST_LEN // PAGE_SIZE
    n_used = DEC_BATCH * n_pages
    n_pool = n_used + n_used // 4 + 1
    page_table = jax.random.permutation(ks[0], n_pool)[:n_used].reshape(DEC_BATCH, n_pages).astype(jnp.int32)
    dt0 = jnp.exp(jax.random.uniform(ks[1], (N_SSD_LAYERS, SSD_HEADS), f32, math.log(1e-3), math.log(1e-1)))
    dt_bias = dt0 + jnp.log(-jnp.expm1(-dt0))
    a_log = jnp.log(jax.random.uniform(ks[2], (N_SSD_LAYERS, SSD_HEADS), f32, 1.0, 16.0))
    return {
        'x_prompt': jax.random.normal(ks[3], (BATCH, SEQ, D_MODEL), f32),
        'x_sample': jax.random.normal(ks[4], (DEC_BATCH, DEC_SEQ, D_MODEL), f32),
        'cache_k': jax.random.normal(ks[5], (N_ATTN_LAYERS, n_pool, PAGE_SIZE, N_KV_HEADS, HEAD_DIM), f32),
        'cache_v': jax.random.normal(ks[6], (N_ATTN_LAYERS, n_pool, PAGE_SIZE, N_KV_HEADS, HEAD_DIM), f32),
        'cache_kidx': jax.random.normal(ks[7], (N_ATTN_LAYERS, n_pool, PAGE_SIZE, IDX_DIM), f32),
        'page_table': page_table,
        'state_ssm': 0.1 * jax.random.normal(ks[8], (N_SSD_LAYERS, DEC_BATCH, SSD_HEADS, SSD_HEAD_DIM, D_STATE), f32),
        'state_conv': jax.random.normal(ks[9], (N_SSD_LAYERS, DEC_BATCH, CONV_W - 1, CONV_DIM), f32),
        'meta_tokens': jax.random.normal(ks[10], (N_META, D_MODEL), f32),
        'norm_ffn_a': gain(ks[11], (DEPTH, D_MODEL)),
        'w_ffn_a_in': nrm(ks[12], (DEPTH, D_MODEL, 2 * FFN_HIDDEN), D_MODEL),
        'w_ffn_a_out': nrm(ks[13], (DEPTH, FFN_HIDDEN, D_MODEL), FFN_HIDDEN),
        'norm_mix': gain(ks[14], (DEPTH, D_MODEL)),
        'norm_ffn_b': gain(ks[15], (DEPTH, D_MODEL)),
        'w_ffn_b_in': nrm(ks[16], (DEPTH, D_MODEL, 2 * FFN_HIDDEN), D_MODEL),
        'w_ffn_b_out': nrm(ks[17], (DEPTH, FFN_HIDDEN, D_MODEL), FFN_HIDDEN),
        'w_attn_in': nrm(ks[18], (N_ATTN_LAYERS, D_MODEL, ATTN_IN_DIM), D_MODEL),
        'q_norm': gain(ks[19], (N_ATTN_LAYERS, HEAD_DIM)),
        'k_norm': gain(ks[20], (N_ATTN_LAYERS, HEAD_DIM)),
        'kidx_norm': gain(ks[21], (N_ATTN_LAYERS, IDX_DIM)),
        'w_attn_out': nrm(ks[22], (N_ATTN_LAYERS, Q_DIM, D_MODEL), Q_DIM),
        'w_ssd_in': nrm(ks[23], (N_SSD_LAYERS, D_MODEL, SSD_IN_DIM), D_MODEL),
        'conv_w': nrm(ks[24], (N_SSD_LAYERS, CONV_W, CONV_DIM), CONV_W),
        'conv_b': 0.01 * jax.random.normal(ks[25], (N_SSD_LAYERS, CONV_DIM), f32),
        'dt_bias': dt_bias,
        'a_log': a_log,
        'd_skip': gain(ks[26], (N_SSD_LAYERS, SSD_HEADS)),
        'gate_norm': gain(ks[27], (N_SSD_LAYERS, D_INNER)),
        'w_ssd_out': nrm(ks[28], (N_SSD_LAYERS, D_INNER, D_MODEL), D_INNER),
    }


def reference(x_prompt, x_sample, cache_k, cache_v, cache_kidx, page_table, state_ssm, state_conv,
              meta_tokens, norm_ffn_a, w_ffn_a_in, w_ffn_a_out, norm_mix, norm_ffn_b, w_ffn_b_in, w_ffn_b_out,
              w_attn_in, q_norm, k_norm, kidx_norm, w_attn_out,
              w_ssd_in, conv_w, conv_b, dt_bias, a_log, d_skip, gate_norm, w_ssd_out):
    B = x_prompt.shape[0]
    meta = jnp.broadcast_to(meta_tokens.astype(x_prompt.dtype)[None], (B, N_META, D_MODEL))
    xp = jnp.concatenate([meta, x_prompt], axis=1)
    xs = x_sample
    topk_p = min(TOPK_MAX, SEQ // 4)
    topk_s = min(TOPK_MAX, (PAST_LEN + DEC_SEQ) // 4)
    kp_l, vp_l, kip_l, ks_l, vs_l, kis_l = [], [], [], [], [], []
    hp_l, cp_l, hs_l, cs_l = [], [], [], []
    for i in range(DEPTH):
        xp = xp + 0.5 * swiglu_ffn(xp, norm_ffn_a[i], w_ffn_a_in[i], w_ffn_a_out[i])
        xs = xs + 0.5 * swiglu_ffn(xs, norm_ffn_a[i], w_ffn_a_in[i], w_ffn_a_out[i])
        j = i // N_MIXERS
        if i % N_MIXERS == 0:
            ap = (w_attn_in[j], q_norm[j], k_norm[j], kidx_norm[j])
            qp, kp, vp, qip, kip, wip = attn_project(xp, norm_mix[i], *ap)
            xp = xp + dsa_prompt(qp, kp, vp, qip, kip, wip, topk_p) @ w_attn_out[j]
            qs, ksn, vsn, qis, kis, wis = attn_project(xs, norm_mix[i], *ap)
            xs = xs + dsa_sample(qs, ksn, vsn, qis, kis, wis, cache_k[j], cache_v[j], cache_kidx[j],
                                 page_table, topk_s) @ w_attn_out[j]
            kp_l.append(kp)
            vp_l.append(vp)
            kip_l.append(kip)
            ks_l.append(ksn)
            vs_l.append(vsn)
            kis_l.append(kis)
        else:
            sp = (w_ssd_in[j], conv_w[j], conv_b[j], dt_bias[j], a_log[j], d_skip[j], gate_norm[j], w_ssd_out[j])
            conv0 = jnp.zeros((B, CONV_W - 1, CONV_DIM), xp.dtype)
            h0 = jnp.zeros((B, SSD_HEADS, SSD_HEAD_DIM, D_STATE), jnp.float32)
            yp, hp, cp = ssd_mixer(xp, norm_mix[i], *sp, conv0, h0,
                                   ((0, N_META, N_META), (N_META, SEQ, CHUNK)))
            xp = xp + yp
            ys, hs, cs = ssd_mixer(xs, norm_mix[i], *sp, state_conv[j], state_ssm[j],
                                   ((0, DEC_SEQ, DEC_SEQ),))
            xs = xs + ys
            hp_l.append(hp)
            cp_l.append(cp)
            hs_l.append(hs)
            cs_l.append(cs)
        xp = xp + 0.5 * swiglu_ffn(xp, norm_ffn_b[i], w_ffn_b_in[i], w_ffn_b_out[i])
        xs = xs + 0.5 * swiglu_ffn(xs, norm_ffn_b[i], w_ffn_b_in[i], w_ffn_b_out[i])
    y_prompt = xp[:, N_META:]
    y_sample = xs
    new_k_prompt = jnp.stack(kp_l)
    new_v_prompt = jnp.stack(vp_l)
    new_kidx_prompt = jnp.stack(kip_l)
    new_k_sample = jnp.stack(ks_l)
    new_v_sample = jnp.stack(vs_l)
    new_kidx_sample = jnp.stack(kis_l)
    new_ssm_prompt = jnp.stack(hp_l)
    new_conv_prompt = jnp.stack(cp_l)
    new_ssm_sample = jnp.stack(hs_l)
    new_conv_sample = jnp.stack(cs_l)
    return (y_prompt, y_sample, new_k_prompt, new_v_prompt, new_kidx_prompt,
            new_k_sample, new_v_sample, new_kidx_sample,
            new_ssm_prompt, new_conv_prompt, new_ssm_sample, new_conv_sample)
```

```python
import functools
import math

import jax
import jax.numpy as jnp
from jax import lax
from jax.experimental import pallas as pl
from jax.experimental.pallas import tpu as pltpu

F32 = jnp.float32
BF16 = jnp.bfloat16
I32 = jnp.int32

N_META = 16
N_HEADS = 16
HEAD_DIM = 64
N_KV_HEADS = 4
HEADS_PER_KV = N_HEADS // N_KV_HEADS
N_IDX_HEADS = 8
IDX_DIM = 64
TOPK_MAX = 256
SSD_HEAD_DIM = 64
SSD_GROUPS = 4
D_STATE = 128
CONV_W = 4
EPS = 1e-6

LANES = 128
SUBLANES = 8
BLK = 128
NEG = -1e30
INT_MIN = -2 ** 31
VMEM_LIMIT = 56 * 1024 * 1024


def _cparams(sem, vmem=VMEM_LIMIT):
    return pltpu.CompilerParams(dimension_semantics=sem, vmem_limit_bytes=vmem)


def _row_tile(rows, pref):
    best = None
    for d in range(SUBLANES, min(rows, pref) + 1, SUBLANES):
        if rows % d == 0:
            best = d
    assert best is not None, rows
    return best


def _rms(x, g):
    var = jnp.mean(x * x, axis=-1, keepdims=True)
    return x * lax.rsqrt(var + EPS) * g


def _dot(a, b):
    return jnp.dot(a, b, preferred_element_type=F32)


def _dot_t0(a, b):
    return lax.dot_general(a, b, (((0,), (0,)), ((), ())), preferred_element_type=F32)


def _dot_t1(a, b):
    return lax.dot_general(a, b, (((1,), (1,)), ((), ())), preferred_element_type=F32)


def _split2(a):
    hi = a.astype(BF16)
    lo = (a - hi.astype(F32)).astype(BF16)
    return hi, lo


def _split3(a):
    a0 = a.astype(BF16)
    r = a - a0.astype(F32)
    a1 = r.astype(BF16)
    a2 = (r - a1.astype(F32)).astype(BF16)
    return a0, a1, a2


def _dot_sel(a, m, fn=_dot):
    a0, a1, a2 = _split3(a)
    return fn(a0, m) + fn(a1, m) + fn(a2, m)


def _silu(x):
    return x * (1.0 / (1.0 + jnp.exp(-x)))


def _softplus(x):
    return jnp.maximum(x, 0.0) + jnp.log(1.0 + jnp.exp(-jnp.abs(x)))


def _ffn_kernel(x_ref, g_ref, wa_ref, wb_ref, wo_ref, o_ref, xn_ref, acc_ref):
    j = pl.program_id(1)

    @pl.when(j == 0)
    def _():
        xn_ref[...] = _rms(x_ref[...], g_ref[...]).astype(BF16)
        acc_ref[...] = jnp.zeros_like(acc_ref)

    xn = xn_ref[...]
    a = _dot(xn, wa_ref[...])
    b = _dot(xn, wb_ref[...])
    h = (_silu(a) * b).astype(BF16)
    acc_ref[...] += _dot(h, wo_ref[...])

    @pl.when(j == pl.num_programs(1) - 1)
    def _():
        o_ref[...] = x_ref[...] + 0.5 * acc_ref[...]


def _ffn(x, g, w_in, w_out):
    rows, d = x.shape
    hid = w_out.shape[0]
    tm = _row_tile(rows, 1024)
    th = 256 if hid % 256 == 0 else LANES
    nh = hid // th
    return pl.pallas_call(
        _ffn_kernel,
        grid=(rows // tm, nh),
        in_specs=[
            pl.BlockSpec((tm, d), lambda i, j: (i, 0)),
            pl.BlockSpec((1, d), lambda i, j: (0, 0)),
            pl.BlockSpec((d, th), lambda i, j: (0, j)),
            pl.BlockSpec((d, th), lambda i, j: (0, j + nh)),
            pl.BlockSpec((th, d), lambda i, j: (j, 0)),
        ],
        out_specs=pl.BlockSpec((tm, d), lambda i, j: (i, 0)),
        out_shape=jax.ShapeDtypeStruct((rows, d), F32),
        scratch_shapes=[pltpu.VMEM((tm, d), BF16), pltpu.VMEM((tm, d), F32)],
        compiler_params=_cparams(("parallel", "arbitrary")),
        name="ffn",
    )(x, g.reshape(1, d), w_in, w_in, w_out)


def _attn_in_kernel(x_ref, g_ref, w_ref, qg_ref, kg_ref, kig_ref,
                    gq_ref, eq_ref, gk_ref, ek_ref,
                    q_o, k_o, v_o, qi_o, ki_o, wi_o):
    qd = N_HEADS * HEAD_DIM
    kd = N_KV_HEADS * HEAD_DIM
    qid = N_IDX_HEADS * IDX_DIM
    xn = _rms(x_ref[...], g_ref[...]).astype(BF16)
    h = _dot(xn, w_ref[...])

    def head_norm(t, gsum_ref, gexp_ref, gain):
        ss = _dot_sel(t * t, gsum_ref[...])
        rs = lax.rsqrt(ss * (1.0 / HEAD_DIM) + EPS)
        return t * _dot_sel(rs, gexp_ref[...]) * gain

    o = 0
    q = h[:, o:o + qd]
    q_o[...] = (head_norm(q, gq_ref, eq_ref, qg_ref[...]) * (HEAD_DIM ** -0.5)).astype(BF16)
    o += qd
    k_o[...] = head_norm(h[:, o:o + kd], gk_ref, ek_ref, kg_ref[...])
    o += kd
    v_o[...] = h[:, o:o + kd]
    o += kd
    qi_o[...] = (h[:, o:o + qid] * (IDX_DIM ** -0.5)).astype(BF16)
    o += qid
    ki = h[:, o:o + LANES]
    var = jnp.sum(ki * ki, axis=-1, keepdims=True) * (1.0 / IDX_DIM)
    ki_o[...] = ki * lax.rsqrt(var + EPS) * kig_ref[...]
    o += LANES
    wi_o[...] = h[:, o:o + LANES] * (N_IDX_HEADS ** -0.5)


def _seg_mats(n_heads, hd):
    col = jnp.arange(n_heads * hd) // hd
    gsum = (col[:, None] == jnp.arange(LANES)[None, :]).astype(BF16)
    return gsum, gsum.T


def _attn_in(x, g, w_pad, q_gain, k_gain, ki_gain):
    rows, d = x.shape
    n = w_pad.shape[1]
    qd, kd, qid = N_HEADS * HEAD_DIM, N_KV_HEADS * HEAD_DIM, N_IDX_HEADS * IDX_DIM
    tm = _row_tile(rows, 512)
    gq, eq = _seg_mats(N_HEADS, HEAD_DIM)
    gk, ek = _seg_mats(N_KV_HEADS, HEAD_DIM)
    const = lambda shape: pl.BlockSpec(shape, lambda i: (0,) * len(shape))
    rowb = lambda w: pl.BlockSpec((tm, w), lambda i: (i, 0))
    return pl.pallas_call(
        _attn_in_kernel,
        grid=(rows // tm,),
        in_specs=[rowb(d), const((1, d)), const((d, n)), const((1, qd)), const((1, kd)),
                  const((1, LANES)), const((qd, LANES)), const((LANES, qd)),
                  const((kd, LANES)), const((LANES, kd))],
        out_specs=[rowb(qd), rowb(kd), rowb(kd), rowb(qid), rowb(LANES), rowb(LANES)],
        out_shape=[jax.ShapeDtypeStruct((rows, qd), BF16),
                   jax.ShapeDtypeStruct((rows, kd), F32),
                   jax.ShapeDtypeStruct((rows, kd), F32),
                   jax.ShapeDtypeStruct((rows, qid), BF16),
                   jax.ShapeDtypeStruct((rows, LANES), F32),
                   jax.ShapeDtypeStruct((rows, LANES), F32)],
        compiler_params=_cparams(("parallel",)),
        name="attn_in",
    )(x, g.reshape(1, d), w_pad,
      jnp.tile(q_gain, N_HEADS).reshape(1, qd), jnp.tile(k_gain, N_KV_HEADS).reshape(1, kd),
      jnp.pad(ki_gain, (0, LANES - IDX_DIM)).reshape(1, LANES), gq, eq, gk, ek)


def _norm_proj_kernel(x_ref, g_ref, w_ref, ws_ref, oa_ref, ob_ref, os_ref, xn_ref, *, na):
    j = pl.program_id(1)

    @pl.when(j == 0)
    def _():
        xn = _rms(x_ref[...], g_ref[...]).astype(BF16)
        xn_ref[...] = xn
        os_ref[...] = _dot(xn, ws_ref[...])

    r = _dot(xn_ref[...], w_ref[...])

    @pl.when(j < na)
    def _():
        oa_ref[...] = r

    @pl.when(j >= na)
    def _():
        ob_ref[...] = r


def _norm_proj(x, g, w_main, w_side, n_a):
    rows, d = x.shape
    n = w_main.shape[1]
    tm = _row_tile(rows, 1024)
    tn = math.gcd(math.gcd(n_a, n - n_a), 1024)
    na = n_a // tn
    return pl.pallas_call(
        functools.partial(_norm_proj_kernel, na=na),
        grid=(rows // tm, n // tn),
        in_specs=[
            pl.BlockSpec((tm, d), lambda i, j: (i, 0)),
            pl.BlockSpec((1, d), lambda i, j: (0, 0)),
            pl.BlockSpec((d, tn), lambda i, j: (0, j)),
            pl.BlockSpec((d, LANES), lambda i, j: (0, 0)),
        ],
        out_specs=[pl.BlockSpec((tm, tn), lambda i, j: (i, jnp.minimum(j, na - 1))),
                   pl.BlockSpec((tm, tn), lambda i, j: (i, jnp.maximum(j - na, 0))),
                   pl.BlockSpec((tm, LANES), lambda i, j: (i, 0))],
        out_shape=[jax.ShapeDtypeStruct((rows, n_a), F32),
                   jax.ShapeDtypeStruct((rows, n - n_a), F32),
                   jax.ShapeDtypeStruct((rows, LANES), F32)],
        scratch_shapes=[pltpu.VMEM((tm, d), BF16)],
        compiler_params=_cparams(("parallel", "arbitrary")),
        name="norm_proj",
    )(x, g.reshape(1, d), w_main, w_side)


def _out_proj_kernel(x_ref, y_ref, w_ref, o_ref):
    o_ref[...] = x_ref[...] + _dot(y_ref[...], w_ref[...])


def _out_proj(x, y, w):
    rows, d = x.shape
    k = y.shape[1]
    tm = _row_tile(rows, 512)
    return pl.pallas_call(
        _out_proj_kernel,
        grid=(rows // tm,),
        in_specs=[pl.BlockSpec((tm, d), lambda i: (i, 0)),
                  pl.BlockSpec((tm, k), lambda i: (i, 0)),
                  pl.BlockSpec((k, d), lambda i: (0, 0))],
        out_specs=pl.BlockSpec((tm, d), lambda i: (i, 0)),
        out_shape=jax.ShapeDtypeStruct((rows, d), F32),
        compiler_params=_cparams(("parallel",)),
        name="out_proj",
    )(x, y, w)


def _sort_key(score):
    bits = pltpu.bitcast(score, I32)
    return bits ^ ((bits >> 31) & 0x7FFFFFFF)


def _tile_fold(x, op):
    r = x[0:SUBLANES]
    for t in range(1, x.shape[0] // SUBLANES):
        r = op(r, x[t * SUBLANES:(t + 1) * SUBLANES])
    return r


def _dsa_prompt_kernel(qit_ref, wt_ref, ki_ref, qt_ref, k_ref, v_ref, o_ref,
                       key_ref, bias_ref, s_ref, acc_ref, *, pad, topk):
    i = pl.program_id(1)
    nch = i + 1
    row = lax.broadcasted_iota(I32, (BLK, BLK), 0)
    lane = lax.broadcasted_iota(I32, (BLK, BLK), 1)
    t_pos = i * BLK + lane

    def p1(c, carry):
        dots = _dot(ki_ref[c], qit_ref[...])
        sc = jnp.zeros((BLK, BLK), F32)
        for h in range(N_IDX_HEADS):
            sc = sc + wt_ref[h:h + 1, :] * jnp.maximum(dots[:, h * BLK:(h + 1) * BLK], 0.0)
        s_pos = c * BLK + row
        valid = (s_pos <= t_pos) & (s_pos >= pad)
        key_ref[c] = jnp.where(valid, _sort_key(sc), INT_MIN)
        return carry

    lax.fori_loop(0, nch, p1, 0)

    def count(pred):
        def body(c, cnt):
            s_pos = c * BLK + row
            return cnt + jnp.where(pred(key_ref[c], s_pos), 1, 0)
        cnt = lax.fori_loop(0, nch, body, jnp.zeros((BLK, BLK), I32))
        return jnp.sum(cnt, axis=0, keepdims=True)

    thr = jnp.where(count(lambda k, s: k >= 0) >= topk, 0, INT_MIN).astype(I32)

    def bit_step(it, thr):
        cand = thr + (jnp.int32(1) << (30 - it))
        return jnp.where(count(lambda k, s: k >= cand) >= topk, cand, thr)

    thr = lax.fori_loop(0, 31, bit_step, thr)
    n_gt = count(lambda k, s: k > thr)
    n_ge = count(lambda k, s: k >= thr)
    need = topk - n_gt
    tied = (n_ge > topk) & (thr > INT_MIN)
    any_tied = jnp.max(tied.astype(I32))
    nbits = (key_ref.shape[0] * BLK).bit_length()

    def idx_search():
        def step(it, lo):
            cand = lo + (jnp.int32(1) << (nbits - 1 - it))
            below = count(lambda k, s: (k == thr) & (s < cand))
            return jnp.where(below < need, cand, lo)
        return lax.fori_loop(0, nbits, step, jnp.zeros((1, BLK), I32))

    jcut = lax.cond(any_tied > 0, idx_search,
                    lambda: jnp.full((1, BLK), 2 ** 30, I32))

    def p2(c, carry):
        k = key_ref[c]
        s_pos = c * BLK + row
        sel = (k > thr) | ((k == thr) & (s_pos <= jcut))
        sel = sel & (k > INT_MIN)
        bias_ref[c] = jnp.where(sel, 0.0, NEG)
        return carry

    lax.fori_loop(0, nch, p2, 0)

    gw = HEADS_PER_KV * BLK
    for g in range(N_KV_HEADS):
        def pa(c, m):
            s = _dot(k_ref[g, c], qt_ref[g])
            b = bias_ref[c]
            s = s + jnp.concatenate([b] * HEADS_PER_KV, axis=1)
            s_ref[c] = s
            return jnp.maximum(m, _tile_fold(s, jnp.maximum))

        m = lax.fori_loop(0, nch, pa, jnp.full((SUBLANES, gw), NEG, F32))
        m = jnp.max(m, axis=0, keepdims=True)
        acc_ref[...] = jnp.zeros_like(acc_ref)

        def pb(c, carry):
            p = jnp.exp(s_ref[c] - m).astype(BF16)
            acc_ref[...] += _dot_t0(p, v_ref[g, c])
            return carry

        lax.fori_loop(0, nch, pb, 0)
        acc = acc_ref[...]
        inv = 1.0 / acc[:, HEAD_DIM:HEAD_DIM + 1]
        for hh in range(HEADS_PER_KV):
            h = g * HEADS_PER_KV + hh
            oh = acc[hh * BLK:(hh + 1) * BLK, :HEAD_DIM] * inv[hh * BLK:(hh + 1) * BLK]
            q_row = i * BLK + lax.broadcasted_iota(I32, (BLK, HEAD_DIM), 0)
            oh = jnp.where(q_row >= pad, oh, 0.0)
            o_ref[:, h * HEAD_DIM:(h + 1) * HEAD_DIM] = oh.astype(BF16)


def _dsa_prompt(qit, wt, ki, qt, k, v_aug, pad, topk):
    bsz, nb = qit.shape[:2]
    t_pad = nb * BLK
    gw = HEADS_PER_KV * BLK
    kern = functools.partial(_dsa_prompt_kernel, pad=pad, topk=topk)
    return pl.pallas_call(
        kern,
        grid=(bsz, nb),
        in_specs=[
            pl.BlockSpec((None, None, IDX_DIM, N_IDX_HEADS * BLK), lambda b, i: (b, i, 0, 0)),
            pl.BlockSpec((None, None, N_IDX_HEADS, BLK), lambda b, i: (b, i, 0, 0)),
            pl.BlockSpec((None, nb, BLK, IDX_DIM), lambda b, i: (b, 0, 0, 0)),
            pl.BlockSpec((None, None, N_KV_HEADS, HEAD_DIM, gw), lambda b, i: (b, i, 0, 0, 0)),
            pl.BlockSpec((None, N_KV_HEADS, nb, BLK, HEAD_DIM), lambda b, i: (b, 0, 0, 0, 0)),
            pl.BlockSpec((None, N_KV_HEADS, nb, BLK, LANES), lambda b, i: (b, 0, 0, 0, 0)),
        ],
        out_specs=pl.BlockSpec((None, BLK, N_HEADS * HEAD_DIM), lambda b, i: (b, i, 0)),
        out_shape=jax.ShapeDtypeStruct((bsz, t_pad, N_HEADS * HEAD_DIM), BF16),
        scratch_shapes=[pltpu.VMEM((nb, BLK, BLK), I32),
                        pltpu.VMEM((nb, BLK, BLK), F32),
                        pltpu.VMEM((nb, BLK, gw), F32),
                        pltpu.VMEM((gw, LANES), F32)],
        compiler_params=_cparams(("parallel", "arbitrary")),
        name="dsa_prompt",
    )(qit, wt, ki, qt, k, v_aug)


def _smp_scores_kernel(pt_ref, qi_ref, w_ref, *refs, pg):
    page_refs, o_ref = refs[:pg], refs[pg]
    qi = qi_ref[...]
    w = w_ref[...]
    for p in range(pg):
        d = _dot_t1(qi, page_refs[p][...].astype(BF16))
        o_ref[p:p + 1, :] = jnp.sum(w * jnp.maximum(d, 0.0), axis=0, keepdims=True)


def _smp_scores(page_table, qi, wi, cache_kidx, pg):
    db, npg = page_table.shape
    kern = functools.partial(_smp_scores_kernel, pg=pg)
    page_spec = lambda p: pl.BlockSpec(
        (None, BLK, IDX_DIM), lambda b, j, pt: (pt[b * npg + j * pg + p], 0, 0))
    return pl.pallas_call(
        kern,
        grid_spec=pltpu.PrefetchScalarGridSpec(
            num_scalar_prefetch=1,
            grid=(db, npg // pg),
            in_specs=[pl.BlockSpec((None, N_IDX_HEADS, IDX_DIM), lambda b, j, pt: (b, 0, 0)),
                      pl.BlockSpec((None, N_IDX_HEADS, 1), lambda b, j, pt: (b, 0, 0))]
                     + [page_spec(p) for p in range(pg)],
            out_specs=pl.BlockSpec((None, pg, BLK), lambda b, j, pt: (b, j, 0)),
        ),
        out_shape=jax.ShapeDtypeStruct((db, npg, BLK), F32),
        compiler_params=_cparams(("parallel", "arbitrary")),
        name="sample_scores",
    )(page_table.reshape(-1), qi, wi, *([cache_kidx] * pg))


def _smp_select_kernel(sc_ref, qi_ref, w_ref, kin_ref, bias_ref, nb_ref, *, topk, past):
    npg = sc_ref.shape[0]
    key = _sort_key(sc_ref[...])
    qi = qi_ref[...].astype(F32)
    kn = kin_ref[...].astype(BF16).astype(F32)
    d = jnp.sum(qi * kn, axis=-1, keepdims=True)
    s_new = jnp.sum(w_ref[...] * jnp.maximum(d, 0.0), axis=0, keepdims=True)
    key_new = _sort_key(s_new)
    pos = (lax.broadcasted_iota(I32, (npg, BLK), 0) * BLK
           + lax.broadcasted_iota(I32, (npg, BLK), 1))

    def count(pred_past, pred_new):
        c = jnp.sum(jnp.where(pred_past, 1, 0), axis=1, keepdims=True)
        return jnp.sum(c, axis=0, keepdims=True) + jnp.where(pred_new, 1, 0)

    thr = jnp.where(count(key >= 0, key_new >= 0) >= topk, 0, INT_MIN).astype(I32)

    def bit_step(it, thr):
        cand = thr + (jnp.int32(1) << (30 - it))
        return jnp.where(count(key >= cand, key_new >= cand) >= topk, cand, thr)

    thr = lax.fori_loop(0, 31, bit_step, thr)
    need = topk - count(key > thr, key_new > thr)
    nbits = max(1, past.bit_length())

    def step(it, lo):
        cand = lo + (jnp.int32(1) << (nbits - 1 - it))
        below = count((key == thr) & (pos < cand), (key_new == thr) & (past < cand))
        return jnp.where(below < need, cand, lo)

    jcut = lax.fori_loop(0, nbits, step, jnp.zeros((1, 1), I32))
    sel = (key > thr) | ((key == thr) & (pos <= jcut))
    bias_ref[...] = jnp.where(sel, 0.0, NEG)
    sel_new = (key_new > thr) | ((key_new == thr) & (past <= jcut))
    nb_ref[...] = jnp.broadcast_to(jnp.where(sel_new, 0.0, NEG), nb_ref.shape)


def _smp_select(scores, qi, wi, ki_new, topk):
    db, npg, _ = scores.shape
    kern = functools.partial(_smp_select_kernel, topk=topk, past=npg * BLK)
    return pl.pallas_call(
        kern,
        grid=(db,),
        in_specs=[pl.BlockSpec((None, npg, BLK), lambda b: (b, 0, 0)),
                  pl.BlockSpec((None, N_IDX_HEADS, IDX_DIM), lambda b: (b, 0, 0)),
                  pl.BlockSpec((None, N_IDX_HEADS, 1), lambda b: (b, 0, 0)),
                  pl.BlockSpec((None, 1, IDX_DIM), lambda b: (b, 0, 0))],
        out_specs=[pl.BlockSpec((None, npg, BLK), lambda b: (b, 0, 0)),
                   pl.BlockSpec((None, SUBLANES, LANES), lambda b: (b, 0, 0))],
        out_shape=[jax.ShapeDtypeStruct((db, npg, BLK), F32),
                   jax.ShapeDtypeStruct((db, SUBLANES, LANES), F32)],
        compiler_params=_cparams(("parallel",)),
        name="sample_select",
    )(scores, qi, wi, ki_new)


def _smp_attend_kernel(pt_ref, q_ref, bias_ref, nb_ref, kn_ref, vn_ref, *refs, pg):
    k_refs, v_refs = refs[:pg], refs[pg:2 * pg]
    o_ref, m_ref, l_ref, acc_ref = refs[2 * pg:]
    j = pl.program_id(1)
    kd = N_KV_HEADS * HEAD_DIM

    @pl.when(j == 0)
    def _():
        m_ref[...] = jnp.full_like(m_ref, NEG)
        l_ref[...] = jnp.zeros_like(l_ref)
        acc_ref[...] = jnp.zeros_like(acc_ref)

    q = q_ref[...]
    s = [_dot_t1(q, k_refs[p][...].astype(BF16)) + bias_ref[p:p + 1, :] for p in range(pg)]
    m_old = m_ref[...]
    m_new = m_old
    for p in range(pg):
        m_new = jnp.maximum(m_new, jnp.max(s[p], axis=-1, keepdims=True))
    alpha = jnp.exp(m_old - m_new)
    l = l_ref[...] * alpha
    acc = acc_ref[...] * alpha
    for p in range(pg):
        e = jnp.exp(s[p] - m_new)
        l = l + jnp.sum(e, axis=-1, keepdims=True)
        acc = acc + _dot(e.astype(BF16), v_refs[p][...].astype(BF16))
    m_ref[...] = m_new
    l_ref[...] = l
    acc_ref[...] = acc

    @pl.when(j == pl.num_programs(1) - 1)
    def _():
        qf = q.astype(F32)
        kn = kn_ref[...].astype(BF16).astype(F32)
        vn = vn_ref[...].astype(BF16).astype(F32)
        s_new = jnp.sum(qf * kn, axis=-1, keepdims=True) + nb_ref[0:1, 0:1]
        m_fin = jnp.maximum(m_new, s_new)
        a2 = jnp.exp(m_new - m_fin)
        e_new = jnp.exp(s_new - m_fin)
        l_fin = l * a2 + e_new
        acc_fin = acc * a2 + e_new.astype(BF16).astype(F32) * vn
        res = acc_fin / l_fin
        hgrp = lax.broadcasted_iota(I32, (N_HEADS, HEAD_DIM), 0) // HEADS_PER_KV
        out = jnp.zeros((N_HEADS, HEAD_DIM), F32)
        for g in range(N_KV_HEADS):
            out = out + jnp.where(hgrp == g, res[:, g * HEAD_DIM:(g + 1) * HEAD_DIM], 0.0)
        o_ref[...] = out


def _smp_attend(page_table, q_bd, bias, nbias, k_new, v_new, cache_k, cache_v, pg):
    db, npg = page_table.shape
    kd = N_KV_HEADS * HEAD_DIM
    kern = functools.partial(_smp_attend_kernel, pg=pg)
    page_spec = lambda p: pl.BlockSpec(
        (None, BLK, kd), lambda b, j, pt: (pt[b * npg + j * pg + p], 0, 0))
    per_b = lambda shape: pl.BlockSpec((None,) + shape, lambda b, j, pt: (b, 0, 0))
    return pl.pallas_call(
        kern,
        grid_spec=pltpu.PrefetchScalarGridSpec(
            num_scalar_prefetch=1,
            grid=(db, npg // pg),
            in_specs=[per_b((N_HEADS, kd)),
                      pl.BlockSpec((None, pg, BLK), lambda b, j, pt: (b, j, 0)),
                      per_b((SUBLANES, LANES)), per_b((1, kd)), per_b((1, kd))]
                     + [page_spec(p) for p in range(pg)] * 2,
            out_specs=per_b((N_HEADS, HEAD_DIM)),
            scratch_shapes=[pltpu.VMEM((N_HEADS, 1), F32), pltpu.VMEM((N_HEADS, 1), F32),
                            pltpu.VMEM((N_HEADS, kd), F32)],
        ),
        out_shape=jax.ShapeDtypeStruct((db, N_HEADS, HEAD_DIM), F32),
        compiler_params=_cparams(("parallel", "arbitrary")),
        name="sample_attend",
    )(page_table.reshape(-1), q_bd, bias, nbias, k_new, v_new,
      *([cache_k] * pg), *([cache_v] * pg))


def _ssd_prompt_kernel(z_ref, xbc_ref, dt_ref, dtt_ref, cw_ref, cb_ref, dtb_ref, a_ref,
                       dtbc_ref, ac_ref, dsk_ref, gn_ref, ltri_ref, utri_ref,
                       y_ref, st_ref, xpad_ref, h_ref, yb_ref, *, pad, n_heads):
    c = pl.program_id(1)
    d_in = n_heads * SSD_HEAD_DIM
    gn = SSD_GROUPS * D_STATE
    hpg = n_heads // SSD_GROUPS

    @pl.when(c == 0)
    def _():
        xpad_ref[0:SUBLANES, :] = jnp.zeros((SUBLANES, xpad_ref.shape[1]), F32)
        h_ref[...] = jnp.zeros_like(h_ref)

    xpad_ref[SUBLANES:, :] = xbc_ref[...]
    conv = cb_ref[...] + cw_ref[CONV_W - 1:CONV_W, :] * xbc_ref[...]
    for j in range(CONV_W - 1):
        sh = CONV_W - 1 - j
        conv = conv + cw_ref[j:j + 1, :] * xpad_ref[SUBLANES - sh:SUBLANES - sh + BLK, :]
    xpad_ref[0:SUBLANES, :] = xbc_ref[BLK - SUBLANES:BLK, :]
    xbc = _silu(conv)

    live = (c > 0) | (lax.broadcasted_iota(I32, (BLK, LANES), 0) >= pad)
    dt = jnp.where(live, _softplus(dt_ref[...] + dtb_ref[...]), 0.0)
    acum = _dot_sel(dt * a_ref[...], ltri_ref[...], fn=lambda x, m: _dot(m, x))
    live_t = (c > 0) | (lax.broadcasted_iota(I32, (dtt_ref.shape[0], BLK), 1) >= pad)
    dtt = jnp.where(live_t, _softplus(dtt_ref[...] + dtbc_ref[...]), 0.0)
    acum_t = _dot_sel(dtt * ac_ref[...], utri_ref[...])
    causal = (lax.broadcasted_iota(I32, (BLK, BLK), 0) >= lax.broadcasted_iota(I32, (BLK, BLK), 1))

    for g in range(SSD_GROUPS):
        bm = xbc[:, d_in + g * D_STATE:d_in + (g + 1) * D_STATE].astype(BF16)
        cm = xbc[:, d_in + gn + g * D_STATE:d_in + gn + (g + 1) * D_STATE].astype(BF16)
        cb = _dot_t1(cm, bm)
        for hh in range(hpg):
            h = g * hpg + hh
            col = acum[:, h:h + 1]
            rowv = acum_t[h:h + 1, :]
            a_last = acum_t[h:h + 1, BLK - 1:BLK]
            decay = jnp.exp(jnp.where(causal, col - rowv, NEG))
            xh = xbc[:, h * SSD_HEAD_DIM:(h + 1) * SSD_HEAD_DIM]
            xdt = xh * dt[:, h:h + 1]
            hprev = h_ref[h]
            y = _dot((cb * decay).astype(BF16), xdt.astype(BF16))
            y = y + jnp.exp(col) * _dot_t1(cm, hprev.astype(BF16))
            y = y + dsk_ref[0:1, h:h + 1] * xh
            yb_ref[:, h * SSD_HEAD_DIM:(h + 1) * SSD_HEAD_DIM] = y
            w_s = (xdt * jnp.exp(a_last - col)).astype(BF16)
            h_ref[h] = jnp.exp(a_last) * hprev + _dot_t0(w_s, bm)

    yg = yb_ref[...] * _silu(z_ref[...])
    gsz = d_in // SSD_GROUPS
    for g in range(SSD_GROUPS):
        blk = yg[:, g * gsz:(g + 1) * gsz]
        y_ref[:, g * gsz:(g + 1) * gsz] = _rms(blk, gn_ref[:, g * gsz:(g + 1) * gsz]).astype(BF16)

    @pl.when(c == pl.num_programs(1) - 1)
    def _():
        st_ref[...] = h_ref[...]


def _ssd_prompt(z_xbc, dt_raw, dt_raw_t, conv_w, conv_b, dt_bias, a_log, d_skip, gate_norm,
                bsz, nb, pad, n_heads):
    d_in = n_heads * SSD_HEAD_DIM
    cdim = d_in + 2 * SSD_GROUPS * D_STATE
    assert n_heads <= LANES
    hp = LANES - n_heads
    a = -jnp.exp(a_log.astype(F32))
    ltri = jnp.tril(jnp.ones((BLK, BLK), F32)).astype(BF16)
    kern = functools.partial(_ssd_prompt_kernel, pad=pad, n_heads=n_heads)
    const = lambda shape: pl.BlockSpec(shape, lambda b, c: (0,) * len(shape))
    zb = d_in // cdim if d_in % cdim == 0 else None
    return pl.pallas_call(
        kern,
        grid=(bsz, nb),
        in_specs=[
            pl.BlockSpec((BLK, d_in), lambda b, c: (b * nb + c, 0)),
            pl.BlockSpec((BLK, cdim), lambda b, c: (b * nb + c, 0)),
            pl.BlockSpec((BLK, LANES), lambda b, c: (b * nb + c, 0)),
            pl.BlockSpec((None, None, n_heads, BLK), lambda b, c: (b, c, 0, 0)),
            const((CONV_W, cdim)), const((1, cdim)), const((1, LANES)), const((1, LANES)),
            const((n_heads, 1)), const((n_heads, 1)), const((1, LANES)), const((1, d_in)),
            const((BLK, BLK)), const((BLK, BLK)),
        ],
        out_specs=[pl.BlockSpec((BLK, d_in), lambda b, c: (b * nb + c, 0)),
                   pl.BlockSpec((None, n_heads, SSD_HEAD_DIM, D_STATE), lambda b, c: (b, 0, 0, 0))],
        out_shape=[jax.ShapeDtypeStruct((bsz * nb * BLK, d_in), BF16),
                   jax.ShapeDtypeStruct((bsz, n_heads, SSD_HEAD_DIM, D_STATE), F32)],
        scratch_shapes=[pltpu.VMEM((BLK + SUBLANES, cdim), F32),
                        pltpu.VMEM((n_heads, SSD_HEAD_DIM, D_STATE), F32),
                        pltpu.VMEM((BLK, d_in), F32)],
        compiler_params=_cparams(("parallel", "arbitrary")),
        name="ssd_prompt",
    )(z_xbc[0], z_xbc[1], dt_raw, dt_raw_t, conv_w, conv_b.reshape(1, cdim),
      jnp.pad(dt_bias, (0, hp)).reshape(1, LANES), jnp.pad(a, (0, hp)).reshape(1, LANES),
      dt_bias.reshape(n_heads, 1), a.reshape(n_heads, 1),
      jnp.pad(d_skip, (0, hp)).reshape(1, LANES), gate_norm.reshape(1, d_in), ltri, ltri.T)


def _ssd_sample_kernel(z_ref, xbc_ref, dt_ref, cst_ref, h0_ref, cw_ref, cb_ref, dtb_ref, a_ref,
                       dsk_ref, gn_ref, exp_ref, y_ref, h_ref, *, n_heads):
    d_in = n_heads * SSD_HEAD_DIM
    gn = SSD_GROUPS * D_STATE
    gsz = d_in // SSD_GROUPS
    hpg = n_heads // SSD_GROUPS
    conv = cb_ref[...] + cw_ref[CONV_W - 1:CONV_W, :] * xbc_ref[...]
    for j in range(CONV_W - 1):
        conv = conv + cw_ref[j:j + 1, :] * cst_ref[j:j + 1, :]
    xbc = _silu(conv)
    dt = _softplus(dt_ref[...] + dtb_ref[...])
    pad8 = lambda r: jnp.concatenate([r, jnp.zeros((SUBLANES - 1, r.shape[1]), F32)], axis=0)
    dt_ch = _dot_sel(pad8(dt), exp_ref[...])[0:1]
    da_ch = jnp.exp(_dot_sel(pad8(dt * a_ref[...]), exp_ref[...])[0:1])
    dsk_ch = _dot_sel(pad8(dsk_ref[...]), exp_ref[...])[0:1]
    xh = xbc[:, :d_in]
    xdt = xh * dt_ch
    ones = jnp.ones((SUBLANES, D_STATE), BF16)
    outs = []
    for g in range(SSD_GROUPS):
        sl = slice(g * gsz, (g + 1) * gsz)
        bm = xbc[:, d_in + g * D_STATE:d_in + (g + 1) * D_STATE]
        cm = xbc[:, d_in + gn + g * D_STATE:d_in + gn + (g + 1) * D_STATE]
        da_col = _dot_sel(pad8(da_ch[:, sl]), ones, fn=_dot_t0)
        xdt_col = _dot_sel(pad8(xdt[:, sl]), ones, fn=_dot_t0)
        h0 = h0_ref[g * hpg:(g + 1) * hpg].reshape(gsz, D_STATE)
        hn = da_col * h0 + xdt_col.astype(BF16).astype(F32) * bm.astype(BF16).astype(F32)
        h_ref[g * hpg:(g + 1) * hpg] = hn.reshape(hpg, SSD_HEAD_DIM, D_STATE)
        y_col = jnp.sum(hn * cm, axis=-1, keepdims=True)
        outs.append(y_col)
    y_cols = jnp.concatenate(outs, axis=0)
    rows = []
    eye = (lax.broadcasted_iota(I32, (LANES, LANES), 0)
           == lax.broadcasted_iota(I32, (LANES, LANES), 1)).astype(F32)
    for t in range(d_in // LANES):
        blk = y_cols[t * LANES:(t + 1) * LANES]
        rows.append(jnp.sum(blk * eye, axis=0, keepdims=True))
    y = jnp.concatenate(rows, axis=1) + dsk_ch * xh
    y = y * _silu(z_ref[...])
    for g in range(SSD_GROUPS):
        sl = slice(g * gsz, (g + 1) * gsz)
        y_ref[:, sl] = _rms(y[:, sl], gn_ref[:, sl]).astype(BF16)


def _ssd_sample(z, xbc_raw, dt_raw, conv_state, h0, conv_w, conv_b, dt_bias, a_log, d_skip,
                gate_norm, n_heads):
    db = z.shape[0]
    d_in = n_heads * SSD_HEAD_DIM
    cdim = d_in + 2 * SSD_GROUPS * D_STATE
    hp = LANES - n_heads
    a = -jnp.exp(a_log.astype(F32))
    expand = (jnp.arange(LANES)[:, None] == (jnp.arange(d_in) // SSD_HEAD_DIM)[None, :]).astype(BF16)
    kern = functools.partial(_ssd_sample_kernel, n_heads=n_heads)
    const = lambda shape: pl.BlockSpec(shape, lambda b: (0,) * len(shape))
    per_b = lambda shape: pl.BlockSpec((None,) + shape, lambda b: (b,) + (0,) * len(shape))
    y, h = pl.pallas_call(
        kern,
        grid=(db,),
        in_specs=[per_b((1, d_in)), per_b((1, cdim)), per_b((1, LANES)), per_b((CONV_W - 1, cdim)),
                  per_b((n_heads, SSD_HEAD_DIM, D_STATE)),
                  const((CONV_W, cdim)), const((1, cdim)), const((1, LANES)), const((1, LANES)),
                  const((1, LANES)), const((1, d_in)), const((LANES, d_in))],
        out_specs=[per_b((1, d_in)), per_b((n_heads, SSD_HEAD_DIM, D_STATE))],
        out_shape=[jax.ShapeDtypeStruct((db, 1, d_in), BF16),
                   jax.ShapeDtypeStruct((db, n_heads, SSD_HEAD_DIM, D_STATE), F32)],
        compiler_params=_cparams(("parallel",)),
        name="ssd_sample",
    )(z.reshape(db, 1, d_in), xbc_raw.reshape(db, 1, cdim), dt_raw.reshape(db, 1, LANES),
      conv_state, h0, conv_w, conv_b.reshape(1, cdim),
      jnp.pad(dt_bias, (0, hp)).reshape(1, LANES), jnp.pad(a, (0, hp)).reshape(1, LANES),
      jnp.pad(d_skip, (0, hp)).reshape(1, LANES), gate_norm.reshape(1, d_in), expand)
    return y.reshape(db, d_in), h


def kernel(x_prompt, x_sample, cache_k, cache_v, cache_kidx, page_table, state_ssm, state_conv,
           meta_tokens, norm_ffn_a, w_ffn_a_in, w_ffn_a_out, norm_mix, norm_ffn_b, w_ffn_b_in,
           w_ffn_b_out, w_attn_in, q_norm, k_norm, kidx_norm, w_attn_out,
           w_ssd_in, conv_w, conv_b, dt_bias, a_log, d_skip, gate_norm, w_ssd_out):
    bsz, seq, d = x_prompt.shape
    db = x_sample.shape[0]
    assert x_sample.shape[1] == 1
    t_real = N_META + seq
    nb = -(-t_real // BLK)
    t_pad = nb * BLK
    pad = t_pad - t_real
    npg = page_table.shape[1]
    past = npg * BLK
    topk_p = min(TOPK_MAX, seq // 4)
    topk_s = min(TOPK_MAX, (past + 1) // 4)
    qd, kd, qid = N_HEADS * HEAD_DIM, N_KV_HEADS * HEAD_DIM, N_IDX_HEADS * IDX_DIM
    d_in = w_ssd_out.shape[1]
    n_heads = d_in // SSD_HEAD_DIM
    cdim = d_in + 2 * SSD_GROUPS * D_STATE
    depth = norm_mix.shape[0]

    meta = jnp.broadcast_to(meta_tokens.astype(F32)[None], (bsz, N_META, d))
    xp = jnp.concatenate([jnp.zeros((bsz, pad, d), F32), meta, x_prompt], axis=1)
    xp = xp.reshape(bsz * t_pad, d)
    xs = x_sample.reshape(db, d)

    outs = {k: [] for k in ("kp", "vp", "kip", "ks", "vs", "kis", "hp", "cp", "hs", "cs")}
    for i in range(depth):
        wa_in, wa_out = w_ffn_a_in[i].astype(BF16), w_ffn_a_out[i].astype(BF16)
        xp = _ffn(xp, norm_ffn_a[i], wa_in, wa_out)
        xs = _ffn(xs, norm_ffn_a[i], wa_in, wa_out)
        j = i // 2
        if i % 2 == 0:
            w = w_attn_in[j]
            o1, o2 = qd + 2 * kd + qid, qd + 2 * kd + qid + IDX_DIM
            w_pad = jnp.concatenate(
                [w[:, :o1], jnp.pad(w[:, o1:o2], ((0, 0), (0, LANES - IDX_DIM))),
                 jnp.pad(w[:, o2:], ((0, 0), (0, LANES - N_IDX_HEADS)))], axis=1).astype(BF16)
            w_out = w_attn_out[j].astype(BF16)

            q, k, v, qi, ki, wi = _attn_in(xp, norm_mix[i], w_pad, q_norm[j], k_norm[j], kidx_norm[j])
            qt = q.reshape(bsz, nb, BLK, N_KV_HEADS, HEADS_PER_KV, HEAD_DIM)
            qt = qt.transpose(0, 1, 3, 5, 4, 2).reshape(bsz, nb, N_KV_HEADS, HEAD_DIM, HEADS_PER_KV * BLK)
            qit = qi.reshape(bsz, nb, BLK, N_IDX_HEADS, IDX_DIM)
            qit = qit.transpose(0, 1, 4, 3, 2).reshape(bsz, nb, IDX_DIM, N_IDX_HEADS * BLK)
            wt = wi[:, :N_IDX_HEADS].reshape(bsz, nb, BLK, N_IDX_HEADS).transpose(0, 1, 3, 2)
            kib = ki[:, :IDX_DIM].astype(BF16).reshape(bsz, nb, BLK, IDX_DIM)
            kb = k.astype(BF16).reshape(bsz, nb, BLK, N_KV_HEADS, HEAD_DIM).transpose(0, 3, 1, 2, 4)
            vb = v.astype(BF16).reshape(bsz, nb, BLK, N_KV_HEADS, HEAD_DIM).transpose(0, 3, 1, 2, 4)
            v_aug = jnp.concatenate([vb, jnp.ones_like(vb)], axis=-1)
            o = _dsa_prompt(qit, wt, kib, qt, kb, v_aug, pad, topk_p)
            xp = _out_proj(xp, o.reshape(bsz * t_pad, qd), w_out)
            outs["kp"].append(k.reshape(bsz, t_pad, N_KV_HEADS, HEAD_DIM)[:, pad:])
            outs["vp"].append(v.reshape(bsz, t_pad, N_KV_HEADS, HEAD_DIM)[:, pad:])
            outs["kip"].append(ki[:, :IDX_DIM].reshape(bsz, t_pad, IDX_DIM)[:, pad:])

            q, k, v, qi, ki, wi = _attn_in(xs, norm_mix[i], w_pad, q_norm[j], k_norm[j], kidx_norm[j])
            pg = math.gcd(npg, 16)
            qi3 = qi.reshape(db, N_IDX_HEADS, IDX_DIM)
            wi3 = wi[:, :N_IDX_HEADS].reshape(db, N_IDX_HEADS, 1)
            scores = _smp_scores(page_table, qi3, wi3, cache_kidx[j], pg)
            bias, nbias = _smp_select(scores, qi3, wi3, ki[:, :IDX_DIM].reshape(db, 1, IDX_DIM), topk_s)
            hsel = (jnp.arange(N_HEADS)[:, None] // HEADS_PER_KV == jnp.arange(N_KV_HEADS)[None, :])
            q_bd = (q.reshape(db, N_HEADS, 1, HEAD_DIM) * hsel[None, :, :, None].astype(BF16))
            q_bd = q_bd.reshape(db, N_HEADS, kd)
            npool = cache_k.shape[1]
            o = _smp_attend(page_table, q_bd, bias, nbias, k.reshape(db, 1, kd), v.reshape(db, 1, kd),
                            cache_k[j].reshape(npool, BLK, kd), cache_v[j].reshape(npool, BLK, kd), pg)
            xs = _out_proj(xs, o.reshape(db, qd).astype(BF16), w_out)
            outs["ks"].append(k.reshape(db, 1, N_KV_HEADS, HEAD_DIM))
            outs["vs"].append(v.reshape(db, 1, N_KV_HEADS, HEAD_DIM))
            outs["kis"].append(ki[:, :IDX_DIM].reshape(db, 1, IDX_DIM))
        else:
            w = w_ssd_in[j]
            w_main = w[:, :d_in + cdim].astype(BF16)
            w_dt = jnp.pad(w[:, d_in + cdim:], ((0, 0), (0, LANES - n_heads))).astype(BF16)
            w_out = w_ssd_out[j].astype(BF16)
            sp = (conv_w[j], conv_b[j], dt_bias[j], a_log[j], d_skip[j], gate_norm[j])

            z, xbc_raw, dt_raw = _norm_proj(xp, norm_mix[i], w_main, w_dt, d_in)
            dt_t = dt_raw[:, :n_heads].reshape(bsz, nb, BLK, n_heads).transpose(0, 1, 3, 2)
            y, hfin = _ssd_prompt((z, xbc_raw), dt_raw, dt_t, *sp, bsz, nb, pad, n_heads)
            xp = _out_proj(xp, y, w_out)
            outs["hp"].append(hfin)
            outs["cp"].append(xbc_raw.reshape(bsz, t_pad, cdim)[:, t_pad - (CONV_W - 1):])

            z, xbc_raw, dt_raw = _norm_proj(xs, norm_mix[i], w_main, w_dt, d_in)
            y, hnew = _ssd_sample(z, xbc_raw, dt_raw, state_conv[j], state_ssm[j], *sp, n_heads)
            xs = _out_proj(xs, y, w_out)
            outs["hs"].append(hnew)
            outs["cs"].append(jnp.concatenate([state_conv[j][:, 1:], xbc_raw[:, None, :]], axis=1))
        wb_in, wb_out = w_ffn_b_in[i].astype(BF16), w_ffn_b_out[i].astype(BF16)
        xp = _ffn(xp, norm_ffn_b[i], wb_in, wb_out)
        xs = _ffn(xs, norm_ffn_b[i], wb_in, wb_out)

    y_prompt = xp.reshape(bsz, t_pad, d)[:, pad + N_META:]
    y_sample = xs.reshape(db, 1, d)
    st = lambda key: jnp.stack(outs[key])
    return (y_prompt, y_sample, st("kp"), st("vp"), st("kip"), st("ks"), st("vs"), st("kis"),
            st("hp"), st("cp"), st("hs"), st("cs"))
```

```python
import functools
import math

import jax
import jax.numpy as jnp
from jax import lax
from jax.experimental import pallas as pl
from jax.experimental.pallas import tpu as pltpu

F32 = jnp.float32
BF16 = jnp.bfloat16
I32 = jnp.int32

N_META = 16
N_HEADS = 16
HEAD_DIM = 64
N_KV_HEADS = 4
HEADS_PER_KV = N_HEADS // N_KV_HEADS
N_IDX_HEADS = 8
IDX_DIM = 64
TOPK_MAX = 256
SSD_HEAD_DIM = 64
SSD_GROUPS = 4
D_STATE = 128
CONV_W = 4
EPS = 1e-6

LANES = 128
SUBLANES = 8
BLK = 128
NEG = -1e30
SAFE_LOGIT = 40.0
VT_ROWS = 80
INT_MIN = -2 ** 31
VMEM_LIMIT = 56 * 1024 * 1024


def _cparams(sem, vmem=VMEM_LIMIT):
    return pltpu.CompilerParams(dimension_semantics=sem, vmem_limit_bytes=vmem)


def _row_tile(rows, pref):
    best = None
    for d in range(SUBLANES, min(rows, pref) + 1, SUBLANES):
        if rows % d == 0:
            best = d
    assert best is not None, rows
    return best


def _rms(x, g):
    var = jnp.mean(x * x, axis=-1, keepdims=True)
    return x * lax.rsqrt(var + EPS) * g


def _dot(a, b):
    return jnp.dot(a, b, preferred_element_type=F32)


def _dot_t0(a, b):
    return lax.dot_general(a, b, (((0,), (0,)), ((), ())), preferred_element_type=F32)


def _dot_t1(a, b):
    return lax.dot_general(a, b, (((1,), (1,)), ((), ())), preferred_element_type=F32)


def _split2(a):
    hi = a.astype(BF16)
    lo = (a - hi.astype(F32)).astype(BF16)
    return hi, lo


def _split3(a):
    a0 = a.astype(BF16)
    r = a - a0.astype(F32)
    a1 = r.astype(BF16)
    a2 = (r - a1.astype(F32)).astype(BF16)
    return a0, a1, a2


def _dot_sel(a, m, fn=_dot):
    a0, a1, a2 = _split3(a)
    return fn(a0, m) + fn(a1, m) + fn(a2, m)


def _silu(x):
    return x * (1.0 / (1.0 + jnp.exp(-x)))


def _softplus(x):
    return jnp.maximum(x, 0.0) + jnp.log(1.0 + jnp.exp(-jnp.abs(x)))


def _ffn_kernel(x_ref, g_ref, wa_ref, wb_ref, wo_ref, o_ref, xn_ref, acc_ref):
    j = pl.program_id(1)

    @pl.when(j == 0)
    def _():
        xn_ref[...] = _rms(x_ref[...], g_ref[...]).astype(BF16)
        acc_ref[...] = jnp.zeros_like(acc_ref)

    xn = xn_ref[...]
    a = _dot(xn, wa_ref[...])
    b = _dot(xn, wb_ref[...])
    h = (_silu(a) * b).astype(BF16)
    acc_ref[...] += _dot(h, wo_ref[...])

    @pl.when(j == pl.num_programs(1) - 1)
    def _():
        o_ref[...] = x_ref[...] + 0.5 * acc_ref[...]


def _ffn(x, g, w_in, w_out):
    rows, d = x.shape
    hid = w_out.shape[0]
    tm = _row_tile(rows, 1024)
    th = 256 if hid % 256 == 0 else LANES
    nh = hid // th
    return pl.pallas_call(
        _ffn_kernel,
        grid=(rows // tm, nh),
        in_specs=[
            pl.BlockSpec((tm, d), lambda i, j: (i, 0)),
            pl.BlockSpec((1, d), lambda i, j: (0, 0)),
            pl.BlockSpec((d, th), lambda i, j: (0, j)),
            pl.BlockSpec((d, th), lambda i, j: (0, j + nh)),
            pl.BlockSpec((th, d), lambda i, j: (j, 0)),
        ],
        out_specs=pl.BlockSpec((tm, d), lambda i, j: (i, 0)),
        out_shape=jax.ShapeDtypeStruct((rows, d), F32),
        scratch_shapes=[pltpu.VMEM((tm, d), BF16), pltpu.VMEM((tm, d), F32)],
        compiler_params=_cparams(("parallel", "arbitrary")),
        name="ffn",
    )(x, g.reshape(1, d), w_in, w_in, w_out)


def _attn_project(x_ref, g_ref, w_ref, qg_ref, kg_ref, kig_ref, gq_ref, eq_ref, gk_ref, ek_ref):
    qd = N_HEADS * HEAD_DIM
    kd = N_KV_HEADS * HEAD_DIM
    qid = N_IDX_HEADS * IDX_DIM
    xn = _rms(x_ref[...], g_ref[...]).astype(BF16)
    h = _dot(xn, w_ref[...])

    def head_norm(t, gsum_ref, gexp_ref, gain):
        ss = _dot_sel(t * t, gsum_ref[...])
        rs = lax.rsqrt(ss * (1.0 / HEAD_DIM) + EPS)
        return t * _dot_sel(rs, gexp_ref[...]) * gain

    o = 0
    q = head_norm(h[:, o:o + qd], gq_ref, eq_ref, qg_ref[...]) * (HEAD_DIM ** -0.5)
    o += qd
    k = head_norm(h[:, o:o + kd], gk_ref, ek_ref, kg_ref[...])
    o += kd
    v = h[:, o:o + kd]
    o += kd
    qi = h[:, o:o + qid] * (IDX_DIM ** -0.5)
    o += qid
    ki = h[:, o:o + LANES]
    var = jnp.sum(ki * ki, axis=-1, keepdims=True) * (1.0 / IDX_DIM)
    ki = ki * lax.rsqrt(var + EPS) * kig_ref[...]
    o += LANES
    wi = h[:, o:o + LANES] * (N_IDX_HEADS ** -0.5)
    return q, k, v, qi, ki, wi


def _attn_in_sample_kernel(*refs):
    q_o, k_o, v_o, qi_o, ki_o, wi_o = refs[10:]
    q, k, v, qi, ki, wi = _attn_project(*refs[:10])
    q_o[...] = q.astype(BF16)
    k_o[...] = k
    v_o[...] = v
    qi_o[...] = qi.astype(BF16)
    ki_o[...] = ki
    wi_o[...] = wi


def _attn_in_prompt_kernel(*refs):
    k_o, v_o, ki_o, qt_o, qit_o, wt_o, kg_o, vt_o, kib_o = refs[10:]
    q, k, v, qi, ki, wi = _attn_project(*refs[:10])
    k_o[...] = k
    v_o[...] = v
    ki_o[...] = ki
    kib_o[...] = ki[:, :IDX_DIM].astype(BF16)
    for g in range(N_KV_HEADS):
        kg_o[g] = k[:, g * HEAD_DIM:(g + 1) * HEAD_DIM].astype(BF16)
    tail = (lax.broadcasted_iota(I32, (VT_ROWS - HEAD_DIM, BLK), 0) == 0).astype(BF16)
    for r in range(q.shape[0] // BLK):
        rows = slice(r * BLK, (r + 1) * BLK)
        for t in range(N_HEADS * HEAD_DIM // LANES):
            tt = q[rows, t * LANES:(t + 1) * LANES].T.astype(BF16)
            for u in range(LANES // HEAD_DIM):
                h = t * (LANES // HEAD_DIM) + u
                g, hh = h // HEADS_PER_KV, h % HEADS_PER_KV
                qt_o[r, g, :, hh * BLK:(hh + 1) * BLK] = tt[u * HEAD_DIM:(u + 1) * HEAD_DIM]
        for t in range(N_IDX_HEADS * IDX_DIM // LANES):
            tt = qi[rows, t * LANES:(t + 1) * LANES].T.astype(BF16)
            for u in range(LANES // IDX_DIM):
                h = t * (LANES // IDX_DIM) + u
                qit_o[r, :, h * BLK:(h + 1) * BLK] = tt[u * IDX_DIM:(u + 1) * IDX_DIM]
        wt_o[r] = wi[rows].T[0:N_IDX_HEADS]
        for t in range(N_KV_HEADS * HEAD_DIM // LANES):
            tt = v[rows, t * LANES:(t + 1) * LANES].T.astype(BF16)
            for u in range(LANES // HEAD_DIM):
                vt_o[r, t * (LANES // HEAD_DIM) + u] = jnp.concatenate(
                    [tt[u * HEAD_DIM:(u + 1) * HEAD_DIM], tail], axis=0)


def _seg_mats(n_heads, hd):
    col = jnp.arange(n_heads * hd) // hd
    gsum = (col[:, None] == jnp.arange(LANES)[None, :]).astype(BF16)
    return gsum, gsum.T


def _attn_in(x, g, w_pad, q_gain, k_gain, ki_gain, prompt):
    rows, d = x.shape
    n = w_pad.shape[1]
    qd, kd, qid = N_HEADS * HEAD_DIM, N_KV_HEADS * HEAD_DIM, N_IDX_HEADS * IDX_DIM
    gw = HEADS_PER_KV * BLK
    tm = _row_tile(rows, 512)
    gq, eq = _seg_mats(N_HEADS, HEAD_DIM)
    gk, ek = _seg_mats(N_KV_HEADS, HEAD_DIM)
    const = lambda shape: pl.BlockSpec(shape, lambda i: (0,) * len(shape))
    rowb = lambda w: pl.BlockSpec((tm, w), lambda i: (i, 0))
    sds = jax.ShapeDtypeStruct
    if prompt:
        assert tm % BLK == 0
        nbk, tb = rows // BLK, tm // BLK
        blkb = lambda *s: pl.BlockSpec((tb,) + s, lambda i: (i,) + (0,) * len(s))
        kern = _attn_in_prompt_kernel
        out_specs = [rowb(kd), rowb(kd), rowb(LANES), blkb(N_KV_HEADS, HEAD_DIM, gw),
                     blkb(IDX_DIM, N_IDX_HEADS * BLK), blkb(N_IDX_HEADS, BLK),
                     pl.BlockSpec((N_KV_HEADS, tm, HEAD_DIM), lambda i: (0, i, 0)),
                     blkb(N_KV_HEADS, VT_ROWS, BLK),
                     rowb(IDX_DIM)]
        out_shape = [sds((rows, kd), F32), sds((rows, kd), F32), sds((rows, LANES), F32),
                     sds((nbk, N_KV_HEADS, HEAD_DIM, gw), BF16),
                     sds((nbk, IDX_DIM, N_IDX_HEADS * BLK), BF16),
                     sds((nbk, N_IDX_HEADS, BLK), F32),
                     sds((N_KV_HEADS, rows, HEAD_DIM), BF16),
                     sds((nbk, N_KV_HEADS, VT_ROWS, BLK), BF16),
                     sds((rows, IDX_DIM), BF16)]
    else:
        kern = _attn_in_sample_kernel
        out_specs = [rowb(qd), rowb(kd), rowb(kd), rowb(qid), rowb(LANES), rowb(LANES)]
        out_shape = [sds((rows, qd), BF16), sds((rows, kd), F32), sds((rows, kd), F32),
                     sds((rows, qid), BF16), sds((rows, LANES), F32), sds((rows, LANES), F32)]
    return pl.pallas_call(
        kern,
        grid=(rows // tm,),
        in_specs=[rowb(d), const((1, d)), const((d, n)), const((1, qd)), const((1, kd)),
                  const((1, LANES)), const((qd, LANES)), const((LANES, qd)),
                  const((kd, LANES)), const((LANES, kd))],
        out_specs=out_specs,
        out_shape=out_shape,
        compiler_params=_cparams(("parallel",)),
        name="attn_in_prompt" if prompt else "attn_in_sample",
    )(x, g.reshape(1, d), w_pad,
      jnp.tile(q_gain, N_HEADS).reshape(1, qd), jnp.tile(k_gain, N_KV_HEADS).reshape(1, kd),
      jnp.pad(ki_gain, (0, LANES - IDX_DIM)).reshape(1, LANES), gq, eq, gk, ek)


def _norm_proj_kernel(x_ref, g_ref, w_ref, ws_ref, oa_ref, ob_ref, os_ref, xn_ref, *, na):
    j = pl.program_id(1)

    @pl.when(j == 0)
    def _():
        xn = _rms(x_ref[...], g_ref[...]).astype(BF16)
        xn_ref[...] = xn
        os_ref[...] = _dot(xn, ws_ref[...])

    r = _dot(xn_ref[...], w_ref[...])

    @pl.when(j < na)
    def _():
        oa_ref[...] = r

    @pl.when(j >= na)
    def _():
        ob_ref[...] = r


def _norm_proj(x, g, w_main, w_side, n_a):
    rows, d = x.shape
    n = w_main.shape[1]
    tm = _row_tile(rows, 1024)
    tn = math.gcd(math.gcd(n_a, n - n_a), 1024)
    na = n_a // tn
    return pl.pallas_call(
        functools.partial(_norm_proj_kernel, na=na),
        grid=(rows // tm, n // tn),
        in_specs=[
            pl.BlockSpec((tm, d), lambda i, j: (i, 0)),
            pl.BlockSpec((1, d), lambda i, j: (0, 0)),
            pl.BlockSpec((d, tn), lambda i, j: (0, j)),
            pl.BlockSpec((d, LANES), lambda i, j: (0, 0)),
        ],
        out_specs=[pl.BlockSpec((tm, tn), lambda i, j: (i, jnp.minimum(j, na - 1))),
                   pl.BlockSpec((tm, tn), lambda i, j: (i, jnp.maximum(j - na, 0))),
                   pl.BlockSpec((tm, LANES), lambda i, j: (i, 0))],
        out_shape=[jax.ShapeDtypeStruct((rows, n_a), F32),
                   jax.ShapeDtypeStruct((rows, n - n_a), F32),
                   jax.ShapeDtypeStruct((rows, LANES), F32)],
        scratch_shapes=[pltpu.VMEM((tm, d), BF16)],
        compiler_params=_cparams(("parallel", "arbitrary")),
        name="norm_proj",
    )(x, g.reshape(1, d), w_main, w_side)


def _out_proj_kernel(x_ref, y_ref, w_ref, o_ref):
    o_ref[...] = x_ref[...] + _dot(y_ref[...], w_ref[...])


def _out_proj(x, y, w):
    rows, d = x.shape
    k = y.shape[1]
    tm = _row_tile(rows, 512)
    return pl.pallas_call(
        _out_proj_kernel,
        grid=(rows // tm,),
        in_specs=[pl.BlockSpec((tm, d), lambda i: (i, 0)),
                  pl.BlockSpec((tm, k), lambda i: (i, 0)),
                  pl.BlockSpec((k, d), lambda i: (0, 0))],
        out_specs=pl.BlockSpec((tm, d), lambda i: (i, 0)),
        out_shape=jax.ShapeDtypeStruct((rows, d), F32),
        compiler_params=_cparams(("parallel",)),
        name="out_proj",
    )(x, y, w)


def _sort_key(score):
    bits = pltpu.bitcast(score, I32)
    return bits ^ ((bits >> 31) & 0x7FFFFFFF)


def _tile_fold(x, op):
    r = x[0:SUBLANES]
    for t in range(1, x.shape[0] // SUBLANES):
        r = op(r, x[t * SUBLANES:(t + 1) * SUBLANES])
    return r


def _dsa_prompt_kernel(smax_ref, qit_ref, wt_ref, ki_ref, qt_ref, k_ref, vt_ref, o_ref,
                       key_ref, bias_ref, acc_ref, *, pad, topk):
    i = pl.program_id(1)
    nch = i + 1
    npair = (nch + 1) // 2
    last = ki_ref.shape[0] - 1
    row = lax.broadcasted_iota(I32, (BLK, BLK), 0)
    lane = lax.broadcasted_iota(I32, (BLK, BLK), 1)
    t_pos = i * BLK + lane

    def p1(j, carry):
        for u in range(2):
            c = 2 * j + u
            dots = _dot(ki_ref[jnp.minimum(c, last)], qit_ref[...])
            sc = jnp.zeros((BLK, BLK), F32)
            for h in range(N_IDX_HEADS):
                sc = sc + wt_ref[h:h + 1, :] * jnp.maximum(dots[:, h * BLK:(h + 1) * BLK], 0.0)
            s_pos = c * BLK + row
            valid = (s_pos <= t_pos) & (s_pos >= pad)
            key_ref[c] = jnp.where(valid, _sort_key(sc), INT_MIN)
        return carry

    lax.fori_loop(0, npair, p1, 0)

    def count(pred):
        def body(j, cnt):
            for u in range(2):
                c = 2 * j + u
                cnt = cnt + jnp.where(pred(key_ref[c], c * BLK + row), 1, 0)
            return cnt
        cnt = lax.fori_loop(0, npair, body, jnp.zeros((BLK, BLK), I32))
        return jnp.sum(cnt, axis=0, keepdims=True)

    thr = jnp.where(count(lambda k, s: k >= 0) >= topk, 0, INT_MIN).astype(I32)

    def bit_step(it, thr):
        cand = thr + (jnp.int32(1) << (30 - it))
        return jnp.where(count(lambda k, s: k >= cand) >= topk, cand, thr)

    thr = lax.fori_loop(0, 31, bit_step, thr)
    n_gt = count(lambda k, s: k > thr)
    n_ge = count(lambda k, s: k >= thr)
    need = topk - n_gt
    tied = (n_ge > topk) & (thr > INT_MIN)
    any_tied = jnp.max(tied.astype(I32))
    nbits = (key_ref.shape[0] * BLK).bit_length()

    def idx_search():
        def step(it, lo):
            cand = lo + (jnp.int32(1) << (nbits - 1 - it))
            below = count(lambda k, s: (k == thr) & (s < cand))
            return jnp.where(below < need, cand, lo)
        return lax.fori_loop(0, nbits, step, jnp.zeros((1, BLK), I32))

    jcut = lax.cond(any_tied > 0, idx_search,
                    lambda: jnp.full((1, BLK), 2 ** 30, I32))

    def p2(j, carry):
        for u in range(2):
            c = 2 * j + u
            k = key_ref[c]
            sel = (k > thr) | ((k == thr) & (c * BLK + row <= jcut))
            sel = sel & (k > INT_MIN)
            bias_ref[c] = jnp.where(sel, 0.0, NEG)
        return carry

    lax.fori_loop(0, npair, p2, 0)

    gw = HEADS_PER_KV * BLK

    def logits(c):
        bb = jnp.concatenate([bias_ref[c]] * HEADS_PER_KV, axis=1)
        cr = jnp.minimum(c, last)
        return [_dot(k_ref[g, cr], qt_ref[g]) + bb for g in range(N_KV_HEADS)]

    def attend(shift):
        acc_ref[...] = jnp.zeros_like(acc_ref)

        def body(j, carry):
            p = []
            for u in range(2):
                s = logits(2 * j + u)
                if shift is not None:
                    s = [s[g] - shift[g] for g in range(N_KV_HEADS)]
                p.append([jnp.exp(x).astype(BF16) for x in s])
            c1 = jnp.minimum(2 * j + 1, last)
            for g in range(N_KV_HEADS):
                vt = jnp.concatenate([vt_ref[2 * j, g], vt_ref[c1, g]], axis=1)
                pp = jnp.concatenate([p[0][g], p[1][g]], axis=0)
                acc_ref[g] += _dot(vt, pp)
            return carry

        lax.fori_loop(0, npair, body, 0)

    safe = smax_ref[0] <= SAFE_LOGIT

    @pl.when(safe)
    def _():
        attend(None)

    @pl.when(jnp.logical_not(safe))
    def _():
        def pa(j, m):
            for u in range(2):
                s = logits(2 * j + u)
                m = tuple(jnp.maximum(m[g], _tile_fold(s[g], jnp.maximum)) for g in range(N_KV_HEADS))
            return m

        m0 = tuple(jnp.full((SUBLANES, gw), NEG, F32) for _ in range(N_KV_HEADS))
        m = lax.fori_loop(0, npair, pa, m0)
        attend([jnp.max(x, axis=0, keepdims=True) for x in m])

    q_row = i * BLK + lax.broadcasted_iota(I32, (BLK, LANES), 0)
    for g in range(N_KV_HEADS):
        a = acc_ref[g]
        res = a[0:HEAD_DIM] * (1.0 / a[HEAD_DIM:HEAD_DIM + 1])
        for t in range(HEADS_PER_KV // 2):
            two = jnp.concatenate([res[:, (2 * t + u) * BLK:(2 * t + u + 1) * BLK] for u in range(2)],
                                  axis=0)
            two = jnp.where(q_row >= pad, two.T, 0.0)
            lo = (g * HEADS_PER_KV + 2 * t) * HEAD_DIM
            o_ref[:, lo:lo + LANES] = two.astype(BF16)


def _dsa_prompt(smax, qit, wt, kib, qt, kg, vt, pad, topk):
    bsz, nb = qit.shape[:2]
    t_pad = nb * BLK
    gw = HEADS_PER_KV * BLK
    kern = functools.partial(_dsa_prompt_kernel, pad=pad, topk=topk)
    return pl.pallas_call(
        kern,
        grid=(bsz, nb),
        in_specs=[
            pl.BlockSpec(memory_space=pltpu.SMEM),
            pl.BlockSpec((None, None, IDX_DIM, N_IDX_HEADS * BLK), lambda b, i: (b, i, 0, 0)),
            pl.BlockSpec((None, None, N_IDX_HEADS, BLK), lambda b, i: (b, i, 0, 0)),
            pl.BlockSpec((None, nb, BLK, IDX_DIM), lambda b, i: (b, 0, 0, 0)),
            pl.BlockSpec((None, None, N_KV_HEADS, HEAD_DIM, gw), lambda b, i: (b, i, 0, 0, 0)),
            pl.BlockSpec((N_KV_HEADS, None, nb, BLK, HEAD_DIM), lambda b, i: (0, b, 0, 0, 0)),
            pl.BlockSpec((None, nb, N_KV_HEADS, VT_ROWS, BLK), lambda b, i: (b, 0, 0, 0, 0)),
        ],
        out_specs=pl.BlockSpec((None, BLK, N_HEADS * HEAD_DIM), lambda b, i: (b, i, 0)),
        out_shape=jax.ShapeDtypeStruct((bsz, t_pad, N_HEADS * HEAD_DIM), BF16),
        scratch_shapes=[pltpu.VMEM((nb + 1, BLK, BLK), I32),
                        pltpu.VMEM((nb + 1, BLK, BLK), F32),
                        pltpu.VMEM((N_KV_HEADS, VT_ROWS, gw), F32)],
        compiler_params=_cparams(("parallel", "arbitrary")),
        name="dsa_prompt",
    )(smax, qit, wt, kib, qt, kg, vt)


def _smp_scores_kernel(pt_ref, qi_ref, w_ref, *refs, pg):
    page_refs, o_ref = refs[:pg], refs[pg]
    qi = qi_ref[...]
    w = w_ref[...]
    for p in range(pg):
        d = _dot_t1(qi, page_refs[p][...].astype(BF16))
        o_ref[p:p + 1, :] = jnp.sum(w * jnp.maximum(d, 0.0), axis=0, keepdims=True)


def _smp_scores(page_table, qi, wi, cache_kidx, pg):
    db, npg = page_table.shape
    kern = functools.partial(_smp_scores_kernel, pg=pg)
    page_spec = lambda p: pl.BlockSpec(
        (None, BLK, IDX_DIM), lambda b, j, pt: (pt[b * npg + j * pg + p], 0, 0))
    return pl.pallas_call(
        kern,
        grid_spec=pltpu.PrefetchScalarGridSpec(
            num_scalar_prefetch=1,
            grid=(db, npg // pg),
            in_specs=[pl.BlockSpec((None, N_IDX_HEADS, IDX_DIM), lambda b, j, pt: (b, 0, 0)),
                      pl.BlockSpec((None, N_IDX_HEADS, 1), lambda b, j, pt: (b, 0, 0))]
                     + [page_spec(p) for p in range(pg)],
            out_specs=pl.BlockSpec((None, pg, BLK), lambda b, j, pt: (b, j, 0)),
        ),
        out_shape=jax.ShapeDtypeStruct((db, npg, BLK), F32),
        compiler_params=_cparams(("parallel", "arbitrary")),
        name="sample_scores",
    )(page_table.reshape(-1), qi, wi, *([cache_kidx] * pg))


def _smp_select_kernel(sc_ref, qi_ref, w_ref, kin_ref, bias_ref, nb_ref, *, topk, past):
    npg = sc_ref.shape[0]
    key = _sort_key(sc_ref[...])
    qi = qi_ref[...].astype(F32)
    kn = kin_ref[...].astype(BF16).astype(F32)
    d = jnp.sum(qi * kn, axis=-1, keepdims=True)
    s_new = jnp.sum(w_ref[...] * jnp.maximum(d, 0.0), axis=0, keepdims=True)
    key_new = _sort_key(s_new)
    pos = (lax.broadcasted_iota(I32, (npg, BLK), 0) * BLK
           + lax.broadcasted_iota(I32, (npg, BLK), 1))

    def count(pred_past, pred_new):
        c = jnp.sum(jnp.where(pred_past, 1, 0), axis=1, keepdims=True)
        return jnp.sum(c, axis=0, keepdims=True) + jnp.where(pred_new, 1, 0)

    thr = jnp.where(count(key >= 0, key_new >= 0) >= topk, 0, INT_MIN).astype(I32)

    def bit_step(it, thr):
        cand = thr + (jnp.int32(1) << (30 - it))
        return jnp.where(count(key >= cand, key_new >= cand) >= topk, cand, thr)

    thr = lax.fori_loop(0, 31, bit_step, thr)
    need = topk - count(key > thr, key_new > thr)
    nbits = max(1, past.bit_length())

    def step(it, lo):
        cand = lo + (jnp.int32(1) << (nbits - 1 - it))
        below = count((key == thr) & (pos < cand), (key_new == thr) & (past < cand))
        return jnp.where(below < need, cand, lo)

    jcut = lax.fori_loop(0, nbits, step, jnp.zeros((1, 1), I32))
    sel = (key > thr) | ((key == thr) & (pos <= jcut))
    bias_ref[...] = jnp.where(sel, 0.0, NEG)
    sel_new = (key_new > thr) | ((key_new == thr) & (past <= jcut))
    nb_ref[...] = jnp.broadcast_to(jnp.where(sel_new, 0.0, NEG), nb_ref.shape)


def _smp_select(scores, qi, wi, ki_new, topk):
    db, npg, _ = scores.shape
    kern = functools.partial(_smp_select_kernel, topk=topk, past=npg * BLK)
    return pl.pallas_call(
        kern,
        grid=(db,),
        in_specs=[pl.BlockSpec((None, npg, BLK), lambda b: (b, 0, 0)),
                  pl.BlockSpec((None, N_IDX_HEADS, IDX_DIM), lambda b: (b, 0, 0)),
                  pl.BlockSpec((None, N_IDX_HEADS, 1), lambda b: (b, 0, 0)),
                  pl.BlockSpec((None, 1, IDX_DIM), lambda b: (b, 0, 0))],
        out_specs=[pl.BlockSpec((None, npg, BLK), lambda b: (b, 0, 0)),
                   pl.BlockSpec((None, SUBLANES, LANES), lambda b: (b, 0, 0))],
        out_shape=[jax.ShapeDtypeStruct((db, npg, BLK), F32),
                   jax.ShapeDtypeStruct((db, SUBLANES, LANES), F32)],
        compiler_params=_cparams(("parallel",)),
        name="sample_select",
    )(scores, qi, wi, ki_new)


def _smp_attend_kernel(pt_ref, q_ref, bias_ref, nb_ref, kn_ref, vn_ref, *refs, pg):
    k_refs, v_refs = refs[:pg], refs[pg:2 * pg]
    o_ref, m_ref, l_ref, acc_ref = refs[2 * pg:]
    j = pl.program_id(1)
    kd = N_KV_HEADS * HEAD_DIM

    @pl.when(j == 0)
    def _():
        m_ref[...] = jnp.full_like(m_ref, NEG)
        l_ref[...] = jnp.zeros_like(l_ref)
        acc_ref[...] = jnp.zeros_like(acc_ref)

    q = q_ref[...]
    s = [_dot_t1(q, k_refs[p][...].astype(BF16)) + bias_ref[p:p + 1, :] for p in range(pg)]
    m_old = m_ref[...]
    m_new = m_old
    for p in range(pg):
        m_new = jnp.maximum(m_new, jnp.max(s[p], axis=-1, keepdims=True))
    alpha = jnp.exp(m_old - m_new)
    l = l_ref[...] * alpha
    acc = acc_ref[...] * alpha
    for p in range(pg):
        e = jnp.exp(s[p] - m_new)
        l = l + jnp.sum(e, axis=-1, keepdims=True)
        acc = acc + _dot(e.astype(BF16), v_refs[p][...].astype(BF16))
    m_ref[...] = m_new
    l_ref[...] = l
    acc_ref[...] = acc

    @pl.when(j == pl.num_programs(1) - 1)
    def _():
        qf = q.astype(F32)
        kn = kn_ref[...].astype(BF16).astype(F32)
        vn = vn_ref[...].astype(BF16).astype(F32)
        s_new = jnp.sum(qf * kn, axis=-1, keepdims=True) + nb_ref[0:1, 0:1]
        m_fin = jnp.maximum(m_new, s_new)
        a2 = jnp.exp(m_new - m_fin)
        e_new = jnp.exp(s_new - m_fin)
        l_fin = l * a2 + e_new
        acc_fin = acc * a2 + e_new.astype(BF16).astype(F32) * vn
        res = acc_fin / l_fin
        hgrp = lax.broadcasted_iota(I32, (N_HEADS, HEAD_DIM), 0) // HEADS_PER_KV
        out = jnp.zeros((N_HEADS, HEAD_DIM), F32)
        for g in range(N_KV_HEADS):
            out = out + jnp.where(hgrp == g, res[:, g * HEAD_DIM:(g + 1) * HEAD_DIM], 0.0)
        o_ref[...] = out


def _smp_attend(page_table, q_bd, bias, nbias, k_new, v_new, cache_k, cache_v, pg):
    db, npg = page_table.shape
    kd = N_KV_HEADS * HEAD_DIM
    kern = functools.partial(_smp_attend_kernel, pg=pg)
    page_spec = lambda p: pl.BlockSpec(
        (None, BLK, kd), lambda b, j, pt: (pt[b * npg + j * pg + p], 0, 0))
    per_b = lambda shape: pl.BlockSpec((None,) + shape, lambda b, j, pt: (b, 0, 0))
    return pl.pallas_call(
        kern,
        grid_spec=pltpu.PrefetchScalarGridSpec(
            num_scalar_prefetch=1,
            grid=(db, npg // pg),
            in_specs=[per_b((N_HEADS, kd)),
                      pl.BlockSpec((None, pg, BLK), lambda b, j, pt: (b, j, 0)),
                      per_b((SUBLANES, LANES)), per_b((1, kd)), per_b((1, kd))]
                     + [page_spec(p) for p in range(pg)] * 2,
            out_specs=per_b((N_HEADS, HEAD_DIM)),
            scratch_shapes=[pltpu.VMEM((N_HEADS, 1), F32), pltpu.VMEM((N_HEADS, 1), F32),
                            pltpu.VMEM((N_HEADS, kd), F32)],
        ),
        out_shape=jax.ShapeDtypeStruct((db, N_HEADS, HEAD_DIM), F32),
        compiler_params=_cparams(("parallel", "arbitrary")),
        name="sample_attend",
    )(page_table.reshape(-1), q_bd, bias, nbias, k_new, v_new,
      *([cache_k] * pg), *([cache_v] * pg))


def _ssd_prompt_kernel(z_ref, xbc_ref, dt_ref, dtt_ref, cw_ref, cb_ref, dtb_ref, a_ref,
                       dtbc_ref, ac_ref, dsk_ref, gn_ref, ltri_ref, utri_ref,
                       y_ref, st_ref, xpad_ref, h_ref, yb_ref, *, pad, n_heads):
    c = pl.program_id(1)
    d_in = n_heads * SSD_HEAD_DIM
    gn = SSD_GROUPS * D_STATE
    hpg = n_heads // SSD_GROUPS

    @pl.when(c == 0)
    def _():
        xpad_ref[0:SUBLANES, :] = jnp.zeros((SUBLANES, xpad_ref.shape[1]), F32)
        h_ref[...] = jnp.zeros_like(h_ref)

    xpad_ref[SUBLANES:, :] = xbc_ref[...]
    conv = cb_ref[...] + cw_ref[CONV_W - 1:CONV_W, :] * xbc_ref[...]
    for j in range(CONV_W - 1):
        sh = CONV_W - 1 - j
        conv = conv + cw_ref[j:j + 1, :] * xpad_ref[SUBLANES - sh:SUBLANES - sh + BLK, :]
    xpad_ref[0:SUBLANES, :] = xbc_ref[BLK - SUBLANES:BLK, :]
    xbc = _silu(conv)

    live = (c > 0) | (lax.broadcasted_iota(I32, (BLK, LANES), 0) >= pad)
    dt = jnp.where(live, _softplus(dt_ref[...] + dtb_ref[...]), 0.0)
    acum = _dot_sel(dt * a_ref[...], ltri_ref[...], fn=lambda x, m: _dot(m, x))
    live_t = (c > 0) | (lax.broadcasted_iota(I32, (dtt_ref.shape[0], BLK), 1) >= pad)
    dtt = jnp.where(live_t, _softplus(dtt_ref[...] + dtbc_ref[...]), 0.0)
    acum_t = _dot_sel(dtt * ac_ref[...], utri_ref[...])
    causal = (lax.broadcasted_iota(I32, (BLK, BLK), 0) >= lax.broadcasted_iota(I32, (BLK, BLK), 1))

    for g in range(SSD_GROUPS):
        bm = xbc[:, d_in + g * D_STATE:d_in + (g + 1) * D_STATE].astype(BF16)
        cm = xbc[:, d_in + gn + g * D_STATE:d_in + gn + (g + 1) * D_STATE].astype(BF16)
        cb = _dot_t1(cm, bm)
        for hh in range(hpg):
            h = g * hpg + hh
            col = acum[:, h:h + 1]
            rowv = acum_t[h:h + 1, :]
            a_last = acum_t[h:h + 1, BLK - 1:BLK]
            decay = jnp.exp(jnp.where(causal, col - rowv, NEG))
            xh = xbc[:, h * SSD_HEAD_DIM:(h + 1) * SSD_HEAD_DIM]
            xdt = xh * dt[:, h:h + 1]
            hprev = h_ref[h]
            y = _dot((cb * decay).astype(BF16), xdt.astype(BF16))
            y = y + jnp.exp(col) * _dot_t1(cm, hprev.astype(BF16))
            y = y + dsk_ref[0:1, h:h + 1] * xh
            yb_ref[:, h * SSD_HEAD_DIM:(h + 1) * SSD_HEAD_DIM] = y
            w_s = (xdt * jnp.exp(a_last - col)).astype(BF16)
            h_ref[h] = jnp.exp(a_last) * hprev + _dot_t0(w_s, bm)

    yg = yb_ref[...] * _silu(z_ref[...])
    gsz = d_in // SSD_GROUPS
    for g in range(SSD_GROUPS):
        blk = yg[:, g * gsz:(g + 1) * gsz]
        y_ref[:, g * gsz:(g + 1) * gsz] = _rms(blk, gn_ref[:, g * gsz:(g + 1) * gsz]).astype(BF16)

    @pl.when(c == pl.num_programs(1) - 1)
    def _():
        st_ref[...] = h_ref[...]


def _ssd_prompt(z_xbc, dt_raw, dt_raw_t, conv_w, conv_b, dt_bias, a_log, d_skip, gate_norm,
                bsz, nb, pad, n_heads):
    d_in = n_heads * SSD_HEAD_DIM
    cdim = d_in + 2 * SSD_GROUPS * D_STATE
    assert n_heads <= LANES
    hp = LANES - n_heads
    a = -jnp.exp(a_log.astype(F32))
    ltri = jnp.tril(jnp.ones((BLK, BLK), F32)).astype(BF16)
    kern = functools.partial(_ssd_prompt_kernel, pad=pad, n_heads=n_heads)
    const = lambda shape: pl.BlockSpec(shape, lambda b, c: (0,) * len(shape))
    zb = d_in // cdim if d_in % cdim == 0 else None
    return pl.pallas_call(
        kern,
        grid=(bsz, nb),
        in_specs=[
            pl.BlockSpec((BLK, d_in), lambda b, c: (b * nb + c, 0)),
            pl.BlockSpec((BLK, cdim), lambda b, c: (b * nb + c, 0)),
            pl.BlockSpec((BLK, LANES), lambda b, c: (b * nb + c, 0)),
            pl.BlockSpec((None, None, n_heads, BLK), lambda b, c: (b, c, 0, 0)),
            const((CONV_W, cdim)), const((1, cdim)), const((1, LANES)), const((1, LANES)),
            const((n_heads, 1)), const((n_heads, 1)), const((1, LANES)), const((1, d_in)),
            const((BLK, BLK)), const((BLK, BLK)),
        ],
        out_specs=[pl.BlockSpec((BLK, d_in), lambda b, c: (b * nb + c, 0)),
                   pl.BlockSpec((None, n_heads, SSD_HEAD_DIM, D_STATE), lambda b, c: (b, 0, 0, 0))],
        out_shape=[jax.ShapeDtypeStruct((bsz * nb * BLK, d_in), BF16),
                   jax.ShapeDtypeStruct((bsz, n_heads, SSD_HEAD_DIM, D_STATE), F32)],
        scratch_shapes=[pltpu.VMEM((BLK + SUBLANES, cdim), F32),
                        pltpu.VMEM((n_heads, SSD_HEAD_DIM, D_STATE), F32),
                        pltpu.VMEM((BLK, d_in), F32)],
        compiler_params=_cparams(("parallel", "arbitrary")),
        name="ssd_prompt",
    )(z_xbc[0], z_xbc[1], dt_raw, dt_raw_t, conv_w, conv_b.reshape(1, cdim),
      jnp.pad(dt_bias, (0, hp)).reshape(1, LANES), jnp.pad(a, (0, hp)).reshape(1, LANES),
      dt_bias.reshape(n_heads, 1), a.reshape(n_heads, 1),
      jnp.pad(d_skip, (0, hp)).reshape(1, LANES), gate_norm.reshape(1, d_in), ltri, ltri.T)


def _ssd_sample_kernel(z_ref, xbc_ref, dt_ref, cst_ref, h0_ref, cw_ref, cb_ref, dtb_ref, a_ref,
                       dsk_ref, gn_ref, exp_ref, y_ref, h_ref, *, n_heads):
    d_in = n_heads * SSD_HEAD_DIM
    gn = SSD_GROUPS * D_STATE
    gsz = d_in // SSD_GROUPS
    hpg = n_heads // SSD_GROUPS
    conv = cb_ref[...] + cw_ref[CONV_W - 1:CONV_W, :] * xbc_ref[...]
    for j in range(CONV_W - 1):
        conv = conv + cw_ref[j:j + 1, :] * cst_ref[j:j + 1, :]
    xbc = _silu(conv)
    dt = _softplus(dt_ref[...] + dtb_ref[...])
    pad8 = lambda r: jnp.concatenate([r, jnp.zeros((SUBLANES - 1, r.shape[1]), F32)], axis=0)
    dt_ch = _dot_sel(pad8(dt), exp_ref[...])[0:1]
    da_ch = jnp.exp(_dot_sel(pad8(dt * a_ref[...]), exp_ref[...])[0:1])
    dsk_ch = _dot_sel(pad8(dsk_ref[...]), exp_ref[...])[0:1]
    xh = xbc[:, :d_in]
    xdt = xh * dt_ch
    ones = jnp.ones((SUBLANES, D_STATE), BF16)
    outs = []
    for g in range(SSD_GROUPS):
        sl = slice(g * gsz, (g + 1) * gsz)
        bm = xbc[:, d_in + g * D_STATE:d_in + (g + 1) * D_STATE]
        cm = xbc[:, d_in + gn + g * D_STATE:d_in + gn + (g + 1) * D_STATE]
        da_col = _dot_sel(pad8(da_ch[:, sl]), ones, fn=_dot_t0)
        xdt_col = _dot_sel(pad8(xdt[:, sl]), ones, fn=_dot_t0)
        h0 = h0_ref[g * hpg:(g + 1) * hpg].reshape(gsz, D_STATE)
        hn = da_col * h0 + xdt_col.astype(BF16).astype(F32) * bm.astype(BF16).astype(F32)
        h_ref[g * hpg:(g + 1) * hpg] = hn.reshape(hpg, SSD_HEAD_DIM, D_STATE)
        y_col = jnp.sum(hn * cm, axis=-1, keepdims=True)
        outs.append(y_col)
    y_cols = jnp.concatenate(outs, axis=0)
    rows = []
    eye = (lax.broadcasted_iota(I32, (LANES, LANES), 0)
           == lax.broadcasted_iota(I32, (LANES, LANES), 1)).astype(F32)
    for t in range(d_in // LANES):
        blk = y_cols[t * LANES:(t + 1) * LANES]
        rows.append(jnp.sum(blk * eye, axis=0, keepdims=True))
    y = jnp.concatenate(rows, axis=1) + dsk_ch * xh
    y = y * _silu(z_ref[...])
    for g in range(SSD_GROUPS):
        sl = slice(g * gsz, (g + 1) * gsz)
        y_ref[:, sl] = _rms(y[:, sl], gn_ref[:, sl]).astype(BF16)


def _ssd_sample(z, xbc_raw, dt_raw, conv_state, h0, conv_w, conv_b, dt_bias, a_log, d_skip,
                gate_norm, n_heads):
    db = z.shape[0]
    d_in = n_heads * SSD_HEAD_DIM
    cdim = d_in + 2 * SSD_GROUPS * D_STATE
    hp = LANES - n_heads
    a = -jnp.exp(a_log.astype(F32))
    expand = (jnp.arange(LANES)[:, None] == (jnp.arange(d_in) // SSD_HEAD_DIM)[None, :]).astype(BF16)
    kern = functools.partial(_ssd_sample_kernel, n_heads=n_heads)
    const = lambda shape: pl.BlockSpec(shape, lambda b: (0,) * len(shape))
    per_b = lambda shape: pl.BlockSpec((None,) + shape, lambda b: (b,) + (0,) * len(shape))
    y, h = pl.pallas_call(
        kern,
        grid=(db,),
        in_specs=[per_b((1, d_in)), per_b((1, cdim)), per_b((1, LANES)), per_b((CONV_W - 1, cdim)),
                  per_b((n_heads, SSD_HEAD_DIM, D_STATE)),
                  const((CONV_W, cdim)), const((1, cdim)), const((1, LANES)), const((1, LANES)),
                  const((1, LANES)), const((1, d_in)), const((LANES, d_in))],
        out_specs=[per_b((1, d_in)), per_b((n_heads, SSD_HEAD_DIM, D_STATE))],
        out_shape=[jax.ShapeDtypeStruct((db, 1, d_in), BF16),
                   jax.ShapeDtypeStruct((db, n_heads, SSD_HEAD_DIM, D_STATE), F32)],
        compiler_params=_cparams(("parallel",)),
        name="ssd_sample",
    )(z.reshape(db, 1, d_in), xbc_raw.reshape(db, 1, cdim), dt_raw.reshape(db, 1, LANES),
      conv_state, h0, conv_w, conv_b.reshape(1, cdim),
      jnp.pad(dt_bias, (0, hp)).reshape(1, LANES), jnp.pad(a, (0, hp)).reshape(1, LANES),
      jnp.pad(d_skip, (0, hp)).reshape(1, LANES), gate_norm.reshape(1, d_in), expand)
    return y.reshape(db, d_in), h


def kernel(x_prompt, x_sample, cache_k, cache_v, cache_kidx, page_table, state_ssm, state_conv,
           meta_tokens, norm_ffn_a, w_ffn_a_in, w_ffn_a_out, norm_mix, norm_ffn_b, w_ffn_b_in,
           w_ffn_b_out, w_attn_in, q_norm, k_norm, kidx_norm, w_attn_out,
           w_ssd_in, conv_w, conv_b, dt_bias, a_log, d_skip, gate_norm, w_ssd_out):
    bsz, seq, d = x_prompt.shape
    db = x_sample.shape[0]
    assert x_sample.shape[1] == 1
    t_real = N_META + seq
    nb = -(-t_real // BLK)
    t_pad = nb * BLK
    pad = t_pad - t_real
    npg = page_table.shape[1]
    past = npg * BLK
    topk_p = min(TOPK_MAX, seq // 4)
    topk_s = min(TOPK_MAX, (past + 1) // 4)
    qd, kd, qid = N_HEADS * HEAD_DIM, N_KV_HEADS * HEAD_DIM, N_IDX_HEADS * IDX_DIM
    d_in = w_ssd_out.shape[1]
    n_heads = d_in // SSD_HEAD_DIM
    cdim = d_in + 2 * SSD_GROUPS * D_STATE
    depth = norm_mix.shape[0]

    meta = jnp.broadcast_to(meta_tokens.astype(F32)[None], (bsz, N_META, d))
    xp = jnp.concatenate([jnp.zeros((bsz, pad, d), F32), meta, x_prompt], axis=1)
    xp = xp.reshape(bsz * t_pad, d)
    xs = x_sample.reshape(db, d)

    outs = {k: [] for k in ("kp", "vp", "kip", "ks", "vs", "kis", "hp", "cp", "hs", "cs")}
    for i in range(depth):
        wa_in, wa_out = w_ffn_a_in[i].astype(BF16), w_ffn_a_out[i].astype(BF16)
        xp = _ffn(xp, norm_ffn_a[i], wa_in, wa_out)
        xs = _ffn(xs, norm_ffn_a[i], wa_in, wa_out)
        j = i // 2
        if i % 2 == 0:
            w = w_attn_in[j]
            o1, o2 = qd + 2 * kd + qid, qd + 2 * kd + qid + IDX_DIM
            w_pad = jnp.concatenate(
                [w[:, :o1], jnp.pad(w[:, o1:o2], ((0, 0), (0, LANES - IDX_DIM))),
                 jnp.pad(w[:, o2:], ((0, 0), (0, LANES - N_IDX_HEADS)))], axis=1).astype(BF16)
            w_out = w_attn_out[j].astype(BF16)

            k, v, ki, qt, qit, wt, kg, vt, kib = _attn_in(
                xp, norm_mix[i], w_pad, q_norm[j], k_norm[j], kidx_norm[j], prompt=True)
            gw = HEADS_PER_KV * BLK
            smax = (1.02 * math.sqrt(HEAD_DIM) * jnp.max(jnp.abs(q_norm[j]))
                    * jnp.max(jnp.abs(k_norm[j]))).reshape(1).astype(F32)
            o = _dsa_prompt(smax,
                            qit.reshape(bsz, nb, IDX_DIM, N_IDX_HEADS * BLK),
                            wt.reshape(bsz, nb, N_IDX_HEADS, BLK),
                            kib.reshape(bsz, nb, BLK, IDX_DIM),
                            qt.reshape(bsz, nb, N_KV_HEADS, HEAD_DIM, gw),
                            kg.reshape(N_KV_HEADS, bsz, nb, BLK, HEAD_DIM),
                            vt.reshape(bsz, nb, N_KV_HEADS, VT_ROWS, BLK),
                            pad, topk_p)
            xp = _out_proj(xp, o.reshape(bsz * t_pad, qd), w_out)
            outs["kp"].append(k.reshape(bsz, t_pad, N_KV_HEADS, HEAD_DIM)[:, pad:])
            outs["vp"].append(v.reshape(bsz, t_pad, N_KV_HEADS, HEAD_DIM)[:, pad:])
            outs["kip"].append(ki[:, :IDX_DIM].reshape(bsz, t_pad, IDX_DIM)[:, pad:])

            q, k, v, qi, ki, wi = _attn_in(
                xs, norm_mix[i], w_pad, q_norm[j], k_norm[j], kidx_norm[j], prompt=False)
            pg = math.gcd(npg, 16)
            qi3 = qi.reshape(db, N_IDX_HEADS, IDX_DIM)
            wi3 = wi[:, :N_IDX_HEADS].reshape(db, N_IDX_HEADS, 1)
            scores = _smp_scores(page_table, qi3, wi3, cache_kidx[j], pg)
            bias, nbias = _smp_select(scores, qi3, wi3, ki[:, :IDX_DIM].reshape(db, 1, IDX_DIM), topk_s)
            hsel = (jnp.arange(N_HEADS)[:, None] // HEADS_PER_KV == jnp.arange(N_KV_HEADS)[None, :])
            q_bd = (q.reshape(db, N_HEADS, 1, HEAD_DIM) * hsel[None, :, :, None].astype(BF16))
            q_bd = q_bd.reshape(db, N_HEADS, kd)
            npool = cache_k.shape[1]
            o = _smp_attend(page_table, q_bd, bias, nbias, k.reshape(db, 1, kd), v.reshape(db, 1, kd),
                            cache_k[j].reshape(npool, BLK, kd), cache_v[j].reshape(npool, BLK, kd), pg)
            xs = _out_proj(xs, o.reshape(db, qd).astype(BF16), w_out)
            outs["ks"].append(k.reshape(db, 1, N_KV_HEADS, HEAD_DIM))
            outs["vs"].append(v.reshape(db, 1, N_KV_HEADS, HEAD_DIM))
            outs["kis"].append(ki[:, :IDX_DIM].reshape(db, 1, IDX_DIM))
        else:
            w = w_ssd_in[j]
            w_main = w[:, :d_in + cdim].astype(BF16)
            w_dt = jnp.pad(w[:, d_in + cdim:], ((0, 0), (0, LANES - n_heads))).astype(BF16)
            w_out = w_ssd_out[j].astype(BF16)
            sp = (conv_w[j], conv_b[j], dt_bias[j], a_log[j], d_skip[j], gate_norm[j])

            z, xbc_raw, dt_raw = _norm_proj(xp, norm_mix[i], w_main, w_dt, d_in)
            dt_t = dt_raw[:, :n_heads].reshape(bsz, nb, BLK, n_heads).transpose(0, 1, 3, 2)
            y, hfin = _ssd_prompt((z, xbc_raw), dt_raw, dt_t, *sp, bsz, nb, pad, n_heads)
            xp = _out_proj(xp, y, w_out)
            outs["hp"].append(hfin)
            outs["cp"].append(xbc_raw.reshape(bsz, t_pad, cdim)[:, t_pad - (CONV_W - 1):])

            z, xbc_raw, dt_raw = _norm_proj(xs, norm_mix[i], w_main, w_dt, d_in)
            y, hnew = _ssd_sample(z, xbc_raw, dt_raw, state_conv[j], state_ssm[j], *sp, n_heads)
            xs = _out_proj(xs, y, w_out)
            outs["hs"].append(hnew)
            outs["cs"].append(jnp.concatenate([state_conv[j][:, 1:], xbc_raw[:, None, :]], axis=1))
        wb_in, wb_out = w_ffn_b_in[i].astype(BF16), w_ffn_b_out[i].astype(BF16)
        xp = _ffn(xp, norm_ffn_b[i], wb_in, wb_out)
        xs = _ffn(xs, norm_ffn_b[i], wb_in, wb_out)

    y_prompt = xp.reshape(bsz, t_pad, d)[:, pad + N_META:]
    y_sample = xs.reshape(db, 1, d)
    st = lambda key: jnp.stack(outs[key])
    return (y_prompt, y_sample, st("kp"), st("vp"), st("kip"), st("ks"), st("vs"), st("kis"),
            st("hp"), st("cp"), st("hs"), st("cs"))
```

```python
import functools
import math

import jax
import jax.numpy as jnp
from jax import lax
from jax.experimental import pallas as pl
from jax.experimental.pallas import tpu as pltpu

F32 = jnp.float32
BF16 = jnp.bfloat16
I32 = jnp.int32

N_META = 16
N_HEADS = 16
HEAD_DIM = 64
N_KV_HEADS = 4
HEADS_PER_KV = N_HEADS // N_KV_HEADS
N_IDX_HEADS = 8
IDX_DIM = 64
TOPK_MAX = 256
SSD_HEAD_DIM = 64
SSD_GROUPS = 4
D_STATE = 128
CONV_W = 4
EPS = 1e-6

LANES = 128
SUBLANES = 8
BLK = 128
NEG = -1e30
SAFE_LOGIT = 40.0
VT_ROWS = 80
INT_MIN = -2 ** 31
VMEM_LIMIT = 56 * 1024 * 1024


def _cparams(sem, vmem=VMEM_LIMIT):
    return pltpu.CompilerParams(dimension_semantics=sem, vmem_limit_bytes=vmem)


def _row_tile(rows, pref):
    best = None
    for d in range(SUBLANES, min(rows, pref) + 1, SUBLANES):
        if rows % d == 0:
            best = d
    assert best is not None, rows
    return best


def _rms(x, g):
    var = jnp.mean(x * x, axis=-1, keepdims=True)
    return x * lax.rsqrt(var + EPS) * g


def _dot(a, b):
    return jnp.dot(a, b, preferred_element_type=F32)


def _dot_t0(a, b):
    return lax.dot_general(a, b, (((0,), (0,)), ((), ())), preferred_element_type=F32)


def _dot_t1(a, b):
    return lax.dot_general(a, b, (((1,), (1,)), ((), ())), preferred_element_type=F32)


def _split2(a):
    hi = a.astype(BF16)
    lo = (a - hi.astype(F32)).astype(BF16)
    return hi, lo


def _split3(a):
    a0 = a.astype(BF16)
    r = a - a0.astype(F32)
    a1 = r.astype(BF16)
    a2 = (r - a1.astype(F32)).astype(BF16)
    return a0, a1, a2


def _dot_sel(a, m, fn=_dot):
    a0, a1, a2 = _split3(a)
    return fn(a0, m) + fn(a1, m) + fn(a2, m)


def _silu(x):
    return x * (1.0 / (1.0 + jnp.exp(-x)))


def _softplus(x):
    return jnp.maximum(x, 0.0) + jnp.log(1.0 + jnp.exp(-jnp.abs(x)))


def _ffn_kernel(x_ref, g_ref, wa_ref, wb_ref, wo_ref, o_ref, xn_ref, acc_ref):
    j = pl.program_id(1)

    @pl.when(j == 0)
    def _():
        xn_ref[...] = _rms(x_ref[...], g_ref[...]).astype(BF16)
        acc_ref[...] = jnp.zeros_like(acc_ref)

    xn = xn_ref[...]
    a = _dot(xn, wa_ref[...])
    b = _dot(xn, wb_ref[...])
    h = (_silu(a) * b).astype(BF16)
    acc_ref[...] += _dot(h, wo_ref[...])

    @pl.when(j == pl.num_programs(1) - 1)
    def _():
        o_ref[...] = x_ref[...] + 0.5 * acc_ref[...]


def _ffn(x, g, w_in, w_out):
    rows, d = x.shape
    hid = w_out.shape[0]
    tm = _row_tile(rows, 1024)
    th = 256 if hid % 256 == 0 else LANES
    nh = hid // th
    return pl.pallas_call(
        _ffn_kernel,
        grid=(rows // tm, nh),
        in_specs=[
            pl.BlockSpec((tm, d), lambda i, j: (i, 0)),
            pl.BlockSpec((1, d), lambda i, j: (0, 0)),
            pl.BlockSpec((d, th), lambda i, j: (0, j)),
            pl.BlockSpec((d, th), lambda i, j: (0, j + nh)),
            pl.BlockSpec((th, d), lambda i, j: (j, 0)),
        ],
        out_specs=pl.BlockSpec((tm, d), lambda i, j: (i, 0)),
        out_shape=jax.ShapeDtypeStruct((rows, d), F32),
        scratch_shapes=[pltpu.VMEM((tm, d), BF16), pltpu.VMEM((tm, d), F32)],
        compiler_params=_cparams(("parallel", "arbitrary")),
        name="ffn",
    )(x, g.reshape(1, d), w_in, w_in, w_out)


def _attn_project(x_ref, g_ref, w_ref, qg_ref, kg_ref, kig_ref, gq_ref, eq_ref, gk_ref, ek_ref):
    qd = N_HEADS * HEAD_DIM
    kd = N_KV_HEADS * HEAD_DIM
    qid = N_IDX_HEADS * IDX_DIM
    xn = _rms(x_ref[...], g_ref[...]).astype(BF16)
    h = _dot(xn, w_ref[...])

    def head_norm(t, gsum_ref, gexp_ref, gain):
        ss = _dot_sel(t * t, gsum_ref[...])
        rs = lax.rsqrt(ss * (1.0 / HEAD_DIM) + EPS)
        return t * _dot_sel(rs, gexp_ref[...]) * gain

    o = 0
    q = head_norm(h[:, o:o + qd], gq_ref, eq_ref, qg_ref[...]) * (HEAD_DIM ** -0.5)
    o += qd
    k = head_norm(h[:, o:o + kd], gk_ref, ek_ref, kg_ref[...])
    o += kd
    v = h[:, o:o + kd]
    o += kd
    qi = h[:, o:o + qid] * (IDX_DIM ** -0.5)
    o += qid
    ki = h[:, o:o + LANES]
    var = jnp.sum(ki * ki, axis=-1, keepdims=True) * (1.0 / IDX_DIM)
    ki = ki * lax.rsqrt(var + EPS) * kig_ref[...]
    o += LANES
    wi = h[:, o:o + LANES] * (N_IDX_HEADS ** -0.5)
    return q, k, v, qi, ki, wi


def _attn_in_sample_kernel(*refs):
    q_o, k_o, v_o, qi_o, ki_o, wi_o = refs[10:]
    q, k, v, qi, ki, wi = _attn_project(*refs[:10])
    q_o[...] = q.astype(BF16)
    k_o[...] = k
    v_o[...] = v
    qi_o[...] = qi.astype(BF16)
    ki_o[...] = ki
    wi_o[...] = wi


def _attn_in_prompt_kernel(*refs):
    k_o, v_o, ki_o, qt_o, qit_o, wt_o, kg_o, vt_o, kib_o = refs[10:]
    q, k, v, qi, ki, wi = _attn_project(*refs[:10])
    k_o[...] = k
    v_o[...] = v
    ki_o[...] = ki
    kib_o[...] = ki[:, :IDX_DIM].astype(BF16)
    for g in range(N_KV_HEADS):
        kg_o[g] = k[:, g * HEAD_DIM:(g + 1) * HEAD_DIM].astype(BF16)
    tail = (lax.broadcasted_iota(I32, (VT_ROWS - HEAD_DIM, BLK), 0) == 0).astype(BF16)
    for r in range(q.shape[0] // BLK):
        rows = slice(r * BLK, (r + 1) * BLK)
        for t in range(N_HEADS * HEAD_DIM // LANES):
            tt = q[rows, t * LANES:(t + 1) * LANES].T.astype(BF16)
            for u in range(LANES // HEAD_DIM):
                h = t * (LANES // HEAD_DIM) + u
                g, hh = h // HEADS_PER_KV, h % HEADS_PER_KV
                qt_o[r, g, :, hh * BLK:(hh + 1) * BLK] = tt[u * HEAD_DIM:(u + 1) * HEAD_DIM]
        for t in range(N_IDX_HEADS * IDX_DIM // LANES):
            tt = qi[rows, t * LANES:(t + 1) * LANES].T.astype(BF16)
            for u in range(LANES // IDX_DIM):
                h = t * (LANES // IDX_DIM) + u
                qit_o[r, :, h * BLK:(h + 1) * BLK] = tt[u * IDX_DIM:(u + 1) * IDX_DIM]
        wt_o[r] = wi[rows].T[0:N_IDX_HEADS]
        for t in range(N_KV_HEADS * HEAD_DIM // LANES):
            tt = v[rows, t * LANES:(t + 1) * LANES].T.astype(BF16)
            for u in range(LANES // HEAD_DIM):
                vt_o[r, t * (LANES // HEAD_DIM) + u] = jnp.concatenate(
                    [tt[u * HEAD_DIM:(u + 1) * HEAD_DIM], tail], axis=0)


def _seg_mats(n_heads, hd):
    col = jnp.arange(n_heads * hd) // hd
    gsum = (col[:, None] == jnp.arange(LANES)[None, :]).astype(BF16)
    return gsum, gsum.T


def _attn_in(x, g, w_pad, q_gain, k_gain, ki_gain, prompt):
    rows, d = x.shape
    n = w_pad.shape[1]
    qd, kd, qid = N_HEADS * HEAD_DIM, N_KV_HEADS * HEAD_DIM, N_IDX_HEADS * IDX_DIM
    gw = HEADS_PER_KV * BLK
    tm = _row_tile(rows, 512)
    gq, eq = _seg_mats(N_HEADS, HEAD_DIM)
    gk, ek = _seg_mats(N_KV_HEADS, HEAD_DIM)
    const = lambda shape: pl.BlockSpec(shape, lambda i: (0,) * len(shape))
    rowb = lambda w: pl.BlockSpec((tm, w), lambda i: (i, 0))
    sds = jax.ShapeDtypeStruct
    if prompt:
        assert tm % BLK == 0
        nbk, tb = rows // BLK, tm // BLK
        blkb = lambda *s: pl.BlockSpec((tb,) + s, lambda i: (i,) + (0,) * len(s))
        kern = _attn_in_prompt_kernel
        out_specs = [rowb(kd), rowb(kd), rowb(LANES), blkb(N_KV_HEADS, HEAD_DIM, gw),
                     blkb(IDX_DIM, N_IDX_HEADS * BLK), blkb(N_IDX_HEADS, BLK),
                     pl.BlockSpec((N_KV_HEADS, tm, HEAD_DIM), lambda i: (0, i, 0)),
                     blkb(N_KV_HEADS, VT_ROWS, BLK),
                     rowb(IDX_DIM)]
        out_shape = [sds((rows, kd), F32), sds((rows, kd), F32), sds((rows, LANES), F32),
                     sds((nbk, N_KV_HEADS, HEAD_DIM, gw), BF16),
                     sds((nbk, IDX_DIM, N_IDX_HEADS * BLK), BF16),
                     sds((nbk, N_IDX_HEADS, BLK), F32),
                     sds((N_KV_HEADS, rows, HEAD_DIM), BF16),
                     sds((nbk, N_KV_HEADS, VT_ROWS, BLK), BF16),
                     sds((rows, IDX_DIM), BF16)]
    else:
        kern = _attn_in_sample_kernel
        out_specs = [rowb(qd), rowb(kd), rowb(kd), rowb(qid), rowb(LANES), rowb(LANES)]
        out_shape = [sds((rows, qd), BF16), sds((rows, kd), F32), sds((rows, kd), F32),
                     sds((rows, qid), BF16), sds((rows, LANES), F32), sds((rows, LANES), F32)]
    return pl.pallas_call(
        kern,
        grid=(rows // tm,),
        in_specs=[rowb(d), const((1, d)), const((d, n)), const((1, qd)), const((1, kd)),
                  const((1, LANES)), const((qd, LANES)), const((LANES, qd)),
                  const((kd, LANES)), const((LANES, kd))],
        out_specs=out_specs,
        out_shape=out_shape,
        compiler_params=_cparams(("parallel",)),
        name="attn_in_prompt" if prompt else "attn_in_sample",
    )(x, g.reshape(1, d), w_pad,
      jnp.tile(q_gain, N_HEADS).reshape(1, qd), jnp.tile(k_gain, N_KV_HEADS).reshape(1, kd),
      jnp.pad(ki_gain, (0, LANES - IDX_DIM)).reshape(1, LANES), gq, eq, gk, ek)


def _norm_proj_kernel(x_ref, g_ref, w_ref, ws_ref, oa_ref, ob_ref, os_ref, xn_ref, *, na):
    j = pl.program_id(1)

    @pl.when(j == 0)
    def _():
        xn = _rms(x_ref[...], g_ref[...]).astype(BF16)
        xn_ref[...] = xn
        os_ref[...] = _dot(xn, ws_ref[...])

    r = _dot(xn_ref[...], w_ref[...])

    @pl.when(j < na)
    def _():
        oa_ref[...] = r

    @pl.when(j >= na)
    def _():
        ob_ref[...] = r


def _norm_proj(x, g, w_main, w_side, n_a):
    rows, d = x.shape
    n = w_main.shape[1]
    tm = _row_tile(rows, 1024)
    tn = math.gcd(math.gcd(n_a, n - n_a), 1024)
    na = n_a // tn
    return pl.pallas_call(
        functools.partial(_norm_proj_kernel, na=na),
        grid=(rows // tm, n // tn),
        in_specs=[
            pl.BlockSpec((tm, d), lambda i, j: (i, 0)),
            pl.BlockSpec((1, d), lambda i, j: (0, 0)),
            pl.BlockSpec((d, tn), lambda i, j: (0, j)),
            pl.BlockSpec((d, LANES), lambda i, j: (0, 0)),
        ],
        out_specs=[pl.BlockSpec((tm, tn), lambda i, j: (i, jnp.minimum(j, na - 1))),
                   pl.BlockSpec((tm, tn), lambda i, j: (i, jnp.maximum(j - na, 0))),
                   pl.BlockSpec((tm, LANES), lambda i, j: (i, 0))],
        out_shape=[jax.ShapeDtypeStruct((rows, n_a), F32),
                   jax.ShapeDtypeStruct((rows, n - n_a), F32),
                   jax.ShapeDtypeStruct((rows, LANES), F32)],
        scratch_shapes=[pltpu.VMEM((tm, d), BF16)],
        compiler_params=_cparams(("parallel", "arbitrary")),
        name="norm_proj",
    )(x, g.reshape(1, d), w_main, w_side)


def _out_proj_kernel(x_ref, y_ref, w_ref, o_ref):
    o_ref[...] = x_ref[...] + _dot(y_ref[...], w_ref[...])


def _out_proj(x, y, w):
    rows, d = x.shape
    k = y.shape[1]
    tm = _row_tile(rows, 512)
    return pl.pallas_call(
        _out_proj_kernel,
        grid=(rows // tm,),
        in_specs=[pl.BlockSpec((tm, d), lambda i: (i, 0)),
                  pl.BlockSpec((tm, k), lambda i: (i, 0)),
                  pl.BlockSpec((k, d), lambda i: (0, 0))],
        out_specs=pl.BlockSpec((tm, d), lambda i: (i, 0)),
        out_shape=jax.ShapeDtypeStruct((rows, d), F32),
        compiler_params=_cparams(("parallel",)),
        name="out_proj",
    )(x, y, w)


def _sort_key(score):
    bits = pltpu.bitcast(score, I32)
    return bits ^ ((bits >> 31) & 0x7FFFFFFF)


def _tile_fold(x, op):
    r = x[0:SUBLANES]
    for t in range(1, x.shape[0] // SUBLANES):
        r = op(r, x[t * SUBLANES:(t + 1) * SUBLANES])
    return r


def _dsa_prompt_kernel(smax_ref, qit_ref, wt_ref, ki_ref, qt_ref, k_ref, vt_ref, o_ref,
                       key_ref, bias_ref, acc_ref, *, pad, topk):
    i = pl.program_id(1)
    nch = i + 1
    npair = (nch + 1) // 2
    last = ki_ref.shape[0] - 1
    row = lax.broadcasted_iota(I32, (BLK, BLK), 0)
    lane = lax.broadcasted_iota(I32, (BLK, BLK), 1)
    t_pos = i * BLK + lane

    def p1(j, carry):
        for u in range(2):
            c = 2 * j + u
            dots = _dot(ki_ref[jnp.minimum(c, last)], qit_ref[...])
            sc = jnp.zeros((BLK, BLK), F32)
            for h in range(N_IDX_HEADS):
                sc = sc + wt_ref[h:h + 1, :] * jnp.maximum(dots[:, h * BLK:(h + 1) * BLK], 0.0)
            s_pos = c * BLK + row
            valid = (s_pos <= t_pos) & (s_pos >= pad)
            key_ref[c] = jnp.where(valid, _sort_key(sc), INT_MIN)
        return carry

    lax.fori_loop(0, npair, p1, 0)

    def count(pred):
        def body(j, cnt):
            for u in range(2):
                c = 2 * j + u
                cnt = cnt + jnp.where(pred(key_ref[c], c * BLK + row), 1, 0)
            return cnt
        cnt = lax.fori_loop(0, npair, body, jnp.zeros((BLK, BLK), I32))
        return jnp.sum(cnt, axis=0, keepdims=True)

    thr = jnp.where(count(lambda k, s: k >= 0) >= topk, 0, INT_MIN).astype(I32)

    def bit_step(it, thr):
        cand = thr + (jnp.int32(1) << (30 - it))
        return jnp.where(count(lambda k, s: k >= cand) >= topk, cand, thr)

    thr = lax.fori_loop(0, 31, bit_step, thr)
    n_gt = count(lambda k, s: k > thr)
    n_ge = count(lambda k, s: k >= thr)
    need = topk - n_gt
    tied = (n_ge > topk) & (thr > INT_MIN)
    any_tied = jnp.max(tied.astype(I32))
    nbits = (key_ref.shape[0] * BLK).bit_length()

    def idx_search():
        def step(it, lo):
            cand = lo + (jnp.int32(1) << (nbits - 1 - it))
            below = count(lambda k, s: (k == thr) & (s < cand))
            return jnp.where(below < need, cand, lo)
        return lax.fori_loop(0, nbits, step, jnp.zeros((1, BLK), I32))

    jcut = lax.cond(any_tied > 0, idx_search,
                    lambda: jnp.full((1, BLK), 2 ** 30, I32))

    def p2(j, carry):
        for u in range(2):
            c = 2 * j + u
            k = key_ref[c]
            sel = (k > thr) | ((k == thr) & (c * BLK + row <= jcut))
            sel = sel & (k > INT_MIN)
            bias_ref[c] = jnp.where(sel, 0.0, NEG)
        return carry

    lax.fori_loop(0, npair, p2, 0)

    gw = HEADS_PER_KV * BLK

    def logits(c):
        bb = jnp.concatenate([bias_ref[c]] * HEADS_PER_KV, axis=1)
        cr = jnp.minimum(c, last)
        return [_dot(k_ref[g, cr], qt_ref[g]) + bb for g in range(N_KV_HEADS)]

    def attend(shift):
        acc_ref[...] = jnp.zeros_like(acc_ref)

        def body(j, carry):
            p = []
            for u in range(2):
                s = logits(2 * j + u)
                if shift is not None:
                    s = [s[g] - shift[g] for g in range(N_KV_HEADS)]
                p.append([jnp.exp(x).astype(BF16) for x in s])
            c1 = jnp.minimum(2 * j + 1, last)
            for g in range(N_KV_HEADS):
                vt = jnp.concatenate([vt_ref[2 * j, g], vt_ref[c1, g]], axis=1)
                pp = jnp.concatenate([p[0][g], p[1][g]], axis=0)
                acc_ref[g] += _dot(vt, pp)
            return carry

        lax.fori_loop(0, npair, body, 0)

    safe = smax_ref[0] <= SAFE_LOGIT

    @pl.when(safe)
    def _():
        attend(None)

    @pl.when(jnp.logical_not(safe))
    def _():
        def pa(j, m):
            for u in range(2):
                s = logits(2 * j + u)
                m = tuple(jnp.maximum(m[g], _tile_fold(s[g], jnp.maximum)) for g in range(N_KV_HEADS))
            return m

        m0 = tuple(jnp.full((SUBLANES, gw), NEG, F32) for _ in range(N_KV_HEADS))
        m = lax.fori_loop(0, npair, pa, m0)
        attend([jnp.max(x, axis=0, keepdims=True) for x in m])

    q_row = i * BLK + lax.broadcasted_iota(I32, (BLK, LANES), 0)
    for g in range(N_KV_HEADS):
        a = acc_ref[g]
        res = a[0:HEAD_DIM] * (1.0 / a[HEAD_DIM:HEAD_DIM + 1])
        for t in range(HEADS_PER_KV // 2):
            two = jnp.concatenate([res[:, (2 * t + u) * BLK:(2 * t + u + 1) * BLK] for u in range(2)],
                                  axis=0)
            two = jnp.where(q_row >= pad, two.T, 0.0)
            lo = (g * HEADS_PER_KV + 2 * t) * HEAD_DIM
            o_ref[:, lo:lo + LANES] = two.astype(BF16)


def _dsa_prompt(smax, qit, wt, kib, qt, kg, vt, pad, topk):
    bsz, nb = qit.shape[:2]
    t_pad = nb * BLK
    gw = HEADS_PER_KV * BLK
    kern = functools.partial(_dsa_prompt_kernel, pad=pad, topk=topk)
    return pl.pallas_call(
        kern,
        grid=(bsz, nb),
        in_specs=[
            pl.BlockSpec(memory_space=pltpu.SMEM),
            pl.BlockSpec((None, None, IDX_DIM, N_IDX_HEADS * BLK), lambda b, i: (b, i, 0, 0)),
            pl.BlockSpec((None, None, N_IDX_HEADS, BLK), lambda b, i: (b, i, 0, 0)),
            pl.BlockSpec((None, nb, BLK, IDX_DIM), lambda b, i: (b, 0, 0, 0)),
            pl.BlockSpec((None, None, N_KV_HEADS, HEAD_DIM, gw), lambda b, i: (b, i, 0, 0, 0)),
            pl.BlockSpec((N_KV_HEADS, None, nb, BLK, HEAD_DIM), lambda b, i: (0, b, 0, 0, 0)),
            pl.BlockSpec((None, nb, N_KV_HEADS, VT_ROWS, BLK), lambda b, i: (b, 0, 0, 0, 0)),
        ],
        out_specs=pl.BlockSpec((None, BLK, N_HEADS * HEAD_DIM), lambda b, i: (b, i, 0)),
        out_shape=jax.ShapeDtypeStruct((bsz, t_pad, N_HEADS * HEAD_DIM), BF16),
        scratch_shapes=[pltpu.VMEM((nb + 1, BLK, BLK), I32),
                        pltpu.VMEM((nb + 1, BLK, BLK), F32),
                        pltpu.VMEM((N_KV_HEADS, VT_ROWS, gw), F32)],
        compiler_params=_cparams(("parallel", "arbitrary")),
        name="dsa_prompt",
    )(smax, qit, wt, kib, qt, kg, vt)


def _smp_scores_kernel(pt_ref, qi_ref, w_ref, *refs, pg):
    page_refs, o_ref = refs[:pg], refs[pg]
    qi = qi_ref[...]
    w = w_ref[...]
    for p in range(pg):
        d = _dot(qi, page_refs[p][...].astype(BF16))
        o_ref[p:p + 1, :] = jnp.sum(w * jnp.maximum(d, 0.0), axis=0, keepdims=True)


def _smp_scores(page_table, qi, wi, kidx_t, pg):
    db, npg = page_table.shape
    kern = functools.partial(_smp_scores_kernel, pg=pg)
    page_spec = lambda p: pl.BlockSpec(
        (None, IDX_DIM, BLK), lambda b, j, pt: (pt[b * npg + j * pg + p], 0, 0))
    return pl.pallas_call(
        kern,
        grid_spec=pltpu.PrefetchScalarGridSpec(
            num_scalar_prefetch=1,
            grid=(db, npg // pg),
            in_specs=[pl.BlockSpec((None, N_IDX_HEADS, IDX_DIM), lambda b, j, pt: (b, 0, 0)),
                      pl.BlockSpec((None, N_IDX_HEADS, 1), lambda b, j, pt: (b, 0, 0))]
                     + [page_spec(p) for p in range(pg)],
            out_specs=pl.BlockSpec((None, pg, BLK), lambda b, j, pt: (b, j, 0)),
        ),
        out_shape=jax.ShapeDtypeStruct((db, npg, BLK), F32),
        compiler_params=_cparams(("parallel", "arbitrary")),
        name="sample_scores",
    )(page_table.reshape(-1), qi, wi, *([kidx_t] * pg))


def _smp_select_kernel(sc_ref, qi_ref, w_ref, kin_ref, bias_ref, nb_ref, *, topk, past):
    db, npg, _ = sc_ref.shape
    key = _sort_key(sc_ref[...])
    qi = qi_ref[...].astype(F32)
    kn = kin_ref[...].astype(BF16).astype(F32)
    d = jnp.sum(qi * kn, axis=-1, keepdims=True)
    s_new = jnp.sum(w_ref[...] * jnp.maximum(d, 0.0), axis=1, keepdims=True)
    key_new = _sort_key(s_new)
    pos = (lax.broadcasted_iota(I32, (db, npg, BLK), 1) * BLK
           + lax.broadcasted_iota(I32, (db, npg, BLK), 2))

    def count(pred_past, pred_new):
        c = jnp.sum(jnp.where(pred_past, 1, 0), axis=2, keepdims=True)
        return jnp.sum(c, axis=1, keepdims=True) + jnp.where(pred_new, 1, 0)

    thr = jnp.where(count(key >= 0, key_new >= 0) >= topk, 0, INT_MIN).astype(I32)

    def bit_step(it, thr):
        cand = thr + (jnp.int32(1) << (30 - it))
        return jnp.where(count(key >= cand, key_new >= cand) >= topk, cand, thr)

    thr = lax.fori_loop(0, 31, bit_step, thr)
    need = topk - count(key > thr, key_new > thr)
    nbits = max(1, past.bit_length())

    def step(it, lo):
        cand = lo + (jnp.int32(1) << (nbits - 1 - it))
        below = count((key == thr) & (pos < cand), (key_new == thr) & (past < cand))
        return jnp.where(below < need, cand, lo)

    jcut = lax.fori_loop(0, nbits, step, jnp.zeros((db, 1, 1), I32))
    sel = (key > thr) | ((key == thr) & (pos <= jcut))
    bias_ref[...] = jnp.where(sel, 0.0, NEG)
    sel_new = (key_new > thr) | ((key_new == thr) & (past <= jcut))
    nb_ref[...] = jnp.broadcast_to(jnp.where(sel_new, 0.0, NEG), nb_ref.shape)


def _smp_select(scores, qi, wi, ki_new, topk):
    db, npg, _ = scores.shape
    kern = functools.partial(_smp_select_kernel, topk=topk, past=npg * BLK)
    full = lambda *s: pl.BlockSpec(s, lambda i: (0,) * len(s))
    return pl.pallas_call(
        kern,
        grid=(1,),
        in_specs=[full(db, npg, BLK), full(db, N_IDX_HEADS, IDX_DIM), full(db, N_IDX_HEADS, 1),
                  full(db, 1, IDX_DIM)],
        out_specs=[full(db, npg, BLK), full(db, SUBLANES, LANES)],
        out_shape=[jax.ShapeDtypeStruct((db, npg, BLK), F32),
                   jax.ShapeDtypeStruct((db, SUBLANES, LANES), F32)],
        compiler_params=_cparams(("arbitrary",)),
        name="sample_select",
    )(scores, qi, wi, ki_new)


def _smp_attend_kernel(pt_ref, q_ref, bias_ref, nb_ref, kn_ref, vn_ref, *refs, pg):
    k_refs, v_refs = refs[:pg], refs[pg:2 * pg]
    o_ref, m_ref, l_ref, acc_ref = refs[2 * pg:]
    j = pl.program_id(1)
    kd = N_KV_HEADS * HEAD_DIM

    @pl.when(j == 0)
    def _():
        m_ref[...] = jnp.full_like(m_ref, NEG)
        l_ref[...] = jnp.zeros_like(l_ref)
        acc_ref[...] = jnp.zeros_like(acc_ref)

    q = q_ref[...]
    s = [_dot(q, k_refs[p][...].astype(BF16)) + bias_ref[p:p + 1, :] for p in range(pg)]
    m_old = m_ref[...]
    m_new = m_old
    for p in range(pg):
        m_new = jnp.maximum(m_new, jnp.max(s[p], axis=-1, keepdims=True))
    alpha = jnp.exp(m_old - m_new)
    l = l_ref[...] * alpha
    acc = acc_ref[...] * alpha
    for p in range(pg):
        e = jnp.exp(s[p] - m_new)
        l = l + jnp.sum(e, axis=-1, keepdims=True)
        acc = acc + _dot_t1(e.astype(BF16), v_refs[p][...].astype(BF16))
    m_ref[...] = m_new
    l_ref[...] = l
    acc_ref[...] = acc

    @pl.when(j == pl.num_programs(1) - 1)
    def _():
        qf = q.astype(F32)
        kn = kn_ref[...].astype(BF16).astype(F32)
        vn = vn_ref[...].astype(BF16).astype(F32)
        s_new = jnp.sum(qf * kn, axis=-1, keepdims=True) + nb_ref[0:1, 0:1]
        m_fin = jnp.maximum(m_new, s_new)
        a2 = jnp.exp(m_new - m_fin)
        e_new = jnp.exp(s_new - m_fin)
        l_fin = l * a2 + e_new
        acc_fin = acc * a2 + e_new.astype(BF16).astype(F32) * vn
        res = acc_fin / l_fin
        hgrp = lax.broadcasted_iota(I32, (N_HEADS, HEAD_DIM), 0) // HEADS_PER_KV
        out = jnp.zeros((N_HEADS, HEAD_DIM), F32)
        for g in range(N_KV_HEADS):
            out = out + jnp.where(hgrp == g, res[:, g * HEAD_DIM:(g + 1) * HEAD_DIM], 0.0)
        o_ref[...] = out


def _smp_attend(page_table, q_bd, bias, nbias, k_new, v_new, k_t, v_t, pg):
    db, npg = page_table.shape
    kd = N_KV_HEADS * HEAD_DIM
    kern = functools.partial(_smp_attend_kernel, pg=pg)
    page_spec = lambda p: pl.BlockSpec(
        (None, kd, BLK), lambda b, j, pt: (pt[b * npg + j * pg + p], 0, 0))
    per_b = lambda shape: pl.BlockSpec((None,) + shape, lambda b, j, pt: (b, 0, 0))
    return pl.pallas_call(
        kern,
        grid_spec=pltpu.PrefetchScalarGridSpec(
            num_scalar_prefetch=1,
            grid=(db, npg // pg),
            in_specs=[per_b((N_HEADS, kd)),
                      pl.BlockSpec((None, pg, BLK), lambda b, j, pt: (b, j, 0)),
                      per_b((SUBLANES, LANES)), per_b((1, kd)), per_b((1, kd))]
                     + [page_spec(p) for p in range(pg)] * 2,
            out_specs=per_b((N_HEADS, HEAD_DIM)),
            scratch_shapes=[pltpu.VMEM((N_HEADS, 1), F32), pltpu.VMEM((N_HEADS, 1), F32),
                            pltpu.VMEM((N_HEADS, kd), F32)],
        ),
        out_shape=jax.ShapeDtypeStruct((db, N_HEADS, HEAD_DIM), F32),
        compiler_params=_cparams(("parallel", "arbitrary")),
        name="sample_attend",
    )(page_table.reshape(-1), q_bd, bias, nbias, k_new, v_new,
      *([k_t] * pg), *([v_t] * pg))


def _ssd_prompt_kernel(z_ref, xbc_ref, dt_ref, dtt_ref, cw_ref, cb_ref, dtb_ref, a_ref,
                       dtbc_ref, ac_ref, dsk_ref, gn_ref, ltri_ref, utri_ref,
                       y_ref, st_ref, xpad_ref, h_ref, yb_ref, xt_ref, yt_ref, *, pad, n_heads):
    c = pl.program_id(1)
    d_in = n_heads * SSD_HEAD_DIM
    gn = SSD_GROUPS * D_STATE
    hpg = n_heads // SSD_GROUPS

    @pl.when(c == 0)
    def _():
        xpad_ref[0:SUBLANES, :] = jnp.zeros((SUBLANES, xpad_ref.shape[1]), F32)
        h_ref[...] = jnp.zeros_like(h_ref)

    xpad_ref[SUBLANES:, :] = xbc_ref[...]
    conv = cb_ref[...] + cw_ref[CONV_W - 1:CONV_W, :] * xbc_ref[...]
    for j in range(CONV_W - 1):
        sh = CONV_W - 1 - j
        conv = conv + cw_ref[j:j + 1, :] * xpad_ref[SUBLANES - sh:SUBLANES - sh + BLK, :]
    xpad_ref[0:SUBLANES, :] = xbc_ref[BLK - SUBLANES:BLK, :]
    xbc = _silu(conv)

    live = (c > 0) | (lax.broadcasted_iota(I32, (BLK, LANES), 0) >= pad)
    dt = jnp.where(live, _softplus(dt_ref[...] + dtb_ref[...]), 0.0)
    acum = _dot_sel(dt * a_ref[...], ltri_ref[...], fn=lambda x, m: _dot(m, x))
    live_t = (c > 0) | (lax.broadcasted_iota(I32, (dtt_ref.shape[0], BLK), 1) >= pad)
    dtt = jnp.where(live_t, _softplus(dtt_ref[...] + dtbc_ref[...]), 0.0)
    acum_t = _dot_sel(dtt * ac_ref[...], utri_ref[...])
    for t in range(d_in // LANES):
        xt_ref[t * LANES:(t + 1) * LANES, :] = xbc[:, t * LANES:(t + 1) * LANES].T
    a_last = acum_t[:, BLK - 1:BLK]
    ecol_t = jnp.exp(acum_t)
    decs_t = jnp.exp(a_last - acum_t)
    ea_last = jnp.exp(a_last)
    causal_t = (lax.broadcasted_iota(I32, (BLK, BLK), 0) <= lax.broadcasted_iota(I32, (BLK, BLK), 1))
    hp = SSD_HEAD_DIM

    for g in range(SSD_GROUPS):
        bm = xbc[:, d_in + g * D_STATE:d_in + (g + 1) * D_STATE].astype(BF16)
        ct = xbc[:, d_in + gn + g * D_STATE:d_in + gn + (g + 1) * D_STATE].T.astype(BF16)
        cb_t = _dot(bm, ct)
        hprev = h_ref[g * hpg:(g + 1) * hpg].reshape(hpg * hp, D_STATE)
        y_off = _dot(hprev.astype(BF16), ct)
        ws = []
        for hh in range(hpg):
            h = g * hpg + hh
            rows = slice(h * hp, (h + 1) * hp)
            xh = xt_ref[rows, :]
            xdt = xh * dtt[h:h + 1, :]
            decay_t = jnp.exp(jnp.where(causal_t, acum_t[h:h + 1, :] - acum[:, h:h + 1], NEG))
            y = _dot(xdt.astype(BF16), (cb_t * decay_t).astype(BF16))
            y = y + ecol_t[h:h + 1, :] * y_off[hh * hp:(hh + 1) * hp]
            yt_ref[rows, :] = y + dsk_ref[0:1, h:h + 1] * xh
            ws.append((xdt * decs_t[h:h + 1, :]).astype(BF16))
        upd = _dot(jnp.concatenate(ws, axis=0), bm)
        for hh in range(hpg):
            h = g * hpg + hh
            h_ref[h] = ea_last[h:h + 1, :] * hprev[hh * hp:(hh + 1) * hp] + upd[hh * hp:(hh + 1) * hp]

    for t in range(d_in // LANES):
        yb_ref[:, t * LANES:(t + 1) * LANES] = yt_ref[t * LANES:(t + 1) * LANES, :].T

    yg = yb_ref[...] * _silu(z_ref[...])
    gsz = d_in // SSD_GROUPS
    for g in range(SSD_GROUPS):
        blk = yg[:, g * gsz:(g + 1) * gsz]
        y_ref[:, g * gsz:(g + 1) * gsz] = _rms(blk, gn_ref[:, g * gsz:(g + 1) * gsz]).astype(BF16)

    @pl.when(c == pl.num_programs(1) - 1)
    def _():
        st_ref[...] = h_ref[...]


def _ssd_prompt(z_xbc, dt_raw, dt_raw_t, conv_w, conv_b, dt_bias, a_log, d_skip, gate_norm,
                bsz, nb, pad, n_heads):
    d_in = n_heads * SSD_HEAD_DIM
    cdim = d_in + 2 * SSD_GROUPS * D_STATE
    assert n_heads <= LANES
    hp = LANES - n_heads
    a = -jnp.exp(a_log.astype(F32))
    ltri = jnp.tril(jnp.ones((BLK, BLK), F32)).astype(BF16)
    kern = functools.partial(_ssd_prompt_kernel, pad=pad, n_heads=n_heads)
    const = lambda shape: pl.BlockSpec(shape, lambda b, c: (0,) * len(shape))
    zb = d_in // cdim if d_in % cdim == 0 else None
    return pl.pallas_call(
        kern,
        grid=(bsz, nb),
        in_specs=[
            pl.BlockSpec((BLK, d_in), lambda b, c: (b * nb + c, 0)),
            pl.BlockSpec((BLK, cdim), lambda b, c: (b * nb + c, 0)),
            pl.BlockSpec((BLK, LANES), lambda b, c: (b * nb + c, 0)),
            pl.BlockSpec((None, None, n_heads, BLK), lambda b, c: (b, c, 0, 0)),
            const((CONV_W, cdim)), const((1, cdim)), const((1, LANES)), const((1, LANES)),
            const((n_heads, 1)), const((n_heads, 1)), const((1, LANES)), const((1, d_in)),
            const((BLK, BLK)), const((BLK, BLK)),
        ],
        out_specs=[pl.BlockSpec((BLK, d_in), lambda b, c: (b * nb + c, 0)),
                   pl.BlockSpec((None, n_heads, SSD_HEAD_DIM, D_STATE), lambda b, c: (b, 0, 0, 0))],
        out_shape=[jax.ShapeDtypeStruct((bsz * nb * BLK, d_in), BF16),
                   jax.ShapeDtypeStruct((bsz, n_heads, SSD_HEAD_DIM, D_STATE), F32)],
        scratch_shapes=[pltpu.VMEM((BLK + SUBLANES, cdim), F32),
                        pltpu.VMEM((n_heads, SSD_HEAD_DIM, D_STATE), F32),
                        pltpu.VMEM((BLK, d_in), F32),
                        pltpu.VMEM((d_in, BLK), F32),
                        pltpu.VMEM((d_in, BLK), F32)],
        compiler_params=_cparams(("parallel", "arbitrary")),
        name="ssd_prompt",
    )(z_xbc[0], z_xbc[1], dt_raw, dt_raw_t, conv_w, conv_b.reshape(1, cdim),
      jnp.pad(dt_bias, (0, hp)).reshape(1, LANES), jnp.pad(a, (0, hp)).reshape(1, LANES),
      dt_bias.reshape(n_heads, 1), a.reshape(n_heads, 1),
      jnp.pad(d_skip, (0, hp)).reshape(1, LANES), gate_norm.reshape(1, d_in), ltri, ltri.T)


def _ssd_sample_kernel(z_ref, xbc_ref, dt_ref, cst_ref, h0_ref, cw_ref, cb_ref, dtb_ref, a_ref,
                       dsk_ref, gn_ref, exp_ref, y_ref, h_ref, *, n_heads):
    d_in = n_heads * SSD_HEAD_DIM
    gn = SSD_GROUPS * D_STATE
    gsz = d_in // SSD_GROUPS
    hpg = n_heads // SSD_GROUPS
    conv = cb_ref[...] + cw_ref[CONV_W - 1:CONV_W, :] * xbc_ref[...]
    for j in range(CONV_W - 1):
        conv = conv + cw_ref[j:j + 1, :] * cst_ref[j:j + 1, :]
    xbc = _silu(conv)
    dt = _softplus(dt_ref[...] + dtb_ref[...])
    pad8 = lambda r: jnp.concatenate([r, jnp.zeros((SUBLANES - 1, r.shape[1]), F32)], axis=0)
    dt_ch = _dot_sel(pad8(dt), exp_ref[...])[0:1]
    da_ch = jnp.exp(_dot_sel(pad8(dt * a_ref[...]), exp_ref[...])[0:1])
    dsk_ch = _dot_sel(pad8(dsk_ref[...]), exp_ref[...])[0:1]
    xh = xbc[:, :d_in]
    xdt = xh * dt_ch
    ones = jnp.ones((SUBLANES, D_STATE), BF16)
    outs = []
    for g in range(SSD_GROUPS):
        sl = slice(g * gsz, (g + 1) * gsz)
        bm = xbc[:, d_in + g * D_STATE:d_in + (g + 1) * D_STATE]
        cm = xbc[:, d_in + gn + g * D_STATE:d_in + gn + (g + 1) * D_STATE]
        da_col = _dot_sel(pad8(da_ch[:, sl]), ones, fn=_dot_t0)
        xdt_col = _dot_sel(pad8(xdt[:, sl]), ones, fn=_dot_t0)
        h0 = h0_ref[g * hpg:(g + 1) * hpg].reshape(gsz, D_STATE)
        hn = da_col * h0 + xdt_col.astype(BF16).astype(F32) * bm.astype(BF16).astype(F32)
        h_ref[g * hpg:(g + 1) * hpg] = hn.reshape(hpg, SSD_HEAD_DIM, D_STATE)
        y_col = jnp.sum(hn * cm, axis=-1, keepdims=True)
        outs.append(y_col)
    y_cols = jnp.concatenate(outs, axis=0)
    rows = []
    eye = (lax.broadcasted_iota(I32, (LANES, LANES), 0)
           == lax.broadcasted_iota(I32, (LANES, LANES), 1)).astype(F32)
    for t in range(d_in // LANES):
        blk = y_cols[t * LANES:(t + 1) * LANES]
        rows.append(jnp.sum(blk * eye, axis=0, keepdims=True))
    y = jnp.concatenate(rows, axis=1) + dsk_ch * xh
    y = y * _silu(z_ref[...])
    for g in range(SSD_GROUPS):
        sl = slice(g * gsz, (g + 1) * gsz)
        y_ref[:, sl] = _rms(y[:, sl], gn_ref[:, sl]).astype(BF16)


def _ssd_sample(z, xbc_raw, dt_raw, conv_state, h0, conv_w, conv_b, dt_bias, a_log, d_skip,
                gate_norm, n_heads):
    db = z.shape[0]
    d_in = n_heads * SSD_HEAD_DIM
    cdim = d_in + 2 * SSD_GROUPS * D_STATE
    hp = LANES - n_heads
    a = -jnp.exp(a_log.astype(F32))
    expand = (jnp.arange(LANES)[:, None] == (jnp.arange(d_in) // SSD_HEAD_DIM)[None, :]).astype(BF16)
    kern = functools.partial(_ssd_sample_kernel, n_heads=n_heads)
    const = lambda shape: pl.BlockSpec(shape, lambda b: (0,) * len(shape))
    per_b = lambda shape: pl.BlockSpec((None,) + shape, lambda b: (b,) + (0,) * len(shape))
    y, h = pl.pallas_call(
        kern,
        grid=(db,),
        in_specs=[per_b((1, d_in)), per_b((1, cdim)), per_b((1, LANES)), per_b((CONV_W - 1, cdim)),
                  per_b((n_heads, SSD_HEAD_DIM, D_STATE)),
                  const((CONV_W, cdim)), const((1, cdim)), const((1, LANES)), const((1, LANES)),
                  const((1, LANES)), const((1, d_in)), const((LANES, d_in))],
        out_specs=[per_b((1, d_in)), per_b((n_heads, SSD_HEAD_DIM, D_STATE))],
        out_shape=[jax.ShapeDtypeStruct((db, 1, d_in), BF16),
                   jax.ShapeDtypeStruct((db, n_heads, SSD_HEAD_DIM, D_STATE), F32)],
        compiler_params=_cparams(("parallel",)),
        name="ssd_sample",
    )(z.reshape(db, 1, d_in), xbc_raw.reshape(db, 1, cdim), dt_raw.reshape(db, 1, LANES),
      conv_state, h0, conv_w, conv_b.reshape(1, cdim),
      jnp.pad(dt_bias, (0, hp)).reshape(1, LANES), jnp.pad(a, (0, hp)).reshape(1, LANES),
      jnp.pad(d_skip, (0, hp)).reshape(1, LANES), gate_norm.reshape(1, d_in), expand)
    return y.reshape(db, d_in), h


def kernel(x_prompt, x_sample, cache_k, cache_v, cache_kidx, page_table, state_ssm, state_conv,
           meta_tokens, norm_ffn_a, w_ffn_a_in, w_ffn_a_out, norm_mix, norm_ffn_b, w_ffn_b_in,
           w_ffn_b_out, w_attn_in, q_norm, k_norm, kidx_norm, w_attn_out,
           w_ssd_in, conv_w, conv_b, dt_bias, a_log, d_skip, gate_norm, w_ssd_out):
    bsz, seq, d = x_prompt.shape
    db = x_sample.shape[0]
    assert x_sample.shape[1] == 1
    t_real = N_META + seq
    nb = -(-t_real // BLK)
    t_pad = nb * BLK
    pad = t_pad - t_real
    npg = page_table.shape[1]
    past = npg * BLK
    topk_p = min(TOPK_MAX, seq // 4)
    topk_s = min(TOPK_MAX, (past + 1) // 4)
    qd, kd, qid = N_HEADS * HEAD_DIM, N_KV_HEADS * HEAD_DIM, N_IDX_HEADS * IDX_DIM
    d_in = w_ssd_out.shape[1]
    n_heads = d_in // SSD_HEAD_DIM
    cdim = d_in + 2 * SSD_GROUPS * D_STATE
    depth = norm_mix.shape[0]

    meta = jnp.broadcast_to(meta_tokens.astype(F32)[None], (bsz, N_META, d))
    xp = jnp.concatenate([jnp.zeros((bsz, pad, d), F32), meta, x_prompt], axis=1)
    xp = xp.reshape(bsz * t_pad, d)
    xs = x_sample.reshape(db, d)

    outs = {k: [] for k in ("kp", "vp", "kip", "ks", "vs", "kis", "hp", "cp", "hs", "cs")}
    for i in range(depth):
        wa_in, wa_out = w_ffn_a_in[i].astype(BF16), w_ffn_a_out[i].astype(BF16)
        xp = _ffn(xp, norm_ffn_a[i], wa_in, wa_out)
        xs = _ffn(xs, norm_ffn_a[i], wa_in, wa_out)
        j = i // 2
        if i % 2 == 0:
            w = w_attn_in[j]
            o1, o2 = qd + 2 * kd + qid, qd + 2 * kd + qid + IDX_DIM
            w_pad = jnp.concatenate(
                [w[:, :o1], jnp.pad(w[:, o1:o2], ((0, 0), (0, LANES - IDX_DIM))),
                 jnp.pad(w[:, o2:], ((0, 0), (0, LANES - N_IDX_HEADS)))], axis=1).astype(BF16)
            w_out = w_attn_out[j].astype(BF16)

            k, v, ki, qt, qit, wt, kg, vt, kib = _attn_in(
                xp, norm_mix[i], w_pad, q_norm[j], k_norm[j], kidx_norm[j], prompt=True)
            gw = HEADS_PER_KV * BLK
            smax = (1.02 * math.sqrt(HEAD_DIM) * jnp.max(jnp.abs(q_norm[j]))
                    * jnp.max(jnp.abs(k_norm[j]))).reshape(1).astype(F32)
            o = _dsa_prompt(smax,
                            qit.reshape(bsz, nb, IDX_DIM, N_IDX_HEADS * BLK),
                            wt.reshape(bsz, nb, N_IDX_HEADS, BLK),
                            kib.reshape(bsz, nb, BLK, IDX_DIM),
                            qt.reshape(bsz, nb, N_KV_HEADS, HEAD_DIM, gw),
                            kg.reshape(N_KV_HEADS, bsz, nb, BLK, HEAD_DIM),
                            vt.reshape(bsz, nb, N_KV_HEADS, VT_ROWS, BLK),
                            pad, topk_p)
            xp = _out_proj(xp, o.reshape(bsz * t_pad, qd), w_out)
            outs["kp"].append(k.reshape(bsz, t_pad, N_KV_HEADS, HEAD_DIM)[:, pad:])
            outs["vp"].append(v.reshape(bsz, t_pad, N_KV_HEADS, HEAD_DIM)[:, pad:])
            outs["kip"].append(ki[:, :IDX_DIM].reshape(bsz, t_pad, IDX_DIM)[:, pad:])

            q, k, v, qi, ki, wi = _attn_in(
                xs, norm_mix[i], w_pad, q_norm[j], k_norm[j], kidx_norm[j], prompt=False)
            pg = math.gcd(npg, 16)
            qi3 = qi.reshape(db, N_IDX_HEADS, IDX_DIM)
            wi3 = wi[:, :N_IDX_HEADS].reshape(db, N_IDX_HEADS, 1)
            scores = _smp_scores(page_table, qi3, wi3, cache_kidx[j].transpose(0, 2, 1), pg)
            bias, nbias = _smp_select(scores, qi3, wi3, ki[:, :IDX_DIM].reshape(db, 1, IDX_DIM), topk_s)
            hsel = (jnp.arange(N_HEADS)[:, None] // HEADS_PER_KV == jnp.arange(N_KV_HEADS)[None, :])
            q_bd = (q.reshape(db, N_HEADS, 1, HEAD_DIM) * hsel[None, :, :, None].astype(BF16))
            q_bd = q_bd.reshape(db, N_HEADS, kd)
            npool = cache_k.shape[1]
            o = _smp_attend(page_table, q_bd, bias, nbias, k.reshape(db, 1, kd), v.reshape(db, 1, kd),
                            cache_k[j].transpose(0, 2, 3, 1).reshape(npool, kd, BLK),
                            cache_v[j].transpose(0, 2, 3, 1).reshape(npool, kd, BLK), pg)
            xs = _out_proj(xs, o.reshape(db, qd).astype(BF16), w_out)
            outs["ks"].append(k.reshape(db, 1, N_KV_HEADS, HEAD_DIM))
            outs["vs"].append(v.reshape(db, 1, N_KV_HEADS, HEAD_DIM))
            outs["kis"].append(ki[:, :IDX_DIM].reshape(db, 1, IDX_DIM))
        else:
            w = w_ssd_in[j]
            w_main = w[:, :d_in + cdim].astype(BF16)
            w_dt = jnp.pad(w[:, d_in + cdim:], ((0, 0), (0, LANES - n_heads))).astype(BF16)
            w_out = w_ssd_out[j].astype(BF16)
            sp = (conv_w[j], conv_b[j], dt_bias[j], a_log[j], d_skip[j], gate_norm[j])

            z, xbc_raw, dt_raw = _norm_proj(xp, norm_mix[i], w_main, w_dt, d_in)
            dt_t = dt_raw[:, :n_heads].reshape(bsz, nb, BLK, n_heads).transpose(0, 1, 3, 2)
            y, hfin = _ssd_prompt((z, xbc_raw), dt_raw, dt_t, *sp, bsz, nb, pad, n_heads)
            xp = _out_proj(xp, y, w_out)
            outs["hp"].append(hfin)
            outs["cp"].append(xbc_raw.reshape(bsz, t_pad, cdim)[:, t_pad - (CONV_W - 1):])

            z, xbc_raw, dt_raw = _norm_proj(xs, norm_mix[i], w_main, w_dt, d_in)
            y, hnew = _ssd_sample(z, xbc_raw, dt_raw, state_conv[j], state_ssm[j], *sp, n_heads)
            xs = _out_proj(xs, y, w_out)
            outs["hs"].append(hnew)
            outs["cs"].append(jnp.concatenate([state_conv[j][:, 1:], xbc_raw[:, None, :]], axis=1))
        wb_in, wb_out = w_ffn_b_in[i].astype(BF16), w_ffn_b_out[i].astype(BF16)
        xp = _ffn(xp, norm_ffn_b[i], wb_in, wb_out)
        xs = _ffn(xs, norm_ffn_b[i], wb_in, wb_out)

    y_prompt = xp.reshape(bsz, t_pad, d)[:, pad + N_META:]
    y_sample = xs.reshape(db, 1, d)
    st = lambda key: jnp.stack(outs[key])
    return (y_prompt, y_sample, st("kp"), st("vp"), st("kip"), st("ks"), st("vs"), st("kis"),
            st("hp"), st("cp"), st("hs"), st("cs"))
```

```python
import functools
import math

import jax
import jax.numpy as jnp
from jax import lax
from jax.experimental import pallas as pl
from jax.experimental.pallas import tpu as pltpu

F32 = jnp.float32
BF16 = jnp.bfloat16
I32 = jnp.int32
I16 = jnp.int16

N_META = 16
N_HEADS = 16
HEAD_DIM = 64
N_KV_HEADS = 4
HEADS_PER_KV = N_HEADS // N_KV_HEADS
N_IDX_HEADS = 8
IDX_DIM = 64
TOPK_MAX = 256
SSD_HEAD_DIM = 64
SSD_GROUPS = 4
D_STATE = 128
CONV_W = 4
EPS = 1e-6

LANES = 128
SUBLANES = 8
BLK = 128
NEG = -1e30
SAFE_LOGIT = 40.0
VT_ROWS = 80
INT_MIN = -2 ** 31
HALF = 2 ** 15
VMEM_LIMIT = 56 * 1024 * 1024


def _cparams(sem, vmem=VMEM_LIMIT):
    return pltpu.CompilerParams(dimension_semantics=sem, vmem_limit_bytes=vmem)


def _row_tile(rows, pref):
    best = None
    for d in range(SUBLANES, min(rows, pref) + 1, SUBLANES):
        if rows % d == 0:
            best = d
    assert best is not None, rows
    return best


def _rms(x, g):
    var = jnp.mean(x * x, axis=-1, keepdims=True)
    return x * lax.rsqrt(var + EPS) * g


def _dot(a, b):
    return jnp.dot(a, b, preferred_element_type=F32)


def _dot_t0(a, b):
    return lax.dot_general(a, b, (((0,), (0,)), ((), ())), preferred_element_type=F32)


def _dot_t1(a, b):
    return lax.dot_general(a, b, (((1,), (1,)), ((), ())), preferred_element_type=F32)


def _split2(a):
    hi = a.astype(BF16)
    lo = (a - hi.astype(F32)).astype(BF16)
    return hi, lo


def _split3(a):
    a0 = a.astype(BF16)
    r = a - a0.astype(F32)
    a1 = r.astype(BF16)
    a2 = (r - a1.astype(F32)).astype(BF16)
    return a0, a1, a2


def _dot_sel(a, m, fn=_dot):
    a0, a1, a2 = _split3(a)
    return fn(a0, m) + fn(a1, m) + fn(a2, m)


def _silu(x):
    return x * (1.0 / (1.0 + jnp.exp(-x)))


def _softplus(x):
    return jnp.maximum(x, 0.0) + jnp.log(1.0 + jnp.exp(-jnp.abs(x)))


def _ffn_kernel(x_ref, g_ref, wa_ref, wb_ref, wo_ref, o_ref, xn_ref, acc_ref):
    xn_ref[...] = _rms(x_ref[...], g_ref[...]).astype(BF16)
    acc_ref[...] = jnp.zeros_like(acc_ref)

    def chunk(j, carry):
        xn = xn_ref[...]
        a = _dot(xn, wa_ref[j])
        b = _dot(xn, wb_ref[j])
        h = (_silu(a) * b).astype(BF16)
        acc_ref[...] += _dot(h, wo_ref[j])
        return carry

    lax.fori_loop(0, wa_ref.shape[0], chunk, 0)
    o_ref[...] = x_ref[...] + 0.5 * acc_ref[...]


def _ffn_weights(w_in, w_out):
    d, hid = w_in.shape[0], w_out.shape[0]
    th = 256 if hid % 256 == 0 else LANES
    nh = hid // th
    w_in = w_in.astype(BF16)
    wa = w_in[:, :hid].reshape(d, nh, th).transpose(1, 0, 2)
    wb = w_in[:, hid:].reshape(d, nh, th).transpose(1, 0, 2)
    return wa, wb, w_out.astype(BF16).reshape(nh, th, d)


def _ffn(x, g, weights):
    wa, wb, wo = weights
    rows, d = x.shape
    nh, _, th = wa.shape
    tm = _row_tile(rows, 1024)
    resident = lambda shape: pl.BlockSpec(shape, lambda i: (0,) * len(shape),
                                          pipeline_mode=pl.Buffered(1))
    return pl.pallas_call(
        _ffn_kernel,
        grid=(rows // tm,),
        in_specs=[
            pl.BlockSpec((tm, d), lambda i: (i, 0)),
            resident((1, d)), resident((nh, d, th)), resident((nh, d, th)), resident((nh, th, d)),
        ],
        out_specs=pl.BlockSpec((tm, d), lambda i: (i, 0)),
        out_shape=jax.ShapeDtypeStruct((rows, d), F32),
        scratch_shapes=[pltpu.VMEM((tm, d), BF16), pltpu.VMEM((tm, d), F32)],
        compiler_params=_cparams(("parallel",)),
        name="ffn",
    )(x, g.reshape(1, d), wa, wb, wo)


def _attn_project(x_ref, g_ref, w_ref, qg_ref, kg_ref, kig_ref, gq_ref, eq_ref, gk_ref, ek_ref):
    qd = N_HEADS * HEAD_DIM
    kd = N_KV_HEADS * HEAD_DIM
    qid = N_IDX_HEADS * IDX_DIM
    xn = _rms(x_ref[...], g_ref[...]).astype(BF16)
    h = _dot(xn, w_ref[...])

    def head_norm(t, gsum_ref, gexp_ref, gain):
        ss = _dot_sel(t * t, gsum_ref[...])
        rs = lax.rsqrt(ss * (1.0 / HEAD_DIM) + EPS)
        return t * _dot_sel(rs, gexp_ref[...]) * gain

    o = 0
    q = head_norm(h[:, o:o + qd], gq_ref, eq_ref, qg_ref[...]) * (HEAD_DIM ** -0.5)
    o += qd
    k = head_norm(h[:, o:o + kd], gk_ref, ek_ref, kg_ref[...])
    o += kd
    v = h[:, o:o + kd]
    o += kd
    qi = h[:, o:o + qid] * (IDX_DIM ** -0.5)
    o += qid
    ki = h[:, o:o + LANES]
    var = jnp.sum(ki * ki, axis=-1, keepdims=True) * (1.0 / IDX_DIM)
    ki = ki * lax.rsqrt(var + EPS) * kig_ref[...]
    o += LANES
    wi = h[:, o:o + LANES] * (N_IDX_HEADS ** -0.5)
    return q, k, v, qi, ki, wi


def _attn_in_sample_kernel(*refs):
    q_o, k_o, v_o, qi_o, ki_o, wi_o = refs[10:]
    q, k, v, qi, ki, wi = _attn_project(*refs[:10])
    q_o[...] = q.astype(BF16)
    k_o[...] = k
    v_o[...] = v
    qi_o[...] = qi.astype(BF16)
    ki_o[...] = ki
    wi_o[...] = wi


def _attn_in_prompt_kernel(*refs):
    k_o, v_o, ki_o, qt_o, qit_o, wt_o, kg_o, vt_o, kib_o = refs[10:]
    q, k, v, qi, ki, wi = _attn_project(*refs[:10])
    k_o[...] = k
    v_o[...] = v
    ki_o[...] = ki
    kib_o[...] = ki[:, :IDX_DIM].astype(BF16)
    for g in range(N_KV_HEADS):
        kg_o[g] = k[:, g * HEAD_DIM:(g + 1) * HEAD_DIM].astype(BF16)
    tail = (lax.broadcasted_iota(I32, (VT_ROWS - HEAD_DIM, BLK), 0) == 0).astype(BF16)
    for r in range(q.shape[0] // BLK):
        rows = slice(r * BLK, (r + 1) * BLK)
        for t in range(N_HEADS * HEAD_DIM // LANES):
            tt = q[rows, t * LANES:(t + 1) * LANES].T.astype(BF16)
            for u in range(LANES // HEAD_DIM):
                h = t * (LANES // HEAD_DIM) + u
                g, hh = h // HEADS_PER_KV, h % HEADS_PER_KV
                qt_o[r, g, :, hh * BLK:(hh + 1) * BLK] = tt[u * HEAD_DIM:(u + 1) * HEAD_DIM]
        for t in range(N_IDX_HEADS * IDX_DIM // LANES):
            tt = qi[rows, t * LANES:(t + 1) * LANES].T.astype(BF16)
            for u in range(LANES // IDX_DIM):
                h = t * (LANES // IDX_DIM) + u
                qit_o[r, :, h * BLK:(h + 1) * BLK] = tt[u * IDX_DIM:(u + 1) * IDX_DIM]
        wt_o[r] = wi[rows].T[0:N_IDX_HEADS]
        for t in range(N_KV_HEADS * HEAD_DIM // LANES):
            tt = v[rows, t * LANES:(t + 1) * LANES].T.astype(BF16)
            for u in range(LANES // HEAD_DIM):
                vt_o[r, t * (LANES // HEAD_DIM) + u] = jnp.concatenate(
                    [tt[u * HEAD_DIM:(u + 1) * HEAD_DIM], tail], axis=0)


def _seg_mats(n_heads, hd):
    col = jnp.arange(n_heads * hd) // hd
    gsum = (col[:, None] == jnp.arange(LANES)[None, :]).astype(BF16)
    return gsum, gsum.T


def _attn_in(x, g, w_pad, q_gain, k_gain, ki_gain, prompt):
    rows, d = x.shape
    n = w_pad.shape[1]
    qd, kd, qid = N_HEADS * HEAD_DIM, N_KV_HEADS * HEAD_DIM, N_IDX_HEADS * IDX_DIM
    gw = HEADS_PER_KV * BLK
    tm = _row_tile(rows, 512)
    gq, eq = _seg_mats(N_HEADS, HEAD_DIM)
    gk, ek = _seg_mats(N_KV_HEADS, HEAD_DIM)
    const = lambda shape: pl.BlockSpec(shape, lambda i: (0,) * len(shape))
    rowb = lambda w: pl.BlockSpec((tm, w), lambda i: (i, 0))
    sds = jax.ShapeDtypeStruct
    if prompt:
        assert tm % BLK == 0
        nbk, tb = rows // BLK, tm // BLK
        blkb = lambda *s: pl.BlockSpec((tb,) + s, lambda i: (i,) + (0,) * len(s))
        kern = _attn_in_prompt_kernel
        out_specs = [rowb(kd), rowb(kd), rowb(LANES), blkb(N_KV_HEADS, HEAD_DIM, gw),
                     blkb(IDX_DIM, N_IDX_HEADS * BLK), blkb(N_IDX_HEADS, BLK),
                     pl.BlockSpec((N_KV_HEADS, tm, HEAD_DIM), lambda i: (0, i, 0)),
                     blkb(N_KV_HEADS, VT_ROWS, BLK),
                     rowb(IDX_DIM)]
        out_shape = [sds((rows, kd), F32), sds((rows, kd), F32), sds((rows, LANES), F32),
                     sds((nbk, N_KV_HEADS, HEAD_DIM, gw), BF16),
                     sds((nbk, IDX_DIM, N_IDX_HEADS * BLK), BF16),
                     sds((nbk, N_IDX_HEADS, BLK), F32),
                     sds((N_KV_HEADS, rows, HEAD_DIM), BF16),
                     sds((nbk, N_KV_HEADS, VT_ROWS, BLK), BF16),
                     sds((rows, IDX_DIM), BF16)]
    else:
        kern = _attn_in_sample_kernel
        out_specs = [rowb(qd), rowb(kd), rowb(kd), rowb(qid), rowb(LANES), rowb(LANES)]
        out_shape = [sds((rows, qd), BF16), sds((rows, kd), F32), sds((rows, kd), F32),
                     sds((rows, qid), BF16), sds((rows, LANES), F32), sds((rows, LANES), F32)]
    return pl.pallas_call(
        kern,
        grid=(rows // tm,),
        in_specs=[rowb(d), const((1, d)), const((d, n)), const((1, qd)), const((1, kd)),
                  const((1, LANES)), const((qd, LANES)), const((LANES, qd)),
                  const((kd, LANES)), const((LANES, kd))],
        out_specs=out_specs,
        out_shape=out_shape,
        compiler_params=_cparams(("parallel",)),
        name="attn_in_prompt" if prompt else "attn_in_sample",
    )(x, g.reshape(1, d), w_pad,
      jnp.tile(q_gain, N_HEADS).reshape(1, qd), jnp.tile(k_gain, N_KV_HEADS).reshape(1, kd),
      jnp.pad(ki_gain, (0, LANES - IDX_DIM)).reshape(1, LANES), gq, eq, gk, ek)


def _norm_proj_kernel(x_ref, g_ref, w_ref, ws_ref, oa_ref, ob_ref, os_ref, xn_ref, *, na):
    j = pl.program_id(1)

    @pl.when(j == 0)
    def _():
        xn = _rms(x_ref[...], g_ref[...]).astype(BF16)
        xn_ref[...] = xn
        os_ref[...] = _dot(xn, ws_ref[...])

    r = _dot(xn_ref[...], w_ref[...])

    @pl.when(j < na)
    def _():
        oa_ref[...] = r

    @pl.when(j >= na)
    def _():
        ob_ref[...] = r


def _norm_proj(x, g, w_main, w_side, n_a):
    rows, d = x.shape
    n = w_main.shape[1]
    tm = _row_tile(rows, 1024)
    tn = math.gcd(math.gcd(n_a, n - n_a), 1024)
    na = n_a // tn
    return pl.pallas_call(
        functools.partial(_norm_proj_kernel, na=na),
        grid=(rows // tm, n // tn),
        in_specs=[
            pl.BlockSpec((tm, d), lambda i, j: (i, 0)),
            pl.BlockSpec((1, d), lambda i, j: (0, 0)),
            pl.BlockSpec((d, tn), lambda i, j: (0, j)),
            pl.BlockSpec((d, LANES), lambda i, j: (0, 0)),
        ],
        out_specs=[pl.BlockSpec((tm, tn), lambda i, j: (i, jnp.minimum(j, na - 1))),
                   pl.BlockSpec((tm, tn), lambda i, j: (i, jnp.maximum(j - na, 0))),
                   pl.BlockSpec((tm, LANES), lambda i, j: (i, 0))],
        out_shape=[jax.ShapeDtypeStruct((rows, n_a), F32),
                   jax.ShapeDtypeStruct((rows, n - n_a), F32),
                   jax.ShapeDtypeStruct((rows, LANES), F32)],
        scratch_shapes=[pltpu.VMEM((tm, d), BF16)],
        compiler_params=_cparams(("parallel", "arbitrary")),
        name="norm_proj",
    )(x, g.reshape(1, d), w_main, w_side)


def _out_proj_kernel(x_ref, y_ref, w_ref, o_ref):
    o_ref[...] = x_ref[...] + _dot(y_ref[...], w_ref[...])


def _out_proj(x, y, w):
    rows, d = x.shape
    k = y.shape[1]
    tm = _row_tile(rows, 512)
    return pl.pallas_call(
        _out_proj_kernel,
        grid=(rows // tm,),
        in_specs=[pl.BlockSpec((tm, d), lambda i: (i, 0)),
                  pl.BlockSpec((tm, k), lambda i: (i, 0)),
                  pl.BlockSpec((k, d), lambda i: (0, 0))],
        out_specs=pl.BlockSpec((tm, d), lambda i: (i, 0)),
        out_shape=jax.ShapeDtypeStruct((rows, d), F32),
        compiler_params=_cparams(("parallel",)),
        name="out_proj",
    )(x, y, w)


def _sort_key(score):
    bits = pltpu.bitcast(score, I32)
    return bits ^ ((bits >> 31) & 0x7FFFFFFF)


def _tile_fold(x, op):
    r = x[0:SUBLANES]
    for t in range(1, x.shape[0] // SUBLANES):
        r = op(r, x[t * SUBLANES:(t + 1) * SUBLANES])
    return r


def _dsa_prompt_kernel(smax_ref, qit_ref, wt_ref, ki_ref, qt_ref, k_ref, vt_ref, o_ref,
                       key_ref, bias_ref, acc_ref, hi_ref, lo_ref, *, pad, topk):
    i = pl.program_id(1)
    nch = i + 1
    npair = (nch + 1) // 2
    last = ki_ref.shape[0] - 1
    row = lax.broadcasted_iota(I32, (BLK, BLK), 0)
    lane = lax.broadcasted_iota(I32, (BLK, BLK), 1)
    t_pos = i * BLK + lane

    def idx_dots(c):
        return _dot(ki_ref[jnp.minimum(c, last)], qit_ref[...])

    def idx_keys(c, dots):
        sc = jnp.zeros((BLK, BLK), F32)
        for h in range(N_IDX_HEADS):
            sc = sc + wt_ref[h:h + 1, :] * jnp.maximum(dots[:, h * BLK:(h + 1) * BLK], 0.0)
        s_pos = c * BLK + row
        valid = (s_pos <= t_pos) & (s_pos >= pad)
        key = jnp.where(valid, _sort_key(sc), INT_MIN)
        key_ref[c] = key
        hi_ref[c] = (key >> 16).astype(I16)
        lo_ref[c] = ((key & 0xFFFF) - HALF).astype(I16)

    def p1(j, carry):
        for u in range(2):
            idx_keys(2 * j + u, idx_dots(2 * j + u))
        return carry

    lax.fori_loop(0, npair, p1, 0)

    def count(pred):
        def body(j, cnt):
            for u in range(2):
                c = 2 * j + u
                cnt = cnt + jnp.where(pred(key_ref[c], c * BLK + row), 1, 0)
            return cnt
        cnt = lax.fori_loop(0, npair, body, jnp.zeros((BLK, BLK), I32))
        return jnp.sum(cnt, axis=0, keepdims=True)

    def count16(ref, pred):
        one, zero = jnp.int16(1), jnp.int16(0)

        def body(j, cnt):
            for u in range(2):
                cnt = cnt + jnp.where(pred(ref[2 * j + u]), one, zero)
            return cnt
        cnt = lax.fori_loop(0, npair, body, jnp.zeros((BLK, BLK), I16))
        return jnp.sum(cnt.astype(I32), axis=0, keepdims=True)

    def search16(ref, want):
        def at_least(cand):
            c16 = cand.astype(I16)
            return count16(ref, lambda v: v >= c16) >= want

        t = jnp.where(at_least(jnp.zeros((1, BLK), I32)), 0, -HALF).astype(I32)

        def bit_step(it, t):
            cand = t + (jnp.int32(1) << (14 - it))
            return jnp.where(at_least(cand), cand, t)

        return lax.fori_loop(0, 15, bit_step, t)

    t_hi = search16(hi_ref, topk)
    t_hi16 = t_hi.astype(I16)
    n_gt_hi = count16(hi_ref, lambda v: v > t_hi16)

    def mask_lo(j, carry):
        for u in range(2):
            c = 2 * j + u
            lo_ref[c] = jnp.where(hi_ref[c] == t_hi16, lo_ref[c], jnp.int16(-HALF))
        return carry

    lax.fori_loop(0, npair, mask_lo, 0)
    t_lo = search16(lo_ref, topk - n_gt_hi)
    t_lo16 = t_lo.astype(I16)
    thr = t_hi * (2 * HALF) + (t_lo + HALF)
    n_gt = n_gt_hi + count16(lo_ref, lambda v: v > t_lo16)
    n_ge = n_gt_hi + count16(lo_ref, lambda v: v >= t_lo16)
    need = topk - n_gt
    tied = (n_ge > topk) & (thr > INT_MIN)
    any_tied = jnp.max(tied.astype(I32))
    nbits = (key_ref.shape[0] * BLK).bit_length()

    def idx_search():
        def step(it, lo):
            cand = lo + (jnp.int32(1) << (nbits - 1 - it))
            below = count(lambda k, s: (k == thr) & (s < cand))
            return jnp.where(below < need, cand, lo)
        return lax.fori_loop(0, nbits, step, jnp.zeros((1, BLK), I32))

    jcut = lax.cond(any_tied > 0, idx_search,
                    lambda: jnp.full((1, BLK), 2 ** 30, I32))

    def p2(j, carry):
        for u in range(2):
            c = 2 * j + u
            k = key_ref[c]
            sel = (k > thr) | ((k == thr) & (c * BLK + row <= jcut))
            sel = sel & (k > INT_MIN)
            bias_ref[c] = jnp.where(sel, 0.0, NEG)
        return carry

    lax.fori_loop(0, npair, p2, 0)

    gw = HEADS_PER_KV * BLK

    def logits(c):
        bb = jnp.concatenate([bias_ref[c]] * HEADS_PER_KV, axis=1)
        cr = jnp.minimum(c, last)
        return [_dot(k_ref[g, cr], qt_ref[g]) + bb for g in range(N_KV_HEADS)]

    def attend(shift):
        acc_ref[...] = jnp.zeros_like(acc_ref)

        def body(j, carry):
            p = []
            for u in range(2):
                s = logits(2 * j + u)
                if shift is not None:
                    s = [s[g] - shift[g] for g in range(N_KV_HEADS)]
                p.append([jnp.exp(x).astype(BF16) for x in s])
            c1 = jnp.minimum(2 * j + 1, last)
            for g in range(N_KV_HEADS):
                vt = jnp.concatenate([vt_ref[2 * j, g], vt_ref[c1, g]], axis=1)
                pp = jnp.concatenate([p[0][g], p[1][g]], axis=0)
                acc_ref[g] += _dot(vt, pp)
            return carry

        lax.fori_loop(0, npair, body, 0)

    safe = smax_ref[0] <= SAFE_LOGIT

    @pl.when(safe)
    def _():
        attend(None)

    @pl.when(jnp.logical_not(safe))
    def _():
        def pa(j, m):
            for u in range(2):
                s = logits(2 * j + u)
                m = tuple(jnp.maximum(m[g], _tile_fold(s[g], jnp.maximum)) for g in range(N_KV_HEADS))
            return m

        m0 = tuple(jnp.full((SUBLANES, gw), NEG, F32) for _ in range(N_KV_HEADS))
        m = lax.fori_loop(0, npair, pa, m0)
        attend([jnp.max(x, axis=0, keepdims=True) for x in m])

    q_row = i * BLK + lax.broadcasted_iota(I32, (BLK, LANES), 0)
    for g in range(N_KV_HEADS):
        a = acc_ref[g]
        res = a[0:HEAD_DIM] * (1.0 / a[HEAD_DIM:HEAD_DIM + 1])
        for t in range(HEADS_PER_KV // 2):
            two = jnp.concatenate([res[:, (2 * t + u) * BLK:(2 * t + u + 1) * BLK] for u in range(2)],
                                  axis=0)
            two = jnp.where(q_row >= pad, two.T, 0.0)
            lo = (g * HEADS_PER_KV + 2 * t) * HEAD_DIM
            o_ref[:, lo:lo + LANES] = two.astype(BF16)


def _dsa_prompt(smax, qit, wt, kib, qt, kg, vt, pad, topk):
    bsz, nb = qit.shape[:2]
    t_pad = nb * BLK
    gw = HEADS_PER_KV * BLK
    kern = functools.partial(_dsa_prompt_kernel, pad=pad, topk=topk)
    return pl.pallas_call(
        kern,
        grid=(bsz, nb),
        in_specs=[
            pl.BlockSpec(memory_space=pltpu.SMEM),
            pl.BlockSpec((None, None, IDX_DIM, N_IDX_HEADS * BLK), lambda b, i: (b, i, 0, 0)),
            pl.BlockSpec((None, None, N_IDX_HEADS, BLK), lambda b, i: (b, i, 0, 0)),
            pl.BlockSpec((None, nb, BLK, IDX_DIM), lambda b, i: (b, 0, 0, 0)),
            pl.BlockSpec((None, None, N_KV_HEADS, HEAD_DIM, gw), lambda b, i: (b, i, 0, 0, 0)),
            pl.BlockSpec((N_KV_HEADS, None, nb, BLK, HEAD_DIM), lambda b, i: (0, b, 0, 0, 0)),
            pl.BlockSpec((None, nb, N_KV_HEADS, VT_ROWS, BLK), lambda b, i: (b, 0, 0, 0, 0)),
        ],
        out_specs=pl.BlockSpec((None, BLK, N_HEADS * HEAD_DIM), lambda b, i: (b, i, 0)),
        out_shape=jax.ShapeDtypeStruct((bsz, t_pad, N_HEADS * HEAD_DIM), BF16),
        scratch_shapes=[pltpu.VMEM((nb + 1, BLK, BLK), I32),
                        pltpu.VMEM((nb + 1, BLK, BLK), F32),
                        pltpu.VMEM((N_KV_HEADS, VT_ROWS, gw), F32),
                        pltpu.VMEM((nb + 1, BLK, BLK), I16),
                        pltpu.VMEM((nb + 1, BLK, BLK), I16)],
        compiler_params=_cparams(("parallel", "arbitrary")),
        name="dsa_prompt",
    )(smax, qit, wt, kib, qt, kg, vt)


def _smp_scores_kernel(pt_ref, qi_ref, w_ref, *refs, pg):
    page_refs, o_ref = refs[:pg], refs[pg]
    qi = qi_ref[...]
    w = w_ref[...]
    for p in range(pg):
        d = _dot(qi, page_refs[p][...].astype(BF16))
        o_ref[p:p + 1, :] = jnp.sum(w * jnp.maximum(d, 0.0), axis=0, keepdims=True)


def _smp_scores(page_table, qi, wi, kidx_t, pg):
    db, npg = page_table.shape
    kern = functools.partial(_smp_scores_kernel, pg=pg)
    page_spec = lambda p: pl.BlockSpec(
        (None, IDX_DIM, BLK), lambda b, j, pt: (pt[b * npg + j * pg + p], 0, 0))
    return pl.pallas_call(
        kern,
        grid_spec=pltpu.PrefetchScalarGridSpec(
            num_scalar_prefetch=1,
            grid=(db, npg // pg),
            in_specs=[pl.BlockSpec((None, N_IDX_HEADS, IDX_DIM), lambda b, j, pt: (b, 0, 0)),
                      pl.BlockSpec((None, N_IDX_HEADS, 1), lambda b, j, pt: (b, 0, 0))]
                     + [page_spec(p) for p in range(pg)],
            out_specs=pl.BlockSpec((None, pg, BLK), lambda b, j, pt: (b, j, 0)),
        ),
        out_shape=jax.ShapeDtypeStruct((db, npg, BLK), F32),
        compiler_params=_cparams(("parallel", "arbitrary")),
        name="sample_scores",
    )(page_table.reshape(-1), qi, wi, *([kidx_t] * pg))


def _smp_select_kernel(sc_ref, qi_ref, w_ref, kin_ref, bias_ref, nb_ref, *, topk, past):
    db, npg, _ = sc_ref.shape
    key = _sort_key(sc_ref[...])
    qi = qi_ref[...].astype(F32)
    kn = kin_ref[...].astype(BF16).astype(F32)
    d = jnp.sum(qi * kn, axis=-1, keepdims=True)
    s_new = jnp.sum(w_ref[...] * jnp.maximum(d, 0.0), axis=1, keepdims=True)
    key_new = _sort_key(s_new)
    pos = (lax.broadcasted_iota(I32, (db, npg, BLK), 1) * BLK
           + lax.broadcasted_iota(I32, (db, npg, BLK), 2))

    def count(pred_past, pred_new):
        c = jnp.sum(jnp.where(pred_past, 1, 0), axis=2, keepdims=True)
        return jnp.sum(c, axis=1, keepdims=True) + jnp.where(pred_new, 1, 0)

    thr = jnp.where(count(key >= 0, key_new >= 0) >= topk, 0, INT_MIN).astype(I32)

    def bit_step(it, thr):
        cand = thr + (jnp.int32(1) << (30 - it))
        return jnp.where(count(key >= cand, key_new >= cand) >= topk, cand, thr)

    thr = lax.fori_loop(0, 31, bit_step, thr)
    need = topk - count(key > thr, key_new > thr)
    nbits = max(1, past.bit_length())

    def step(it, lo):
        cand = lo + (jnp.int32(1) << (nbits - 1 - it))
        below = count((key == thr) & (pos < cand), (key_new == thr) & (past < cand))
        return jnp.where(below < need, cand, lo)

    jcut = lax.fori_loop(0, nbits, step, jnp.zeros((db, 1, 1), I32))
    sel = (key > thr) | ((key == thr) & (pos <= jcut))
    bias_ref[...] = jnp.where(sel, 0.0, NEG)
    sel_new = (key_new > thr) | ((key_new == thr) & (past <= jcut))
    nb_ref[...] = jnp.broadcast_to(jnp.where(sel_new, 0.0, NEG), nb_ref.shape)


def _smp_select(scores, qi, wi, ki_new, topk):
    db, npg, _ = scores.shape
    kern = functools.partial(_smp_select_kernel, topk=topk, past=npg * BLK)
    full = lambda *s: pl.BlockSpec(s, lambda i: (0,) * len(s))
    return pl.pallas_call(
        kern,
        grid=(1,),
        in_specs=[full(db, npg, BLK), full(db, N_IDX_HEADS, IDX_DIM), full(db, N_IDX_HEADS, 1),
                  full(db, 1, IDX_DIM)],
        out_specs=[full(db, npg, BLK), full(db, SUBLANES, LANES)],
        out_shape=[jax.ShapeDtypeStruct((db, npg, BLK), F32),
                   jax.ShapeDtypeStruct((db, SUBLANES, LANES), F32)],
        compiler_params=_cparams(("arbitrary",)),
        name="sample_select",
    )(scores, qi, wi, ki_new)


def _smp_attend_kernel(pt_ref, q_ref, bias_ref, nb_ref, kn_ref, vn_ref, *refs, pg):
    k_refs, v_refs = refs[:pg], refs[pg:2 * pg]
    o_ref, m_ref, l_ref, acc_ref = refs[2 * pg:]
    j = pl.program_id(1)
    kd = N_KV_HEADS * HEAD_DIM

    @pl.when(j == 0)
    def _():
        m_ref[...] = jnp.full_like(m_ref, NEG)
        l_ref[...] = jnp.zeros_like(l_ref)
        acc_ref[...] = jnp.zeros_like(acc_ref)

    q = q_ref[...]
    s = [_dot(q, k_refs[p][...].astype(BF16)) + bias_ref[p:p + 1, :] for p in range(pg)]
    m_old = m_ref[...]
    m_new = m_old
    for p in range(pg):
        m_new = jnp.maximum(m_new, jnp.max(s[p], axis=-1, keepdims=True))
    alpha = jnp.exp(m_old - m_new)
    l = l_ref[...] * alpha
    acc = acc_ref[...] * alpha
    for p in range(pg):
        e = jnp.exp(s[p] - m_new)
        l = l + jnp.sum(e, axis=-1, keepdims=True)
        acc = acc + _dot_t1(e.astype(BF16), v_refs[p][...].astype(BF16))
    m_ref[...] = m_new
    l_ref[...] = l
    acc_ref[...] = acc

    @pl.when(j == pl.num_programs(1) - 1)
    def _():
        qf = q.astype(F32)
        kn = kn_ref[...].astype(BF16).astype(F32)
        vn = vn_ref[...].astype(BF16).astype(F32)
        s_new = jnp.sum(qf * kn, axis=-1, keepdims=True) + nb_ref[0:1, 0:1]
        m_fin = jnp.maximum(m_new, s_new)
        a2 = jnp.exp(m_new - m_fin)
        e_new = jnp.exp(s_new - m_fin)
        l_fin = l * a2 + e_new
        acc_fin = acc * a2 + e_new.astype(BF16).astype(F32) * vn
        res = acc_fin / l_fin
        hgrp = lax.broadcasted_iota(I32, (N_HEADS, HEAD_DIM), 0) // HEADS_PER_KV
        out = jnp.zeros((N_HEADS, HEAD_DIM), F32)
        for g in range(N_KV_HEADS):
            out = out + jnp.where(hgrp == g, res[:, g * HEAD_DIM:(g + 1) * HEAD_DIM], 0.0)
        o_ref[...] = out


def _smp_attend(page_table, q_bd, bias, nbias, k_new, v_new, k_t, v_t, pg):
    db, npg = page_table.shape
    kd = N_KV_HEADS * HEAD_DIM
    kern = functools.partial(_smp_attend_kernel, pg=pg)
    page_spec = lambda p: pl.BlockSpec(
        (None, kd, BLK), lambda b, j, pt: (pt[b * npg + j * pg + p], 0, 0))
    per_b = lambda shape: pl.BlockSpec((None,) + shape, lambda b, j, pt: (b, 0, 0))
    return pl.pallas_call(
        kern,
        grid_spec=pltpu.PrefetchScalarGridSpec(
            num_scalar_prefetch=1,
            grid=(db, npg // pg),
            in_specs=[per_b((N_HEADS, kd)),
                      pl.BlockSpec((None, pg, BLK), lambda b, j, pt: (b, j, 0)),
                      per_b((SUBLANES, LANES)), per_b((1, kd)), per_b((1, kd))]
                     + [page_spec(p) for p in range(pg)] * 2,
            out_specs=per_b((N_HEADS, HEAD_DIM)),
            scratch_shapes=[pltpu.VMEM((N_HEADS, 1), F32), pltpu.VMEM((N_HEADS, 1), F32),
                            pltpu.VMEM((N_HEADS, kd), F32)],
        ),
        out_shape=jax.ShapeDtypeStruct((db, N_HEADS, HEAD_DIM), F32),
        compiler_params=_cparams(("parallel", "arbitrary")),
        name="sample_attend",
    )(page_table.reshape(-1), q_bd, bias, nbias, k_new, v_new,
      *([k_t] * pg), *([v_t] * pg))


def _ssd_prompt_kernel(z_ref, xbc_ref, dt_ref, dtt_ref, cw_ref, cb_ref, dtb_ref, a_ref,
                       dtbc_ref, ac_ref, dsk_ref, gn_ref, ltri_ref, utri_ref,
                       y_ref, st_ref, xpad_ref, h_ref, yb_ref, xt_ref, yt_ref, *, pad, n_heads):
    c = pl.program_id(1)
    d_in = n_heads * SSD_HEAD_DIM
    gn = SSD_GROUPS * D_STATE
    hpg = n_heads // SSD_GROUPS

    @pl.when(c == 0)
    def _():
        xpad_ref[0:SUBLANES, :] = jnp.zeros((SUBLANES, xpad_ref.shape[1]), F32)
        h_ref[...] = jnp.zeros_like(h_ref)

    xpad_ref[SUBLANES:, :] = xbc_ref[...]
    conv = cb_ref[...] + cw_ref[CONV_W - 1:CONV_W, :] * xbc_ref[...]
    for j in range(CONV_W - 1):
        sh = CONV_W - 1 - j
        conv = conv + cw_ref[j:j + 1, :] * xpad_ref[SUBLANES - sh:SUBLANES - sh + BLK, :]
    xpad_ref[0:SUBLANES, :] = xbc_ref[BLK - SUBLANES:BLK, :]
    xbc = _silu(conv)

    live = (c > 0) | (lax.broadcasted_iota(I32, (BLK, LANES), 0) >= pad)
    dt = jnp.where(live, _softplus(dt_ref[...] + dtb_ref[...]), 0.0)
    acum = _dot_sel(dt * a_ref[...], ltri_ref[...], fn=lambda x, m: _dot(m, x))
    live_t = (c > 0) | (lax.broadcasted_iota(I32, (dtt_ref.shape[0], BLK), 1) >= pad)
    dtt = jnp.where(live_t, _softplus(dtt_ref[...] + dtbc_ref[...]), 0.0)
    acum_t = _dot_sel(dtt * ac_ref[...], utri_ref[...])
    for t in range(d_in // LANES):
        xt_ref[t * LANES:(t + 1) * LANES, :] = xbc[:, t * LANES:(t + 1) * LANES].T
    a_last = acum_t[:, BLK - 1:BLK]
    ecol_t = jnp.exp(acum_t)
    decs_t = jnp.exp(a_last - acum_t)
    ea_last = jnp.exp(a_last)
    causal_t = (lax.broadcasted_iota(I32, (BLK, BLK), 0) <= lax.broadcasted_iota(I32, (BLK, BLK), 1))
    hp = SSD_HEAD_DIM

    for g in range(SSD_GROUPS):
        bm = xbc[:, d_in + g * D_STATE:d_in + (g + 1) * D_STATE].astype(BF16)
        ct = xbc[:, d_in + gn + g * D_STATE:d_in + gn + (g + 1) * D_STATE].T.astype(BF16)
        cb_t = _dot(bm, ct)
        hprev = h_ref[g * hpg:(g + 1) * hpg].reshape(hpg * hp, D_STATE)
        y_off = _dot(hprev.astype(BF16), ct)
        ws = []
        for hh in range(hpg):
            h = g * hpg + hh
            rows = slice(h * hp, (h + 1) * hp)
            xh = xt_ref[rows, :]
            xdt = xh * dtt[h:h + 1, :]
            decay_t = jnp.exp(jnp.where(causal_t, acum_t[h:h + 1, :] - acum[:, h:h + 1], NEG))
            y = _dot(xdt.astype(BF16), (cb_t * decay_t).astype(BF16))
            y = y + ecol_t[h:h + 1, :] * y_off[hh * hp:(hh + 1) * hp]
            yt_ref[rows, :] = y + dsk_ref[0:1, h:h + 1] * xh
            ws.append((xdt * decs_t[h:h + 1, :]).astype(BF16))
        upd = _dot(jnp.concatenate(ws, axis=0), bm)
        for hh in range(hpg):
            h = g * hpg + hh
            h_ref[h] = ea_last[h:h + 1, :] * hprev[hh * hp:(hh + 1) * hp] + upd[hh * hp:(hh + 1) * hp]

    for t in range(d_in // LANES):
        yb_ref[:, t * LANES:(t + 1) * LANES] = yt_ref[t * LANES:(t + 1) * LANES, :].T

    yg = yb_ref[...] * _silu(z_ref[...])
    gsz = d_in // SSD_GROUPS
    for g in range(SSD_GROUPS):
        blk = yg[:, g * gsz:(g + 1) * gsz]
        y_ref[:, g * gsz:(g + 1) * gsz] = _rms(blk, gn_ref[:, g * gsz:(g + 1) * gsz]).astype(BF16)

    @pl.when(c == pl.num_programs(1) - 1)
    def _():
        st_ref[...] = h_ref[...]


def _ssd_prompt(z_xbc, dt_raw, dt_raw_t, conv_w, conv_b, dt_bias, a_log, d_skip, gate_norm,
                bsz, nb, pad, n_heads):
    d_in = n_heads * SSD_HEAD_DIM
    cdim = d_in + 2 * SSD_GROUPS * D_STATE
    assert n_heads <= LANES
    hp = LANES - n_heads
    a = -jnp.exp(a_log.astype(F32))
    ltri = jnp.tril(jnp.ones((BLK, BLK), F32)).astype(BF16)
    kern = functools.partial(_ssd_prompt_kernel, pad=pad, n_heads=n_heads)
    const = lambda shape: pl.BlockSpec(shape, lambda b, c: (0,) * len(shape))
    zb = d_in // cdim if d_in % cdim == 0 else None
    return pl.pallas_call(
        kern,
        grid=(bsz, nb),
        in_specs=[
            pl.BlockSpec((BLK, d_in), lambda b, c: (b * nb + c, 0)),
            pl.BlockSpec((BLK, cdim), lambda b, c: (b * nb + c, 0)),
            pl.BlockSpec((BLK, LANES), lambda b, c: (b * nb + c, 0)),
            pl.BlockSpec((None, None, n_heads, BLK), lambda b, c: (b, c, 0, 0)),
            const((CONV_W, cdim)), const((1, cdim)), const((1, LANES)), const((1, LANES)),
            const((n_heads, 1)), const((n_heads, 1)), const((1, LANES)), const((1, d_in)),
            const((BLK, BLK)), const((BLK, BLK)),
        ],
        out_specs=[pl.BlockSpec((BLK, d_in), lambda b, c: (b * nb + c, 0)),
                   pl.BlockSpec((None, n_heads, SSD_HEAD_DIM, D_STATE), lambda b, c: (b, 0, 0, 0))],
        out_shape=[jax.ShapeDtypeStruct((bsz * nb * BLK, d_in), BF16),
                   jax.ShapeDtypeStruct((bsz, n_heads, SSD_HEAD_DIM, D_STATE), F32)],
        scratch_shapes=[pltpu.VMEM((BLK + SUBLANES, cdim), F32),
                        pltpu.VMEM((n_heads, SSD_HEAD_DIM, D_STATE), F32),
                        pltpu.VMEM((BLK, d_in), F32),
                        pltpu.VMEM((d_in, BLK), F32),
                        pltpu.VMEM((d_in, BLK), F32)],
        compiler_params=_cparams(("parallel", "arbitrary")),
        name="ssd_prompt",
    )(z_xbc[0], z_xbc[1], dt_raw, dt_raw_t, conv_w, conv_b.reshape(1, cdim),
      jnp.pad(dt_bias, (0, hp)).reshape(1, LANES), jnp.pad(a, (0, hp)).reshape(1, LANES),
      dt_bias.reshape(n_heads, 1), a.reshape(n_heads, 1),
      jnp.pad(d_skip, (0, hp)).reshape(1, LANES), gate_norm.reshape(1, d_in), ltri, ltri.T)


def _ssd_sample_kernel(z_ref, xbc_ref, dt_ref, cst_ref, h0_ref, cw_ref, cb_ref, dtb_ref, a_ref,
                       dsk_ref, gn_ref, exp_ref, y_ref, h_ref, *, n_heads):
    d_in = n_heads * SSD_HEAD_DIM
    gn = SSD_GROUPS * D_STATE
    gsz = d_in // SSD_GROUPS
    hpg = n_heads // SSD_GROUPS
    conv = cb_ref[...] + cw_ref[CONV_W - 1:CONV_W, :] * xbc_ref[...]
    for j in range(CONV_W - 1):
        conv = conv + cw_ref[j:j + 1, :] * cst_ref[j:j + 1, :]
    xbc = _silu(conv)
    dt = _softplus(dt_ref[...] + dtb_ref[...])
    pad8 = lambda r: jnp.concatenate([r, jnp.zeros((SUBLANES - 1, r.shape[1]), F32)], axis=0)
    dt_ch = _dot_sel(pad8(dt), exp_ref[...])[0:1]
    da_ch = jnp.exp(_dot_sel(pad8(dt * a_ref[...]), exp_ref[...])[0:1])
    dsk_ch = _dot_sel(pad8(dsk_ref[...]), exp_ref[...])[0:1]
    xh = xbc[:, :d_in]
    xdt = xh * dt_ch
    ones = jnp.ones((SUBLANES, D_STATE), BF16)
    outs = []
    for g in range(SSD_GROUPS):
        sl = slice(g * gsz, (g + 1) * gsz)
        bm = xbc[:, d_in + g * D_STATE:d_in + (g + 1) * D_STATE]
        cm = xbc[:, d_in + gn + g * D_STATE:d_in + gn + (g + 1) * D_STATE]
        da_col = _dot_sel(pad8(da_ch[:, sl]), ones, fn=_dot_t0)
        xdt_col = _dot_sel(pad8(xdt[:, sl]), ones, fn=_dot_t0)
        h0 = h0_ref[g * hpg:(g + 1) * hpg].reshape(gsz, D_STATE)
        hn = da_col * h0 + xdt_col.astype(BF16).astype(F32) * bm.astype(BF16).astype(F32)
        h_ref[g * hpg:(g + 1) * hpg] = hn.reshape(hpg, SSD_HEAD_DIM, D_STATE)
        y_col = jnp.sum(hn * cm, axis=-1, keepdims=True)
        outs.append(y_col)
    y_cols = jnp.concatenate(outs, axis=0)
    rows = []
    eye = (lax.broadcasted_iota(I32, (LANES, LANES), 0)
           == lax.broadcasted_iota(I32, (LANES, LANES), 1)).astype(F32)
    for t in range(d_in // LANES):
        blk = y_cols[t * LANES:(t + 1) * LANES]
        rows.append(jnp.sum(blk * eye, axis=0, keepdims=True))
    y = jnp.concatenate(rows, axis=1) + dsk_ch * xh
    y = y * _silu(z_ref[...])
    for g in range(SSD_GROUPS):
        sl = slice(g * gsz, (g + 1) * gsz)
        y_ref[:, sl] = _rms(y[:, sl], gn_ref[:, sl]).astype(BF16)


def _ssd_sample(z, xbc_raw, dt_raw, conv_state, h0, conv_w, conv_b, dt_bias, a_log, d_skip,
                gate_norm, n_heads):
    db = z.shape[0]
    d_in = n_heads * SSD_HEAD_DIM
    cdim = d_in + 2 * SSD_GROUPS * D_STATE
    hp = LANES - n_heads
    a = -jnp.exp(a_log.astype(F32))
    expand = (jnp.arange(LANES)[:, None] == (jnp.arange(d_in) // SSD_HEAD_DIM)[None, :]).astype(BF16)
    kern = functools.partial(_ssd_sample_kernel, n_heads=n_heads)
    const = lambda shape: pl.BlockSpec(shape, lambda b: (0,) * len(shape))
    per_b = lambda shape: pl.BlockSpec((None,) + shape, lambda b: (b,) + (0,) * len(shape))
    y, h = pl.pallas_call(
        kern,
        grid=(db,),
        in_specs=[per_b((1, d_in)), per_b((1, cdim)), per_b((1, LANES)), per_b((CONV_W - 1, cdim)),
                  per_b((n_heads, SSD_HEAD_DIM, D_STATE)),
                  const((CONV_W, cdim)), const((1, cdim)), const((1, LANES)), const((1, LANES)),
                  const((1, LANES)), const((1, d_in)), const((LANES, d_in))],
        out_specs=[per_b((1, d_in)), per_b((n_heads, SSD_HEAD_DIM, D_STATE))],
        out_shape=[jax.ShapeDtypeStruct((db, 1, d_in), BF16),
                   jax.ShapeDtypeStruct((db, n_heads, SSD_HEAD_DIM, D_STATE), F32)],
        compiler_params=_cparams(("parallel",)),
        name="ssd_sample",
    )(z.reshape(db, 1, d_in), xbc_raw.reshape(db, 1, cdim), dt_raw.reshape(db, 1, LANES),
      conv_state, h0, conv_w, conv_b.reshape(1, cdim),
      jnp.pad(dt_bias, (0, hp)).reshape(1, LANES), jnp.pad(a, (0, hp)).reshape(1, LANES),
      jnp.pad(d_skip, (0, hp)).reshape(1, LANES), gate_norm.reshape(1, d_in), expand)
    return y.reshape(db, d_in), h


def kernel(x_prompt, x_sample, cache_k, cache_v, cache_kidx, page_table, state_ssm, state_conv,
           meta_tokens, norm_ffn_a, w_ffn_a_in, w_ffn_a_out, norm_mix, norm_ffn_b, w_ffn_b_in,
           w_ffn_b_out, w_attn_in, q_norm, k_norm, kidx_norm, w_attn_out,
           w_ssd_in, conv_w, conv_b, dt_bias, a_log, d_skip, gate_norm, w_ssd_out):
    bsz, seq, d = x_prompt.shape
    db = x_sample.shape[0]
    assert x_sample.shape[1] == 1
    t_real = N_META + seq
    nb = -(-t_real // BLK)
    t_pad = nb * BLK
    pad = t_pad - t_real
    npg = page_table.shape[1]
    past = npg * BLK
    topk_p = min(TOPK_MAX, seq // 4)
    topk_s = min(TOPK_MAX, (past + 1) // 4)
    qd, kd, qid = N_HEADS * HEAD_DIM, N_KV_HEADS * HEAD_DIM, N_IDX_HEADS * IDX_DIM
    d_in = w_ssd_out.shape[1]
    n_heads = d_in // SSD_HEAD_DIM
    cdim = d_in + 2 * SSD_GROUPS * D_STATE
    depth = norm_mix.shape[0]

    meta = jnp.broadcast_to(meta_tokens.astype(F32)[None], (bsz, N_META, d))
    xp = jnp.concatenate([jnp.zeros((bsz, pad, d), F32), meta, x_prompt], axis=1)
    xp = xp.reshape(bsz * t_pad, d)
    xs = x_sample.reshape(db, d)

    outs = {k: [] for k in ("kp", "vp", "kip", "ks", "vs", "kis", "hp", "cp", "hs", "cs")}
    for i in range(depth):
        wts = _ffn_weights(w_ffn_a_in[i], w_ffn_a_out[i])
        xp = _ffn(xp, norm_ffn_a[i], wts)
        xs = _ffn(xs, norm_ffn_a[i], wts)
        j = i // 2
        if i % 2 == 0:
            w = w_attn_in[j]
            o1, o2 = qd + 2 * kd + qid, qd + 2 * kd + qid + IDX_DIM
            w_pad = jnp.concatenate(
                [w[:, :o1], jnp.pad(w[:, o1:o2], ((0, 0), (0, LANES - IDX_DIM))),
                 jnp.pad(w[:, o2:], ((0, 0), (0, LANES - N_IDX_HEADS)))], axis=1).astype(BF16)
            w_out = w_attn_out[j].astype(BF16)

            k, v, ki, qt, qit, wt, kg, vt, kib = _attn_in(
                xp, norm_mix[i], w_pad, q_norm[j], k_norm[j], kidx_norm[j], prompt=True)
            gw = HEADS_PER_KV * BLK
            smax = (1.02 * math.sqrt(HEAD_DIM) * jnp.max(jnp.abs(q_norm[j]))
                    * jnp.max(jnp.abs(k_norm[j]))).reshape(1).astype(F32)
            o = _dsa_prompt(smax,
                            qit.reshape(bsz, nb, IDX_DIM, N_IDX_HEADS * BLK),
                            wt.reshape(bsz, nb, N_IDX_HEADS, BLK),
                            kib.reshape(bsz, nb, BLK, IDX_DIM),
                            qt.reshape(bsz, nb, N_KV_HEADS, HEAD_DIM, gw),
                            kg.reshape(N_KV_HEADS, bsz, nb, BLK, HEAD_DIM),
                            vt.reshape(bsz, nb, N_KV_HEADS, VT_ROWS, BLK),
                            pad, topk_p)
            xp = _out_proj(xp, o.reshape(bsz * t_pad, qd), w_out)
            outs["kp"].append(k.reshape(bsz, t_pad, N_KV_HEADS, HEAD_DIM)[:, pad:])
            outs["vp"].append(v.reshape(bsz, t_pad, N_KV_HEADS, HEAD_DIM)[:, pad:])
            outs["kip"].append(ki[:, :IDX_DIM].reshape(bsz, t_pad, IDX_DIM)[:, pad:])

            q, k, v, qi, ki, wi = _attn_in(
                xs, norm_mix[i], w_pad, q_norm[j], k_norm[j], kidx_norm[j], prompt=False)
            pg = math.gcd(npg, 16)
            qi3 = qi.reshape(db, N_IDX_HEADS, IDX_DIM)
            wi3 = wi[:, :N_IDX_HEADS].reshape(db, N_IDX_HEADS, 1)
            scores = _smp_scores(page_table, qi3, wi3, cache_kidx[j].transpose(0, 2, 1), pg)
            bias, nbias = _smp_select(scores, qi3, wi3, ki[:, :IDX_DIM].reshape(db, 1, IDX_DIM), topk_s)
            hsel = (jnp.arange(N_HEADS)[:, None] // HEADS_PER_KV == jnp.arange(N_KV_HEADS)[None, :])
            q_bd = (q.reshape(db, N_HEADS, 1, HEAD_DIM) * hsel[None, :, :, None].astype(BF16))
            q_bd = q_bd.reshape(db, N_HEADS, kd)
            npool = cache_k.shape[1]
            o = _smp_attend(page_table, q_bd, bias, nbias, k.reshape(db, 1, kd), v.reshape(db, 1, kd),
                            cache_k[j].transpose(0, 2, 3, 1).reshape(npool, kd, BLK),
                            cache_v[j].transpose(0, 2, 3, 1).reshape(npool, kd, BLK), pg)
            xs = _out_proj(xs, o.reshape(db, qd).astype(BF16), w_out)
            outs["ks"].append(k.reshape(db, 1, N_KV_HEADS, HEAD_DIM))
            outs["vs"].append(v.reshape(db, 1, N_KV_HEADS, HEAD_DIM))
            outs["kis"].append(ki[:, :IDX_DIM].reshape(db, 1, IDX_DIM))
        else:
            w = w_ssd_in[j]
            w_main = w[:, :d_in + cdim].astype(BF16)
            w_dt = jnp.pad(w[:, d_in + cdim:], ((0, 0), (0, LANES - n_heads))).astype(BF16)
            w_out = w_ssd_out[j].astype(BF16)
            sp = (conv_w[j], conv_b[j], dt_bias[j], a_log[j], d_skip[j], gate_norm[j])

            z, xbc_raw, dt_raw = _norm_proj(xp, norm_mix[i], w_main, w_dt, d_in)
            dt_t = dt_raw[:, :n_heads].reshape(bsz, nb, BLK, n_heads).transpose(0, 1, 3, 2)
            y, hfin = _ssd_prompt((z, xbc_raw), dt_raw, dt_t, *sp, bsz, nb, pad, n_heads)
            xp = _out_proj(xp, y, w_out)
            outs["hp"].append(hfin)
            outs["cp"].append(xbc_raw.reshape(bsz, t_pad, cdim)[:, t_pad - (CONV_W - 1):])

            z, xbc_raw, dt_raw = _norm_proj(xs, norm_mix[i], w_main, w_dt, d_in)
            y, hnew = _ssd_sample(z, xbc_raw, dt_raw, state_conv[j], state_ssm[j], *sp, n_heads)
            xs = _out_proj(xs, y, w_out)
            outs["hs"].append(hnew)
            outs["cs"].append(jnp.concatenate([state_conv[j][:, 1:], xbc_raw[:, None, :]], axis=1))
        wts = _ffn_weights(w_ffn_b_in[i], w_ffn_b_out[i])
        xp = _ffn(xp, norm_ffn_b[i], wts)
        xs = _ffn(xs, norm_ffn_b[i], wts)

    y_prompt = xp.reshape(bsz, t_pad, d)[:, pad + N_META:]
    y_sample = xs.reshape(db, 1, d)
    st = lambda key: jnp.stack(outs[key])
    return (y_prompt, y_sample, st("kp"), st("vp"), st("kip"), st("ks"), st("vs"), st("kis"),
            st("hp"), st("cp"), st("hs"), st("cs"))
```

```python
import functools
import math

import jax
import jax.numpy as jnp
from jax import lax
from jax.experimental import pallas as pl
from jax.experimental.pallas import tpu as pltpu

F32 = jnp.float32
BF16 = jnp.bfloat16
I32 = jnp.int32

N_META = 16
N_HEADS = 16
HEAD_DIM = 64
N_KV_HEADS = 4
HEADS_PER_KV = N_HEADS // N_KV_HEADS
N_IDX_HEADS = 8
IDX_DIM = 64
TOPK_MAX = 256
SSD_HEAD_DIM = 64
SSD_GROUPS = 4
D_STATE = 128
CONV_W = 4
EPS = 1e-6

LANES = 128
SUBLANES = 8
BLK = 128
NEG = -1e30
SAFE_LOGIT = 40.0
VT_ROWS = 80
INT_MIN = -2 ** 31
VMEM_LIMIT = 56 * 1024 * 1024


def _cparams(sem, vmem=VMEM_LIMIT):
    return pltpu.CompilerParams(dimension_semantics=sem, vmem_limit_bytes=vmem)


def _row_tile(rows, pref):
    best = None
    for d in range(SUBLANES, min(rows, pref) + 1, SUBLANES):
        if rows % d == 0:
            best = d
    assert best is not None, rows
    return best


def _rms(x, g):
    var = jnp.mean(x * x, axis=-1, keepdims=True)
    return x * lax.rsqrt(var + EPS) * g


def _dot(a, b):
    return jnp.dot(a, b, preferred_element_type=F32)


def _dot_t0(a, b):
    return lax.dot_general(a, b, (((0,), (0,)), ((), ())), preferred_element_type=F32)


def _dot_t1(a, b):
    return lax.dot_general(a, b, (((1,), (1,)), ((), ())), preferred_element_type=F32)


def _split2(a):
    hi = a.astype(BF16)
    lo = (a - hi.astype(F32)).astype(BF16)
    return hi, lo


def _split3(a):
    a0 = a.astype(BF16)
    r = a - a0.astype(F32)
    a1 = r.astype(BF16)
    a2 = (r - a1.astype(F32)).astype(BF16)
    return a0, a1, a2


def _dot_sel(a, m, fn=_dot):
    a0, a1, a2 = _split3(a)
    return fn(a0, m) + fn(a1, m) + fn(a2, m)


def _silu(x):
    return x * (1.0 / (1.0 + jnp.exp(-x)))


def _softplus(x):
    return jnp.maximum(x, 0.0) + jnp.log(1.0 + jnp.exp(-jnp.abs(x)))


def _ffn_kernel(x_ref, g_ref, wa_ref, wb_ref, wo_ref, o_ref, xn_ref, acc_ref):
    xn_ref[...] = _rms(x_ref[...], g_ref[...]).astype(BF16)
    acc_ref[...] = jnp.zeros_like(acc_ref)

    def chunk(j, carry):
        xn = xn_ref[...]
        a = _dot(xn, wa_ref[j])
        b = _dot(xn, wb_ref[j])
        h = (_silu(a) * b).astype(BF16)
        acc_ref[...] += _dot(h, wo_ref[j])
        return carry

    lax.fori_loop(0, wa_ref.shape[0], chunk, 0)
    o_ref[...] = x_ref[...] + 0.5 * acc_ref[...]


def _ffn_weights(w_in, w_out):
    d, hid = w_in.shape[0], w_out.shape[0]
    th = 256 if hid % 256 == 0 else LANES
    nh = hid // th
    w_in = w_in.astype(BF16)
    wa = w_in[:, :hid].reshape(d, nh, th).transpose(1, 0, 2)
    wb = w_in[:, hid:].reshape(d, nh, th).transpose(1, 0, 2)
    return wa, wb, w_out.astype(BF16).reshape(nh, th, d)


def _ffn(x, g, weights):
    wa, wb, wo = weights
    rows, d = x.shape
    nh, _, th = wa.shape
    tm = _row_tile(rows, 1024)
    resident = lambda shape: pl.BlockSpec(shape, lambda i: (0,) * len(shape),
                                          pipeline_mode=pl.Buffered(1))
    return pl.pallas_call(
        _ffn_kernel,
        grid=(rows // tm,),
        in_specs=[
            pl.BlockSpec((tm, d), lambda i: (i, 0)),
            resident((1, d)), resident((nh, d, th)), resident((nh, d, th)), resident((nh, th, d)),
        ],
        out_specs=pl.BlockSpec((tm, d), lambda i: (i, 0)),
        out_shape=jax.ShapeDtypeStruct((rows, d), F32),
        scratch_shapes=[pltpu.VMEM((tm, d), BF16), pltpu.VMEM((tm, d), F32)],
        compiler_params=_cparams(("parallel",)),
        name="ffn",
    )(x, g.reshape(1, d), wa, wb, wo)


def _attn_project(x_ref, g_ref, w_ref, qg_ref, kg_ref, kig_ref, gq_ref, eq_ref, gk_ref, ek_ref):
    qd = N_HEADS * HEAD_DIM
    kd = N_KV_HEADS * HEAD_DIM
    qid = N_IDX_HEADS * IDX_DIM
    xn = _rms(x_ref[...], g_ref[...]).astype(BF16)
    h = _dot(xn, w_ref[...])

    def head_norm(t, gsum_ref, gexp_ref, gain):
        ss = _dot_sel(t * t, gsum_ref[...])
        rs = lax.rsqrt(ss * (1.0 / HEAD_DIM) + EPS)
        return t * _dot_sel(rs, gexp_ref[...]) * gain

    o = 0
    q = head_norm(h[:, o:o + qd], gq_ref, eq_ref, qg_ref[...]) * (HEAD_DIM ** -0.5)
    o += qd
    k = head_norm(h[:, o:o + kd], gk_ref, ek_ref, kg_ref[...])
    o += kd
    v = h[:, o:o + kd]
    o += kd
    qi = h[:, o:o + qid] * (IDX_DIM ** -0.5)
    o += qid
    ki = h[:, o:o + LANES]
    var = jnp.sum(ki * ki, axis=-1, keepdims=True) * (1.0 / IDX_DIM)
    ki = ki * lax.rsqrt(var + EPS) * kig_ref[...]
    o += LANES
    wi = h[:, o:o + LANES] * (N_IDX_HEADS ** -0.5)
    return q, k, v, qi, ki, wi


def _attn_in_sample_kernel(*refs):
    q_o, k_o, v_o, qi_o, ki_o, wi_o = refs[10:]
    q, k, v, qi, ki, wi = _attn_project(*refs[:10])
    q_o[...] = q.astype(BF16)
    k_o[...] = k
    v_o[...] = v
    qi_o[...] = qi.astype(BF16)
    ki_o[...] = ki
    wi_o[...] = wi


def _attn_in_prompt_kernel(*refs):
    k_o, v_o, ki_o, qt_o, qit_o, wt_o, kg_o, vt_o, kib_o = refs[10:]
    q, k, v, qi, ki, wi = _attn_project(*refs[:10])
    k_o[...] = k
    v_o[...] = v
    ki_o[...] = ki
    kib_o[...] = ki[:, :IDX_DIM].astype(BF16)
    for g in range(N_KV_HEADS):
        kg_o[g] = k[:, g * HEAD_DIM:(g + 1) * HEAD_DIM].astype(BF16)
    tail = (lax.broadcasted_iota(I32, (VT_ROWS - HEAD_DIM, BLK), 0) == 0).astype(BF16)
    for r in range(q.shape[0] // BLK):
        rows = slice(r * BLK, (r + 1) * BLK)
        for t in range(N_HEADS * HEAD_DIM // LANES):
            tt = q[rows, t * LANES:(t + 1) * LANES].T.astype(BF16)
            for u in range(LANES // HEAD_DIM):
                h = t * (LANES // HEAD_DIM) + u
                g, hh = h // HEADS_PER_KV, h % HEADS_PER_KV
                qt_o[r, g, :, hh * BLK:(hh + 1) * BLK] = tt[u * HEAD_DIM:(u + 1) * HEAD_DIM]
        for t in range(N_IDX_HEADS * IDX_DIM // LANES):
            tt = qi[rows, t * LANES:(t + 1) * LANES].T.astype(BF16)
            for u in range(LANES // IDX_DIM):
                h = t * (LANES // IDX_DIM) + u
                qit_o[r, :, h * BLK:(h + 1) * BLK] = tt[u * IDX_DIM:(u + 1) * IDX_DIM]
        wt_o[r] = wi[rows].T[0:N_IDX_HEADS]
        for t in range(N_KV_HEADS * HEAD_DIM // LANES):
            tt = v[rows, t * LANES:(t + 1) * LANES].T.astype(BF16)
            for u in range(LANES // HEAD_DIM):
                vt_o[r, t * (LANES // HEAD_DIM) + u] = jnp.concatenate(
                    [tt[u * HEAD_DIM:(u + 1) * HEAD_DIM], tail], axis=0)


def _seg_mats(n_heads, hd):
    col = jnp.arange(n_heads * hd) // hd
    gsum = (col[:, None] == jnp.arange(LANES)[None, :]).astype(BF16)
    return gsum, gsum.T


def _attn_in(x, g, w_pad, q_gain, k_gain, ki_gain, prompt):
    rows, d = x.shape
    n = w_pad.shape[1]
    qd, kd, qid = N_HEADS * HEAD_DIM, N_KV_HEADS * HEAD_DIM, N_IDX_HEADS * IDX_DIM
    gw = HEADS_PER_KV * BLK
    tm = _row_tile(rows, 512)
    gq, eq = _seg_mats(N_HEADS, HEAD_DIM)
    gk, ek = _seg_mats(N_KV_HEADS, HEAD_DIM)
    const = lambda shape: pl.BlockSpec(shape, lambda i: (0,) * len(shape))
    rowb = lambda w: pl.BlockSpec((tm, w), lambda i: (i, 0))
    sds = jax.ShapeDtypeStruct
    if prompt:
        assert tm % BLK == 0
        nbk, tb = rows // BLK, tm // BLK
        blkb = lambda *s: pl.BlockSpec((tb,) + s, lambda i: (i,) + (0,) * len(s))
        kern = _attn_in_prompt_kernel
        out_specs = [rowb(kd), rowb(kd), rowb(LANES), blkb(N_KV_HEADS, HEAD_DIM, gw),
                     blkb(IDX_DIM, N_IDX_HEADS * BLK), blkb(N_IDX_HEADS, BLK),
                     pl.BlockSpec((N_KV_HEADS, tm, HEAD_DIM), lambda i: (0, i, 0)),
                     blkb(N_KV_HEADS, VT_ROWS, BLK),
                     rowb(IDX_DIM)]
        out_shape = [sds((rows, kd), F32), sds((rows, kd), F32), sds((rows, LANES), F32),
                     sds((nbk, N_KV_HEADS, HEAD_DIM, gw), BF16),
                     sds((nbk, IDX_DIM, N_IDX_HEADS * BLK), BF16),
                     sds((nbk, N_IDX_HEADS, BLK), F32),
                     sds((N_KV_HEADS, rows, HEAD_DIM), BF16),
                     sds((nbk, N_KV_HEADS, VT_ROWS, BLK), BF16),
                     sds((rows, IDX_DIM), BF16)]
    else:
        kern = _attn_in_sample_kernel
        out_specs = [rowb(qd), rowb(kd), rowb(kd), rowb(qid), rowb(LANES), rowb(LANES)]
        out_shape = [sds((rows, qd), BF16), sds((rows, kd), F32), sds((rows, kd), F32),
                     sds((rows, qid), BF16), sds((rows, LANES), F32), sds((rows, LANES), F32)]
    return pl.pallas_call(
        kern,
        grid=(rows // tm,),
        in_specs=[rowb(d), const((1, d)), const((d, n)), const((1, qd)), const((1, kd)),
                  const((1, LANES)), const((qd, LANES)), const((LANES, qd)),
                  const((kd, LANES)), const((LANES, kd))],
        out_specs=out_specs,
        out_shape=out_shape,
        compiler_params=_cparams(("parallel",)),
        name="attn_in_prompt" if prompt else "attn_in_sample",
    )(x, g.reshape(1, d), w_pad,
      jnp.tile(q_gain, N_HEADS).reshape(1, qd), jnp.tile(k_gain, N_KV_HEADS).reshape(1, kd),
      jnp.pad(ki_gain, (0, LANES - IDX_DIM)).reshape(1, LANES), gq, eq, gk, ek)


def _norm_proj_kernel(x_ref, g_ref, w_ref, ws_ref, oa_ref, ob_ref, os_ref, xn_ref, *, na):
    j = pl.program_id(1)

    @pl.when(j == 0)
    def _():
        xn = _rms(x_ref[...], g_ref[...]).astype(BF16)
        xn_ref[...] = xn
        os_ref[...] = _dot(xn, ws_ref[...])

    r = _dot(xn_ref[...], w_ref[...])

    @pl.when(j < na)
    def _():
        oa_ref[...] = r

    @pl.when(j >= na)
    def _():
        ob_ref[...] = r


def _norm_proj(x, g, w_main, w_side, n_a):
    rows, d = x.shape
    n = w_main.shape[1]
    tm = _row_tile(rows, 1024)
    tn = math.gcd(math.gcd(n_a, n - n_a), 1024)
    na = n_a // tn
    return pl.pallas_call(
        functools.partial(_norm_proj_kernel, na=na),
        grid=(rows // tm, n // tn),
        in_specs=[
            pl.BlockSpec((tm, d), lambda i, j: (i, 0)),
            pl.BlockSpec((1, d), lambda i, j: (0, 0)),
            pl.BlockSpec((d, tn), lambda i, j: (0, j)),
            pl.BlockSpec((d, LANES), lambda i, j: (0, 0)),
        ],
        out_specs=[pl.BlockSpec((tm, tn), lambda i, j: (i, jnp.minimum(j, na - 1))),
                   pl.BlockSpec((tm, tn), lambda i, j: (i, jnp.maximum(j - na, 0))),
                   pl.BlockSpec((tm, LANES), lambda i, j: (i, 0))],
        out_shape=[jax.ShapeDtypeStruct((rows, n_a), F32),
                   jax.ShapeDtypeStruct((rows, n - n_a), F32),
                   jax.ShapeDtypeStruct((rows, LANES), F32)],
        scratch_shapes=[pltpu.VMEM((tm, d), BF16)],
        compiler_params=_cparams(("parallel", "arbitrary")),
        name="norm_proj",
    )(x, g.reshape(1, d), w_main, w_side)


def _out_proj_kernel(x_ref, y_ref, w_ref, o_ref):
    o_ref[...] = x_ref[...] + _dot(y_ref[...], w_ref[...])


def _out_proj(x, y, w):
    rows, d = x.shape
    k = y.shape[1]
    tm = _row_tile(rows, 512)
    return pl.pallas_call(
        _out_proj_kernel,
        grid=(rows // tm,),
        in_specs=[pl.BlockSpec((tm, d), lambda i: (i, 0)),
                  pl.BlockSpec((tm, k), lambda i: (i, 0)),
                  pl.BlockSpec((k, d), lambda i: (0, 0))],
        out_specs=pl.BlockSpec((tm, d), lambda i: (i, 0)),
        out_shape=jax.ShapeDtypeStruct((rows, d), F32),
        compiler_params=_cparams(("parallel",)),
        name="out_proj",
    )(x, y, w)


def _sort_key(score):
    bits = pltpu.bitcast(score, I32)
    return bits ^ ((bits >> 31) & 0x7FFFFFFF)


def _tile_fold(x, op):
    r = x[0:SUBLANES]
    for t in range(1, x.shape[0] // SUBLANES):
        r = op(r, x[t * SUBLANES:(t + 1) * SUBLANES])
    return r


def _dsa_prompt_kernel(smax_ref, qit_ref, wt_ref, ki_ref, qt_ref, k_ref, vt_ref, o_ref,
                       key_ref, bias_ref, acc_ref, *, pad, topk):
    i = pl.program_id(1)
    nch = i + 1
    npair = (nch + 1) // 2
    last = ki_ref.shape[0] - 1
    row = lax.broadcasted_iota(I32, (BLK, BLK), 0)
    lane = lax.broadcasted_iota(I32, (BLK, BLK), 1)
    t_pos = i * BLK + lane

    def idx_dots(c):
        return _dot(ki_ref[jnp.minimum(c, last)], qit_ref[...])

    def idx_keys(c, dots):
        sc = jnp.zeros((BLK, BLK), F32)
        for h in range(N_IDX_HEADS):
            sc = sc + wt_ref[h:h + 1, :] * jnp.maximum(dots[:, h * BLK:(h + 1) * BLK], 0.0)
        s_pos = c * BLK + row
        valid = (s_pos <= t_pos) & (s_pos >= pad)
        key_ref[c] = jnp.where(valid, _sort_key(sc), INT_MIN)

    def p1(j, carry):
        for u in range(2):
            idx_keys(2 * j + u, idx_dots(2 * j + u))
        return carry

    lax.fori_loop(0, npair, p1, 0)

    def count(pred):
        def body(j, cnt):
            for u in range(2):
                c = 2 * j + u
                cnt = cnt + jnp.where(pred(key_ref[c], c * BLK + row), 1, 0)
            return cnt
        cnt = lax.fori_loop(0, npair, body, jnp.zeros((BLK, BLK), I32))
        return jnp.sum(cnt, axis=0, keepdims=True)

    thr = jnp.where(count(lambda k, s: k >= 0) >= topk, 0, INT_MIN).astype(I32)

    def bit_step(it, thr):
        cand = thr + (jnp.int32(1) << (30 - it))
        return jnp.where(count(lambda k, s: k >= cand) >= topk, cand, thr)

    thr = lax.fori_loop(0, 31, bit_step, thr)
    n_gt = count(lambda k, s: k > thr)
    n_ge = count(lambda k, s: k >= thr)
    need = topk - n_gt
    tied = (n_ge > topk) & (thr > INT_MIN)
    any_tied = jnp.max(tied.astype(I32))
    nbits = (key_ref.shape[0] * BLK).bit_length()

    def idx_search():
        def step(it, lo):
            cand = lo + (jnp.int32(1) << (nbits - 1 - it))
            below = count(lambda k, s: (k == thr) & (s < cand))
            return jnp.where(below < need, cand, lo)
        return lax.fori_loop(0, nbits, step, jnp.zeros((1, BLK), I32))

    jcut = lax.cond(any_tied > 0, idx_search,
                    lambda: jnp.full((1, BLK), 2 ** 30, I32))

    def p2(j, carry):
        for u in range(2):
            c = 2 * j + u
            k = key_ref[c]
            sel = (k > thr) | ((k == thr) & (c * BLK + row <= jcut))
            sel = sel & (k > INT_MIN)
            bias_ref[c] = jnp.where(sel, 0.0, NEG)
        return carry

    lax.fori_loop(0, npair, p2, 0)

    gw = HEADS_PER_KV * BLK

    def logits(c):
        bb = jnp.concatenate([bias_ref[c]] * HEADS_PER_KV, axis=1)
        cr = jnp.minimum(c, last)
        return [_dot(k_ref[g, cr], qt_ref[g]) + bb for g in range(N_KV_HEADS)]

    def attend(shift):
        acc_ref[...] = jnp.zeros_like(acc_ref)

        def body(j, carry):
            p = []
            for u in range(2):
                s = logits(2 * j + u)
                if shift is not None:
                    s = [s[g] - shift[g] for g in range(N_KV_HEADS)]
                p.append([jnp.exp(x).astype(BF16) for x in s])
            c1 = jnp.minimum(2 * j + 1, last)
            for g in range(N_KV_HEADS):
                vt = jnp.concatenate([vt_ref[2 * j, g], vt_ref[c1, g]], axis=1)
                pp = jnp.concatenate([p[0][g], p[1][g]], axis=0)
                acc_ref[g] += _dot(vt, pp)
            return carry

        lax.fori_loop(0, npair, body, 0)

    safe = smax_ref[0] <= SAFE_LOGIT

    @pl.when(safe)
    def _():
        attend(None)

    @pl.when(jnp.logical_not(safe))
    def _():
        def pa(j, m):
            for u in range(2):
                s = logits(2 * j + u)
                m = tuple(jnp.maximum(m[g], _tile_fold(s[g], jnp.maximum)) for g in range(N_KV_HEADS))
            return m

        m0 = tuple(jnp.full((SUBLANES, gw), NEG, F32) for _ in range(N_KV_HEADS))
        m = lax.fori_loop(0, npair, pa, m0)
        attend([jnp.max(x, axis=0, keepdims=True) for x in m])

    q_row = i * BLK + lax.broadcasted_iota(I32, (BLK, LANES), 0)
    for g in range(N_KV_HEADS):
        a = acc_ref[g]
        res = a[0:HEAD_DIM] * (1.0 / a[HEAD_DIM:HEAD_DIM + 1])
        for t in range(HEADS_PER_KV // 2):
            two = jnp.concatenate([res[:, (2 * t + u) * BLK:(2 * t + u + 1) * BLK] for u in range(2)],
                                  axis=0)
            two = jnp.where(q_row >= pad, two.T, 0.0)
            lo = (g * HEADS_PER_KV + 2 * t) * HEAD_DIM
            o_ref[:, lo:lo + LANES] = two.astype(BF16)


def _dsa_prompt(smax, qit, wt, kib, qt, kg, vt, pad, topk):
    bsz, nb = qit.shape[:2]
    t_pad = nb * BLK
    gw = HEADS_PER_KV * BLK
    kern = functools.partial(_dsa_prompt_kernel, pad=pad, topk=topk)
    return pl.pallas_call(
        kern,
        grid=(bsz, nb),
        in_specs=[
            pl.BlockSpec(memory_space=pltpu.SMEM),
            pl.BlockSpec((None, None, IDX_DIM, N_IDX_HEADS * BLK), lambda b, i: (b, i, 0, 0)),
            pl.BlockSpec((None, None, N_IDX_HEADS, BLK), lambda b, i: (b, i, 0, 0)),
            pl.BlockSpec((None, nb, BLK, IDX_DIM), lambda b, i: (b, 0, 0, 0)),
            pl.BlockSpec((None, None, N_KV_HEADS, HEAD_DIM, gw), lambda b, i: (b, i, 0, 0, 0)),
            pl.BlockSpec((N_KV_HEADS, None, nb, BLK, HEAD_DIM), lambda b, i: (0, b, 0, 0, 0)),
            pl.BlockSpec((None, nb, N_KV_HEADS, VT_ROWS, BLK), lambda b, i: (b, 0, 0, 0, 0)),
        ],
        out_specs=pl.BlockSpec((None, BLK, N_HEADS * HEAD_DIM), lambda b, i: (b, i, 0)),
        out_shape=jax.ShapeDtypeStruct((bsz, t_pad, N_HEADS * HEAD_DIM), BF16),
        scratch_shapes=[pltpu.VMEM((nb + 1, BLK, BLK), I32),
                        pltpu.VMEM((nb + 1, BLK, BLK), F32),
                        pltpu.VMEM((N_KV_HEADS, VT_ROWS, gw), F32)],
        compiler_params=_cparams(("parallel", "arbitrary")),
        name="dsa_prompt",
    )(smax, qit, wt, kib, qt, kg, vt)


def _smp_scores_kernel(pt_ref, qi_ref, w_ref, *refs, pg):
    page_refs, o_ref = refs[:pg], refs[pg]
    qi = qi_ref[...]
    w = w_ref[...]
    for p in range(pg):
        d = _dot(qi, page_refs[p][...].astype(BF16))
        o_ref[p:p + 1, :] = jnp.sum(w * jnp.maximum(d, 0.0), axis=0, keepdims=True)


def _smp_scores(page_table, qi, wi, kidx_t, pg):
    db, npg = page_table.shape
    kern = functools.partial(_smp_scores_kernel, pg=pg)
    page_spec = lambda p: pl.BlockSpec(
        (None, IDX_DIM, BLK), lambda b, j, pt: (pt[b * npg + j * pg + p], 0, 0))
    return pl.pallas_call(
        kern,
        grid_spec=pltpu.PrefetchScalarGridSpec(
            num_scalar_prefetch=1,
            grid=(db, npg // pg),
            in_specs=[pl.BlockSpec((None, N_IDX_HEADS, IDX_DIM), lambda b, j, pt: (b, 0, 0)),
                      pl.BlockSpec((None, N_IDX_HEADS, 1), lambda b, j, pt: (b, 0, 0))]
                     + [page_spec(p) for p in range(pg)],
            out_specs=pl.BlockSpec((None, pg, BLK), lambda b, j, pt: (b, j, 0)),
        ),
        out_shape=jax.ShapeDtypeStruct((db, npg, BLK), F32),
        compiler_params=_cparams(("parallel", "arbitrary")),
        name="sample_scores",
    )(page_table.reshape(-1), qi, wi, *([kidx_t] * pg))


def _smp_select_kernel(sc_ref, qi_ref, w_ref, kin_ref, bias_ref, nb_ref, *, topk, past):
    db, npg, _ = sc_ref.shape
    key = _sort_key(sc_ref[...])
    qi = qi_ref[...].astype(F32)
    kn = kin_ref[...].astype(BF16).astype(F32)
    d = jnp.sum(qi * kn, axis=-1, keepdims=True)
    s_new = jnp.sum(w_ref[...] * jnp.maximum(d, 0.0), axis=1, keepdims=True)
    key_new = _sort_key(s_new)
    pos = (lax.broadcasted_iota(I32, (db, npg, BLK), 1) * BLK
           + lax.broadcasted_iota(I32, (db, npg, BLK), 2))

    def count(pred_past, pred_new):
        c = jnp.sum(jnp.where(pred_past, 1, 0), axis=2, keepdims=True)
        return jnp.sum(c, axis=1, keepdims=True) + jnp.where(pred_new, 1, 0)

    thr = jnp.where(count(key >= 0, key_new >= 0) >= topk, 0, INT_MIN).astype(I32)

    def bit_step(it, thr):
        cand = thr + (jnp.int32(1) << (30 - it))
        return jnp.where(count(key >= cand, key_new >= cand) >= topk, cand, thr)

    thr = lax.fori_loop(0, 31, bit_step, thr)
    need = topk - count(key > thr, key_new > thr)
    nbits = max(1, past.bit_length())

    def step(it, lo):
        cand = lo + (jnp.int32(1) << (nbits - 1 - it))
        below = count((key == thr) & (pos < cand), (key_new == thr) & (past < cand))
        return jnp.where(below < need, cand, lo)

    jcut = lax.fori_loop(0, nbits, step, jnp.zeros((db, 1, 1), I32))
    sel = (key > thr) | ((key == thr) & (pos <= jcut))
    bias_ref[...] = jnp.where(sel, 0.0, NEG)
    sel_new = (key_new > thr) | ((key_new == thr) & (past <= jcut))
    nb_ref[...] = jnp.broadcast_to(jnp.where(sel_new, 0.0, NEG), nb_ref.shape)


def _smp_select(scores, qi, wi, ki_new, topk):
    db, npg, _ = scores.shape
    kern = functools.partial(_smp_select_kernel, topk=topk, past=npg * BLK)
    full = lambda *s: pl.BlockSpec(s, lambda i: (0,) * len(s))
    return pl.pallas_call(
        kern,
        grid=(1,),
        in_specs=[full(db, npg, BLK), full(db, N_IDX_HEADS, IDX_DIM), full(db, N_IDX_HEADS, 1),
                  full(db, 1, IDX_DIM)],
        out_specs=[full(db, npg, BLK), full(db, SUBLANES, LANES)],
        out_shape=[jax.ShapeDtypeStruct((db, npg, BLK), F32),
                   jax.ShapeDtypeStruct((db, SUBLANES, LANES), F32)],
        compiler_params=_cparams(("arbitrary",)),
        name="sample_select",
    )(scores, qi, wi, ki_new)


def _smp_attend_kernel(pt_ref, q_ref, bias_ref, nb_ref, kn_ref, vn_ref, *refs, pg):
    k_refs, v_refs = refs[:pg], refs[pg:2 * pg]
    o_ref, m_ref, l_ref, acc_ref = refs[2 * pg:]
    j = pl.program_id(1)
    kd = N_KV_HEADS * HEAD_DIM

    @pl.when(j == 0)
    def _():
        m_ref[...] = jnp.full_like(m_ref, NEG)
        l_ref[...] = jnp.zeros_like(l_ref)
        acc_ref[...] = jnp.zeros_like(acc_ref)

    q = q_ref[...]
    s = [_dot(q, k_refs[p][...].astype(BF16)) + bias_ref[p:p + 1, :] for p in range(pg)]
    m_old = m_ref[...]
    m_new = m_old
    for p in range(pg):
        m_new = jnp.maximum(m_new, jnp.max(s[p], axis=-1, keepdims=True))
    alpha = jnp.exp(m_old - m_new)
    l = l_ref[...] * alpha
    acc = acc_ref[...] * alpha
    for p in range(pg):
        e = jnp.exp(s[p] - m_new)
        l = l + jnp.sum(e, axis=-1, keepdims=True)
        acc = acc + _dot_t1(e.astype(BF16), v_refs[p][...].astype(BF16))
    m_ref[...] = m_new
    l_ref[...] = l
    acc_ref[...] = acc

    @pl.when(j == pl.num_programs(1) - 1)
    def _():
        qf = q.astype(F32)
        kn = kn_ref[...].astype(BF16).astype(F32)
        vn = vn_ref[...].astype(BF16).astype(F32)
        s_new = jnp.sum(qf * kn, axis=-1, keepdims=True) + nb_ref[0:1, 0:1]
        m_fin = jnp.maximum(m_new, s_new)
        a2 = jnp.exp(m_new - m_fin)
        e_new = jnp.exp(s_new - m_fin)
        l_fin = l * a2 + e_new
        acc_fin = acc * a2 + e_new.astype(BF16).astype(F32) * vn
        res = acc_fin / l_fin
        hgrp = lax.broadcasted_iota(I32, (N_HEADS, HEAD_DIM), 0) // HEADS_PER_KV
        out = jnp.zeros((N_HEADS, HEAD_DIM), F32)
        for g in range(N_KV_HEADS):
            out = out + jnp.where(hgrp == g, res[:, g * HEAD_DIM:(g + 1) * HEAD_DIM], 0.0)
        o_ref[...] = out


def _smp_attend(page_table, q_bd, bias, nbias, k_new, v_new, k_t, v_t, pg):
    db, npg = page_table.shape
    kd = N_KV_HEADS * HEAD_DIM
    kern = functools.partial(_smp_attend_kernel, pg=pg)
    page_spec = lambda p: pl.BlockSpec(
        (None, kd, BLK), lambda b, j, pt: (pt[b * npg + j * pg + p], 0, 0))
    per_b = lambda shape: pl.BlockSpec((None,) + shape, lambda b, j, pt: (b, 0, 0))
    return pl.pallas_call(
        kern,
        grid_spec=pltpu.PrefetchScalarGridSpec(
            num_scalar_prefetch=1,
            grid=(db, npg // pg),
            in_specs=[per_b((N_HEADS, kd)),
                      pl.BlockSpec((None, pg, BLK), lambda b, j, pt: (b, j, 0)),
                      per_b((SUBLANES, LANES)), per_b((1, kd)), per_b((1, kd))]
                     + [page_spec(p) for p in range(pg)] * 2,
            out_specs=per_b((N_HEADS, HEAD_DIM)),
            scratch_shapes=[pltpu.VMEM((N_HEADS, 1), F32), pltpu.VMEM((N_HEADS, 1), F32),
                            pltpu.VMEM((N_HEADS, kd), F32)],
        ),
        out_shape=jax.ShapeDtypeStruct((db, N_HEADS, HEAD_DIM), F32),
        compiler_params=_cparams(("parallel", "arbitrary")),
        name="sample_attend",
    )(page_table.reshape(-1), q_bd, bias, nbias, k_new, v_new,
      *([k_t] * pg), *([v_t] * pg))


def _ssd_prompt_kernel(x_ref, g_ref, wz_ref, wx_ref, wdt_ref, wo_ref,
                       cw_ref, cb_ref, dtb_ref, a_ref,
                       dtbc_ref, ac_ref, dsk_ref, gn_ref, ltri_ref, utri_ref,
                       o_ref, st_ref, cv_ref, xpad_ref, h_ref, yb_ref, xt_ref, yt_ref, z_ref,
                       *, pad, n_heads):
    c = pl.program_id(1)
    d_in = n_heads * SSD_HEAD_DIM
    gn = SSD_GROUPS * D_STATE
    hpg = n_heads // SSD_GROUPS

    @pl.when(c == 0)
    def _():
        xpad_ref[0:SUBLANES, :] = jnp.zeros((SUBLANES, xpad_ref.shape[1]), F32)
        h_ref[...] = jnp.zeros_like(h_ref)

    xn = _rms(x_ref[...], g_ref[...]).astype(BF16)
    z_ref[...] = _dot(xn, wz_ref[...])
    dt_raw = _dot(xn, wdt_ref[...])
    dt_raw_t = dt_raw.T[0:n_heads]

    xpad_ref[SUBLANES:, :] = _dot(xn, wx_ref[...])
    conv = cb_ref[...] + cw_ref[CONV_W - 1:CONV_W, :] * xpad_ref[SUBLANES:, :]
    for j in range(CONV_W - 1):
        sh = CONV_W - 1 - j
        conv = conv + cw_ref[j:j + 1, :] * xpad_ref[SUBLANES - sh:SUBLANES - sh + BLK, :]
    xpad_ref[0:SUBLANES, :] = xpad_ref[BLK:BLK + SUBLANES, :]
    xbc = _silu(conv)

    live = (c > 0) | (lax.broadcasted_iota(I32, (BLK, LANES), 0) >= pad)
    dt = jnp.where(live, _softplus(dt_raw + dtb_ref[...]), 0.0)
    acum = _dot_sel(dt * a_ref[...], ltri_ref[...], fn=lambda x, m: _dot(m, x))
    live_t = (c > 0) | (lax.broadcasted_iota(I32, (n_heads, BLK), 1) >= pad)
    dtt = jnp.where(live_t, _softplus(dt_raw_t + dtbc_ref[...]), 0.0)
    acum_t = _dot_sel(dtt * ac_ref[...], utri_ref[...])
    for t in range(d_in // LANES):
        xt_ref[t * LANES:(t + 1) * LANES, :] = xbc[:, t * LANES:(t + 1) * LANES].T
    a_last = acum_t[:, BLK - 1:BLK]
    ecol_t = jnp.exp(acum_t)
    decs_t = jnp.exp(a_last - acum_t)
    ea_last = jnp.exp(a_last)
    causal_t = (lax.broadcasted_iota(I32, (BLK, BLK), 0) <= lax.broadcasted_iota(I32, (BLK, BLK), 1))
    hp = SSD_HEAD_DIM

    for g in range(SSD_GROUPS):
        bm = xbc[:, d_in + g * D_STATE:d_in + (g + 1) * D_STATE].astype(BF16)
        ct = xbc[:, d_in + gn + g * D_STATE:d_in + gn + (g + 1) * D_STATE].T.astype(BF16)
        cb_t = _dot(bm, ct)
        hprev = h_ref[g * hpg:(g + 1) * hpg].reshape(hpg * hp, D_STATE)
        y_off = _dot(hprev.astype(BF16), ct)
        ws = []
        for hh in range(hpg):
            h = g * hpg + hh
            rows = slice(h * hp, (h + 1) * hp)
            xh = xt_ref[rows, :]
            xdt = xh * dtt[h:h + 1, :]
            decay_t = jnp.exp(jnp.where(causal_t, acum_t[h:h + 1, :] - acum[:, h:h + 1], NEG))
            y = _dot(xdt.astype(BF16), (cb_t * decay_t).astype(BF16))
            y = y + ecol_t[h:h + 1, :] * y_off[hh * hp:(hh + 1) * hp]
            yt_ref[rows, :] = y + dsk_ref[0:1, h:h + 1] * xh
            ws.append((xdt * decs_t[h:h + 1, :]).astype(BF16))
        upd = _dot(jnp.concatenate(ws, axis=0), bm)
        for hh in range(hpg):
            h = g * hpg + hh
            h_ref[h] = ea_last[h:h + 1, :] * hprev[hh * hp:(hh + 1) * hp] + upd[hh * hp:(hh + 1) * hp]

    for t in range(d_in // LANES):
        yb_ref[:, t * LANES:(t + 1) * LANES] = yt_ref[t * LANES:(t + 1) * LANES, :].T

    yg = yb_ref[...] * _silu(z_ref[...])
    gsz = d_in // SSD_GROUPS
    yn = jnp.concatenate(
        [_rms(yg[:, g * gsz:(g + 1) * gsz], gn_ref[:, g * gsz:(g + 1) * gsz]).astype(BF16)
         for g in range(SSD_GROUPS)], axis=1)
    o_ref[...] = x_ref[...] + _dot(yn, wo_ref[...])

    @pl.when(c == pl.num_programs(1) - 1)
    def _():
        st_ref[...] = h_ref[...]
        cv_ref[...] = xpad_ref[0:SUBLANES, :]


def _ssd_prompt(x, g, w_z, w_xbc, w_dt, w_out, conv_w, conv_b, dt_bias, a_log, d_skip, gate_norm,
                bsz, nb, pad, n_heads):
    d = x.shape[1]
    d_in = n_heads * SSD_HEAD_DIM
    cdim = d_in + 2 * SSD_GROUPS * D_STATE
    assert n_heads <= LANES
    hp = LANES - n_heads
    a = -jnp.exp(a_log.astype(F32))
    ltri = jnp.tril(jnp.ones((BLK, BLK), F32)).astype(BF16)
    kern = functools.partial(_ssd_prompt_kernel, pad=pad, n_heads=n_heads)
    const = lambda shape: pl.BlockSpec(shape, lambda b, c: (0,) * len(shape),
                                       pipeline_mode=pl.Buffered(1))
    return pl.pallas_call(
        kern,
        grid=(bsz, nb),
        in_specs=[
            pl.BlockSpec((BLK, d), lambda b, c: (b * nb + c, 0)),
            const((1, d)), const((d, d_in)), const((d, cdim)), const((d, LANES)), const((d_in, d)),
            const((CONV_W, cdim)), const((1, cdim)), const((1, LANES)), const((1, LANES)),
            const((n_heads, 1)), const((n_heads, 1)), const((1, LANES)), const((1, d_in)),
            const((BLK, BLK)), const((BLK, BLK)),
        ],
        out_specs=[pl.BlockSpec((BLK, d), lambda b, c: (b * nb + c, 0)),
                   pl.BlockSpec((None, n_heads, SSD_HEAD_DIM, D_STATE), lambda b, c: (b, 0, 0, 0)),
                   pl.BlockSpec((None, SUBLANES, cdim), lambda b, c: (b, 0, 0))],
        out_shape=[jax.ShapeDtypeStruct((bsz * nb * BLK, d), F32),
                   jax.ShapeDtypeStruct((bsz, n_heads, SSD_HEAD_DIM, D_STATE), F32),
                   jax.ShapeDtypeStruct((bsz, SUBLANES, cdim), F32)],
        scratch_shapes=[pltpu.VMEM((BLK + SUBLANES, cdim), F32),
                        pltpu.VMEM((n_heads, SSD_HEAD_DIM, D_STATE), F32),
                        pltpu.VMEM((BLK, d_in), F32),
                        pltpu.VMEM((d_in, BLK), F32),
                        pltpu.VMEM((d_in, BLK), F32),
                        pltpu.VMEM((BLK, d_in), F32)],
        compiler_params=_cparams(("parallel", "arbitrary")),
        name="ssd_prompt",
    )(x, g.reshape(1, d), w_z, w_xbc, w_dt, w_out, conv_w, conv_b.reshape(1, cdim),
      jnp.pad(dt_bias, (0, hp)).reshape(1, LANES), jnp.pad(a, (0, hp)).reshape(1, LANES),
      dt_bias.reshape(n_heads, 1), a.reshape(n_heads, 1),
      jnp.pad(d_skip, (0, hp)).reshape(1, LANES), gate_norm.reshape(1, d_in), ltri, ltri.T)


def _ssd_sample_kernel(z_ref, xbc_ref, dt_ref, cst_ref, h0_ref, cw_ref, cb_ref, dtb_ref, a_ref,
                       dsk_ref, gn_ref, exp_ref, y_ref, h_ref, *, n_heads):
    d_in = n_heads * SSD_HEAD_DIM
    gn = SSD_GROUPS * D_STATE
    gsz = d_in // SSD_GROUPS
    hpg = n_heads // SSD_GROUPS
    conv = cb_ref[...] + cw_ref[CONV_W - 1:CONV_W, :] * xbc_ref[...]
    for j in range(CONV_W - 1):
        conv = conv + cw_ref[j:j + 1, :] * cst_ref[j:j + 1, :]
    xbc = _silu(conv)
    dt = _softplus(dt_ref[...] + dtb_ref[...])
    pad8 = lambda r: jnp.concatenate([r, jnp.zeros((SUBLANES - 1, r.shape[1]), F32)], axis=0)
    dt_ch = _dot_sel(pad8(dt), exp_ref[...])[0:1]
    da_ch = jnp.exp(_dot_sel(pad8(dt * a_ref[...]), exp_ref[...])[0:1])
    dsk_ch = _dot_sel(pad8(dsk_ref[...]), exp_ref[...])[0:1]
    xh = xbc[:, :d_in]
    xdt = xh * dt_ch
    ones = jnp.ones((SUBLANES, D_STATE), BF16)
    outs = []
    for g in range(SSD_GROUPS):
        sl = slice(g * gsz, (g + 1) * gsz)
        bm = xbc[:, d_in + g * D_STATE:d_in + (g + 1) * D_STATE]
        cm = xbc[:, d_in + gn + g * D_STATE:d_in + gn + (g + 1) * D_STATE]
        da_col = _dot_sel(pad8(da_ch[:, sl]), ones, fn=_dot_t0)
        xdt_col = _dot_sel(pad8(xdt[:, sl]), ones, fn=_dot_t0)
        h0 = h0_ref[g * hpg:(g + 1) * hpg].reshape(gsz, D_STATE)
        hn = da_col * h0 + xdt_col.astype(BF16).astype(F32) * bm.astype(BF16).astype(F32)
        h_ref[g * hpg:(g + 1) * hpg] = hn.reshape(hpg, SSD_HEAD_DIM, D_STATE)
        y_col = jnp.sum(hn * cm, axis=-1, keepdims=True)
        outs.append(y_col)
    y_cols = jnp.concatenate(outs, axis=0)
    rows = []
    eye = (lax.broadcasted_iota(I32, (LANES, LANES), 0)
           == lax.broadcasted_iota(I32, (LANES, LANES), 1)).astype(F32)
    for t in range(d_in // LANES):
        blk = y_cols[t * LANES:(t + 1) * LANES]
        rows.append(jnp.sum(blk * eye, axis=0, keepdims=True))
    y = jnp.concatenate(rows, axis=1) + dsk_ch * xh
    y = y * _silu(z_ref[...])
    for g in range(SSD_GROUPS):
        sl = slice(g * gsz, (g + 1) * gsz)
        y_ref[:, sl] = _rms(y[:, sl], gn_ref[:, sl]).astype(BF16)


def _ssd_sample(z, xbc_raw, dt_raw, conv_state, h0, conv_w, conv_b, dt_bias, a_log, d_skip,
                gate_norm, n_heads):
    db = z.shape[0]
    d_in = n_heads * SSD_HEAD_DIM
    cdim = d_in + 2 * SSD_GROUPS * D_STATE
    hp = LANES - n_heads
    a = -jnp.exp(a_log.astype(F32))
    expand = (jnp.arange(LANES)[:, None] == (jnp.arange(d_in) // SSD_HEAD_DIM)[None, :]).astype(BF16)
    kern = functools.partial(_ssd_sample_kernel, n_heads=n_heads)
    const = lambda shape: pl.BlockSpec(shape, lambda b: (0,) * len(shape))
    per_b = lambda shape: pl.BlockSpec((None,) + shape, lambda b: (b,) + (0,) * len(shape))
    y, h = pl.pallas_call(
        kern,
        grid=(db,),
        in_specs=[per_b((1, d_in)), per_b((1, cdim)), per_b((1, LANES)), per_b((CONV_W - 1, cdim)),
                  per_b((n_heads, SSD_HEAD_DIM, D_STATE)),
                  const((CONV_W, cdim)), const((1, cdim)), const((1, LANES)), const((1, LANES)),
                  const((1, LANES)), const((1, d_in)), const((LANES, d_in))],
        out_specs=[per_b((1, d_in)), per_b((n_heads, SSD_HEAD_DIM, D_STATE))],
        out_shape=[jax.ShapeDtypeStruct((db, 1, d_in), BF16),
                   jax.ShapeDtypeStruct((db, n_heads, SSD_HEAD_DIM, D_STATE), F32)],
        compiler_params=_cparams(("parallel",)),
        name="ssd_sample",
    )(z.reshape(db, 1, d_in), xbc_raw.reshape(db, 1, cdim), dt_raw.reshape(db, 1, LANES),
      conv_state, h0, conv_w, conv_b.reshape(1, cdim),
      jnp.pad(dt_bias, (0, hp)).reshape(1, LANES), jnp.pad(a, (0, hp)).reshape(1, LANES),
      jnp.pad(d_skip, (0, hp)).reshape(1, LANES), gate_norm.reshape(1, d_in), expand)
    return y.reshape(db, d_in), h


def kernel(x_prompt, x_sample, cache_k, cache_v, cache_kidx, page_table, state_ssm, state_conv,
           meta_tokens, norm_ffn_a, w_ffn_a_in, w_ffn_a_out, norm_mix, norm_ffn_b, w_ffn_b_in,
           w_ffn_b_out, w_attn_in, q_norm, k_norm, kidx_norm, w_attn_out,
           w_ssd_in, conv_w, conv_b, dt_bias, a_log, d_skip, gate_norm, w_ssd_out):
    bsz, seq, d = x_prompt.shape
    db = x_sample.shape[0]
    assert x_sample.shape[1] == 1
    t_real = N_META + seq
    nb = -(-t_real // BLK)
    t_pad = nb * BLK
    pad = t_pad - t_real
    npg = page_table.shape[1]
    past = npg * BLK
    topk_p = min(TOPK_MAX, seq // 4)
    topk_s = min(TOPK_MAX, (past + 1) // 4)
    qd, kd, qid = N_HEADS * HEAD_DIM, N_KV_HEADS * HEAD_DIM, N_IDX_HEADS * IDX_DIM
    d_in = w_ssd_out.shape[1]
    n_heads = d_in // SSD_HEAD_DIM
    cdim = d_in + 2 * SSD_GROUPS * D_STATE
    depth = norm_mix.shape[0]

    meta = jnp.broadcast_to(meta_tokens.astype(F32)[None], (bsz, N_META, d))
    xp = jnp.concatenate([jnp.zeros((bsz, pad, d), F32), meta, x_prompt], axis=1)
    xp = xp.reshape(bsz * t_pad, d)
    xs = x_sample.reshape(db, d)

    outs = {k: [] for k in ("kp", "vp", "kip", "ks", "vs", "kis", "hp", "cp", "hs", "cs")}
    for i in range(depth):
        wts = _ffn_weights(w_ffn_a_in[i], w_ffn_a_out[i])
        xp = _ffn(xp, norm_ffn_a[i], wts)
        xs = _ffn(xs, norm_ffn_a[i], wts)
        j = i // 2
        if i % 2 == 0:
            w = w_attn_in[j]
            o1, o2 = qd + 2 * kd + qid, qd + 2 * kd + qid + IDX_DIM
            w_pad = jnp.concatenate(
                [w[:, :o1], jnp.pad(w[:, o1:o2], ((0, 0), (0, LANES - IDX_DIM))),
                 jnp.pad(w[:, o2:], ((0, 0), (0, LANES - N_IDX_HEADS)))], axis=1).astype(BF16)
            w_out = w_attn_out[j].astype(BF16)

            k, v, ki, qt, qit, wt, kg, vt, kib = _attn_in(
                xp, norm_mix[i], w_pad, q_norm[j], k_norm[j], kidx_norm[j], prompt=True)
            gw = HEADS_PER_KV * BLK
            smax = (1.02 * math.sqrt(HEAD_DIM) * jnp.max(jnp.abs(q_norm[j]))
                    * jnp.max(jnp.abs(k_norm[j]))).reshape(1).astype(F32)
            o = _dsa_prompt(smax,
                            qit.reshape(bsz, nb, IDX_DIM, N_IDX_HEADS * BLK),
                            wt.reshape(bsz, nb, N_IDX_HEADS, BLK),
                            kib.reshape(bsz, nb, BLK, IDX_DIM),
                            qt.reshape(bsz, nb, N_KV_HEADS, HEAD_DIM, gw),
                            kg.reshape(N_KV_HEADS, bsz, nb, BLK, HEAD_DIM),
                            vt.reshape(bsz, nb, N_KV_HEADS, VT_ROWS, BLK),
                            pad, topk_p)
            xp = _out_proj(xp, o.reshape(bsz * t_pad, qd), w_out)
            outs["kp"].append(k.reshape(bsz, t_pad, N_KV_HEADS, HEAD_DIM)[:, pad:])
            outs["vp"].append(v.reshape(bsz, t_pad, N_KV_HEADS, HEAD_DIM)[:, pad:])
            outs["kip"].append(ki[:, :IDX_DIM].reshape(bsz, t_pad, IDX_DIM)[:, pad:])

            q, k, v, qi, ki, wi = _attn_in(
                xs, norm_mix[i], w_pad, q_norm[j], k_norm[j], kidx_norm[j], prompt=False)
            pg = math.gcd(npg, 32)
            qi3 = qi.reshape(db, N_IDX_HEADS, IDX_DIM)
            wi3 = wi[:, :N_IDX_HEADS].reshape(db, N_IDX_HEADS, 1)
            scores = _smp_scores(page_table, qi3, wi3, cache_kidx[j].transpose(0, 2, 1), pg)
            bias, nbias = _smp_select(scores, qi3, wi3, ki[:, :IDX_DIM].reshape(db, 1, IDX_DIM), topk_s)
            hsel = (jnp.arange(N_HEADS)[:, None] // HEADS_PER_KV == jnp.arange(N_KV_HEADS)[None, :])
            q_bd = (q.reshape(db, N_HEADS, 1, HEAD_DIM) * hsel[None, :, :, None].astype(BF16))
            q_bd = q_bd.reshape(db, N_HEADS, kd)
            npool = cache_k.shape[1]
            o = _smp_attend(page_table, q_bd, bias, nbias, k.reshape(db, 1, kd), v.reshape(db, 1, kd),
                            cache_k[j].transpose(0, 2, 3, 1).reshape(npool, kd, BLK),
                            cache_v[j].transpose(0, 2, 3, 1).reshape(npool, kd, BLK), pg)
            xs = _out_proj(xs, o.reshape(db, qd).astype(BF16), w_out)
            outs["ks"].append(k.reshape(db, 1, N_KV_HEADS, HEAD_DIM))
            outs["vs"].append(v.reshape(db, 1, N_KV_HEADS, HEAD_DIM))
            outs["kis"].append(ki[:, :IDX_DIM].reshape(db, 1, IDX_DIM))
        else:
            w = w_ssd_in[j]
            w_main = w[:, :d_in + cdim].astype(BF16)
            w_dt = jnp.pad(w[:, d_in + cdim:], ((0, 0), (0, LANES - n_heads))).astype(BF16)
            w_out = w_ssd_out[j].astype(BF16)
            sp = (conv_w[j], conv_b[j], dt_bias[j], a_log[j], d_skip[j], gate_norm[j])

            xp, hfin, ctail = _ssd_prompt(xp, norm_mix[i], w_main[:, :d_in], w_main[:, d_in:], w_dt, w_out,
                                          *sp, bsz, nb, pad, n_heads)
            outs["hp"].append(hfin)
            outs["cp"].append(ctail[:, SUBLANES - (CONV_W - 1):])

            z, xbc_raw, dt_raw = _norm_proj(xs, norm_mix[i], w_main, w_dt, d_in)
            y, hnew = _ssd_sample(z, xbc_raw, dt_raw, state_conv[j], state_ssm[j], *sp, n_heads)
            xs = _out_proj(xs, y, w_out)
            outs["hs"].append(hnew)
            outs["cs"].append(jnp.concatenate([state_conv[j][:, 1:], xbc_raw[:, None, :]], axis=1))
        wts = _ffn_weights(w_ffn_b_in[i], w_ffn_b_out[i])
        xp = _ffn(xp, norm_ffn_b[i], wts)
        xs = _ffn(xs, norm_ffn_b[i], wts)

    y_prompt = xp.reshape(bsz, t_pad, d)[:, pad + N_META:]
    y_sample = xs.reshape(db, 1, d)
    st = lambda key: jnp.stack(outs[key])
    return (y_prompt, y_sample, st("kp"), st("vp"), st("kip"), st("ks"), st("vs"), st("kis"),
            st("hp"), st("cp"), st("hs"), st("cs"))
```

```python
import functools
import math

import jax
import jax.numpy as jnp
from jax import lax
from jax.experimental import pallas as pl
from jax.experimental.pallas import tpu as pltpu

F32 = jnp.float32
BF16 = jnp.bfloat16
I32 = jnp.int32

N_META = 16
N_HEADS = 16
HEAD_DIM = 64
N_KV_HEADS = 4
HEADS_PER_KV = N_HEADS // N_KV_HEADS
N_IDX_HEADS = 8
IDX_DIM = 64
TOPK_MAX = 256
SSD_HEAD_DIM = 64
SSD_GROUPS = 4
D_STATE = 128
CONV_W = 4
EPS = 1e-6

LANES = 128
SUBLANES = 8
BLK = 128
NEG = -1e30
SAFE_LOGIT = 40.0
COUNT_UNROLL = 4
VT_ROWS = 80
INT_MIN = -2 ** 31
VMEM_LIMIT = 56 * 1024 * 1024


def _cparams(sem, vmem=VMEM_LIMIT):
    return pltpu.CompilerParams(dimension_semantics=sem, vmem_limit_bytes=vmem)


def _row_tile(rows, pref):
    best = None
    for d in range(SUBLANES, min(rows, pref) + 1, SUBLANES):
        if rows % d == 0:
            best = d
    assert best is not None, rows
    return best


def _rms(x, g):
    var = jnp.mean(x * x, axis=-1, keepdims=True)
    return x * lax.rsqrt(var + EPS) * g


def _dot(a, b):
    return jnp.dot(a, b, preferred_element_type=F32)


def _dot_t0(a, b):
    return lax.dot_general(a, b, (((0,), (0,)), ((), ())), preferred_element_type=F32)


def _dot_t1(a, b):
    return lax.dot_general(a, b, (((1,), (1,)), ((), ())), preferred_element_type=F32)


def _split2(a):
    hi = a.astype(BF16)
    lo = (a - hi.astype(F32)).astype(BF16)
    return hi, lo


def _split3(a):
    a0 = a.astype(BF16)
    r = a - a0.astype(F32)
    a1 = r.astype(BF16)
    a2 = (r - a1.astype(F32)).astype(BF16)
    return a0, a1, a2


def _dot_sel(a, m, fn=_dot):
    a0, a1, a2 = _split3(a)
    return fn(a0, m) + fn(a1, m) + fn(a2, m)


def _silu(x):
    return x * (1.0 / (1.0 + jnp.exp(-x)))


def _softplus(x):
    return jnp.maximum(x, 0.0) + jnp.log(1.0 + jnp.exp(-jnp.abs(x)))


def _ffn_kernel(x_ref, g_ref, wa_ref, wb_ref, wo_ref, o_ref, xn_ref, acc_ref):
    xn_ref[...] = _rms(x_ref[...], g_ref[...]).astype(BF16)
    acc_ref[...] = jnp.zeros_like(acc_ref)

    def chunk(j, carry):
        xn = xn_ref[...]
        a = _dot(xn, wa_ref[j])
        b = _dot(xn, wb_ref[j])
        h = (_silu(a) * b).astype(BF16)
        acc_ref[...] += _dot(h, wo_ref[j])
        return carry

    lax.fori_loop(0, wa_ref.shape[0], chunk, 0)
    o_ref[...] = x_ref[...] + 0.5 * acc_ref[...]


def _ffn_weights(w_in, w_out):
    d, hid = w_in.shape[0], w_out.shape[0]
    th = 256 if hid % 256 == 0 else LANES
    nh = hid // th
    w_in = w_in.astype(BF16)
    wa = w_in[:, :hid].reshape(d, nh, th).transpose(1, 0, 2)
    wb = w_in[:, hid:].reshape(d, nh, th).transpose(1, 0, 2)
    return wa, wb, w_out.astype(BF16).reshape(nh, th, d)


def _ffn(x, g, weights):
    wa, wb, wo = weights
    rows, d = x.shape
    nh, _, th = wa.shape
    tm = _row_tile(rows, 1024)
    resident = lambda shape: pl.BlockSpec(shape, lambda i: (0,) * len(shape),
                                          pipeline_mode=pl.Buffered(1))
    return pl.pallas_call(
        _ffn_kernel,
        grid=(rows // tm,),
        in_specs=[
            pl.BlockSpec((tm, d), lambda i: (i, 0)),
            resident((1, d)), resident((nh, d, th)), resident((nh, d, th)), resident((nh, th, d)),
        ],
        out_specs=pl.BlockSpec((tm, d), lambda i: (i, 0)),
        out_shape=jax.ShapeDtypeStruct((rows, d), F32),
        scratch_shapes=[pltpu.VMEM((tm, d), BF16), pltpu.VMEM((tm, d), F32)],
        compiler_params=_cparams(("parallel",)),
        name="ffn",
    )(x, g.reshape(1, d), wa, wb, wo)


def _attn_project(x_ref, g_ref, w_ref, qg_ref, kg_ref, kig_ref, gq_ref, eq_ref, gk_ref, ek_ref):
    qd = N_HEADS * HEAD_DIM
    kd = N_KV_HEADS * HEAD_DIM
    qid = N_IDX_HEADS * IDX_DIM
    xn = _rms(x_ref[...], g_ref[...]).astype(BF16)
    h = _dot(xn, w_ref[...])

    def head_norm(t, gsum_ref, gexp_ref, gain):
        ss = _dot_sel(t * t, gsum_ref[...])
        rs = lax.rsqrt(ss * (1.0 / HEAD_DIM) + EPS)
        return t * _dot_sel(rs, gexp_ref[...]) * gain

    o = 0
    q = head_norm(h[:, o:o + qd], gq_ref, eq_ref, qg_ref[...]) * (HEAD_DIM ** -0.5)
    o += qd
    k = head_norm(h[:, o:o + kd], gk_ref, ek_ref, kg_ref[...])
    o += kd
    v = h[:, o:o + kd]
    o += kd
    qi = h[:, o:o + qid] * (IDX_DIM ** -0.5)
    o += qid
    ki = h[:, o:o + LANES]
    var = jnp.sum(ki * ki, axis=-1, keepdims=True) * (1.0 / IDX_DIM)
    ki = ki * lax.rsqrt(var + EPS) * kig_ref[...]
    o += LANES
    wi = h[:, o:o + LANES] * (N_IDX_HEADS ** -0.5)
    return q, k, v, qi, ki, wi


def _attn_in_sample_kernel(*refs):
    q_o, k_o, v_o, qi_o, ki_o, wi_o = refs[10:]
    q, k, v, qi, ki, wi = _attn_project(*refs[:10])
    q_o[...] = q.astype(BF16)
    k_o[...] = k
    v_o[...] = v
    qi_o[...] = qi.astype(BF16)
    ki_o[...] = ki
    wi_o[...] = wi


def _attn_in_prompt_kernel(*refs):
    k_o, v_o, ki_o, qt_o, qit_o, wt_o, kg_o, vt_o, kib_o = refs[10:]
    q, k, v, qi, ki, wi = _attn_project(*refs[:10])
    k_o[...] = k
    v_o[...] = v
    ki_o[...] = ki
    kib_o[...] = ki[:, :IDX_DIM].astype(BF16)
    for g in range(N_KV_HEADS):
        kg_o[g] = k[:, g * HEAD_DIM:(g + 1) * HEAD_DIM].astype(BF16)
    tail = (lax.broadcasted_iota(I32, (VT_ROWS - HEAD_DIM, BLK), 0) == 0).astype(BF16)
    for r in range(q.shape[0] // BLK):
        rows = slice(r * BLK, (r + 1) * BLK)
        for t in range(N_HEADS * HEAD_DIM // LANES):
            tt = q[rows, t * LANES:(t + 1) * LANES].T.astype(BF16)
            for u in range(LANES // HEAD_DIM):
                h = t * (LANES // HEAD_DIM) + u
                g, hh = h // HEADS_PER_KV, h % HEADS_PER_KV
                qt_o[r, g, :, hh * BLK:(hh + 1) * BLK] = tt[u * HEAD_DIM:(u + 1) * HEAD_DIM]
        for t in range(N_IDX_HEADS * IDX_DIM // LANES):
            tt = qi[rows, t * LANES:(t + 1) * LANES].T.astype(BF16)
            for u in range(LANES // IDX_DIM):
                h = t * (LANES // IDX_DIM) + u
                qit_o[r, :, h * BLK:(h + 1) * BLK] = tt[u * IDX_DIM:(u + 1) * IDX_DIM]
        wt_o[r] = wi[rows].T[0:N_IDX_HEADS]
        for t in range(N_KV_HEADS * HEAD_DIM // LANES):
            tt = v[rows, t * LANES:(t + 1) * LANES].T.astype(BF16)
            for u in range(LANES // HEAD_DIM):
                vt_o[r, t * (LANES // HEAD_DIM) + u] = jnp.concatenate(
                    [tt[u * HEAD_DIM:(u + 1) * HEAD_DIM], tail], axis=0)


def _seg_mats(n_heads, hd):
    col = jnp.arange(n_heads * hd) // hd
    gsum = (col[:, None] == jnp.arange(LANES)[None, :]).astype(BF16)
    return gsum, gsum.T


def _attn_in(x, g, w_pad, q_gain, k_gain, ki_gain, prompt):
    rows, d = x.shape
    n = w_pad.shape[1]
    qd, kd, qid = N_HEADS * HEAD_DIM, N_KV_HEADS * HEAD_DIM, N_IDX_HEADS * IDX_DIM
    gw = HEADS_PER_KV * BLK
    tm = _row_tile(rows, 512)
    gq, eq = _seg_mats(N_HEADS, HEAD_DIM)
    gk, ek = _seg_mats(N_KV_HEADS, HEAD_DIM)
    const = lambda shape: pl.BlockSpec(shape, lambda i: (0,) * len(shape))
    rowb = lambda w: pl.BlockSpec((tm, w), lambda i: (i, 0))
    sds = jax.ShapeDtypeStruct
    if prompt:
        assert tm % BLK == 0
        nbk, tb = rows // BLK, tm // BLK
        blkb = lambda *s: pl.BlockSpec((tb,) + s, lambda i: (i,) + (0,) * len(s))
        kern = _attn_in_prompt_kernel
        out_specs = [rowb(kd), rowb(kd), rowb(LANES), blkb(N_KV_HEADS, HEAD_DIM, gw),
                     blkb(IDX_DIM, N_IDX_HEADS * BLK), blkb(N_IDX_HEADS, BLK),
                     pl.BlockSpec((N_KV_HEADS, tm, HEAD_DIM), lambda i: (0, i, 0)),
                     blkb(N_KV_HEADS, VT_ROWS, BLK),
                     rowb(IDX_DIM)]
        out_shape = [sds((rows, kd), F32), sds((rows, kd), F32), sds((rows, LANES), F32),
                     sds((nbk, N_KV_HEADS, HEAD_DIM, gw), BF16),
                     sds((nbk, IDX_DIM, N_IDX_HEADS * BLK), BF16),
                     sds((nbk, N_IDX_HEADS, BLK), F32),
                     sds((N_KV_HEADS, rows, HEAD_DIM), BF16),
                     sds((nbk, N_KV_HEADS, VT_ROWS, BLK), BF16),
                     sds((rows, IDX_DIM), BF16)]
    else:
        kern = _attn_in_sample_kernel
        out_specs = [rowb(qd), rowb(kd), rowb(kd), rowb(qid), rowb(LANES), rowb(LANES)]
        out_shape = [sds((rows, qd), BF16), sds((rows, kd), F32), sds((rows, kd), F32),
                     sds((rows, qid), BF16), sds((rows, LANES), F32), sds((rows, LANES), F32)]
    return pl.pallas_call(
        kern,
        grid=(rows // tm,),
        in_specs=[rowb(d), const((1, d)), const((d, n)), const((1, qd)), const((1, kd)),
                  const((1, LANES)), const((qd, LANES)), const((LANES, qd)),
                  const((kd, LANES)), const((LANES, kd))],
        out_specs=out_specs,
        out_shape=out_shape,
        compiler_params=_cparams(("parallel",)),
        name="attn_in_prompt" if prompt else "attn_in_sample",
    )(x, g.reshape(1, d), w_pad,
      jnp.tile(q_gain, N_HEADS).reshape(1, qd), jnp.tile(k_gain, N_KV_HEADS).reshape(1, kd),
      jnp.pad(ki_gain, (0, LANES - IDX_DIM)).reshape(1, LANES), gq, eq, gk, ek)


def _norm_proj_kernel(x_ref, g_ref, w_ref, ws_ref, oa_ref, ob_ref, os_ref, xn_ref, *, na):
    j = pl.program_id(1)

    @pl.when(j == 0)
    def _():
        xn = _rms(x_ref[...], g_ref[...]).astype(BF16)
        xn_ref[...] = xn
        os_ref[...] = _dot(xn, ws_ref[...])

    r = _dot(xn_ref[...], w_ref[...])

    @pl.when(j < na)
    def _():
        oa_ref[...] = r

    @pl.when(j >= na)
    def _():
        ob_ref[...] = r


def _norm_proj(x, g, w_main, w_side, n_a):
    rows, d = x.shape
    n = w_main.shape[1]
    tm = _row_tile(rows, 1024)
    tn = math.gcd(math.gcd(n_a, n - n_a), 1024)
    na = n_a // tn
    return pl.pallas_call(
        functools.partial(_norm_proj_kernel, na=na),
        grid=(rows // tm, n // tn),
        in_specs=[
            pl.BlockSpec((tm, d), lambda i, j: (i, 0)),
            pl.BlockSpec((1, d), lambda i, j: (0, 0)),
            pl.BlockSpec((d, tn), lambda i, j: (0, j)),
            pl.BlockSpec((d, LANES), lambda i, j: (0, 0)),
        ],
        out_specs=[pl.BlockSpec((tm, tn), lambda i, j: (i, jnp.minimum(j, na - 1))),
                   pl.BlockSpec((tm, tn), lambda i, j: (i, jnp.maximum(j - na, 0))),
                   pl.BlockSpec((tm, LANES), lambda i, j: (i, 0))],
        out_shape=[jax.ShapeDtypeStruct((rows, n_a), F32),
                   jax.ShapeDtypeStruct((rows, n - n_a), F32),
                   jax.ShapeDtypeStruct((rows, LANES), F32)],
        scratch_shapes=[pltpu.VMEM((tm, d), BF16)],
        compiler_params=_cparams(("parallel", "arbitrary")),
        name="norm_proj",
    )(x, g.reshape(1, d), w_main, w_side)


def _out_proj_kernel(x_ref, y_ref, w_ref, o_ref):
    o_ref[...] = x_ref[...] + _dot(y_ref[...], w_ref[...])


def _out_proj(x, y, w):
    rows, d = x.shape
    k = y.shape[1]
    tm = _row_tile(rows, 512)
    return pl.pallas_call(
        _out_proj_kernel,
        grid=(rows // tm,),
        in_specs=[pl.BlockSpec((tm, d), lambda i: (i, 0)),
                  pl.BlockSpec((tm, k), lambda i: (i, 0)),
                  pl.BlockSpec((k, d), lambda i: (0, 0))],
        out_specs=pl.BlockSpec((tm, d), lambda i: (i, 0)),
        out_shape=jax.ShapeDtypeStruct((rows, d), F32),
        compiler_params=_cparams(("parallel",)),
        name="out_proj",
    )(x, y, w)


def _sort_key(score):
    bits = pltpu.bitcast(score, I32)
    return bits ^ ((bits >> 31) & 0x7FFFFFFF)


def _tile_fold(x, op):
    r = x[0:SUBLANES]
    for t in range(1, x.shape[0] // SUBLANES):
        r = op(r, x[t * SUBLANES:(t + 1) * SUBLANES])
    return r


def _dsa_prompt_kernel(smax_ref, qit_ref, wt_ref, ki_ref, qt_ref, k_ref, vt_ref, o_ref,
                       key_ref, bias_ref, acc_ref, *, pad, topk):
    i = pl.program_id(1)
    nch = i + 1
    npair = (nch + 1) // 2
    last = ki_ref.shape[0] - 1
    row = lax.broadcasted_iota(I32, (BLK, BLK), 0)
    lane = lax.broadcasted_iota(I32, (BLK, BLK), 1)
    t_pos = i * BLK + lane

    def idx_dots(c):
        return _dot(ki_ref[jnp.minimum(c, last)], qit_ref[...])

    def idx_keys(c, dots):
        sc = jnp.zeros((BLK, BLK), F32)
        for h in range(N_IDX_HEADS):
            sc = sc + wt_ref[h:h + 1, :] * jnp.maximum(dots[:, h * BLK:(h + 1) * BLK], 0.0)
        s_pos = c * BLK + row
        valid = (s_pos <= t_pos) & (s_pos >= pad)
        key_ref[c] = jnp.where(valid, _sort_key(sc), INT_MIN)

    def p1(j, carry):
        for u in range(2):
            idx_keys(2 * j + u, idx_dots(2 * j + u))
        return carry

    lax.fori_loop(0, npair, p1, 0)

    for u in range(COUNT_UNROLL - 2):
        key_ref[2 * npair + u] = jnp.full((BLK, BLK), INT_MIN, I32)
    ntrip = (nch + COUNT_UNROLL - 1) // COUNT_UNROLL

    def count(pred):
        def body(j, cnt):
            for u in range(COUNT_UNROLL):
                c = COUNT_UNROLL * j + u
                cnt = cnt + jnp.where(pred(key_ref[c], c * BLK + row), 1, 0)
            return cnt
        cnt = lax.fori_loop(0, ntrip, body, jnp.zeros((BLK, BLK), I32))
        return jnp.sum(cnt, axis=0, keepdims=True)

    c0 = count(lambda k, s: k >= 0)
    thr0 = jnp.where(c0 >= topk, 0, INT_MIN).astype(I32)

    def bit_step(it, st):
        thr, n_ge = st
        cand = thr + (jnp.int32(1) << (30 - it))
        cnt = count(lambda k, s: k >= cand)
        ok = cnt >= topk
        return jnp.where(ok, cand, thr), jnp.where(ok, cnt, n_ge)

    thr, n_ge = lax.fori_loop(0, 31, bit_step, (thr0, c0))
    tied = (n_ge > topk) & (thr > INT_MIN)
    any_tied = jnp.max(tied.astype(I32))
    nbits = (key_ref.shape[0] * BLK).bit_length()

    def idx_search():
        need = topk - count(lambda k, s: k > thr)

        def step(it, lo):
            cand = lo + (jnp.int32(1) << (nbits - 1 - it))
            below = count(lambda k, s: (k == thr) & (s < cand))
            return jnp.where(below < need, cand, lo)
        return lax.fori_loop(0, nbits, step, jnp.zeros((1, BLK), I32))

    jcut = lax.cond(any_tied > 0, idx_search,
                    lambda: jnp.full((1, BLK), 2 ** 30, I32))

    def p2(j, carry):
        for u in range(2):
            c = 2 * j + u
            k = key_ref[c]
            sel = (k > thr) | ((k == thr) & (c * BLK + row <= jcut))
            sel = sel & (k > INT_MIN)
            bias_ref[c] = jnp.where(sel, 0.0, NEG)
        return carry

    lax.fori_loop(0, npair, p2, 0)

    gw = HEADS_PER_KV * BLK

    def logits(c):
        bb = jnp.concatenate([bias_ref[c]] * HEADS_PER_KV, axis=1)
        cr = jnp.minimum(c, last)
        return [_dot(k_ref[g, cr], qt_ref[g]) + bb for g in range(N_KV_HEADS)]

    def attend(shift):
        acc_ref[...] = jnp.zeros_like(acc_ref)

        def body(j, carry):
            p = []
            for u in range(2):
                s = logits(2 * j + u)
                if shift is not None:
                    s = [s[g] - shift[g] for g in range(N_KV_HEADS)]
                p.append([jnp.exp(x).astype(BF16) for x in s])
            c1 = jnp.minimum(2 * j + 1, last)
            for g in range(N_KV_HEADS):
                vt = jnp.concatenate([vt_ref[2 * j, g], vt_ref[c1, g]], axis=1)
                pp = jnp.concatenate([p[0][g], p[1][g]], axis=0)
                acc_ref[g] += _dot(vt, pp)
            return carry

        lax.fori_loop(0, npair, body, 0)

    safe = smax_ref[0] <= SAFE_LOGIT

    @pl.when(safe)
    def _():
        attend(None)

    @pl.when(jnp.logical_not(safe))
    def _():
        def pa(j, m):
            for u in range(2):
                s = logits(2 * j + u)
                m = tuple(jnp.maximum(m[g], _tile_fold(s[g], jnp.maximum)) for g in range(N_KV_HEADS))
            return m

        m0 = tuple(jnp.full((SUBLANES, gw), NEG, F32) for _ in range(N_KV_HEADS))
        m = lax.fori_loop(0, npair, pa, m0)
        attend([jnp.max(x, axis=0, keepdims=True) for x in m])

    q_row = i * BLK + lax.broadcasted_iota(I32, (BLK, LANES), 0)
    for g in range(N_KV_HEADS):
        a = acc_ref[g]
        res = a[0:HEAD_DIM] * (1.0 / a[HEAD_DIM:HEAD_DIM + 1])
        for t in range(HEADS_PER_KV // 2):
            two = jnp.concatenate([res[:, (2 * t + u) * BLK:(2 * t + u + 1) * BLK] for u in range(2)],
                                  axis=0)
            two = jnp.where(q_row >= pad, two.T, 0.0)
            lo = (g * HEADS_PER_KV + 2 * t) * HEAD_DIM
            o_ref[:, lo:lo + LANES] = two.astype(BF16)


def _dsa_prompt(smax, qit, wt, kib, qt, kg, vt, pad, topk):
    bsz, nb = qit.shape[:2]
    t_pad = nb * BLK
    gw = HEADS_PER_KV * BLK
    kern = functools.partial(_dsa_prompt_kernel, pad=pad, topk=topk)
    return pl.pallas_call(
        kern,
        grid=(bsz, nb),
        in_specs=[
            pl.BlockSpec(memory_space=pltpu.SMEM),
            pl.BlockSpec((None, None, IDX_DIM, N_IDX_HEADS * BLK), lambda b, i: (b, i, 0, 0)),
            pl.BlockSpec((None, None, N_IDX_HEADS, BLK), lambda b, i: (b, i, 0, 0)),
            pl.BlockSpec((None, nb, BLK, IDX_DIM), lambda b, i: (b, 0, 0, 0)),
            pl.BlockSpec((None, None, N_KV_HEADS, HEAD_DIM, gw), lambda b, i: (b, i, 0, 0, 0)),
            pl.BlockSpec((N_KV_HEADS, None, nb, BLK, HEAD_DIM), lambda b, i: (0, b, 0, 0, 0)),
            pl.BlockSpec((None, nb, N_KV_HEADS, VT_ROWS, BLK), lambda b, i: (b, 0, 0, 0, 0)),
        ],
        out_specs=pl.BlockSpec((None, BLK, N_HEADS * HEAD_DIM), lambda b, i: (b, i, 0)),
        out_shape=jax.ShapeDtypeStruct((bsz, t_pad, N_HEADS * HEAD_DIM), BF16),
        scratch_shapes=[pltpu.VMEM((nb + COUNT_UNROLL - 1, BLK, BLK), I32),
                        pltpu.VMEM((nb + 1, BLK, BLK), F32),
                        pltpu.VMEM((N_KV_HEADS, VT_ROWS, gw), F32)],
        compiler_params=_cparams(("parallel", "arbitrary")),
        name="dsa_prompt",
    )(smax, qit, wt, kib, qt, kg, vt)


def _smp_scores_kernel(pt_ref, qi_ref, w_ref, *refs, pg):
    page_refs, o_ref = refs[:pg], refs[pg]
    qi = qi_ref[...]
    w = w_ref[...]
    for p in range(pg):
        d = _dot(qi, page_refs[p][...].astype(BF16))
        o_ref[p:p + 1, :] = jnp.sum(w * jnp.maximum(d, 0.0), axis=0, keepdims=True)


def _smp_scores(page_table, qi, wi, kidx_t, pg):
    db, npg = page_table.shape
    kern = functools.partial(_smp_scores_kernel, pg=pg)
    page_spec = lambda p: pl.BlockSpec(
        (None, IDX_DIM, BLK), lambda b, j, pt: (pt[b * npg + j * pg + p], 0, 0))
    return pl.pallas_call(
        kern,
        grid_spec=pltpu.PrefetchScalarGridSpec(
            num_scalar_prefetch=1,
            grid=(db, npg // pg),
            in_specs=[pl.BlockSpec((None, N_IDX_HEADS, IDX_DIM), lambda b, j, pt: (b, 0, 0)),
                      pl.BlockSpec((None, N_IDX_HEADS, 1), lambda b, j, pt: (b, 0, 0))]
                     + [page_spec(p) for p in range(pg)],
            out_specs=pl.BlockSpec((None, pg, BLK), lambda b, j, pt: (b, j, 0)),
        ),
        out_shape=jax.ShapeDtypeStruct((db, npg, BLK), F32),
        compiler_params=_cparams(("parallel", "arbitrary")),
        name="sample_scores",
    )(page_table.reshape(-1), qi, wi, *([kidx_t] * pg))


def _smp_select_kernel(sc_ref, qi_ref, w_ref, kin_ref, bias_ref, nb_ref, *, topk, past):
    db, npg, _ = sc_ref.shape
    key = _sort_key(sc_ref[...])
    qi = qi_ref[...].astype(F32)
    kn = kin_ref[...].astype(BF16).astype(F32)
    d = jnp.sum(qi * kn, axis=-1, keepdims=True)
    s_new = jnp.sum(w_ref[...] * jnp.maximum(d, 0.0), axis=1, keepdims=True)
    key_new = _sort_key(s_new)
    pos = (lax.broadcasted_iota(I32, (db, npg, BLK), 1) * BLK
           + lax.broadcasted_iota(I32, (db, npg, BLK), 2))

    def count(pred_past, pred_new):
        c = jnp.sum(jnp.where(pred_past, 1, 0), axis=2, keepdims=True)
        return jnp.sum(c, axis=1, keepdims=True) + jnp.where(pred_new, 1, 0)

    thr = jnp.where(count(key >= 0, key_new >= 0) >= topk, 0, INT_MIN).astype(I32)

    def bit_step(it, thr):
        cand = thr + (jnp.int32(1) << (30 - it))
        return jnp.where(count(key >= cand, key_new >= cand) >= topk, cand, thr)

    thr = lax.fori_loop(0, 31, bit_step, thr)
    need = topk - count(key > thr, key_new > thr)
    nbits = max(1, past.bit_length())

    def step(it, lo):
        cand = lo + (jnp.int32(1) << (nbits - 1 - it))
        below = count((key == thr) & (pos < cand), (key_new == thr) & (past < cand))
        return jnp.where(below < need, cand, lo)

    jcut = lax.fori_loop(0, nbits, step, jnp.zeros((db, 1, 1), I32))
    sel = (key > thr) | ((key == thr) & (pos <= jcut))
    bias_ref[...] = jnp.where(sel, 0.0, NEG)
    sel_new = (key_new > thr) | ((key_new == thr) & (past <= jcut))
    nb_ref[...] = jnp.broadcast_to(jnp.where(sel_new, 0.0, NEG), nb_ref.shape)


def _smp_select(scores, qi, wi, ki_new, topk):
    db, npg, _ = scores.shape
    kern = functools.partial(_smp_select_kernel, topk=topk, past=npg * BLK)
    full = lambda *s: pl.BlockSpec(s, lambda i: (0,) * len(s))
    return pl.pallas_call(
        kern,
        grid=(1,),
        in_specs=[full(db, npg, BLK), full(db, N_IDX_HEADS, IDX_DIM), full(db, N_IDX_HEADS, 1),
                  full(db, 1, IDX_DIM)],
        out_specs=[full(db, npg, BLK), full(db, SUBLANES, LANES)],
        out_shape=[jax.ShapeDtypeStruct((db, npg, BLK), F32),
                   jax.ShapeDtypeStruct((db, SUBLANES, LANES), F32)],
        compiler_params=_cparams(("arbitrary",)),
        name="sample_select",
    )(scores, qi, wi, ki_new)


def _smp_attend_kernel(pt_ref, q_ref, bias_ref, nb_ref, kn_ref, vn_ref, *refs, pg):
    k_refs, v_refs = refs[:pg], refs[pg:2 * pg]
    o_ref, m_ref, l_ref, acc_ref = refs[2 * pg:]
    j = pl.program_id(1)
    kd = N_KV_HEADS * HEAD_DIM

    @pl.when(j == 0)
    def _():
        m_ref[...] = jnp.full_like(m_ref, NEG)
        l_ref[...] = jnp.zeros_like(l_ref)
        acc_ref[...] = jnp.zeros_like(acc_ref)

    q = q_ref[...]
    s = [_dot(q, k_refs[p][...].astype(BF16)) + bias_ref[p:p + 1, :] for p in range(pg)]
    m_old = m_ref[...]
    m_new = m_old
    for p in range(pg):
        m_new = jnp.maximum(m_new, jnp.max(s[p], axis=-1, keepdims=True))
    alpha = jnp.exp(m_old - m_new)
    l = l_ref[...] * alpha
    acc = acc_ref[...] * alpha
    for p in range(pg):
        e = jnp.exp(s[p] - m_new)
        l = l + jnp.sum(e, axis=-1, keepdims=True)
        acc = acc + _dot_t1(e.astype(BF16), v_refs[p][...].astype(BF16))
    m_ref[...] = m_new
    l_ref[...] = l
    acc_ref[...] = acc

    @pl.when(j == pl.num_programs(1) - 1)
    def _():
        qf = q.astype(F32)
        kn = kn_ref[...].astype(BF16).astype(F32)
        vn = vn_ref[...].astype(BF16).astype(F32)
        s_new = jnp.sum(qf * kn, axis=-1, keepdims=True) + nb_ref[0:1, 0:1]
        m_fin = jnp.maximum(m_new, s_new)
        a2 = jnp.exp(m_new - m_fin)
        e_new = jnp.exp(s_new - m_fin)
        l_fin = l * a2 + e_new
        acc_fin = acc * a2 + e_new.astype(BF16).astype(F32) * vn
        res = acc_fin / l_fin
        hgrp = lax.broadcasted_iota(I32, (N_HEADS, HEAD_DIM), 0) // HEADS_PER_KV
        out = jnp.zeros((N_HEADS, HEAD_DIM), F32)
        for g in range(N_KV_HEADS):
            out = out + jnp.where(hgrp == g, res[:, g * HEAD_DIM:(g + 1) * HEAD_DIM], 0.0)
        o_ref[...] = out


def _smp_attend(page_table, q_bd, bias, nbias, k_new, v_new, k_t, v_t, pg):
    db, npg = page_table.shape
    kd = N_KV_HEADS * HEAD_DIM
    kern = functools.partial(_smp_attend_kernel, pg=pg)
    page_spec = lambda p: pl.BlockSpec(
        (None, kd, BLK), lambda b, j, pt: (pt[b * npg + j * pg + p], 0, 0))
    per_b = lambda shape: pl.BlockSpec((None,) + shape, lambda b, j, pt: (b, 0, 0))
    return pl.pallas_call(
        kern,
        grid_spec=pltpu.PrefetchScalarGridSpec(
            num_scalar_prefetch=1,
            grid=(db, npg // pg),
            in_specs=[per_b((N_HEADS, kd)),
                      pl.BlockSpec((None, pg, BLK), lambda b, j, pt: (b, j, 0)),
                      per_b((SUBLANES, LANES)), per_b((1, kd)), per_b((1, kd))]
                     + [page_spec(p) for p in range(pg)] * 2,
            out_specs=per_b((N_HEADS, HEAD_DIM)),
            scratch_shapes=[pltpu.VMEM((N_HEADS, 1), F32), pltpu.VMEM((N_HEADS, 1), F32),
                            pltpu.VMEM((N_HEADS, kd), F32)],
        ),
        out_shape=jax.ShapeDtypeStruct((db, N_HEADS, HEAD_DIM), F32),
        compiler_params=_cparams(("parallel", "arbitrary")),
        name="sample_attend",
    )(page_table.reshape(-1), q_bd, bias, nbias, k_new, v_new,
      *([k_t] * pg), *([v_t] * pg))


def _ssd_prompt_kernel(x_ref, g_ref, wz_ref, wx_ref, wdt_ref, wo_ref,
                       cw_ref, cb_ref, dtb_ref, a_ref,
                       dtbc_ref, ac_ref, dsk_ref, gn_ref, ltri_ref, utri_ref,
                       o_ref, st_ref, cv_ref, xpad_ref, h_ref, yb_ref, xt_ref, yt_ref, z_ref,
                       *, pad, n_heads):
    c = pl.program_id(1)
    d_in = n_heads * SSD_HEAD_DIM
    gn = SSD_GROUPS * D_STATE
    hpg = n_heads // SSD_GROUPS

    @pl.when(c == 0)
    def _():
        xpad_ref[0:SUBLANES, :] = jnp.zeros((SUBLANES, xpad_ref.shape[1]), F32)
        h_ref[...] = jnp.zeros_like(h_ref)

    xn = _rms(x_ref[...], g_ref[...]).astype(BF16)
    z_ref[...] = _dot(xn, wz_ref[...])
    dt_raw = _dot(xn, wdt_ref[...])
    dt_raw_t = dt_raw.T[0:n_heads]

    xpad_ref[SUBLANES:, :] = _dot(xn, wx_ref[...])
    conv = cb_ref[...] + cw_ref[CONV_W - 1:CONV_W, :] * xpad_ref[SUBLANES:, :]
    for j in range(CONV_W - 1):
        sh = CONV_W - 1 - j
        conv = conv + cw_ref[j:j + 1, :] * xpad_ref[SUBLANES - sh:SUBLANES - sh + BLK, :]
    xpad_ref[0:SUBLANES, :] = xpad_ref[BLK:BLK + SUBLANES, :]
    xbc = _silu(conv)

    live = (c > 0) | (lax.broadcasted_iota(I32, (BLK, LANES), 0) >= pad)
    dt = jnp.where(live, _softplus(dt_raw + dtb_ref[...]), 0.0)
    acum = _dot_sel(dt * a_ref[...], ltri_ref[...], fn=lambda x, m: _dot(m, x))
    live_t = (c > 0) | (lax.broadcasted_iota(I32, (n_heads, BLK), 1) >= pad)
    dtt = jnp.where(live_t, _softplus(dt_raw_t + dtbc_ref[...]), 0.0)
    acum_t = _dot_sel(dtt * ac_ref[...], utri_ref[...])
    for t in range(d_in // LANES):
        xt_ref[t * LANES:(t + 1) * LANES, :] = xbc[:, t * LANES:(t + 1) * LANES].T
    a_last = acum_t[:, BLK - 1:BLK]
    ecol_t = jnp.exp(acum_t)
    decs_t = jnp.exp(a_last - acum_t)
    ea_last = jnp.exp(a_last)
    causal_t = (lax.broadcasted_iota(I32, (BLK, BLK), 0) <= lax.broadcasted_iota(I32, (BLK, BLK), 1))
    hp = SSD_HEAD_DIM

    for g in range(SSD_GROUPS):
        bm = xbc[:, d_in + g * D_STATE:d_in + (g + 1) * D_STATE].astype(BF16)
        ct = xbc[:, d_in + gn + g * D_STATE:d_in + gn + (g + 1) * D_STATE].T.astype(BF16)
        cb_t = _dot(bm, ct)
        hprev = h_ref[g * hpg:(g + 1) * hpg].reshape(hpg * hp, D_STATE)
        y_off = _dot(hprev.astype(BF16), ct)
        ws = []
        for hh in range(hpg):
            h = g * hpg + hh
            rows = slice(h * hp, (h + 1) * hp)
            xh = xt_ref[rows, :]
            xdt = xh * dtt[h:h + 1, :]
            decay_t = jnp.exp(jnp.where(causal_t, acum_t[h:h + 1, :] - acum[:, h:h + 1], NEG))
            y = _dot(xdt.astype(BF16), (cb_t * decay_t).astype(BF16))
            y = y + ecol_t[h:h + 1, :] * y_off[hh * hp:(hh + 1) * hp]
            yt_ref[rows, :] = y + dsk_ref[0:1, h:h + 1] * xh
            ws.append((xdt * decs_t[h:h + 1, :]).astype(BF16))
        upd = _dot(jnp.concatenate(ws, axis=0), bm)
        for hh in range(hpg):
            h = g * hpg + hh
            h_ref[h] = ea_last[h:h + 1, :] * hprev[hh * hp:(hh + 1) * hp] + upd[hh * hp:(hh + 1) * hp]

    for t in range(d_in // LANES):
        yb_ref[:, t * LANES:(t + 1) * LANES] = yt_ref[t * LANES:(t + 1) * LANES, :].T

    yg = yb_ref[...] * _silu(z_ref[...])
    gsz = d_in // SSD_GROUPS
    yn = jnp.concatenate(
        [_rms(yg[:, g * gsz:(g + 1) * gsz], gn_ref[:, g * gsz:(g + 1) * gsz]).astype(BF16)
         for g in range(SSD_GROUPS)], axis=1)
    o_ref[...] = x_ref[...] + _dot(yn, wo_ref[...])

    @pl.when(c == pl.num_programs(1) - 1)
    def _():
        st_ref[...] = h_ref[...]
        cv_ref[...] = xpad_ref[0:SUBLANES, :]


def _ssd_prompt(x, g, w_z, w_xbc, w_dt, w_out, conv_w, conv_b, dt_bias, a_log, d_skip, gate_norm,
                bsz, nb, pad, n_heads):
    d = x.shape[1]
    d_in = n_heads * SSD_HEAD_DIM
    cdim = d_in + 2 * SSD_GROUPS * D_STATE
    assert n_heads <= LANES
    hp = LANES - n_heads
    a = -jnp.exp(a_log.astype(F32))
    ltri = jnp.tril(jnp.ones((BLK, BLK), F32)).astype(BF16)
    kern = functools.partial(_ssd_prompt_kernel, pad=pad, n_heads=n_heads)
    const = lambda shape: pl.BlockSpec(shape, lambda b, c: (0,) * len(shape),
                                       pipeline_mode=pl.Buffered(1))
    return pl.pallas_call(
        kern,
        grid=(bsz, nb),
        in_specs=[
            pl.BlockSpec((BLK, d), lambda b, c: (b * nb + c, 0)),
            const((1, d)), const((d, d_in)), const((d, cdim)), const((d, LANES)), const((d_in, d)),
            const((CONV_W, cdim)), const((1, cdim)), const((1, LANES)), const((1, LANES)),
            const((n_heads, 1)), const((n_heads, 1)), const((1, LANES)), const((1, d_in)),
            const((BLK, BLK)), const((BLK, BLK)),
        ],
        out_specs=[pl.BlockSpec((BLK, d), lambda b, c: (b * nb + c, 0)),
                   pl.BlockSpec((None, n_heads, SSD_HEAD_DIM, D_STATE), lambda b, c: (b, 0, 0, 0)),
                   pl.BlockSpec((None, SUBLANES, cdim), lambda b, c: (b, 0, 0))],
        out_shape=[jax.ShapeDtypeStruct((bsz * nb * BLK, d), F32),
                   jax.ShapeDtypeStruct((bsz, n_heads, SSD_HEAD_DIM, D_STATE), F32),
                   jax.ShapeDtypeStruct((bsz, SUBLANES, cdim), F32)],
        scratch_shapes=[pltpu.VMEM((BLK + SUBLANES, cdim), F32),
                        pltpu.VMEM((n_heads, SSD_HEAD_DIM, D_STATE), F32),
                        pltpu.VMEM((BLK, d_in), F32),
                        pltpu.VMEM((d_in, BLK), F32),
                        pltpu.VMEM((d_in, BLK), F32),
                        pltpu.VMEM((BLK, d_in), F32)],
        compiler_params=_cparams(("parallel", "arbitrary")),
        name="ssd_prompt",
    )(x, g.reshape(1, d), w_z, w_xbc, w_dt, w_out, conv_w, conv_b.reshape(1, cdim),
      jnp.pad(dt_bias, (0, hp)).reshape(1, LANES), jnp.pad(a, (0, hp)).reshape(1, LANES),
      dt_bias.reshape(n_heads, 1), a.reshape(n_heads, 1),
      jnp.pad(d_skip, (0, hp)).reshape(1, LANES), gate_norm.reshape(1, d_in), ltri, ltri.T)


def _ssd_sample_kernel(z_ref, xbc_ref, dt_ref, cst_ref, h0_ref, cw_ref, cb_ref, dtb_ref, a_ref,
                       dsk_ref, gn_ref, exp_ref, y_ref, h_ref, *, n_heads):
    d_in = n_heads * SSD_HEAD_DIM
    gn = SSD_GROUPS * D_STATE
    gsz = d_in // SSD_GROUPS
    hpg = n_heads // SSD_GROUPS
    conv = cb_ref[...] + cw_ref[CONV_W - 1:CONV_W, :] * xbc_ref[...]
    for j in range(CONV_W - 1):
        conv = conv + cw_ref[j:j + 1, :] * cst_ref[j:j + 1, :]
    xbc = _silu(conv)
    dt = _softplus(dt_ref[...] + dtb_ref[...])
    pad8 = lambda r: jnp.concatenate([r, jnp.zeros((SUBLANES - 1, r.shape[1]), F32)], axis=0)
    dt_ch = _dot_sel(pad8(dt), exp_ref[...])[0:1]
    da_ch = jnp.exp(_dot_sel(pad8(dt * a_ref[...]), exp_ref[...])[0:1])
    dsk_ch = _dot_sel(pad8(dsk_ref[...]), exp_ref[...])[0:1]
    xh = xbc[:, :d_in]
    xdt = xh * dt_ch
    ones = jnp.ones((SUBLANES, D_STATE), BF16)
    outs = []
    for g in range(SSD_GROUPS):
        sl = slice(g * gsz, (g + 1) * gsz)
        bm = xbc[:, d_in + g * D_STATE:d_in + (g + 1) * D_STATE]
        cm = xbc[:, d_in + gn + g * D_STATE:d_in + gn + (g + 1) * D_STATE]
        da_col = _dot_sel(pad8(da_ch[:, sl]), ones, fn=_dot_t0)
        xdt_col = _dot_sel(pad8(xdt[:, sl]), ones, fn=_dot_t0)
        h0 = h0_ref[g * hpg:(g + 1) * hpg].reshape(gsz, D_STATE)
        hn = da_col * h0 + xdt_col.astype(BF16).astype(F32) * bm.astype(BF16).astype(F32)
        h_ref[g * hpg:(g + 1) * hpg] = hn.reshape(hpg, SSD_HEAD_DIM, D_STATE)
        y_col = jnp.sum(hn * cm, axis=-1, keepdims=True)
        outs.append(y_col)
    y_cols = jnp.concatenate(outs, axis=0)
    rows = []
    eye = (lax.broadcasted_iota(I32, (LANES, LANES), 0)
           == lax.broadcasted_iota(I32, (LANES, LANES), 1)).astype(F32)
    for t in range(d_in // LANES):
        blk = y_cols[t * LANES:(t + 1) * LANES]
        rows.append(jnp.sum(blk * eye, axis=0, keepdims=True))
    y = jnp.concatenate(rows, axis=1) + dsk_ch * xh
    y = y * _silu(z_ref[...])
    for g in range(SSD_GROUPS):
        sl = slice(g * gsz, (g + 1) * gsz)
        y_ref[:, sl] = _rms(y[:, sl], gn_ref[:, sl]).astype(BF16)


def _ssd_sample(z, xbc_raw, dt_raw, conv_state, h0, conv_w, conv_b, dt_bias, a_log, d_skip,
                gate_norm, n_heads):
    db = z.shape[0]
    d_in = n_heads * SSD_HEAD_DIM
    cdim = d_in + 2 * SSD_GROUPS * D_STATE
    hp = LANES - n_heads
    a = -jnp.exp(a_log.astype(F32))
    expand = (jnp.arange(LANES)[:, None] == (jnp.arange(d_in) // SSD_HEAD_DIM)[None, :]).astype(BF16)
    kern = functools.partial(_ssd_sample_kernel, n_heads=n_heads)
    const = lambda shape: pl.BlockSpec(shape, lambda b: (0,) * len(shape))
    per_b = lambda shape: pl.BlockSpec((None,) + shape, lambda b: (b,) + (0,) * len(shape))
    y, h = pl.pallas_call(
        kern,
        grid=(db,),
        in_specs=[per_b((1, d_in)), per_b((1, cdim)), per_b((1, LANES)), per_b((CONV_W - 1, cdim)),
                  per_b((n_heads, SSD_HEAD_DIM, D_STATE)),
                  const((CONV_W, cdim)), const((1, cdim)), const((1, LANES)), const((1, LANES)),
                  const((1, LANES)), const((1, d_in)), const((LANES, d_in))],
        out_specs=[per_b((1, d_in)), per_b((n_heads, SSD_HEAD_DIM, D_STATE))],
        out_shape=[jax.ShapeDtypeStruct((db, 1, d_in), BF16),
                   jax.ShapeDtypeStruct((db, n_heads, SSD_HEAD_DIM, D_STATE), F32)],
        compiler_params=_cparams(("parallel",)),
        name="ssd_sample",
    )(z.reshape(db, 1, d_in), xbc_raw.reshape(db, 1, cdim), dt_raw.reshape(db, 1, LANES),
      conv_state, h0, conv_w, conv_b.reshape(1, cdim),
      jnp.pad(dt_bias, (0, hp)).reshape(1, LANES), jnp.pad(a, (0, hp)).reshape(1, LANES),
      jnp.pad(d_skip, (0, hp)).reshape(1, LANES), gate_norm.reshape(1, d_in), expand)
    return y.reshape(db, d_in), h


def kernel(x_prompt, x_sample, cache_k, cache_v, cache_kidx, page_table, state_ssm, state_conv,
           meta_tokens, norm_ffn_a, w_ffn_a_in, w_ffn_a_out, norm_mix, norm_ffn_b, w_ffn_b_in,
           w_ffn_b_out, w_attn_in, q_norm, k_norm, kidx_norm, w_attn_out,
           w_ssd_in, conv_w, conv_b, dt_bias, a_log, d_skip, gate_norm, w_ssd_out):
    bsz, seq, d = x_prompt.shape
    db = x_sample.shape[0]
    assert x_sample.shape[1] == 1
    t_real = N_META + seq
    nb = -(-t_real // BLK)
    t_pad = nb * BLK
    pad = t_pad - t_real
    npg = page_table.shape[1]
    past = npg * BLK
    topk_p = min(TOPK_MAX, seq // 4)
    topk_s = min(TOPK_MAX, (past + 1) // 4)
    qd, kd, qid = N_HEADS * HEAD_DIM, N_KV_HEADS * HEAD_DIM, N_IDX_HEADS * IDX_DIM
    d_in = w_ssd_out.shape[1]
    n_heads = d_in // SSD_HEAD_DIM
    cdim = d_in + 2 * SSD_GROUPS * D_STATE
    depth = norm_mix.shape[0]

    meta = jnp.broadcast_to(meta_tokens.astype(F32)[None], (bsz, N_META, d))
    xp = jnp.concatenate([jnp.zeros((bsz, pad, d), F32), meta, x_prompt], axis=1)
    xp = xp.reshape(bsz * t_pad, d)
    xs = x_sample.reshape(db, d)

    outs = {k: [] for k in ("kp", "vp", "kip", "ks", "vs", "kis", "hp", "cp", "hs", "cs")}
    for i in range(depth):
        wts = _ffn_weights(w_ffn_a_in[i], w_ffn_a_out[i])
        xp = _ffn(xp, norm_ffn_a[i], wts)
        xs = _ffn(xs, norm_ffn_a[i], wts)
        j = i // 2
        if i % 2 == 0:
            w = w_attn_in[j]
            o1, o2 = qd + 2 * kd + qid, qd + 2 * kd + qid + IDX_DIM
            w_pad = jnp.concatenate(
                [w[:, :o1], jnp.pad(w[:, o1:o2], ((0, 0), (0, LANES - IDX_DIM))),
                 jnp.pad(w[:, o2:], ((0, 0), (0, LANES - N_IDX_HEADS)))], axis=1).astype(BF16)
            w_out = w_attn_out[j].astype(BF16)

            k, v, ki, qt, qit, wt, kg, vt, kib = _attn_in(
                xp, norm_mix[i], w_pad, q_norm[j], k_norm[j], kidx_norm[j], prompt=True)
            gw = HEADS_PER_KV * BLK
            smax = (1.02 * math.sqrt(HEAD_DIM) * jnp.max(jnp.abs(q_norm[j]))
                    * jnp.max(jnp.abs(k_norm[j]))).reshape(1).astype(F32)
            o = _dsa_prompt(smax,
                            qit.reshape(bsz, nb, IDX_DIM, N_IDX_HEADS * BLK),
                            wt.reshape(bsz, nb, N_IDX_HEADS, BLK),
                            kib.reshape(bsz, nb, BLK, IDX_DIM),
                            qt.reshape(bsz, nb, N_KV_HEADS, HEAD_DIM, gw),
                            kg.reshape(N_KV_HEADS, bsz, nb, BLK, HEAD_DIM),
                            vt.reshape(bsz, nb, N_KV_HEADS, VT_ROWS, BLK),
                            pad, topk_p)
            xp = _out_proj(xp, o.reshape(bsz * t_pad, qd), w_out)
            outs["kp"].append(k.reshape(bsz, t_pad, N_KV_HEADS, HEAD_DIM)[:, pad:])
            outs["vp"].append(v.reshape(bsz, t_pad, N_KV_HEADS, HEAD_DIM)[:, pad:])
            outs["kip"].append(ki[:, :IDX_DIM].reshape(bsz, t_pad, IDX_DIM)[:, pad:])

            q, k, v, qi, ki, wi = _attn_in(
                xs, norm_mix[i], w_pad, q_norm[j], k_norm[j], kidx_norm[j], prompt=False)
            pg = math.gcd(npg, 32)
            qi3 = qi.reshape(db, N_IDX_HEADS, IDX_DIM)
            wi3 = wi[:, :N_IDX_HEADS].reshape(db, N_IDX_HEADS, 1)
            scores = _smp_scores(page_table, qi3, wi3, cache_kidx[j].transpose(0, 2, 1), pg)
            bias, nbias = _smp_select(scores, qi3, wi3, ki[:, :IDX_DIM].reshape(db, 1, IDX_DIM), topk_s)
            hsel = (jnp.arange(N_HEADS)[:, None] // HEADS_PER_KV == jnp.arange(N_KV_HEADS)[None, :])
            q_bd = (q.reshape(db, N_HEADS, 1, HEAD_DIM) * hsel[None, :, :, None].astype(BF16))
            q_bd = q_bd.reshape(db, N_HEADS, kd)
            npool = cache_k.shape[1]
            o = _smp_attend(page_table, q_bd, bias, nbias, k.reshape(db, 1, kd), v.reshape(db, 1, kd),
                            cache_k[j].transpose(0, 2, 3, 1).reshape(npool, kd, BLK),
                            cache_v[j].transpose(0, 2, 3, 1).reshape(npool, kd, BLK), pg)
            xs = _out_proj(xs, o.reshape(db, qd).astype(BF16), w_out)
            outs["ks"].append(k.reshape(db, 1, N_KV_HEADS, HEAD_DIM))
            outs["vs"].append(v.reshape(db, 1, N_KV_HEADS, HEAD_DIM))
            outs["kis"].append(ki[:, :IDX_DIM].reshape(db, 1, IDX_DIM))
        else:
            w = w_ssd_in[j]
            w_main = w[:, :d_in + cdim].astype(BF16)
            w_dt = jnp.pad(w[:, d_in + cdim:], ((0, 0), (0, LANES - n_heads))).astype(BF16)
            w_out = w_ssd_out[j].astype(BF16)
            sp = (conv_w[j], conv_b[j], dt_bias[j], a_log[j], d_skip[j], gate_norm[j])

            xp, hfin, ctail = _ssd_prompt(xp, norm_mix[i], w_main[:, :d_in], w_main[:, d_in:], w_dt, w_out,
                                          *sp, bsz, nb, pad, n_heads)
            outs["hp"].append(hfin)
            outs["cp"].append(ctail[:, SUBLANES - (CONV_W - 1):])

            z, xbc_raw, dt_raw = _norm_proj(xs, norm_mix[i], w_main, w_dt, d_in)
            y, hnew = _ssd_sample(z, xbc_raw, dt_raw, state_conv[j], state_ssm[j], *sp, n_heads)
            xs = _out_proj(xs, y, w_out)
            outs["hs"].append(hnew)
            outs["cs"].append(jnp.concatenate([state_conv[j][:, 1:], xbc_raw[:, None, :]], axis=1))
        wts = _ffn_weights(w_ffn_b_in[i], w_ffn_b_out[i])
        xp = _ffn(xp, norm_ffn_b[i], wts)
        xs = _ffn(xs, norm_ffn_b[i], wts)

    y_prompt = xp.reshape(bsz, t_pad, d)[:, pad + N_META:]
    y_sample = xs.reshape(db, 1, d)
    st = lambda key: jnp.stack(outs[key])
    return (y_prompt, y_sample, st("kp"), st("vp"), st("kip"), st("ks"), st("vs"), st("kis"),
            st("hp"), st("cp"), st("hs"), st("cs"))
```

```python
import functools
import math

import jax
import jax.numpy as jnp
from jax import lax
from jax.experimental import pallas as pl
from jax.experimental.pallas import tpu as pltpu

F32 = jnp.float32
BF16 = jnp.bfloat16
I32 = jnp.int32

N_META = 16
N_HEADS = 16
HEAD_DIM = 64
N_KV_HEADS = 4
HEADS_PER_KV = N_HEADS // N_KV_HEADS
N_IDX_HEADS = 8
IDX_DIM = 64
TOPK_MAX = 256
SSD_HEAD_DIM = 64
SSD_GROUPS = 4
D_STATE = 128
CONV_W = 4
EPS = 1e-6

LANES = 128
SUBLANES = 8
BLK = 128
NEG = -1e30
SAFE_LOGIT = 40.0
COUNT_UNROLL = 4
VT_ROWS = 80
INT_MIN = -2 ** 31
VMEM_LIMIT = 56 * 1024 * 1024


def _cparams(sem, vmem=VMEM_LIMIT):
    return pltpu.CompilerParams(dimension_semantics=sem, vmem_limit_bytes=vmem)


def _row_tile(rows, pref):
    best = None
    for d in range(SUBLANES, min(rows, pref) + 1, SUBLANES):
        if rows % d == 0:
            best = d
    assert best is not None, rows
    return best


def _rms(x, g):
    var = jnp.mean(x * x, axis=-1, keepdims=True)
    return x * lax.rsqrt(var + EPS) * g


def _dot(a, b):
    return jnp.dot(a, b, preferred_element_type=F32)


def _dot_t0(a, b):
    return lax.dot_general(a, b, (((0,), (0,)), ((), ())), preferred_element_type=F32)


def _dot_t1(a, b):
    return lax.dot_general(a, b, (((1,), (1,)), ((), ())), preferred_element_type=F32)


def _split2(a):
    hi = a.astype(BF16)
    lo = (a - hi.astype(F32)).astype(BF16)
    return hi, lo


def _split3(a):
    a0 = a.astype(BF16)
    r = a - a0.astype(F32)
    a1 = r.astype(BF16)
    a2 = (r - a1.astype(F32)).astype(BF16)
    return a0, a1, a2


def _dot_sel(a, m, fn=_dot):
    a0, a1, a2 = _split3(a)
    return fn(a0, m) + fn(a1, m) + fn(a2, m)


def _silu(x):
    return x * (1.0 / (1.0 + jnp.exp(-x)))


def _softplus(x):
    return jnp.maximum(x, 0.0) + jnp.log(1.0 + jnp.exp(-jnp.abs(x)))


def _ffn_kernel(x_ref, g_ref, wa_ref, wb_ref, wo_ref, o_ref, xn_ref, acc_ref):
    xn_ref[...] = _rms(x_ref[...], g_ref[...]).astype(BF16)
    acc_ref[...] = jnp.zeros_like(acc_ref)

    def chunk(j, carry):
        xn = xn_ref[...]
        a = _dot(xn, wa_ref[j])
        b = _dot(xn, wb_ref[j])
        h = (_silu(a) * b).astype(BF16)
        acc_ref[...] += _dot(h, wo_ref[j])
        return carry

    lax.fori_loop(0, wa_ref.shape[0], chunk, 0, unroll=True)
    o_ref[...] = x_ref[...] + 0.5 * acc_ref[...]


def _ffn_weights(w_in, w_out):
    d, hid = w_in.shape[0], w_out.shape[0]
    th = 256 if hid % 256 == 0 else LANES
    nh = hid // th
    w_in = w_in.astype(BF16)
    wa = w_in[:, :hid].reshape(d, nh, th).transpose(1, 0, 2)
    wb = w_in[:, hid:].reshape(d, nh, th).transpose(1, 0, 2)
    return wa, wb, w_out.astype(BF16).reshape(nh, th, d)


def _ffn(x, g, weights):
    wa, wb, wo = weights
    rows, d = x.shape
    nh, _, th = wa.shape
    tm = _row_tile(rows, 1024)
    resident = lambda shape: pl.BlockSpec(shape, lambda i: (0,) * len(shape),
                                          pipeline_mode=pl.Buffered(1))
    return pl.pallas_call(
        _ffn_kernel,
        grid=(rows // tm,),
        in_specs=[
            pl.BlockSpec((tm, d), lambda i: (i, 0)),
            resident((1, d)), resident((nh, d, th)), resident((nh, d, th)), resident((nh, th, d)),
        ],
        out_specs=pl.BlockSpec((tm, d), lambda i: (i, 0)),
        out_shape=jax.ShapeDtypeStruct((rows, d), F32),
        scratch_shapes=[pltpu.VMEM((tm, d), BF16), pltpu.VMEM((tm, d), F32)],
        compiler_params=_cparams(("parallel",)),
        name="ffn",
    )(x, g.reshape(1, d), wa, wb, wo)


def _attn_project(x_ref, g_ref, w_ref, qg_ref, kg_ref, kig_ref, gq_ref, eq_ref, gk_ref, ek_ref):
    qd = N_HEADS * HEAD_DIM
    kd = N_KV_HEADS * HEAD_DIM
    qid = N_IDX_HEADS * IDX_DIM
    xn = _rms(x_ref[...], g_ref[...]).astype(BF16)
    h = _dot(xn, w_ref[...])

    def head_norm(t, gsum_ref, gexp_ref, gain):
        ss = _dot_sel(t * t, gsum_ref[...])
        rs = lax.rsqrt(ss * (1.0 / HEAD_DIM) + EPS)
        return t * _dot_sel(rs, gexp_ref[...]) * gain

    o = 0
    q = head_norm(h[:, o:o + qd], gq_ref, eq_ref, qg_ref[...]) * (HEAD_DIM ** -0.5)
    o += qd
    k = head_norm(h[:, o:o + kd], gk_ref, ek_ref, kg_ref[...])
    o += kd
    v = h[:, o:o + kd]
    o += kd
    qi = h[:, o:o + qid] * (IDX_DIM ** -0.5)
    o += qid
    ki = h[:, o:o + LANES]
    var = jnp.sum(ki * ki, axis=-1, keepdims=True) * (1.0 / IDX_DIM)
    ki = ki * lax.rsqrt(var + EPS) * kig_ref[...]
    o += LANES
    wi = h[:, o:o + LANES] * (N_IDX_HEADS ** -0.5)
    return q, k, v, qi, ki, wi


def _attn_in_sample_kernel(*refs):
    q_o, k_o, v_o, qi_o, ki_o, wi_o = refs[10:]
    q, k, v, qi, ki, wi = _attn_project(*refs[:10])
    q_o[...] = q.astype(BF16)
    k_o[...] = k
    v_o[...] = v
    qi_o[...] = qi.astype(BF16)
    ki_o[...] = ki
    wi_o[...] = wi


def _attn_in_prompt_kernel(*refs):
    k_o, v_o, ki_o, qt_o, qit_o, wt_o, kg_o, vt_o, kib_o = refs[10:]
    q, k, v, qi, ki, wi = _attn_project(*refs[:10])
    k_o[...] = k
    v_o[...] = v
    ki_o[...] = ki
    kib_o[...] = ki[:, :IDX_DIM].astype(BF16)
    for g in range(N_KV_HEADS):
        kg_o[g] = k[:, g * HEAD_DIM:(g + 1) * HEAD_DIM].astype(BF16)
    tail = (lax.broadcasted_iota(I32, (VT_ROWS - HEAD_DIM, BLK), 0) == 0).astype(BF16)
    for r in range(q.shape[0] // BLK):
        rows = slice(r * BLK, (r + 1) * BLK)
        for t in range(N_HEADS * HEAD_DIM // LANES):
            tt = q[rows, t * LANES:(t + 1) * LANES].T.astype(BF16)
            for u in range(LANES // HEAD_DIM):
                h = t * (LANES // HEAD_DIM) + u
                g, hh = h // HEADS_PER_KV, h % HEADS_PER_KV
                qt_o[r, g, :, hh * BLK:(hh + 1) * BLK] = tt[u * HEAD_DIM:(u + 1) * HEAD_DIM]
        for t in range(N_IDX_HEADS * IDX_DIM // LANES):
            tt = qi[rows, t * LANES:(t + 1) * LANES].T.astype(BF16)
            for u in range(LANES // IDX_DIM):
                h = t * (LANES // IDX_DIM) + u
                qit_o[r, :, h * BLK:(h + 1) * BLK] = tt[u * IDX_DIM:(u + 1) * IDX_DIM]
        wt_o[r] = wi[rows].T[0:N_IDX_HEADS]
        for t in range(N_KV_HEADS * HEAD_DIM // LANES):
            tt = v[rows, t * LANES:(t + 1) * LANES].T.astype(BF16)
            for u in range(LANES // HEAD_DIM):
                vt_o[r, t * (LANES // HEAD_DIM) + u] = jnp.concatenate(
                    [tt[u * HEAD_DIM:(u + 1) * HEAD_DIM], tail], axis=0)


def _seg_mats(n_heads, hd):
    col = jnp.arange(n_heads * hd) // hd
    gsum = (col[:, None] == jnp.arange(LANES)[None, :]).astype(BF16)
    return gsum, gsum.T


def _attn_in(x, g, w_pad, q_gain, k_gain, ki_gain, prompt):
    rows, d = x.shape
    n = w_pad.shape[1]
    qd, kd, qid = N_HEADS * HEAD_DIM, N_KV_HEADS * HEAD_DIM, N_IDX_HEADS * IDX_DIM
    gw = HEADS_PER_KV * BLK
    tm = _row_tile(rows, 512)
    gq, eq = _seg_mats(N_HEADS, HEAD_DIM)
    gk, ek = _seg_mats(N_KV_HEADS, HEAD_DIM)
    const = lambda shape: pl.BlockSpec(shape, lambda i: (0,) * len(shape))
    rowb = lambda w: pl.BlockSpec((tm, w), lambda i: (i, 0))
    sds = jax.ShapeDtypeStruct
    if prompt:
        assert tm % BLK == 0
        nbk, tb = rows // BLK, tm // BLK
        blkb = lambda *s: pl.BlockSpec((tb,) + s, lambda i: (i,) + (0,) * len(s))
        kern = _attn_in_prompt_kernel
        out_specs = [rowb(kd), rowb(kd), rowb(LANES), blkb(N_KV_HEADS, HEAD_DIM, gw),
                     blkb(IDX_DIM, N_IDX_HEADS * BLK), blkb(N_IDX_HEADS, BLK),
                     pl.BlockSpec((N_KV_HEADS, tm, HEAD_DIM), lambda i: (0, i, 0)),
                     blkb(N_KV_HEADS, VT_ROWS, BLK),
                     rowb(IDX_DIM)]
        out_shape = [sds((rows, kd), F32), sds((rows, kd), F32), sds((rows, LANES), F32),
                     sds((nbk, N_KV_HEADS, HEAD_DIM, gw), BF16),
                     sds((nbk, IDX_DIM, N_IDX_HEADS * BLK), BF16),
                     sds((nbk, N_IDX_HEADS, BLK), F32),
                     sds((N_KV_HEADS, rows, HEAD_DIM), BF16),
                     sds((nbk, N_KV_HEADS, VT_ROWS, BLK), BF16),
                     sds((rows, IDX_DIM), BF16)]
    else:
        kern = _attn_in_sample_kernel
        out_specs = [rowb(qd), rowb(kd), rowb(kd), rowb(qid), rowb(LANES), rowb(LANES)]
        out_shape = [sds((rows, qd), BF16), sds((rows, kd), F32), sds((rows, kd), F32),
                     sds((rows, qid), BF16), sds((rows, LANES), F32), sds((rows, LANES), F32)]
    return pl.pallas_call(
        kern,
        grid=(rows // tm,),
        in_specs=[rowb(d), const((1, d)), const((d, n)), const((1, qd)), const((1, kd)),
                  const((1, LANES)), const((qd, LANES)), const((LANES, qd)),
                  const((kd, LANES)), const((LANES, kd))],
        out_specs=out_specs,
        out_shape=out_shape,
        compiler_params=_cparams(("parallel",)),
        name="attn_in_prompt" if prompt else "attn_in_sample",
    )(x, g.reshape(1, d), w_pad,
      jnp.tile(q_gain, N_HEADS).reshape(1, qd), jnp.tile(k_gain, N_KV_HEADS).reshape(1, kd),
      jnp.pad(ki_gain, (0, LANES - IDX_DIM)).reshape(1, LANES), gq, eq, gk, ek)


def _norm_proj_kernel(x_ref, g_ref, w_ref, ws_ref, oa_ref, ob_ref, os_ref, xn_ref, *, na):
    j = pl.program_id(1)

    @pl.when(j == 0)
    def _():
        xn = _rms(x_ref[...], g_ref[...]).astype(BF16)
        xn_ref[...] = xn
        os_ref[...] = _dot(xn, ws_ref[...])

    r = _dot(xn_ref[...], w_ref[...])

    @pl.when(j < na)
    def _():
        oa_ref[...] = r

    @pl.when(j >= na)
    def _():
        ob_ref[...] = r


def _norm_proj(x, g, w_main, w_side, n_a):
    rows, d = x.shape
    n = w_main.shape[1]
    tm = _row_tile(rows, 1024)
    tn = math.gcd(math.gcd(n_a, n - n_a), 1024)
    na = n_a // tn
    return pl.pallas_call(
        functools.partial(_norm_proj_kernel, na=na),
        grid=(rows // tm, n // tn),
        in_specs=[
            pl.BlockSpec((tm, d), lambda i, j: (i, 0)),
            pl.BlockSpec((1, d), lambda i, j: (0, 0)),
            pl.BlockSpec((d, tn), lambda i, j: (0, j)),
            pl.BlockSpec((d, LANES), lambda i, j: (0, 0)),
        ],
        out_specs=[pl.BlockSpec((tm, tn), lambda i, j: (i, jnp.minimum(j, na - 1))),
                   pl.BlockSpec((tm, tn), lambda i, j: (i, jnp.maximum(j - na, 0))),
                   pl.BlockSpec((tm, LANES), lambda i, j: (i, 0))],
        out_shape=[jax.ShapeDtypeStruct((rows, n_a), F32),
                   jax.ShapeDtypeStruct((rows, n - n_a), F32),
                   jax.ShapeDtypeStruct((rows, LANES), F32)],
        scratch_shapes=[pltpu.VMEM((tm, d), BF16)],
        compiler_params=_cparams(("parallel", "arbitrary")),
        name="norm_proj",
    )(x, g.reshape(1, d), w_main, w_side)


def _out_proj_kernel(x_ref, y_ref, w_ref, o_ref):
    o_ref[...] = x_ref[...] + _dot(y_ref[...], w_ref[...])


def _out_proj(x, y, w):
    rows, d = x.shape
    k = y.shape[1]
    tm = _row_tile(rows, 512)
    return pl.pallas_call(
        _out_proj_kernel,
        grid=(rows // tm,),
        in_specs=[pl.BlockSpec((tm, d), lambda i: (i, 0)),
                  pl.BlockSpec((tm, k), lambda i: (i, 0)),
                  pl.BlockSpec((k, d), lambda i: (0, 0))],
        out_specs=pl.BlockSpec((tm, d), lambda i: (i, 0)),
        out_shape=jax.ShapeDtypeStruct((rows, d), F32),
        compiler_params=_cparams(("parallel",)),
        name="out_proj",
    )(x, y, w)


def _sort_key(score):
    bits = pltpu.bitcast(score, I32)
    return bits ^ ((bits >> 31) & 0x7FFFFFFF)


def _tile_fold(x, op):
    r = x[0:SUBLANES]
    for t in range(1, x.shape[0] // SUBLANES):
        r = op(r, x[t * SUBLANES:(t + 1) * SUBLANES])
    return r


def _dsa_prompt_kernel(smax_ref, qit_ref, wt_ref, ki_ref, qt_ref, k_ref, vt_ref, o_ref,
                       key_ref, bias_ref, acc_ref, *, pad, topk):
    i = pl.program_id(1)
    nch = i + 1
    npair = (nch + 1) // 2
    last = ki_ref.shape[0] - 1
    row = lax.broadcasted_iota(I32, (BLK, BLK), 0)
    lane = lax.broadcasted_iota(I32, (BLK, BLK), 1)
    t_pos = i * BLK + lane

    def idx_dots(c):
        return _dot(ki_ref[jnp.minimum(c, last)], qit_ref[...])

    def idx_keys(c, dots):
        sc = jnp.zeros((BLK, BLK), F32)
        for h in range(N_IDX_HEADS):
            sc = sc + wt_ref[h:h + 1, :] * jnp.maximum(dots[:, h * BLK:(h + 1) * BLK], 0.0)
        s_pos = c * BLK + row
        valid = (s_pos <= t_pos) & (s_pos >= pad)
        key_ref[c] = jnp.where(valid, _sort_key(sc), INT_MIN)

    def p1(j, carry):
        for u in range(2):
            idx_keys(2 * j + u, idx_dots(2 * j + u))
        return carry

    lax.fori_loop(0, npair, p1, 0)

    for u in range(COUNT_UNROLL - 2):
        key_ref[2 * npair + u] = jnp.full((BLK, BLK), INT_MIN, I32)
    ntrip = (nch + COUNT_UNROLL - 1) // COUNT_UNROLL

    def count(pred):
        def body(j, cnt):
            for u in range(COUNT_UNROLL):
                c = COUNT_UNROLL * j + u
                cnt = cnt + jnp.where(pred(key_ref[c], c * BLK + row), 1, 0)
            return cnt
        cnt = lax.fori_loop(0, ntrip, body, jnp.zeros((BLK, BLK), I32))
        return jnp.sum(cnt, axis=0, keepdims=True)

    c0 = count(lambda k, s: k >= 0)
    thr0 = jnp.where(c0 >= topk, 0, INT_MIN).astype(I32)

    def bit_step(it, st):
        thr, n_ge = st
        cand = thr + (jnp.int32(1) << (30 - it))
        cnt = count(lambda k, s: k >= cand)
        ok = cnt >= topk
        return jnp.where(ok, cand, thr), jnp.where(ok, cnt, n_ge)

    thr, n_ge = lax.fori_loop(0, 31, bit_step, (thr0, c0))
    tied = (n_ge > topk) & (thr > INT_MIN)
    any_tied = jnp.max(tied.astype(I32))
    nbits = (key_ref.shape[0] * BLK).bit_length()

    def idx_search():
        need = topk - count(lambda k, s: k > thr)

        def step(it, lo):
            cand = lo + (jnp.int32(1) << (nbits - 1 - it))
            below = count(lambda k, s: (k == thr) & (s < cand))
            return jnp.where(below < need, cand, lo)
        return lax.fori_loop(0, nbits, step, jnp.zeros((1, BLK), I32))

    jcut = lax.cond(any_tied > 0, idx_search,
                    lambda: jnp.full((1, BLK), 2 ** 30, I32))

    def p2(j, carry):
        for u in range(2):
            c = 2 * j + u
            k = key_ref[c]
            sel = (k > thr) | ((k == thr) & (c * BLK + row <= jcut))
            sel = sel & (k > INT_MIN)
            bias_ref[c] = jnp.where(sel, 0.0, NEG)
        return carry

    lax.fori_loop(0, npair, p2, 0)

    gw = HEADS_PER_KV * BLK

    def logits(c):
        bb = jnp.concatenate([bias_ref[c]] * HEADS_PER_KV, axis=1)
        cr = jnp.minimum(c, last)
        return [_dot(k_ref[g, cr], qt_ref[g]) + bb for g in range(N_KV_HEADS)]

    def attend(shift):
        acc_ref[...] = jnp.zeros_like(acc_ref)

        def body(j, carry):
            p = []
            for u in range(2):
                s = logits(2 * j + u)
                if shift is not None:
                    s = [s[g] - shift[g] for g in range(N_KV_HEADS)]
                p.append([jnp.exp(x).astype(BF16) for x in s])
            c1 = jnp.minimum(2 * j + 1, last)
            for g in range(N_KV_HEADS):
                vt = jnp.concatenate([vt_ref[2 * j, g], vt_ref[c1, g]], axis=1)
                pp = jnp.concatenate([p[0][g], p[1][g]], axis=0)
                acc_ref[g] += _dot(vt, pp)
            return carry

        lax.fori_loop(0, npair, body, 0)

    safe = smax_ref[0] <= SAFE_LOGIT

    @pl.when(safe)
    def _():
        attend(None)

    @pl.when(jnp.logical_not(safe))
    def _():
        def pa(j, m):
            for u in range(2):
                s = logits(2 * j + u)
                m = tuple(jnp.maximum(m[g], _tile_fold(s[g], jnp.maximum)) for g in range(N_KV_HEADS))
            return m

        m0 = tuple(jnp.full((SUBLANES, gw), NEG, F32) for _ in range(N_KV_HEADS))
        m = lax.fori_loop(0, npair, pa, m0)
        attend([jnp.max(x, axis=0, keepdims=True) for x in m])

    q_row = i * BLK + lax.broadcasted_iota(I32, (BLK, LANES), 0)
    for g in range(N_KV_HEADS):
        a = acc_ref[g]
        res = a[0:HEAD_DIM] * (1.0 / a[HEAD_DIM:HEAD_DIM + 1])
        for t in range(HEADS_PER_KV // 2):
            two = jnp.concatenate([res[:, (2 * t + u) * BLK:(2 * t + u + 1) * BLK] for u in range(2)],
                                  axis=0)
            two = jnp.where(q_row >= pad, two.T, 0.0)
            lo = (g * HEADS_PER_KV + 2 * t) * HEAD_DIM
            o_ref[:, lo:lo + LANES] = two.astype(BF16)


def _dsa_prompt(smax, qit, wt, kib, qt, kg, vt, pad, topk):
    bsz, nb = qit.shape[:2]
    t_pad = nb * BLK
    gw = HEADS_PER_KV * BLK
    kern = functools.partial(_dsa_prompt_kernel, pad=pad, topk=topk)
    return pl.pallas_call(
        kern,
        grid=(bsz, nb),
        in_specs=[
            pl.BlockSpec(memory_space=pltpu.SMEM),
            pl.BlockSpec((None, None, IDX_DIM, N_IDX_HEADS * BLK), lambda b, i: (b, i, 0, 0)),
            pl.BlockSpec((None, None, N_IDX_HEADS, BLK), lambda b, i: (b, i, 0, 0)),
            pl.BlockSpec((None, nb, BLK, IDX_DIM), lambda b, i: (b, 0, 0, 0)),
            pl.BlockSpec((None, None, N_KV_HEADS, HEAD_DIM, gw), lambda b, i: (b, i, 0, 0, 0)),
            pl.BlockSpec((N_KV_HEADS, None, nb, BLK, HEAD_DIM), lambda b, i: (0, b, 0, 0, 0)),
            pl.BlockSpec((None, nb, N_KV_HEADS, VT_ROWS, BLK), lambda b, i: (b, 0, 0, 0, 0)),
        ],
        out_specs=pl.BlockSpec((None, BLK, N_HEADS * HEAD_DIM), lambda b, i: (b, i, 0)),
        out_shape=jax.ShapeDtypeStruct((bsz, t_pad, N_HEADS * HEAD_DIM), BF16),
        scratch_shapes=[pltpu.VMEM((nb + COUNT_UNROLL - 1, BLK, BLK), I32),
                        pltpu.VMEM((nb + 1, BLK, BLK), F32),
                        pltpu.VMEM((N_KV_HEADS, VT_ROWS, gw), F32)],
        compiler_params=_cparams(("parallel", "arbitrary")),
        name="dsa_prompt",
    )(smax, qit, wt, kib, qt, kg, vt)


def _smp_scores_kernel(pt_ref, qi_ref, w_ref, *refs, pg):
    page_refs, o_ref = refs[:pg], refs[pg]
    qi = qi_ref[...]
    w = w_ref[...]
    for p in range(pg):
        d = _dot(qi, page_refs[p][...].astype(BF16))
        o_ref[p:p + 1, :] = jnp.sum(w * jnp.maximum(d, 0.0), axis=0, keepdims=True)


def _smp_scores(page_table, qi, wi, kidx_t, pg):
    db, npg = page_table.shape
    kern = functools.partial(_smp_scores_kernel, pg=pg)
    page_spec = lambda p: pl.BlockSpec(
        (None, IDX_DIM, BLK), lambda b, j, pt: (pt[b * npg + j * pg + p], 0, 0))
    return pl.pallas_call(
        kern,
        grid_spec=pltpu.PrefetchScalarGridSpec(
            num_scalar_prefetch=1,
            grid=(db, npg // pg),
            in_specs=[pl.BlockSpec((None, N_IDX_HEADS, IDX_DIM), lambda b, j, pt: (b, 0, 0)),
                      pl.BlockSpec((None, N_IDX_HEADS, 1), lambda b, j, pt: (b, 0, 0))]
                     + [page_spec(p) for p in range(pg)],
            out_specs=pl.BlockSpec((None, pg, BLK), lambda b, j, pt: (b, j, 0)),
        ),
        out_shape=jax.ShapeDtypeStruct((db, npg, BLK), F32),
        compiler_params=_cparams(("parallel", "arbitrary")),
        name="sample_scores",
    )(page_table.reshape(-1), qi, wi, *([kidx_t] * pg))


def _smp_select_kernel(sc_ref, qi_ref, w_ref, kin_ref, bias_ref, nb_ref, *, topk, past):
    db, npg, _ = sc_ref.shape
    key = _sort_key(sc_ref[...])
    qi = qi_ref[...].astype(F32)
    kn = kin_ref[...].astype(BF16).astype(F32)
    d = jnp.sum(qi * kn, axis=-1, keepdims=True)
    s_new = jnp.sum(w_ref[...] * jnp.maximum(d, 0.0), axis=1, keepdims=True)
    key_new = _sort_key(s_new)
    pos = (lax.broadcasted_iota(I32, (db, npg, BLK), 1) * BLK
           + lax.broadcasted_iota(I32, (db, npg, BLK), 2))

    def count(pred_past, pred_new):
        c = jnp.sum(jnp.where(pred_past, 1, 0), axis=2, keepdims=True)
        return jnp.sum(c, axis=1, keepdims=True) + jnp.where(pred_new, 1, 0)

    thr = jnp.where(count(key >= 0, key_new >= 0) >= topk, 0, INT_MIN).astype(I32)

    def bit_step(it, thr):
        cand = thr + (jnp.int32(1) << (30 - it))
        return jnp.where(count(key >= cand, key_new >= cand) >= topk, cand, thr)

    thr = lax.fori_loop(0, 31, bit_step, thr)
    need = topk - count(key > thr, key_new > thr)
    nbits = max(1, past.bit_length())

    def step(it, lo):
        cand = lo + (jnp.int32(1) << (nbits - 1 - it))
        below = count((key == thr) & (pos < cand), (key_new == thr) & (past < cand))
        return jnp.where(below < need, cand, lo)

    jcut = lax.fori_loop(0, nbits, step, jnp.zeros((db, 1, 1), I32))
    sel = (key > thr) | ((key == thr) & (pos <= jcut))
    bias_ref[...] = jnp.where(sel, 0.0, NEG)
    sel_new = (key_new > thr) | ((key_new == thr) & (past <= jcut))
    nb_ref[...] = jnp.broadcast_to(jnp.where(sel_new, 0.0, NEG), nb_ref.shape)


def _smp_select(scores, qi, wi, ki_new, topk):
    db, npg, _ = scores.shape
    kern = functools.partial(_smp_select_kernel, topk=topk, past=npg * BLK)
    full = lambda *s: pl.BlockSpec(s, lambda i: (0,) * len(s))
    return pl.pallas_call(
        kern,
        grid=(1,),
        in_specs=[full(db, npg, BLK), full(db, N_IDX_HEADS, IDX_DIM), full(db, N_IDX_HEADS, 1),
                  full(db, 1, IDX_DIM)],
        out_specs=[full(db, npg, BLK), full(db, SUBLANES, LANES)],
        out_shape=[jax.ShapeDtypeStruct((db, npg, BLK), F32),
                   jax.ShapeDtypeStruct((db, SUBLANES, LANES), F32)],
        compiler_params=_cparams(("arbitrary",)),
        name="sample_select",
    )(scores, qi, wi, ki_new)


def _smp_attend_kernel(pt_ref, q_ref, bias_ref, nb_ref, kn_ref, vn_ref, *refs, pg):
    k_refs, v_refs = refs[:pg], refs[pg:2 * pg]
    o_ref, m_ref, l_ref, acc_ref = refs[2 * pg:]
    j = pl.program_id(1)
    kd = N_KV_HEADS * HEAD_DIM

    @pl.when(j == 0)
    def _():
        m_ref[...] = jnp.full_like(m_ref, NEG)
        l_ref[...] = jnp.zeros_like(l_ref)
        acc_ref[...] = jnp.zeros_like(acc_ref)

    q = q_ref[...]
    s = [_dot(q, k_refs[p][...].astype(BF16)) + bias_ref[p:p + 1, :] for p in range(pg)]
    m_old = m_ref[...]
    m_new = m_old
    for p in range(pg):
        m_new = jnp.maximum(m_new, jnp.max(s[p], axis=-1, keepdims=True))
    alpha = jnp.exp(m_old - m_new)
    l = l_ref[...] * alpha
    acc = acc_ref[...] * alpha
    for p in range(pg):
        e = jnp.exp(s[p] - m_new)
        l = l + jnp.sum(e, axis=-1, keepdims=True)
        acc = acc + _dot_t1(e.astype(BF16), v_refs[p][...].astype(BF16))
    m_ref[...] = m_new
    l_ref[...] = l
    acc_ref[...] = acc

    @pl.when(j == pl.num_programs(1) - 1)
    def _():
        qf = q.astype(F32)
        kn = kn_ref[...].astype(BF16).astype(F32)
        vn = vn_ref[...].astype(BF16).astype(F32)
        s_new = jnp.sum(qf * kn, axis=-1, keepdims=True) + nb_ref[0:1, 0:1]
        m_fin = jnp.maximum(m_new, s_new)
        a2 = jnp.exp(m_new - m_fin)
        e_new = jnp.exp(s_new - m_fin)
        l_fin = l * a2 + e_new
        acc_fin = acc * a2 + e_new.astype(BF16).astype(F32) * vn
        res = acc_fin / l_fin
        hgrp = lax.broadcasted_iota(I32, (N_HEADS, HEAD_DIM), 0) // HEADS_PER_KV
        out = jnp.zeros((N_HEADS, HEAD_DIM), F32)
        for g in range(N_KV_HEADS):
            out = out + jnp.where(hgrp == g, res[:, g * HEAD_DIM:(g + 1) * HEAD_DIM], 0.0)
        o_ref[...] = out


def _smp_attend(page_table, q_bd, bias, nbias, k_new, v_new, k_t, v_t, pg):
    db, npg = page_table.shape
    kd = N_KV_HEADS * HEAD_DIM
    kern = functools.partial(_smp_attend_kernel, pg=pg)
    page_spec = lambda p: pl.BlockSpec(
        (None, kd, BLK), lambda b, j, pt: (pt[b * npg + j * pg + p], 0, 0))
    per_b = lambda shape: pl.BlockSpec((None,) + shape, lambda b, j, pt: (b, 0, 0))
    return pl.pallas_call(
        kern,
        grid_spec=pltpu.PrefetchScalarGridSpec(
            num_scalar_prefetch=1,
            grid=(db, npg // pg),
            in_specs=[per_b((N_HEADS, kd)),
                      pl.BlockSpec((None, pg, BLK), lambda b, j, pt: (b, j, 0)),
                      per_b((SUBLANES, LANES)), per_b((1, kd)), per_b((1, kd))]
                     + [page_spec(p) for p in range(pg)] * 2,
            out_specs=per_b((N_HEADS, HEAD_DIM)),
            scratch_shapes=[pltpu.VMEM((N_HEADS, 1), F32), pltpu.VMEM((N_HEADS, 1), F32),
                            pltpu.VMEM((N_HEADS, kd), F32)],
        ),
        out_shape=jax.ShapeDtypeStruct((db, N_HEADS, HEAD_DIM), F32),
        compiler_params=_cparams(("parallel", "arbitrary")),
        name="sample_attend",
    )(page_table.reshape(-1), q_bd, bias, nbias, k_new, v_new,
      *([k_t] * pg), *([v_t] * pg))


def _ssd_prompt_kernel(x_ref, g_ref, wz_ref, wx_ref, wdt_ref, wo_ref,
                       cw_ref, cb_ref, dtb_ref, a_ref,
                       dtbc_ref, ac_ref, dsk_ref, gn_ref, ltri_ref, utri_ref,
                       o_ref, st_ref, cv_ref, xpad_ref, h_ref, yb_ref, xt_ref, yt_ref, z_ref,
                       *, pad, n_heads):
    c = pl.program_id(1)
    d_in = n_heads * SSD_HEAD_DIM
    gn = SSD_GROUPS * D_STATE
    hpg = n_heads // SSD_GROUPS

    @pl.when(c == 0)
    def _():
        xpad_ref[0:SUBLANES, :] = jnp.zeros((SUBLANES, xpad_ref.shape[1]), F32)
        h_ref[...] = jnp.zeros_like(h_ref)

    xn = _rms(x_ref[...], g_ref[...]).astype(BF16)
    z_ref[...] = _dot(xn, wz_ref[...])
    dt_raw = _dot(xn, wdt_ref[...])
    dt_raw_t = dt_raw.T[0:n_heads]

    xpad_ref[SUBLANES:, :] = _dot(xn, wx_ref[...])
    conv = cb_ref[...] + cw_ref[CONV_W - 1:CONV_W, :] * xpad_ref[SUBLANES:, :]
    for j in range(CONV_W - 1):
        sh = CONV_W - 1 - j
        conv = conv + cw_ref[j:j + 1, :] * xpad_ref[SUBLANES - sh:SUBLANES - sh + BLK, :]
    xpad_ref[0:SUBLANES, :] = xpad_ref[BLK:BLK + SUBLANES, :]
    xbc = _silu(conv)

    live = (c > 0) | (lax.broadcasted_iota(I32, (BLK, LANES), 0) >= pad)
    dt = jnp.where(live, _softplus(dt_raw + dtb_ref[...]), 0.0)
    acum = _dot_sel(dt * a_ref[...], ltri_ref[...], fn=lambda x, m: _dot(m, x))
    live_t = (c > 0) | (lax.broadcasted_iota(I32, (n_heads, BLK), 1) >= pad)
    dtt = jnp.where(live_t, _softplus(dt_raw_t + dtbc_ref[...]), 0.0)
    acum_t = _dot_sel(dtt * ac_ref[...], utri_ref[...])
    for t in range(d_in // LANES):
        xt_ref[t * LANES:(t + 1) * LANES, :] = xbc[:, t * LANES:(t + 1) * LANES].T
    a_last = acum_t[:, BLK - 1:BLK]
    ecol_t = jnp.exp(acum_t)
    decs_t = jnp.exp(a_last - acum_t)
    ea_last = jnp.exp(a_last)
    causal_t = (lax.broadcasted_iota(I32, (BLK, BLK), 0) <= lax.broadcasted_iota(I32, (BLK, BLK), 1))
    hp = SSD_HEAD_DIM

    for g in range(SSD_GROUPS):
        bm = xbc[:, d_in + g * D_STATE:d_in + (g + 1) * D_STATE].astype(BF16)
        ct = xbc[:, d_in + gn + g * D_STATE:d_in + gn + (g + 1) * D_STATE].T.astype(BF16)
        cb_t = _dot(bm, ct)
        hprev = h_ref[g * hpg:(g + 1) * hpg].reshape(hpg * hp, D_STATE)
        y_off = _dot(hprev.astype(BF16), ct)
        ws = []
        for hh in range(hpg):
            h = g * hpg + hh
            rows = slice(h * hp, (h + 1) * hp)
            xh = xt_ref[rows, :]
            xdt = xh * dtt[h:h + 1, :]
            decay_t = jnp.exp(jnp.where(causal_t, acum_t[h:h + 1, :] - acum[:, h:h + 1], NEG))
            y = _dot(xdt.astype(BF16), (cb_t * decay_t).astype(BF16))
            y = y + ecol_t[h:h + 1, :] * y_off[hh * hp:(hh + 1) * hp]
            yt_ref[rows, :] = y + dsk_ref[0:1, h:h + 1] * xh
            ws.append((xdt * decs_t[h:h + 1, :]).astype(BF16))
        upd = _dot(jnp.concatenate(ws, axis=0), bm)
        for hh in range(hpg):
            h = g * hpg + hh
            h_ref[h] = ea_last[h:h + 1, :] * hprev[hh * hp:(hh + 1) * hp] + upd[hh * hp:(hh + 1) * hp]

    for t in range(d_in // LANES):
        yb_ref[:, t * LANES:(t + 1) * LANES] = yt_ref[t * LANES:(t + 1) * LANES, :].T

    yg = yb_ref[...] * _silu(z_ref[...])
    gsz = d_in // SSD_GROUPS
    yn = jnp.concatenate(
        [_rms(yg[:, g * gsz:(g + 1) * gsz], gn_ref[:, g * gsz:(g + 1) * gsz]).astype(BF16)
         for g in range(SSD_GROUPS)], axis=1)
    o_ref[...] = x_ref[...] + _dot(yn, wo_ref[...])

    @pl.when(c == pl.num_programs(1) - 1)
    def _():
        st_ref[...] = h_ref[...]
        cv_ref[...] = xpad_ref[0:SUBLANES, :]


def _ssd_prompt(x, g, w_z, w_xbc, w_dt, w_out, conv_w, conv_b, dt_bias, a_log, d_skip, gate_norm,
                bsz, nb, pad, n_heads):
    d = x.shape[1]
    d_in = n_heads * SSD_HEAD_DIM
    cdim = d_in + 2 * SSD_GROUPS * D_STATE
    assert n_heads <= LANES
    hp = LANES - n_heads
    a = -jnp.exp(a_log.astype(F32))
    ltri = jnp.tril(jnp.ones((BLK, BLK), F32)).astype(BF16)
    kern = functools.partial(_ssd_prompt_kernel, pad=pad, n_heads=n_heads)
    const = lambda shape: pl.BlockSpec(shape, lambda b, c: (0,) * len(shape),
                                       pipeline_mode=pl.Buffered(1))
    return pl.pallas_call(
        kern,
        grid=(bsz, nb),
        in_specs=[
            pl.BlockSpec((BLK, d), lambda b, c: (b * nb + c, 0)),
            const((1, d)), const((d, d_in)), const((d, cdim)), const((d, LANES)), const((d_in, d)),
            const((CONV_W, cdim)), const((1, cdim)), const((1, LANES)), const((1, LANES)),
            const((n_heads, 1)), const((n_heads, 1)), const((1, LANES)), const((1, d_in)),
            const((BLK, BLK)), const((BLK, BLK)),
        ],
        out_specs=[pl.BlockSpec((BLK, d), lambda b, c: (b * nb + c, 0)),
                   pl.BlockSpec((None, n_heads, SSD_HEAD_DIM, D_STATE), lambda b, c: (b, 0, 0, 0)),
                   pl.BlockSpec((None, SUBLANES, cdim), lambda b, c: (b, 0, 0))],
        out_shape=[jax.ShapeDtypeStruct((bsz * nb * BLK, d), F32),
                   jax.ShapeDtypeStruct((bsz, n_heads, SSD_HEAD_DIM, D_STATE), F32),
                   jax.ShapeDtypeStruct((bsz, SUBLANES, cdim), F32)],
        scratch_shapes=[pltpu.VMEM((BLK + SUBLANES, cdim), F32),
                        pltpu.VMEM((n_heads, SSD_HEAD_DIM, D_STATE), F32),
                        pltpu.VMEM((BLK, d_in), F32),
                        pltpu.VMEM((d_in, BLK), F32),
                        pltpu.VMEM((d_in, BLK), F32),
                        pltpu.VMEM((BLK, d_in), F32)],
        compiler_params=_cparams(("parallel", "arbitrary")),
        name="ssd_prompt",
    )(x, g.reshape(1, d), w_z, w_xbc, w_dt, w_out, conv_w, conv_b.reshape(1, cdim),
      jnp.pad(dt_bias, (0, hp)).reshape(1, LANES), jnp.pad(a, (0, hp)).reshape(1, LANES),
      dt_bias.reshape(n_heads, 1), a.reshape(n_heads, 1),
      jnp.pad(d_skip, (0, hp)).reshape(1, LANES), gate_norm.reshape(1, d_in), ltri, ltri.T)


def _ssd_sample_kernel(z_ref, xbc_ref, dt_ref, cst_ref, h0_ref, cw_ref, cb_ref, dtb_ref, a_ref,
                       dsk_ref, gn_ref, exp_ref, y_ref, h_ref, *, n_heads):
    d_in = n_heads * SSD_HEAD_DIM
    gn = SSD_GROUPS * D_STATE
    gsz = d_in // SSD_GROUPS
    hpg = n_heads // SSD_GROUPS
    conv = cb_ref[...] + cw_ref[CONV_W - 1:CONV_W, :] * xbc_ref[...]
    for j in range(CONV_W - 1):
        conv = conv + cw_ref[j:j + 1, :] * cst_ref[j:j + 1, :]
    xbc = _silu(conv)
    dt = _softplus(dt_ref[...] + dtb_ref[...])
    pad8 = lambda r: jnp.concatenate([r, jnp.zeros((SUBLANES - 1, r.shape[1]), F32)], axis=0)
    dt_ch = _dot_sel(pad8(dt), exp_ref[...])[0:1]
    da_ch = jnp.exp(_dot_sel(pad8(dt * a_ref[...]), exp_ref[...])[0:1])
    dsk_ch = _dot_sel(pad8(dsk_ref[...]), exp_ref[...])[0:1]
    xh = xbc[:, :d_in]
    xdt = xh * dt_ch
    ones = jnp.ones((SUBLANES, D_STATE), BF16)
    outs = []
    for g in range(SSD_GROUPS):
        sl = slice(g * gsz, (g + 1) * gsz)
        bm = xbc[:, d_in + g * D_STATE:d_in + (g + 1) * D_STATE]
        cm = xbc[:, d_in + gn + g * D_STATE:d_in + gn + (g + 1) * D_STATE]
        da_col = _dot_sel(pad8(da_ch[:, sl]), ones, fn=_dot_t0)
        xdt_col = _dot_sel(pad8(xdt[:, sl]), ones, fn=_dot_t0)
        h0 = h0_ref[g * hpg:(g + 1) * hpg].reshape(gsz, D_STATE)
        hn = da_col * h0 + xdt_col.astype(BF16).astype(F32) * bm.astype(BF16).astype(F32)
        h_ref[g * hpg:(g + 1) * hpg] = hn.reshape(hpg, SSD_HEAD_DIM, D_STATE)
        y_col = jnp.sum(hn * cm, axis=-1, keepdims=True)
        outs.append(y_col)
    y_cols = jnp.concatenate(outs, axis=0)
    rows = []
    eye = (lax.broadcasted_iota(I32, (LANES, LANES), 0)
           == lax.broadcasted_iota(I32, (LANES, LANES), 1)).astype(F32)
    for t in range(d_in // LANES):
        blk = y_cols[t * LANES:(t + 1) * LANES]
        rows.append(jnp.sum(blk * eye, axis=0, keepdims=True))
    y = jnp.concatenate(rows, axis=1) + dsk_ch * xh
    y = y * _silu(z_ref[...])
    for g in range(SSD_GROUPS):
        sl = slice(g * gsz, (g + 1) * gsz)
        y_ref[:, sl] = _rms(y[:, sl], gn_ref[:, sl]).astype(BF16)


def _ssd_sample(z, xbc_raw, dt_raw, conv_state, h0, conv_w, conv_b, dt_bias, a_log, d_skip,
                gate_norm, n_heads):
    db = z.shape[0]
    d_in = n_heads * SSD_HEAD_DIM
    cdim = d_in + 2 * SSD_GROUPS * D_STATE
    hp = LANES - n_heads
    a = -jnp.exp(a_log.astype(F32))
    expand = (jnp.arange(LANES)[:, None] == (jnp.arange(d_in) // SSD_HEAD_DIM)[None, :]).astype(BF16)
    kern = functools.partial(_ssd_sample_kernel, n_heads=n_heads)
    const = lambda shape: pl.BlockSpec(shape, lambda b: (0,) * len(shape))
    per_b = lambda shape: pl.BlockSpec((None,) + shape, lambda b: (b,) + (0,) * len(shape))
    y, h = pl.pallas_call(
        kern,
        grid=(db,),
        in_specs=[per_b((1, d_in)), per_b((1, cdim)), per_b((1, LANES)), per_b((CONV_W - 1, cdim)),
                  per_b((n_heads, SSD_HEAD_DIM, D_STATE)),
                  const((CONV_W, cdim)), const((1, cdim)), const((1, LANES)), const((1, LANES)),
                  const((1, LANES)), const((1, d_in)), const((LANES, d_in))],
        out_specs=[per_b((1, d_in)), per_b((n_heads, SSD_HEAD_DIM, D_STATE))],
        out_shape=[jax.ShapeDtypeStruct((db, 1, d_in), BF16),
                   jax.ShapeDtypeStruct((db, n_heads, SSD_HEAD_DIM, D_STATE), F32)],
        compiler_params=_cparams(("parallel",)),
        name="ssd_sample",
    )(z.reshape(db, 1, d_in), xbc_raw.reshape(db, 1, cdim), dt_raw.reshape(db, 1, LANES),
      conv_state, h0, conv_w, conv_b.reshape(1, cdim),
      jnp.pad(dt_bias, (0, hp)).reshape(1, LANES), jnp.pad(a, (0, hp)).reshape(1, LANES),
      jnp.pad(d_skip, (0, hp)).reshape(1, LANES), gate_norm.reshape(1, d_in), expand)
    return y.reshape(db, d_in), h


def kernel(x_prompt, x_sample, cache_k, cache_v, cache_kidx, page_table, state_ssm, state_conv,
           meta_tokens, norm_ffn_a, w_ffn_a_in, w_ffn_a_out, norm_mix, norm_ffn_b, w_ffn_b_in,
           w_ffn_b_out, w_attn_in, q_norm, k_norm, kidx_norm, w_attn_out,
           w_ssd_in, conv_w, conv_b, dt_bias, a_log, d_skip, gate_norm, w_ssd_out):
    bsz, seq, d = x_prompt.shape
    db = x_sample.shape[0]
    assert x_sample.shape[1] == 1
    t_real = N_META + seq
    nb = -(-t_real // BLK)
    t_pad = nb * BLK
    pad = t_pad - t_real
    npg = page_table.shape[1]
    past = npg * BLK
    topk_p = min(TOPK_MAX, seq // 4)
    topk_s = min(TOPK_MAX, (past + 1) // 4)
    qd, kd, qid = N_HEADS * HEAD_DIM, N_KV_HEADS * HEAD_DIM, N_IDX_HEADS * IDX_DIM
    d_in = w_ssd_out.shape[1]
    n_heads = d_in // SSD_HEAD_DIM
    cdim = d_in + 2 * SSD_GROUPS * D_STATE
    depth = norm_mix.shape[0]

    meta = jnp.broadcast_to(meta_tokens.astype(F32)[None], (bsz, N_META, d))
    xp = jnp.concatenate([jnp.zeros((bsz, pad, d), F32), meta, x_prompt], axis=1)
    xp = xp.reshape(bsz * t_pad, d)
    xs = x_sample.reshape(db, d)

    outs = {k: [] for k in ("kp", "vp", "kip", "ks", "vs", "kis", "hp", "cp", "hs", "cs")}
    for i in range(depth):
        wts = _ffn_weights(w_ffn_a_in[i], w_ffn_a_out[i])
        xp = _ffn(xp, norm_ffn_a[i], wts)
        xs = _ffn(xs, norm_ffn_a[i], wts)
        j = i // 2
        if i % 2 == 0:
            w = w_attn_in[j]
            o1, o2 = qd + 2 * kd + qid, qd + 2 * kd + qid + IDX_DIM
            w_pad = jnp.concatenate(
                [w[:, :o1], jnp.pad(w[:, o1:o2], ((0, 0), (0, LANES - IDX_DIM))),
                 jnp.pad(w[:, o2:], ((0, 0), (0, LANES - N_IDX_HEADS)))], axis=1).astype(BF16)
            w_out = w_attn_out[j].astype(BF16)

            k, v, ki, qt, qit, wt, kg, vt, kib = _attn_in(
                xp, norm_mix[i], w_pad, q_norm[j], k_norm[j], kidx_norm[j], prompt=True)
            gw = HEADS_PER_KV * BLK
            smax = (1.02 * math.sqrt(HEAD_DIM) * jnp.max(jnp.abs(q_norm[j]))
                    * jnp.max(jnp.abs(k_norm[j]))).reshape(1).astype(F32)
            o = _dsa_prompt(smax,
                            qit.reshape(bsz, nb, IDX_DIM, N_IDX_HEADS * BLK),
                            wt.reshape(bsz, nb, N_IDX_HEADS, BLK),
                            kib.reshape(bsz, nb, BLK, IDX_DIM),
                            qt.reshape(bsz, nb, N_KV_HEADS, HEAD_DIM, gw),
                            kg.reshape(N_KV_HEADS, bsz, nb, BLK, HEAD_DIM),
                            vt.reshape(bsz, nb, N_KV_HEADS, VT_ROWS, BLK),
                            pad, topk_p)
            xp = _out_proj(xp, o.reshape(bsz * t_pad, qd), w_out)
            outs["kp"].append(k.reshape(bsz, t_pad, N_KV_HEADS, HEAD_DIM)[:, pad:])
            outs["vp"].append(v.reshape(bsz, t_pad, N_KV_HEADS, HEAD_DIM)[:, pad:])
            outs["kip"].append(ki[:, :IDX_DIM].reshape(bsz, t_pad, IDX_DIM)[:, pad:])

            q, k, v, qi, ki, wi = _attn_in(
                xs, norm_mix[i], w_pad, q_norm[j], k_norm[j], kidx_norm[j], prompt=False)
            pg = math.gcd(npg, 32)
            qi3 = qi.reshape(db, N_IDX_HEADS, IDX_DIM)
            wi3 = wi[:, :N_IDX_HEADS].reshape(db, N_IDX_HEADS, 1)
            scores = _smp_scores(page_table, qi3, wi3, cache_kidx[j].transpose(0, 2, 1), pg)
            bias, nbias = _smp_select(scores, qi3, wi3, ki[:, :IDX_DIM].reshape(db, 1, IDX_DIM), topk_s)
            hsel = (jnp.arange(N_HEADS)[:, None] // HEADS_PER_KV == jnp.arange(N_KV_HEADS)[None, :])
            q_bd = (q.reshape(db, N_HEADS, 1, HEAD_DIM) * hsel[None, :, :, None].astype(BF16))
            q_bd = q_bd.reshape(db, N_HEADS, kd)
            npool = cache_k.shape[1]
            o = _smp_attend(page_table, q_bd, bias, nbias, k.reshape(db, 1, kd), v.reshape(db, 1, kd),
                            cache_k[j].transpose(0, 2, 3, 1).reshape(npool, kd, BLK),
                            cache_v[j].transpose(0, 2, 3, 1).reshape(npool, kd, BLK), pg)
            xs = _out_proj(xs, o.reshape(db, qd).astype(BF16), w_out)
            outs["ks"].append(k.reshape(db, 1, N_KV_HEADS, HEAD_DIM))
            outs["vs"].append(v.reshape(db, 1, N_KV_HEADS, HEAD_DIM))
            outs["kis"].append(ki[:, :IDX_DIM].reshape(db, 1, IDX_DIM))
        else:
            w = w_ssd_in[j]
            w_main = w[:, :d_in + cdim].astype(BF16)
            w_dt = jnp.pad(w[:, d_in + cdim:], ((0, 0), (0, LANES - n_heads))).astype(BF16)
            w_out = w_ssd_out[j].astype(BF16)
            sp = (conv_w[j], conv_b[j], dt_bias[j], a_log[j], d_skip[j], gate_norm[j])

            xp, hfin, ctail = _ssd_prompt(xp, norm_mix[i], w_main[:, :d_in], w_main[:, d_in:], w_dt, w_out,
                                          *sp, bsz, nb, pad, n_heads)
            outs["hp"].append(hfin)
            outs["cp"].append(ctail[:, SUBLANES - (CONV_W - 1):])

            z, xbc_raw, dt_raw = _norm_proj(xs, norm_mix[i], w_main, w_dt, d_in)
            y, hnew = _ssd_sample(z, xbc_raw, dt_raw, state_conv[j], state_ssm[j], *sp, n_heads)
            xs = _out_proj(xs, y, w_out)
            outs["hs"].append(hnew)
            outs["cs"].append(jnp.concatenate([state_conv[j][:, 1:], xbc_raw[:, None, :]], axis=1))
        wts = _ffn_weights(w_ffn_b_in[i], w_ffn_b_out[i])
        xp = _ffn(xp, norm_ffn_b[i], wts)
        xs = _ffn(xs, norm_ffn_b[i], wts)

    y_prompt = xp.reshape(bsz, t_pad, d)[:, pad + N_META:]
    y_sample = xs.reshape(db, 1, d)
    st = lambda key: jnp.stack(outs[key])
    return (y_prompt, y_sample, st("kp"), st("vp"), st("kip"), st("ks"), st("vs"), st("kis"),
            st("hp"), st("cp"), st("hs"), st("cs"))
```

```python
import functools
import math

import jax
import jax.numpy as jnp
from jax import lax
from jax.experimental import pallas as pl
from jax.experimental.pallas import tpu as pltpu

F32 = jnp.float32
BF16 = jnp.bfloat16
I32 = jnp.int32

N_META = 16
N_HEADS = 16
HEAD_DIM = 64
N_KV_HEADS = 4
HEADS_PER_KV = N_HEADS // N_KV_HEADS
N_IDX_HEADS = 8
IDX_DIM = 64
TOPK_MAX = 256
SSD_HEAD_DIM = 64
SSD_GROUPS = 4
D_STATE = 128
CONV_W = 4
EPS = 1e-6

LANES = 128
SUBLANES = 8
BLK = 128
NEG = -1e30
SAFE_LOGIT = 40.0
COUNT_UNROLL = 4
VT_ROWS = 80
INT_MIN = -2 ** 31
VMEM_LIMIT = 56 * 1024 * 1024


def _cparams(sem, vmem=VMEM_LIMIT):
    return pltpu.CompilerParams(dimension_semantics=sem, vmem_limit_bytes=vmem)


def _row_tile(rows, pref):
    best = None
    for d in range(SUBLANES, min(rows, pref) + 1, SUBLANES):
        if rows % d == 0:
            best = d
    assert best is not None, rows
    return best


def _rms(x, g):
    var = jnp.mean(x * x, axis=-1, keepdims=True)
    return x * lax.rsqrt(var + EPS) * g


def _dot(a, b):
    return jnp.dot(a, b, preferred_element_type=F32)


def _dot_t0(a, b):
    return lax.dot_general(a, b, (((0,), (0,)), ((), ())), preferred_element_type=F32)


def _dot_t1(a, b):
    return lax.dot_general(a, b, (((1,), (1,)), ((), ())), preferred_element_type=F32)


def _split2(a):
    hi = a.astype(BF16)
    lo = (a - hi.astype(F32)).astype(BF16)
    return hi, lo


def _split3(a):
    a0 = a.astype(BF16)
    r = a - a0.astype(F32)
    a1 = r.astype(BF16)
    a2 = (r - a1.astype(F32)).astype(BF16)
    return a0, a1, a2


def _dot_sel(a, m, fn=_dot):
    a0, a1, a2 = _split3(a)
    return fn(a0, m) + fn(a1, m) + fn(a2, m)


def _silu(x):
    return x * (1.0 / (1.0 + jnp.exp(-x)))


def _softplus(x):
    return jnp.maximum(x, 0.0) + jnp.log(1.0 + jnp.exp(-jnp.abs(x)))


def _ffn_kernel(x_ref, g_ref, wa_ref, wb_ref, wo_ref, o_ref, xn_ref, acc_ref):
    xn_ref[...] = _rms(x_ref[...], g_ref[...]).astype(BF16)
    acc_ref[...] = jnp.zeros_like(acc_ref)

    def chunk(j, carry):
        xn = xn_ref[...]
        a = _dot(xn, wa_ref[j])
        b = _dot(xn, wb_ref[j])
        h = (_silu(a) * b).astype(BF16)
        acc_ref[...] += _dot(h, wo_ref[j])
        return carry

    lax.fori_loop(0, wa_ref.shape[0], chunk, 0, unroll=True)
    o_ref[...] = x_ref[...] + 0.5 * acc_ref[...]


def _ffn_weights(w_in, w_out):
    d, hid = w_in.shape[0], w_out.shape[0]
    th = 256 if hid % 256 == 0 else LANES
    nh = hid // th
    w_in = w_in.astype(BF16)
    wa = w_in[:, :hid].reshape(d, nh, th).transpose(1, 0, 2)
    wb = w_in[:, hid:].reshape(d, nh, th).transpose(1, 0, 2)
    return wa, wb, w_out.astype(BF16).reshape(nh, th, d)


def _ffn(x, g, weights):
    wa, wb, wo = weights
    rows, d = x.shape
    nh, _, th = wa.shape
    tm = _row_tile(rows, 1024)
    resident = lambda shape: pl.BlockSpec(shape, lambda i: (0,) * len(shape),
                                          pipeline_mode=pl.Buffered(1))
    return pl.pallas_call(
        _ffn_kernel,
        grid=(rows // tm,),
        in_specs=[
            pl.BlockSpec((tm, d), lambda i: (i, 0)),
            resident((1, d)), resident((nh, d, th)), resident((nh, d, th)), resident((nh, th, d)),
        ],
        out_specs=pl.BlockSpec((tm, d), lambda i: (i, 0)),
        out_shape=jax.ShapeDtypeStruct((rows, d), F32),
        scratch_shapes=[pltpu.VMEM((tm, d), BF16), pltpu.VMEM((tm, d), F32)],
        compiler_params=_cparams(("parallel",)),
        name="ffn",
    )(x, g.reshape(1, d), wa, wb, wo)


def _attn_project(x_ref, g_ref, w_ref, qg_ref, kg_ref, kig_ref, gq_ref, eq_ref, gk_ref, ek_ref):
    qd = N_HEADS * HEAD_DIM
    kd = N_KV_HEADS * HEAD_DIM
    qid = N_IDX_HEADS * IDX_DIM
    xn = _rms(x_ref[...], g_ref[...]).astype(BF16)
    h = _dot(xn, w_ref[...])

    def head_norm(t, gsum_ref, gexp_ref, gain):
        ss = _dot_sel(t * t, gsum_ref[...])
        rs = lax.rsqrt(ss * (1.0 / HEAD_DIM) + EPS)
        return t * _dot_sel(rs, gexp_ref[...]) * gain

    o = 0
    q = head_norm(h[:, o:o + qd], gq_ref, eq_ref, qg_ref[...]) * (HEAD_DIM ** -0.5)
    o += qd
    k = head_norm(h[:, o:o + kd], gk_ref, ek_ref, kg_ref[...])
    o += kd
    v = h[:, o:o + kd]
    o += kd
    qi = h[:, o:o + qid] * (IDX_DIM ** -0.5)
    o += qid
    ki = h[:, o:o + LANES]
    var = jnp.sum(ki * ki, axis=-1, keepdims=True) * (1.0 / IDX_DIM)
    ki = ki * lax.rsqrt(var + EPS) * kig_ref[...]
    o += LANES
    wi = h[:, o:o + LANES] * (N_IDX_HEADS ** -0.5)
    return q, k, v, qi, ki, wi


def _attn_in_sample_kernel(*refs):
    q_o, k_o, v_o, qi_o, ki_o, wi_o = refs[10:]
    q, k, v, qi, ki, wi = _attn_project(*refs[:10])
    q_o[...] = q.astype(BF16)
    k_o[...] = k
    v_o[...] = v
    qi_o[...] = qi.astype(BF16)
    ki_o[...] = ki
    wi_o[...] = wi


def _attn_in_prompt_kernel(*refs):
    k_o, v_o, ki_o, qt_o, qit_o, wt_o, kg_o, vt_o, kib_o = refs[10:]
    q, k, v, qi, ki, wi = _attn_project(*refs[:10])
    k_o[...] = k
    v_o[...] = v
    ki_o[...] = ki
    kib_o[...] = ki[:, :IDX_DIM].astype(BF16)
    for g in range(N_KV_HEADS):
        kg_o[g] = k[:, g * HEAD_DIM:(g + 1) * HEAD_DIM].astype(BF16)
    tail = (lax.broadcasted_iota(I32, (VT_ROWS - HEAD_DIM, BLK), 0) == 0).astype(BF16)
    for r in range(q.shape[0] // BLK):
        rows = slice(r * BLK, (r + 1) * BLK)
        for t in range(N_HEADS * HEAD_DIM // LANES):
            tt = q[rows, t * LANES:(t + 1) * LANES].T.astype(BF16)
            for u in range(LANES // HEAD_DIM):
                h = t * (LANES // HEAD_DIM) + u
                g, hh = h // HEADS_PER_KV, h % HEADS_PER_KV
                qt_o[r, g, :, hh * BLK:(hh + 1) * BLK] = tt[u * HEAD_DIM:(u + 1) * HEAD_DIM]
        for t in range(N_IDX_HEADS * IDX_DIM // LANES):
            tt = qi[rows, t * LANES:(t + 1) * LANES].T.astype(BF16)
            for u in range(LANES // IDX_DIM):
                h = t * (LANES // IDX_DIM) + u
                qit_o[r, :, h * BLK:(h + 1) * BLK] = tt[u * IDX_DIM:(u + 1) * IDX_DIM]
        wt_o[r] = wi[rows].T[0:N_IDX_HEADS]
        for t in range(N_KV_HEADS * HEAD_DIM // LANES):
            tt = v[rows, t * LANES:(t + 1) * LANES].T.astype(BF16)
            for u in range(LANES // HEAD_DIM):
                vt_o[r, t * (LANES // HEAD_DIM) + u] = jnp.concatenate(
                    [tt[u * HEAD_DIM:(u + 1) * HEAD_DIM], tail], axis=0)


def _seg_mats(n_heads, hd):
    col = jnp.arange(n_heads * hd) // hd
    gsum = (col[:, None] == jnp.arange(LANES)[None, :]).astype(BF16)
    return gsum, gsum.T


def _attn_in(x, g, w_pad, q_gain, k_gain, ki_gain, prompt):
    rows, d = x.shape
    n = w_pad.shape[1]
    qd, kd, qid = N_HEADS * HEAD_DIM, N_KV_HEADS * HEAD_DIM, N_IDX_HEADS * IDX_DIM
    gw = HEADS_PER_KV * BLK
    tm = _row_tile(rows, 512)
    gq, eq = _seg_mats(N_HEADS, HEAD_DIM)
    gk, ek = _seg_mats(N_KV_HEADS, HEAD_DIM)
    const = lambda shape: pl.BlockSpec(shape, lambda i: (0,) * len(shape))
    rowb = lambda w: pl.BlockSpec((tm, w), lambda i: (i, 0))
    sds = jax.ShapeDtypeStruct
    if prompt:
        assert tm % BLK == 0
        nbk, tb = rows // BLK, tm // BLK
        blkb = lambda *s: pl.BlockSpec((tb,) + s, lambda i: (i,) + (0,) * len(s))
        kern = _attn_in_prompt_kernel
        out_specs = [rowb(kd), rowb(kd), rowb(LANES), blkb(N_KV_HEADS, HEAD_DIM, gw),
                     blkb(IDX_DIM, N_IDX_HEADS * BLK), blkb(N_IDX_HEADS, BLK),
                     pl.BlockSpec((N_KV_HEADS, tm, HEAD_DIM), lambda i: (0, i, 0)),
                     blkb(N_KV_HEADS, VT_ROWS, BLK),
                     rowb(IDX_DIM)]
        out_shape = [sds((rows, kd), F32), sds((rows, kd), F32), sds((rows, LANES), F32),
                     sds((nbk, N_KV_HEADS, HEAD_DIM, gw), BF16),
                     sds((nbk, IDX_DIM, N_IDX_HEADS * BLK), BF16),
                     sds((nbk, N_IDX_HEADS, BLK), F32),
                     sds((N_KV_HEADS, rows, HEAD_DIM), BF16),
                     sds((nbk, N_KV_HEADS, VT_ROWS, BLK), BF16),
                     sds((rows, IDX_DIM), BF16)]
    else:
        kern = _attn_in_sample_kernel
        out_specs = [rowb(qd), rowb(kd), rowb(kd), rowb(qid), rowb(LANES), rowb(LANES)]
        out_shape = [sds((rows, qd), BF16), sds((rows, kd), F32), sds((rows, kd), F32),
                     sds((rows, qid), BF16), sds((rows, LANES), F32), sds((rows, LANES), F32)]
    return pl.pallas_call(
        kern,
        grid=(rows // tm,),
        in_specs=[rowb(d), const((1, d)), const((d, n)), const((1, qd)), const((1, kd)),
                  const((1, LANES)), const((qd, LANES)), const((LANES, qd)),
                  const((kd, LANES)), const((LANES, kd))],
        out_specs=out_specs,
        out_shape=out_shape,
        compiler_params=_cparams(("parallel",)),
        name="attn_in_prompt" if prompt else "attn_in_sample",
    )(x, g.reshape(1, d), w_pad,
      jnp.tile(q_gain, N_HEADS).reshape(1, qd), jnp.tile(k_gain, N_KV_HEADS).reshape(1, kd),
      jnp.pad(ki_gain, (0, LANES - IDX_DIM)).reshape(1, LANES), gq, eq, gk, ek)


def _norm_proj_kernel(x_ref, g_ref, w_ref, ws_ref, oa_ref, ob_ref, os_ref, xn_ref, *, na):
    j = pl.program_id(1)

    @pl.when(j == 0)
    def _():
        xn = _rms(x_ref[...], g_ref[...]).astype(BF16)
        xn_ref[...] = xn
        os_ref[...] = _dot(xn, ws_ref[...])

    r = _dot(xn_ref[...], w_ref[...])

    @pl.when(j < na)
    def _():
        oa_ref[...] = r

    @pl.when(j >= na)
    def _():
        ob_ref[...] = r


def _norm_proj(x, g, w_main, w_side, n_a):
    rows, d = x.shape
    n = w_main.shape[1]
    tm = _row_tile(rows, 1024)
    tn = math.gcd(math.gcd(n_a, n - n_a), 1024)
    na = n_a // tn
    return pl.pallas_call(
        functools.partial(_norm_proj_kernel, na=na),
        grid=(rows // tm, n // tn),
        in_specs=[
            pl.BlockSpec((tm, d), lambda i, j: (i, 0)),
            pl.BlockSpec((1, d), lambda i, j: (0, 0)),
            pl.BlockSpec((d, tn), lambda i, j: (0, j)),
            pl.BlockSpec((d, LANES), lambda i, j: (0, 0)),
        ],
        out_specs=[pl.BlockSpec((tm, tn), lambda i, j: (i, jnp.minimum(j, na - 1))),
                   pl.BlockSpec((tm, tn), lambda i, j: (i, jnp.maximum(j - na, 0))),
                   pl.BlockSpec((tm, LANES), lambda i, j: (i, 0))],
        out_shape=[jax.ShapeDtypeStruct((rows, n_a), F32),
                   jax.ShapeDtypeStruct((rows, n - n_a), F32),
                   jax.ShapeDtypeStruct((rows, LANES), F32)],
        scratch_shapes=[pltpu.VMEM((tm, d), BF16)],
        compiler_params=_cparams(("parallel", "arbitrary")),
        name="norm_proj",
    )(x, g.reshape(1, d), w_main, w_side)


def _out_proj_kernel(x_ref, y_ref, w_ref, o_ref):
    o_ref[...] = x_ref[...] + _dot(y_ref[...], w_ref[...])


def _out_proj(x, y, w):
    rows, d = x.shape
    k = y.shape[1]
    tm = _row_tile(rows, 512)
    return pl.pallas_call(
        _out_proj_kernel,
        grid=(rows // tm,),
        in_specs=[pl.BlockSpec((tm, d), lambda i: (i, 0)),
                  pl.BlockSpec((tm, k), lambda i: (i, 0)),
                  pl.BlockSpec((k, d), lambda i: (0, 0))],
        out_specs=pl.BlockSpec((tm, d), lambda i: (i, 0)),
        out_shape=jax.ShapeDtypeStruct((rows, d), F32),
        compiler_params=_cparams(("parallel",)),
        name="out_proj",
    )(x, y, w)


def _sort_key(score):
    bits = pltpu.bitcast(score, I32)
    return bits ^ ((bits >> 31) & 0x7FFFFFFF)


def _tile_fold(x, op):
    r = x[0:SUBLANES]
    for t in range(1, x.shape[0] // SUBLANES):
        r = op(r, x[t * SUBLANES:(t + 1) * SUBLANES])
    return r


def _dsa_prompt_kernel(smax_ref, qit_ref, wt_ref, ki_ref, qt_ref, k_ref, vt_ref, o_ref,
                       key_ref, bias_ref, acc_ref, *, pad, topk):
    i = pl.program_id(1)
    nch = i + 1
    npair = (nch + 1) // 2
    last = ki_ref.shape[0] - 1
    row = lax.broadcasted_iota(I32, (BLK, BLK), 0)
    lane = lax.broadcasted_iota(I32, (BLK, BLK), 1)
    t_pos = i * BLK + lane

    def idx_dots(c):
        return _dot(ki_ref[jnp.minimum(c, last)], qit_ref[...])

    def idx_keys(c, dots):
        sc = jnp.zeros((BLK, BLK), F32)
        for h in range(N_IDX_HEADS):
            sc = sc + wt_ref[h:h + 1, :] * jnp.maximum(dots[:, h * BLK:(h + 1) * BLK], 0.0)
        s_pos = c * BLK + row
        valid = (s_pos <= t_pos) & (s_pos >= pad)
        key_ref[c] = jnp.where(valid, _sort_key(sc), INT_MIN)

    def p1(j, carry):
        for u in range(2):
            idx_keys(2 * j + u, idx_dots(2 * j + u))
        return carry

    lax.fori_loop(0, npair, p1, 0)

    for u in range(COUNT_UNROLL - 2):
        key_ref[2 * npair + u] = jnp.full((BLK, BLK), INT_MIN, I32)
    ntrip = (nch + COUNT_UNROLL - 1) // COUNT_UNROLL

    def count(pred):
        def body(j, cnt):
            for u in range(COUNT_UNROLL):
                c = COUNT_UNROLL * j + u
                cnt = cnt + jnp.where(pred(key_ref[c], c * BLK + row), 1, 0)
            return cnt
        cnt = lax.fori_loop(0, ntrip, body, jnp.zeros((BLK, BLK), I32))
        return jnp.sum(cnt, axis=0, keepdims=True)

    c0 = count(lambda k, s: k >= 0)
    thr0 = jnp.where(c0 >= topk, 0, INT_MIN).astype(I32)

    def bit_step(it, st):
        thr, n_ge = st
        cand = thr + (jnp.int32(1) << (30 - it))
        cnt = count(lambda k, s: k >= cand)
        ok = cnt >= topk
        return jnp.where(ok, cand, thr), jnp.where(ok, cnt, n_ge)

    thr, n_ge = lax.fori_loop(0, 31, bit_step, (thr0, c0))
    tied = (n_ge > topk) & (thr > INT_MIN)
    any_tied = jnp.max(tied.astype(I32))
    nbits = (key_ref.shape[0] * BLK).bit_length()

    def idx_search():
        need = topk - count(lambda k, s: k > thr)

        def step(it, lo):
            cand = lo + (jnp.int32(1) << (nbits - 1 - it))
            below = count(lambda k, s: (k == thr) & (s < cand))
            return jnp.where(below < need, cand, lo)
        return lax.fori_loop(0, nbits, step, jnp.zeros((1, BLK), I32))

    jcut = lax.cond(any_tied > 0, idx_search,
                    lambda: jnp.full((1, BLK), 2 ** 30, I32))

    def p2(j, carry):
        for u in range(2):
            c = 2 * j + u
            k = key_ref[c]
            sel = (k > thr) | ((k == thr) & (c * BLK + row <= jcut))
            sel = sel & (k > INT_MIN)
            bias_ref[c] = jnp.where(sel, 0.0, NEG)
        return carry

    lax.fori_loop(0, npair, p2, 0)

    gw = HEADS_PER_KV * BLK

    def logits(c):
        bb = jnp.concatenate([bias_ref[c]] * HEADS_PER_KV, axis=1)
        cr = jnp.minimum(c, last)
        return [_dot(k_ref[g, cr], qt_ref[g]) + bb for g in range(N_KV_HEADS)]

    def attend(shift):
        acc_ref[...] = jnp.zeros_like(acc_ref)

        def body(j, carry):
            p = []
            for u in range(2):
                s = logits(2 * j + u)
                if shift is not None:
                    s = [s[g] - shift[g] for g in range(N_KV_HEADS)]
                p.append([jnp.exp(x).astype(BF16) for x in s])
            c1 = jnp.minimum(2 * j + 1, last)
            for g in range(N_KV_HEADS):
                vt = jnp.concatenate([vt_ref[2 * j, g], vt_ref[c1, g]], axis=1)
                pp = jnp.concatenate([p[0][g], p[1][g]], axis=0)
                acc_ref[g] += _dot(vt, pp)
            return carry

        lax.fori_loop(0, npair, body, 0)

    safe = smax_ref[0] <= SAFE_LOGIT

    @pl.when(safe)
    def _():
        attend(None)

    @pl.when(jnp.logical_not(safe))
    def _():
        def pa(j, m):
            for u in range(2):
                s = logits(2 * j + u)
                m = tuple(jnp.maximum(m[g], _tile_fold(s[g], jnp.maximum)) for g in range(N_KV_HEADS))
            return m

        m0 = tuple(jnp.full((SUBLANES, gw), NEG, F32) for _ in range(N_KV_HEADS))
        m = lax.fori_loop(0, npair, pa, m0)
        attend([jnp.max(x, axis=0, keepdims=True) for x in m])

    q_row = i * BLK + lax.broadcasted_iota(I32, (BLK, LANES), 0)
    for g in range(N_KV_HEADS):
        a = acc_ref[g]
        res = a[0:HEAD_DIM] * (1.0 / a[HEAD_DIM:HEAD_DIM + 1])
        for t in range(HEADS_PER_KV // 2):
            two = jnp.concatenate([res[:, (2 * t + u) * BLK:(2 * t + u + 1) * BLK] for u in range(2)],
                                  axis=0)
            two = jnp.where(q_row >= pad, two.T, 0.0)
            lo = (g * HEADS_PER_KV + 2 * t) * HEAD_DIM
            o_ref[:, lo:lo + LANES] = two.astype(BF16)


def _dsa_prompt(smax, qit, wt, kib, qt, kg, vt, pad, topk):
    bsz, nb = qit.shape[:2]
    t_pad = nb * BLK
    gw = HEADS_PER_KV * BLK
    kern = functools.partial(_dsa_prompt_kernel, pad=pad, topk=topk)
    return pl.pallas_call(
        kern,
        grid=(bsz, nb),
        in_specs=[
            pl.BlockSpec(memory_space=pltpu.SMEM),
            pl.BlockSpec((None, None, IDX_DIM, N_IDX_HEADS * BLK), lambda b, i: (b, i, 0, 0)),
            pl.BlockSpec((None, None, N_IDX_HEADS, BLK), lambda b, i: (b, i, 0, 0)),
            pl.BlockSpec((None, nb, BLK, IDX_DIM), lambda b, i: (b, 0, 0, 0)),
            pl.BlockSpec((None, None, N_KV_HEADS, HEAD_DIM, gw), lambda b, i: (b, i, 0, 0, 0)),
            pl.BlockSpec((N_KV_HEADS, None, nb, BLK, HEAD_DIM), lambda b, i: (0, b, 0, 0, 0)),
            pl.BlockSpec((None, nb, N_KV_HEADS, VT_ROWS, BLK), lambda b, i: (b, 0, 0, 0, 0)),
        ],
        out_specs=pl.BlockSpec((None, BLK, N_HEADS * HEAD_DIM), lambda b, i: (b, i, 0)),
        out_shape=jax.ShapeDtypeStruct((bsz, t_pad, N_HEADS * HEAD_DIM), BF16),
        scratch_shapes=[pltpu.VMEM((nb + COUNT_UNROLL - 1, BLK, BLK), I32),
                        pltpu.VMEM((nb + 1, BLK, BLK), F32),
                        pltpu.VMEM((N_KV_HEADS, VT_ROWS, gw), F32)],
        compiler_params=_cparams(("parallel", "arbitrary")),
        name="dsa_prompt",
    )(smax, qit, wt, kib, qt, kg, vt)


def _dsa_pair_kernel(smax_ref, qita_ref, qitb_ref, wta_ref, wtb_ref, qta_ref, qtb_ref,
                     ki_ref, k_ref, vt_ref, oa_ref, ob_ref,
                     key_ref, bias_ref, acc_ref, qit_s, wt_s, qt_s, *, pad, topk, nb):
    j = pl.program_id(1)
    n_a = j + 1
    blocks = (j, nb - 1 - j)
    ntrip = nb + 1
    ncount = key_ref.shape[0]
    last = ki_ref.shape[0] - 1
    row = lax.broadcasted_iota(I32, (BLK, BLK), 0)
    lane = lax.broadcasted_iota(I32, (BLK, BLK), 1)
    gw = HEADS_PER_KV * BLK

    qit_s[0], qit_s[1] = qita_ref[...], qitb_ref[...]
    wt_s[0], wt_s[1] = wta_ref[...], wtb_ref[...]
    qt_s[0], qt_s[1] = qta_ref[...], qtb_ref[...]

    def trip(t):
        blk = (t >= n_a).astype(I32)
        c = t - blk * n_a
        return blk, c, jnp.minimum(c, last)

    def pick(blk, a, b):
        if isinstance(a, tuple):
            return tuple(pick(blk, x, y) for x, y in zip(a, b))
        return jnp.where(blk == 1, b, a)

    for t in range(ncount):
        if t >= ntrip:
            key_ref[t] = jnp.full((BLK, BLK), INT_MIN, I32)
            continue
        blk, c, cr = trip(t)
        dots = _dot(ki_ref[cr], qit_s[blk])
        sc = jnp.zeros((BLK, BLK), F32)
        for h in range(N_IDX_HEADS):
            sc = sc + wt_s[blk, h:h + 1, :] * jnp.maximum(dots[:, h * BLK:(h + 1) * BLK], 0.0)
        s_pos = c * BLK + row
        t_pos = pick(blk, blocks[0], blocks[1]) * BLK + lane
        valid = (s_pos <= t_pos) & (s_pos >= pad)
        key_ref[t] = jnp.where(valid, _sort_key(sc), INT_MIN)

    def count(pred, arg):
        cnt = [jnp.zeros((SUBLANES, BLK), I32), jnp.zeros((SUBLANES, BLK), I32)]
        for t in range(ncount):
            blk, c, _ = trip(t)
            hit = jnp.where(pred(key_ref[t], pick(blk, arg[0], arg[1]), c * BLK + row), 1, 0)
            part = _tile_fold(hit, jnp.add)
            cnt[0] = cnt[0] + jnp.where(blk == 0, part, 0)
            cnt[1] = cnt[1] + jnp.where(blk == 1, part, 0)
        return tuple(jnp.sum(x, axis=0, keepdims=True) for x in cnt)

    zero = jnp.zeros((1, BLK), I32)
    c0 = count(lambda k, a, s: k >= a, (zero, zero))
    thr0 = tuple(jnp.where(x >= topk, 0, INT_MIN).astype(I32) for x in c0)

    def bit_step(it, st):
        thr, n_ge = st[:2], st[2:]
        cand = tuple(x + (jnp.int32(1) << (30 - it)) for x in thr)
        cnt = count(lambda k, a, s: k >= a, cand)
        ok = tuple(x >= topk for x in cnt)
        return (tuple(jnp.where(ok[b], cand[b], thr[b]) for b in range(2))
                + tuple(jnp.where(ok[b], cnt[b], n_ge[b]) for b in range(2)))

    st = lax.fori_loop(0, 31, bit_step, thr0 + c0)
    thr, n_ge = st[:2], st[2:]
    tied = ((n_ge[0] > topk) & (thr[0] > INT_MIN)) | ((n_ge[1] > topk) & (thr[1] > INT_MIN))
    any_tied = jnp.max(tied.astype(I32))
    nbits = (nb * BLK).bit_length()

    def idx_search():
        n_gt = count(lambda k, a, s: k > a, thr)
        need = tuple(topk - x for x in n_gt)

        def step(it, lo):
            cand = tuple(x + (jnp.int32(1) << (nbits - 1 - it)) for x in lo)
            below = count(lambda k, a, s: (k == a[0]) & (s < a[1]),
                          ((thr[0], cand[0]), (thr[1], cand[1])))
            return tuple(jnp.where(below[b] < need[b], cand[b], lo[b]) for b in range(2))
        return lax.fori_loop(0, nbits, step, (zero, zero))

    big = jnp.full((1, BLK), 2 ** 30, I32)
    jcut = lax.cond(any_tied > 0, idx_search, lambda: (big, big))

    for t in range(ntrip):
        blk, c, _ = trip(t)
        k = key_ref[t]
        th = pick(blk, thr[0], thr[1])
        sel = (k > th) | ((k == th) & (c * BLK + row <= pick(blk, jcut[0], jcut[1])))
        bias_ref[t] = jnp.where(sel & (k > INT_MIN), 0.0, NEG)

    def logits(t):
        blk, _, cr = trip(t)
        bb = jnp.concatenate([bias_ref[t]] * HEADS_PER_KV, axis=1)
        return blk, cr, [_dot(k_ref[g, cr], qt_s[blk, g]) + bb for g in range(N_KV_HEADS)]

    def attend(shift):
        acc_ref[...] = jnp.zeros_like(acc_ref)
        for t in range(ntrip):
            blk, cr, s = logits(t)
            for g in range(N_KV_HEADS):
                x = s[g] if shift is None else s[g] - pick(blk, shift[0][g], shift[1][g])
                acc_ref[blk, g] += _dot(vt_ref[cr, g], jnp.exp(x).astype(BF16))

    safe = smax_ref[0] <= SAFE_LOGIT

    @pl.when(safe)
    def _():
        attend(None)

    @pl.when(jnp.logical_not(safe))
    def _():
        m = [[jnp.full((SUBLANES, gw), NEG, F32) for _ in range(N_KV_HEADS)] for _ in range(2)]
        for t in range(ntrip):
            blk, _, s = logits(t)
            for g in range(N_KV_HEADS):
                part = _tile_fold(s[g], jnp.maximum)
                m[0][g] = jnp.maximum(m[0][g], jnp.where(blk == 0, part, NEG))
                m[1][g] = jnp.maximum(m[1][g], jnp.where(blk == 1, part, NEG))
        attend([[jnp.max(x, axis=0, keepdims=True) for x in mb] for mb in m])

    for b, o_ref in enumerate((oa_ref, ob_ref)):
        q_row = blocks[b] * BLK + lax.broadcasted_iota(I32, (BLK, LANES), 0)
        for g in range(N_KV_HEADS):
            a = acc_ref[b, g]
            res = a[0:HEAD_DIM] * (1.0 / a[HEAD_DIM:HEAD_DIM + 1])
            for t in range(HEADS_PER_KV // 2):
                two = jnp.concatenate([res[:, (2 * t + u) * BLK:(2 * t + u + 1) * BLK] for u in range(2)],
                                      axis=0)
                two = jnp.where(q_row >= pad, two.T, 0.0)
                lo = (g * HEADS_PER_KV + 2 * t) * HEAD_DIM
                o_ref[:, lo:lo + LANES] = two.astype(BF16)


def _dsa_prompt_pairs(smax, qit, wt, kib, qt, kg, vt, pad, topk):
    bsz, nb = qit.shape[:2]
    steps = (nb + 1) // 2
    gw = HEADS_PER_KV * BLK
    qd = N_HEADS * HEAD_DIM
    ncount = nb + 1
    kern = functools.partial(_dsa_pair_kernel, pad=pad, topk=topk, nb=nb)
    first = lambda *s: pl.BlockSpec((None, None) + s, lambda b, j: (b, j) + (0,) * len(s))
    second = lambda *s: pl.BlockSpec((None, None) + s, lambda b, j: (b, nb - 1 - j) + (0,) * len(s))
    oa, ob = pl.pallas_call(
        kern,
        grid=(bsz, steps),
        in_specs=[
            pl.BlockSpec(memory_space=pltpu.SMEM),
            first(IDX_DIM, N_IDX_HEADS * BLK), second(IDX_DIM, N_IDX_HEADS * BLK),
            first(N_IDX_HEADS, BLK), second(N_IDX_HEADS, BLK),
            first(N_KV_HEADS, HEAD_DIM, gw), second(N_KV_HEADS, HEAD_DIM, gw),
            pl.BlockSpec((None, nb, BLK, IDX_DIM), lambda b, j: (b, 0, 0, 0)),
            pl.BlockSpec((N_KV_HEADS, None, nb, BLK, HEAD_DIM), lambda b, j: (0, b, 0, 0, 0)),
            pl.BlockSpec((None, nb, N_KV_HEADS, VT_ROWS, BLK), lambda b, j: (b, 0, 0, 0, 0)),
        ],
        out_specs=[pl.BlockSpec((None, BLK, qd), lambda b, j: (b, j, 0)),
                   pl.BlockSpec((None, BLK, qd), lambda b, j: (b, steps - 1 - j, 0))],
        out_shape=[jax.ShapeDtypeStruct((bsz, steps * BLK, qd), BF16)] * 2,
        scratch_shapes=[pltpu.VMEM((ncount, BLK, BLK), I32),
                        pltpu.VMEM((nb + 1, BLK, BLK), F32),
                        pltpu.VMEM((2, N_KV_HEADS, VT_ROWS, gw), F32),
                        pltpu.VMEM((2, IDX_DIM, N_IDX_HEADS * BLK), BF16),
                        pltpu.VMEM((2, N_IDX_HEADS, BLK), F32),
                        pltpu.VMEM((2, N_KV_HEADS, HEAD_DIM, gw), BF16)],
        compiler_params=_cparams(("parallel", "arbitrary")),
        name="dsa_prompt",
    )(smax, qit, qit, wt, wt, qt, qt, kib, kg, vt)
    return jnp.concatenate([oa[:, :(nb - steps) * BLK], ob], axis=1)


def _smp_scores_kernel(pt_ref, qi_ref, w_ref, *refs, pg):
    page_refs, o_ref = refs[:pg], refs[pg]
    qi = qi_ref[...]
    w = w_ref[...]
    for p in range(pg):
        d = _dot(qi, page_refs[p][...].astype(BF16))
        o_ref[p:p + 1, :] = jnp.sum(w * jnp.maximum(d, 0.0), axis=0, keepdims=True)


def _smp_scores(page_table, qi, wi, kidx_t, pg):
    db, npg = page_table.shape
    kern = functools.partial(_smp_scores_kernel, pg=pg)
    page_spec = lambda p: pl.BlockSpec(
        (None, IDX_DIM, BLK), lambda b, j, pt: (pt[b * npg + j * pg + p], 0, 0))
    return pl.pallas_call(
        kern,
        grid_spec=pltpu.PrefetchScalarGridSpec(
            num_scalar_prefetch=1,
            grid=(db, npg // pg),
            in_specs=[pl.BlockSpec((None, N_IDX_HEADS, IDX_DIM), lambda b, j, pt: (b, 0, 0)),
                      pl.BlockSpec((None, N_IDX_HEADS, 1), lambda b, j, pt: (b, 0, 0))]
                     + [page_spec(p) for p in range(pg)],
            out_specs=pl.BlockSpec((None, pg, BLK), lambda b, j, pt: (b, j, 0)),
        ),
        out_shape=jax.ShapeDtypeStruct((db, npg, BLK), F32),
        compiler_params=_cparams(("parallel", "arbitrary")),
        name="sample_scores",
    )(page_table.reshape(-1), qi, wi, *([kidx_t] * pg))


def _smp_select_kernel(sc_ref, qi_ref, w_ref, kin_ref, bias_ref, nb_ref, *, topk, past):
    db, npg, _ = sc_ref.shape
    key = _sort_key(sc_ref[...])
    qi = qi_ref[...].astype(F32)
    kn = kin_ref[...].astype(BF16).astype(F32)
    d = jnp.sum(qi * kn, axis=-1, keepdims=True)
    s_new = jnp.sum(w_ref[...] * jnp.maximum(d, 0.0), axis=1, keepdims=True)
    key_new = _sort_key(s_new)
    pos = (lax.broadcasted_iota(I32, (db, npg, BLK), 1) * BLK
           + lax.broadcasted_iota(I32, (db, npg, BLK), 2))

    def count(pred_past, pred_new):
        c = jnp.sum(jnp.where(pred_past, 1, 0), axis=2, keepdims=True)
        return jnp.sum(c, axis=1, keepdims=True) + jnp.where(pred_new, 1, 0)

    thr = jnp.where(count(key >= 0, key_new >= 0) >= topk, 0, INT_MIN).astype(I32)

    def bit_step(it, thr):
        cand = thr + (jnp.int32(1) << (30 - it))
        return jnp.where(count(key >= cand, key_new >= cand) >= topk, cand, thr)

    thr = lax.fori_loop(0, 31, bit_step, thr)
    need = topk - count(key > thr, key_new > thr)
    nbits = max(1, past.bit_length())

    def step(it, lo):
        cand = lo + (jnp.int32(1) << (nbits - 1 - it))
        below = count((key == thr) & (pos < cand), (key_new == thr) & (past < cand))
        return jnp.where(below < need, cand, lo)

    jcut = lax.fori_loop(0, nbits, step, jnp.zeros((db, 1, 1), I32))
    sel = (key > thr) | ((key == thr) & (pos <= jcut))
    bias_ref[...] = jnp.where(sel, 0.0, NEG)
    sel_new = (key_new > thr) | ((key_new == thr) & (past <= jcut))
    nb_ref[...] = jnp.broadcast_to(jnp.where(sel_new, 0.0, NEG), nb_ref.shape)


def _smp_select(scores, qi, wi, ki_new, topk):
    db, npg, _ = scores.shape
    kern = functools.partial(_smp_select_kernel, topk=topk, past=npg * BLK)
    full = lambda *s: pl.BlockSpec(s, lambda i: (0,) * len(s))
    return pl.pallas_call(
        kern,
        grid=(1,),
        in_specs=[full(db, npg, BLK), full(db, N_IDX_HEADS, IDX_DIM), full(db, N_IDX_HEADS, 1),
                  full(db, 1, IDX_DIM)],
        out_specs=[full(db, npg, BLK), full(db, SUBLANES, LANES)],
        out_shape=[jax.ShapeDtypeStruct((db, npg, BLK), F32),
                   jax.ShapeDtypeStruct((db, SUBLANES, LANES), F32)],
        compiler_params=_cparams(("arbitrary",)),
        name="sample_select",
    )(scores, qi, wi, ki_new)


def _smp_attend_kernel(pt_ref, q_ref, bias_ref, nb_ref, kn_ref, vn_ref, *refs, pg):
    k_refs, v_refs = refs[:pg], refs[pg:2 * pg]
    o_ref, m_ref, l_ref, acc_ref = refs[2 * pg:]
    j = pl.program_id(1)
    kd = N_KV_HEADS * HEAD_DIM

    @pl.when(j == 0)
    def _():
        m_ref[...] = jnp.full_like(m_ref, NEG)
        l_ref[...] = jnp.zeros_like(l_ref)
        acc_ref[...] = jnp.zeros_like(acc_ref)

    q = q_ref[...]
    s = [_dot(q, k_refs[p][...].astype(BF16)) + bias_ref[p:p + 1, :] for p in range(pg)]
    m_old = m_ref[...]
    m_new = m_old
    for p in range(pg):
        m_new = jnp.maximum(m_new, jnp.max(s[p], axis=-1, keepdims=True))
    alpha = jnp.exp(m_old - m_new)
    l = l_ref[...] * alpha
    acc = acc_ref[...] * alpha
    for p in range(pg):
        e = jnp.exp(s[p] - m_new)
        l = l + jnp.sum(e, axis=-1, keepdims=True)
        acc = acc + _dot_t1(e.astype(BF16), v_refs[p][...].astype(BF16))
    m_ref[...] = m_new
    l_ref[...] = l
    acc_ref[...] = acc

    @pl.when(j == pl.num_programs(1) - 1)
    def _():
        qf = q.astype(F32)
        kn = kn_ref[...].astype(BF16).astype(F32)
        vn = vn_ref[...].astype(BF16).astype(F32)
        s_new = jnp.sum(qf * kn, axis=-1, keepdims=True) + nb_ref[0:1, 0:1]
        m_fin = jnp.maximum(m_new, s_new)
        a2 = jnp.exp(m_new - m_fin)
        e_new = jnp.exp(s_new - m_fin)
        l_fin = l * a2 + e_new
        acc_fin = acc * a2 + e_new.astype(BF16).astype(F32) * vn
        res = acc_fin / l_fin
        hgrp = lax.broadcasted_iota(I32, (N_HEADS, HEAD_DIM), 0) // HEADS_PER_KV
        out = jnp.zeros((N_HEADS, HEAD_DIM), F32)
        for g in range(N_KV_HEADS):
            out = out + jnp.where(hgrp == g, res[:, g * HEAD_DIM:(g + 1) * HEAD_DIM], 0.0)
        o_ref[...] = out


def _smp_attend(page_table, q_bd, bias, nbias, k_new, v_new, k_t, v_t, pg):
    db, npg = page_table.shape
    kd = N_KV_HEADS * HEAD_DIM
    kern = functools.partial(_smp_attend_kernel, pg=pg)
    page_spec = lambda p: pl.BlockSpec(
        (None, kd, BLK), lambda b, j, pt: (pt[b * npg + j * pg + p], 0, 0))
    per_b = lambda shape: pl.BlockSpec((None,) + shape, lambda b, j, pt: (b, 0, 0))
    return pl.pallas_call(
        kern,
        grid_spec=pltpu.PrefetchScalarGridSpec(
            num_scalar_prefetch=1,
            grid=(db, npg // pg),
            in_specs=[per_b((N_HEADS, kd)),
                      pl.BlockSpec((None, pg, BLK), lambda b, j, pt: (b, j, 0)),
                      per_b((SUBLANES, LANES)), per_b((1, kd)), per_b((1, kd))]
                     + [page_spec(p) for p in range(pg)] * 2,
            out_specs=per_b((N_HEADS, HEAD_DIM)),
            scratch_shapes=[pltpu.VMEM((N_HEADS, 1), F32), pltpu.VMEM((N_HEADS, 1), F32),
                            pltpu.VMEM((N_HEADS, kd), F32)],
        ),
        out_shape=jax.ShapeDtypeStruct((db, N_HEADS, HEAD_DIM), F32),
        compiler_params=_cparams(("parallel", "arbitrary")),
        name="sample_attend",
    )(page_table.reshape(-1), q_bd, bias, nbias, k_new, v_new,
      *([k_t] * pg), *([v_t] * pg))


def _ssd_prompt_kernel(x_ref, g_ref, wz_ref, wx_ref, wdt_ref, wo_ref,
                       cw_ref, cb_ref, dtb_ref, a_ref,
                       dtbc_ref, ac_ref, dsk_ref, gn_ref, ltri_ref, utri_ref,
                       o_ref, st_ref, cv_ref, xpad_ref, h_ref, yb_ref, xt_ref, yt_ref, z_ref,
                       *, pad, n_heads):
    c = pl.program_id(1)
    d_in = n_heads * SSD_HEAD_DIM
    gn = SSD_GROUPS * D_STATE
    hpg = n_heads // SSD_GROUPS

    @pl.when(c == 0)
    def _():
        xpad_ref[0:SUBLANES, :] = jnp.zeros((SUBLANES, xpad_ref.shape[1]), F32)
        h_ref[...] = jnp.zeros_like(h_ref)

    xn = _rms(x_ref[...], g_ref[...]).astype(BF16)
    z_ref[...] = _dot(xn, wz_ref[...])
    dt_raw = _dot(xn, wdt_ref[...])
    dt_raw_t = dt_raw.T[0:n_heads]

    xpad_ref[SUBLANES:, :] = _dot(xn, wx_ref[...])
    conv = cb_ref[...] + cw_ref[CONV_W - 1:CONV_W, :] * xpad_ref[SUBLANES:, :]
    for j in range(CONV_W - 1):
        sh = CONV_W - 1 - j
        conv = conv + cw_ref[j:j + 1, :] * xpad_ref[SUBLANES - sh:SUBLANES - sh + BLK, :]
    xpad_ref[0:SUBLANES, :] = xpad_ref[BLK:BLK + SUBLANES, :]
    xbc = _silu(conv)

    live = (c > 0) | (lax.broadcasted_iota(I32, (BLK, LANES), 0) >= pad)
    dt = jnp.where(live, _softplus(dt_raw + dtb_ref[...]), 0.0)
    acum = _dot_sel(dt * a_ref[...], ltri_ref[...], fn=lambda x, m: _dot(m, x))
    live_t = (c > 0) | (lax.broadcasted_iota(I32, (n_heads, BLK), 1) >= pad)
    dtt = jnp.where(live_t, _softplus(dt_raw_t + dtbc_ref[...]), 0.0)
    acum_t = _dot_sel(dtt * ac_ref[...], utri_ref[...])
    for t in range(d_in // LANES):
        xt_ref[t * LANES:(t + 1) * LANES, :] = xbc[:, t * LANES:(t + 1) * LANES].T
    a_last = acum_t[:, BLK - 1:BLK]
    ecol_t = jnp.exp(acum_t)
    decs_t = jnp.exp(a_last - acum_t)
    ea_last = jnp.exp(a_last)
    causal_t = (lax.broadcasted_iota(I32, (BLK, BLK), 0) <= lax.broadcasted_iota(I32, (BLK, BLK), 1))
    hp = SSD_HEAD_DIM

    for g in range(SSD_GROUPS):
        bm = xbc[:, d_in + g * D_STATE:d_in + (g + 1) * D_STATE].astype(BF16)
        ct = xbc[:, d_in + gn + g * D_STATE:d_in + gn + (g + 1) * D_STATE].T.astype(BF16)
        cb_t = _dot(bm, ct)
        hprev = h_ref[g * hpg:(g + 1) * hpg].reshape(hpg * hp, D_STATE)
        y_off = _dot(hprev.astype(BF16), ct)
        ws = []
        for hh in range(hpg):
            h = g * hpg + hh
            rows = slice(h * hp, (h + 1) * hp)
            xh = xt_ref[rows, :]
            xdt = xh * dtt[h:h + 1, :]
            decay_t = jnp.exp(jnp.where(causal_t, acum_t[h:h + 1, :] - acum[:, h:h + 1], NEG))
            y = _dot(xdt.astype(BF16), (cb_t * decay_t).astype(BF16))
            y = y + ecol_t[h:h + 1, :] * y_off[hh * hp:(hh + 1) * hp]
            yt_ref[rows, :] = y + dsk_ref[0:1, h:h + 1] * xh
            ws.append((xdt * decs_t[h:h + 1, :]).astype(BF16))
        upd = _dot(jnp.concatenate(ws, axis=0), bm)
        for hh in range(hpg):
            h = g * hpg + hh
            h_ref[h] = ea_last[h:h + 1, :] * hprev[hh * hp:(hh + 1) * hp] + upd[hh * hp:(hh + 1) * hp]

    for t in range(d_in // LANES):
        yb_ref[:, t * LANES:(t + 1) * LANES] = yt_ref[t * LANES:(t + 1) * LANES, :].T

    yg = yb_ref[...] * _silu(z_ref[...])
    gsz = d_in // SSD_GROUPS
    yn = jnp.concatenate(
        [_rms(yg[:, g * gsz:(g + 1) * gsz], gn_ref[:, g * gsz:(g + 1) * gsz]).astype(BF16)
         for g in range(SSD_GROUPS)], axis=1)
    o_ref[...] = x_ref[...] + _dot(yn, wo_ref[...])

    @pl.when(c == pl.num_programs(1) - 1)
    def _():
        st_ref[...] = h_ref[...]
        cv_ref[...] = xpad_ref[0:SUBLANES, :]


def _ssd_prompt(x, g, w_z, w_xbc, w_dt, w_out, conv_w, conv_b, dt_bias, a_log, d_skip, gate_norm,
                bsz, nb, pad, n_heads):
    d = x.shape[1]
    d_in = n_heads * SSD_HEAD_DIM
    cdim = d_in + 2 * SSD_GROUPS * D_STATE
    assert n_heads <= LANES
    hp = LANES - n_heads
    a = -jnp.exp(a_log.astype(F32))
    ltri = jnp.tril(jnp.ones((BLK, BLK), F32)).astype(BF16)
    kern = functools.partial(_ssd_prompt_kernel, pad=pad, n_heads=n_heads)
    const = lambda shape: pl.BlockSpec(shape, lambda b, c: (0,) * len(shape),
                                       pipeline_mode=pl.Buffered(1))
    return pl.pallas_call(
        kern,
        grid=(bsz, nb),
        in_specs=[
            pl.BlockSpec((BLK, d), lambda b, c: (b * nb + c, 0)),
            const((1, d)), const((d, d_in)), const((d, cdim)), const((d, LANES)), const((d_in, d)),
            const((CONV_W, cdim)), const((1, cdim)), const((1, LANES)), const((1, LANES)),
            const((n_heads, 1)), const((n_heads, 1)), const((1, LANES)), const((1, d_in)),
            const((BLK, BLK)), const((BLK, BLK)),
        ],
        out_specs=[pl.BlockSpec((BLK, d), lambda b, c: (b * nb + c, 0)),
                   pl.BlockSpec((None, n_heads, SSD_HEAD_DIM, D_STATE), lambda b, c: (b, 0, 0, 0)),
                   pl.BlockSpec((None, SUBLANES, cdim), lambda b, c: (b, 0, 0))],
        out_shape=[jax.ShapeDtypeStruct((bsz * nb * BLK, d), F32),
                   jax.ShapeDtypeStruct((bsz, n_heads, SSD_HEAD_DIM, D_STATE), F32),
                   jax.ShapeDtypeStruct((bsz, SUBLANES, cdim), F32)],
        scratch_shapes=[pltpu.VMEM((BLK + SUBLANES, cdim), F32),
                        pltpu.VMEM((n_heads, SSD_HEAD_DIM, D_STATE), F32),
                        pltpu.VMEM((BLK, d_in), F32),
                        pltpu.VMEM((d_in, BLK), F32),
                        pltpu.VMEM((d_in, BLK), F32),
                        pltpu.VMEM((BLK, d_in), F32)],
        compiler_params=_cparams(("parallel", "arbitrary")),
        name="ssd_prompt",
    )(x, g.reshape(1, d), w_z, w_xbc, w_dt, w_out, conv_w, conv_b.reshape(1, cdim),
      jnp.pad(dt_bias, (0, hp)).reshape(1, LANES), jnp.pad(a, (0, hp)).reshape(1, LANES),
      dt_bias.reshape(n_heads, 1), a.reshape(n_heads, 1),
      jnp.pad(d_skip, (0, hp)).reshape(1, LANES), gate_norm.reshape(1, d_in), ltri, ltri.T)


def _ssd_sample_kernel(z_ref, xbc_ref, dt_ref, cst_ref, h0_ref, cw_ref, cb_ref, dtb_ref, a_ref,
                       dsk_ref, gn_ref, exp_ref, y_ref, h_ref, *, n_heads):
    d_in = n_heads * SSD_HEAD_DIM
    gn = SSD_GROUPS * D_STATE
    gsz = d_in // SSD_GROUPS
    hpg = n_heads // SSD_GROUPS
    conv = cb_ref[...] + cw_ref[CONV_W - 1:CONV_W, :] * xbc_ref[...]
    for j in range(CONV_W - 1):
        conv = conv + cw_ref[j:j + 1, :] * cst_ref[j:j + 1, :]
    xbc = _silu(conv)
    dt = _softplus(dt_ref[...] + dtb_ref[...])
    pad8 = lambda r: jnp.concatenate([r, jnp.zeros((SUBLANES - 1, r.shape[1]), F32)], axis=0)
    dt_ch = _dot_sel(pad8(dt), exp_ref[...])[0:1]
    da_ch = jnp.exp(_dot_sel(pad8(dt * a_ref[...]), exp_ref[...])[0:1])
    dsk_ch = _dot_sel(pad8(dsk_ref[...]), exp_ref[...])[0:1]
    xh = xbc[:, :d_in]
    xdt = xh * dt_ch
    ones = jnp.ones((SUBLANES, D_STATE), BF16)
    outs = []
    for g in range(SSD_GROUPS):
        sl = slice(g * gsz, (g + 1) * gsz)
        bm = xbc[:, d_in + g * D_STATE:d_in + (g + 1) * D_STATE]
        cm = xbc[:, d_in + gn + g * D_STATE:d_in + gn + (g + 1) * D_STATE]
        da_col = _dot_sel(pad8(da_ch[:, sl]), ones, fn=_dot_t0)
        xdt_col = _dot_sel(pad8(xdt[:, sl]), ones, fn=_dot_t0)
        h0 = h0_ref[g * hpg:(g + 1) * hpg].reshape(gsz, D_STATE)
        hn = da_col * h0 + xdt_col.astype(BF16).astype(F32) * bm.astype(BF16).astype(F32)
        h_ref[g * hpg:(g + 1) * hpg] = hn.reshape(hpg, SSD_HEAD_DIM, D_STATE)
        y_col = jnp.sum(hn * cm, axis=-1, keepdims=True)
        outs.append(y_col)
    y_cols = jnp.concatenate(outs, axis=0)
    rows = []
    eye = (lax.broadcasted_iota(I32, (LANES, LANES), 0)
           == lax.broadcasted_iota(I32, (LANES, LANES), 1)).astype(F32)
    for t in range(d_in // LANES):
        blk = y_cols[t * LANES:(t + 1) * LANES]
        rows.append(jnp.sum(blk * eye, axis=0, keepdims=True))
    y = jnp.concatenate(rows, axis=1) + dsk_ch * xh
    y = y * _silu(z_ref[...])
    for g in range(SSD_GROUPS):
        sl = slice(g * gsz, (g + 1) * gsz)
        y_ref[:, sl] = _rms(y[:, sl], gn_ref[:, sl]).astype(BF16)


def _ssd_sample(z, xbc_raw, dt_raw, conv_state, h0, conv_w, conv_b, dt_bias, a_log, d_skip,
                gate_norm, n_heads):
    db = z.shape[0]
    d_in = n_heads * SSD_HEAD_DIM
    cdim = d_in + 2 * SSD_GROUPS * D_STATE
    hp = LANES - n_heads
    a = -jnp.exp(a_log.astype(F32))
    expand = (jnp.arange(LANES)[:, None] == (jnp.arange(d_in) // SSD_HEAD_DIM)[None, :]).astype(BF16)
    kern = functools.partial(_ssd_sample_kernel, n_heads=n_heads)
    const = lambda shape: pl.BlockSpec(shape, lambda b: (0,) * len(shape))
    per_b = lambda shape: pl.BlockSpec((None,) + shape, lambda b: (b,) + (0,) * len(shape))
    y, h = pl.pallas_call(
        kern,
        grid=(db,),
        in_specs=[per_b((1, d_in)), per_b((1, cdim)), per_b((1, LANES)), per_b((CONV_W - 1, cdim)),
                  per_b((n_heads, SSD_HEAD_DIM, D_STATE)),
                  const((CONV_W, cdim)), const((1, cdim)), const((1, LANES)), const((1, LANES)),
                  const((1, LANES)), const((1, d_in)), const((LANES, d_in))],
        out_specs=[per_b((1, d_in)), per_b((n_heads, SSD_HEAD_DIM, D_STATE))],
        out_shape=[jax.ShapeDtypeStruct((db, 1, d_in), BF16),
                   jax.ShapeDtypeStruct((db, n_heads, SSD_HEAD_DIM, D_STATE), F32)],
        compiler_params=_cparams(("parallel",)),
        name="ssd_sample",
    )(z.reshape(db, 1, d_in), xbc_raw.reshape(db, 1, cdim), dt_raw.reshape(db, 1, LANES),
      conv_state, h0, conv_w, conv_b.reshape(1, cdim),
      jnp.pad(dt_bias, (0, hp)).reshape(1, LANES), jnp.pad(a, (0, hp)).reshape(1, LANES),
      jnp.pad(d_skip, (0, hp)).reshape(1, LANES), gate_norm.reshape(1, d_in), expand)
    return y.reshape(db, d_in), h


def kernel(x_prompt, x_sample, cache_k, cache_v, cache_kidx, page_table, state_ssm, state_conv,
           meta_tokens, norm_ffn_a, w_ffn_a_in, w_ffn_a_out, norm_mix, norm_ffn_b, w_ffn_b_in,
           w_ffn_b_out, w_attn_in, q_norm, k_norm, kidx_norm, w_attn_out,
           w_ssd_in, conv_w, conv_b, dt_bias, a_log, d_skip, gate_norm, w_ssd_out):
    bsz, seq, d = x_prompt.shape
    db = x_sample.shape[0]
    assert x_sample.shape[1] == 1
    t_real = N_META + seq
    nb = -(-t_real // BLK)
    t_pad = nb * BLK
    pad = t_pad - t_real
    npg = page_table.shape[1]
    past = npg * BLK
    topk_p = min(TOPK_MAX, seq // 4)
    topk_s = min(TOPK_MAX, (past + 1) // 4)
    qd, kd, qid = N_HEADS * HEAD_DIM, N_KV_HEADS * HEAD_DIM, N_IDX_HEADS * IDX_DIM
    d_in = w_ssd_out.shape[1]
    n_heads = d_in // SSD_HEAD_DIM
    cdim = d_in + 2 * SSD_GROUPS * D_STATE
    depth = norm_mix.shape[0]

    meta = jnp.broadcast_to(meta_tokens.astype(F32)[None], (bsz, N_META, d))
    xp = jnp.concatenate([jnp.zeros((bsz, pad, d), F32), meta, x_prompt], axis=1)
    xp = xp.reshape(bsz * t_pad, d)
    xs = x_sample.reshape(db, d)

    outs = {k: [] for k in ("kp", "vp", "kip", "ks", "vs", "kis", "hp", "cp", "hs", "cs")}
    for i in range(depth):
        wts = _ffn_weights(w_ffn_a_in[i], w_ffn_a_out[i])
        xp = _ffn(xp, norm_ffn_a[i], wts)
        xs = _ffn(xs, norm_ffn_a[i], wts)
        j = i // 2
        if i % 2 == 0:
            w = w_attn_in[j]
            o1, o2 = qd + 2 * kd + qid, qd + 2 * kd + qid + IDX_DIM
            w_pad = jnp.concatenate(
                [w[:, :o1], jnp.pad(w[:, o1:o2], ((0, 0), (0, LANES - IDX_DIM))),
                 jnp.pad(w[:, o2:], ((0, 0), (0, LANES - N_IDX_HEADS)))], axis=1).astype(BF16)
            w_out = w_attn_out[j].astype(BF16)

            k, v, ki, qt, qit, wt, kg, vt, kib = _attn_in(
                xp, norm_mix[i], w_pad, q_norm[j], k_norm[j], kidx_norm[j], prompt=True)
            gw = HEADS_PER_KV * BLK
            smax = (1.02 * math.sqrt(HEAD_DIM) * jnp.max(jnp.abs(q_norm[j]))
                    * jnp.max(jnp.abs(k_norm[j]))).reshape(1).astype(F32)
            o = _dsa_prompt_pairs(smax,
                            qit.reshape(bsz, nb, IDX_DIM, N_IDX_HEADS * BLK),
                            wt.reshape(bsz, nb, N_IDX_HEADS, BLK),
                            kib.reshape(bsz, nb, BLK, IDX_DIM),
                            qt.reshape(bsz, nb, N_KV_HEADS, HEAD_DIM, gw),
                            kg.reshape(N_KV_HEADS, bsz, nb, BLK, HEAD_DIM),
                            vt.reshape(bsz, nb, N_KV_HEADS, VT_ROWS, BLK),
                            pad, topk_p)
            xp = _out_proj(xp, o.reshape(bsz * t_pad, qd), w_out)
            outs["kp"].append(k.reshape(bsz, t_pad, N_KV_HEADS, HEAD_DIM)[:, pad:])
            outs["vp"].append(v.reshape(bsz, t_pad, N_KV_HEADS, HEAD_DIM)[:, pad:])
            outs["kip"].append(ki[:, :IDX_DIM].reshape(bsz, t_pad, IDX_DIM)[:, pad:])

            q, k, v, qi, ki, wi = _attn_in(
                xs, norm_mix[i], w_pad, q_norm[j], k_norm[j], kidx_norm[j], prompt=False)
            pg = math.gcd(npg, 32)
            qi3 = qi.reshape(db, N_IDX_HEADS, IDX_DIM)
            wi3 = wi[:, :N_IDX_HEADS].reshape(db, N_IDX_HEADS, 1)
            scores = _smp_scores(page_table, qi3, wi3, cache_kidx[j].transpose(0, 2, 1), pg)
            bias, nbias = _smp_select(scores, qi3, wi3, ki[:, :IDX_DIM].reshape(db, 1, IDX_DIM), topk_s)
            hsel = (jnp.arange(N_HEADS)[:, None] // HEADS_PER_KV == jnp.arange(N_KV_HEADS)[None, :])
            q_bd = (q.reshape(db, N_HEADS, 1, HEAD_DIM) * hsel[None, :, :, None].astype(BF16))
            q_bd = q_bd.reshape(db, N_HEADS, kd)
            npool = cache_k.shape[1]
            o = _smp_attend(page_table, q_bd, bias, nbias, k.reshape(db, 1, kd), v.reshape(db, 1, kd),
                            cache_k[j].transpose(0, 2, 3, 1).reshape(npool, kd, BLK),
                            cache_v[j].transpose(0, 2, 3, 1).reshape(npool, kd, BLK), pg)
            xs = _out_proj(xs, o.reshape(db, qd).astype(BF16), w_out)
            outs["ks"].append(k.reshape(db, 1, N_KV_HEADS, HEAD_DIM))
            outs["vs"].append(v.reshape(db, 1, N_KV_HEADS, HEAD_DIM))
            outs["kis"].append(ki[:, :IDX_DIM].reshape(db, 1, IDX_DIM))
        else:
            w = w_ssd_in[j]
            w_main = w[:, :d_in + cdim].astype(BF16)
            w_dt = jnp.pad(w[:, d_in + cdim:], ((0, 0), (0, LANES - n_heads))).astype(BF16)
            w_out = w_ssd_out[j].astype(BF16)
            sp = (conv_w[j], conv_b[j], dt_bias[j], a_log[j], d_skip[j], gate_norm[j])

            xp, hfin, ctail = _ssd_prompt(xp, norm_mix[i], w_main[:, :d_in], w_main[:, d_in:], w_dt, w_out,
                                          *sp, bsz, nb, pad, n_heads)
            outs["hp"].append(hfin)
            outs["cp"].append(ctail[:, SUBLANES - (CONV_W - 1):])

            z, xbc_raw, dt_raw = _norm_proj(xs, norm_mix[i], w_main, w_dt, d_in)
            y, hnew = _ssd_sample(z, xbc_raw, dt_raw, state_conv[j], state_ssm[j], *sp, n_heads)
            xs = _out_proj(xs, y, w_out)
            outs["hs"].append(hnew)
            outs["cs"].append(jnp.concatenate([state_conv[j][:, 1:], xbc_raw[:, None, :]], axis=1))
        wts = _ffn_weights(w_ffn_b_in[i], w_ffn_b_out[i])
        xp = _ffn(xp, norm_ffn_b[i], wts)
        xs = _ffn(xs, norm_ffn_b[i], wts)

    y_prompt = xp.reshape(bsz, t_pad, d)[:, pad + N_META:]
    y_sample = xs.reshape(db, 1, d)
    st = lambda key: jnp.stack(outs[key])
    return (y_prompt, y_sample, st("kp"), st("vp"), st("kip"), st("ks"), st("vs"), st("kis"),
            st("hp"), st("cp"), st("hs"), st("cs"))
```

```python
import functools
import math

import jax
import jax.numpy as jnp
from jax import lax
from jax.experimental import pallas as pl
from jax.experimental.pallas import tpu as pltpu

F32 = jnp.float32
BF16 = jnp.bfloat16
I32 = jnp.int32

N_META = 16
N_HEADS = 16
HEAD_DIM = 64
N_KV_HEADS = 4
HEADS_PER_KV = N_HEADS // N_KV_HEADS
N_IDX_HEADS = 8
IDX_DIM = 64
TOPK_MAX = 256
SSD_HEAD_DIM = 64
SSD_GROUPS = 4
D_STATE = 128
CONV_W = 4
EPS = 1e-6

LANES = 128
SUBLANES = 8
BLK = 128
NEG = -1e30
SAFE_LOGIT = 40.0
VT_ROWS = 80
INT_MIN = -2 ** 31
VMEM_LIMIT = 56 * 1024 * 1024


def _cparams(sem, vmem=VMEM_LIMIT):
    return pltpu.CompilerParams(dimension_semantics=sem, vmem_limit_bytes=vmem)


def _row_tile(rows, pref):
    best = None
    for d in range(SUBLANES, min(rows, pref) + 1, SUBLANES):
        if rows % d == 0:
            best = d
    assert best is not None, rows
    return best


def _rms(x, g):
    var = jnp.mean(x * x, axis=-1, keepdims=True)
    return x * lax.rsqrt(var + EPS) * g


def _dot(a, b):
    return jnp.dot(a, b, preferred_element_type=F32)


def _dot_t0(a, b):
    return lax.dot_general(a, b, (((0,), (0,)), ((), ())), preferred_element_type=F32)


def _dot_t1(a, b):
    return lax.dot_general(a, b, (((1,), (1,)), ((), ())), preferred_element_type=F32)


def _split3(a):
    a0 = a.astype(BF16)
    r = a - a0.astype(F32)
    a1 = r.astype(BF16)
    a2 = (r - a1.astype(F32)).astype(BF16)
    return a0, a1, a2


def _dot_sel(a, m, fn=_dot):
    a0, a1, a2 = _split3(a)
    return fn(a0, m) + fn(a1, m) + fn(a2, m)


def _silu(x):
    return x * (1.0 / (1.0 + jnp.exp(-x)))


def _softplus(x):
    return jnp.maximum(x, 0.0) + jnp.log(1.0 + jnp.exp(-jnp.abs(x)))


def _ffn_kernel(x_ref, g_ref, wi_ref, wo_ref, o_ref, xn_ref, acc_ref):
    hid = wo_ref.shape[0]
    th = 256 if hid % 256 == 0 else LANES
    xn_ref[...] = _rms(x_ref[...], g_ref[...]).astype(BF16)
    acc_ref[...] = jnp.zeros_like(acc_ref)
    for j in range(hid // th):
        xn = xn_ref[...]
        a = _dot(xn, wi_ref[:, j * th:(j + 1) * th])
        b = _dot(xn, wi_ref[:, hid + j * th:hid + (j + 1) * th])
        h = (_silu(a) * b).astype(BF16)
        acc_ref[...] += _dot(h, wo_ref[j * th:(j + 1) * th, :])
    o_ref[...] = x_ref[...] + 0.5 * acc_ref[...]


def _ffn_weights(w_in, w_out):
    return w_in.astype(BF16), w_out.astype(BF16)


def _ffn(x, g, weights):
    w_in, w_out = weights
    rows, d = x.shape
    hid = w_out.shape[0]
    tm = _row_tile(rows, 1024)
    resident = lambda shape: pl.BlockSpec(shape, lambda i: (0,) * len(shape),
                                          pipeline_mode=pl.Buffered(1))
    return pl.pallas_call(
        _ffn_kernel,
        grid=(rows // tm,),
        in_specs=[
            pl.BlockSpec((tm, d), lambda i: (i, 0)),
            resident((1, d)), resident((d, 2 * hid)), resident((hid, d)),
        ],
        out_specs=pl.BlockSpec((tm, d), lambda i: (i, 0)),
        out_shape=jax.ShapeDtypeStruct((rows, d), F32),
        scratch_shapes=[pltpu.VMEM((tm, d), BF16), pltpu.VMEM((tm, d), F32)],
        compiler_params=_cparams(("parallel",)),
        name="ffn",
    )(x, g.reshape(1, d), w_in, w_out)


def _attn_project(x_ref, g_ref, w_ref, qg_ref, kg_ref, kig_ref, gq_ref, eq_ref, gk_ref, ek_ref):
    qd = N_HEADS * HEAD_DIM
    kd = N_KV_HEADS * HEAD_DIM
    qid = N_IDX_HEADS * IDX_DIM
    xn = _rms(x_ref[...], g_ref[...]).astype(BF16)
    h = _dot(xn, w_ref[...])

    def head_norm(t, gsum_ref, gexp_ref, gain):
        ss = _dot_sel(t * t, gsum_ref[...])
        rs = lax.rsqrt(ss * (1.0 / HEAD_DIM) + EPS)
        return t * _dot_sel(rs, gexp_ref[...]) * gain

    o = 0
    q = head_norm(h[:, o:o + qd], gq_ref, eq_ref, qg_ref[...]) * (HEAD_DIM ** -0.5)
    o += qd
    k = head_norm(h[:, o:o + kd], gk_ref, ek_ref, kg_ref[...])
    o += kd
    v = h[:, o:o + kd]
    o += kd
    qi = h[:, o:o + qid] * (IDX_DIM ** -0.5)
    o += qid
    ki = h[:, o:o + LANES]
    var = jnp.sum(ki * ki, axis=-1, keepdims=True) * (1.0 / IDX_DIM)
    ki = ki * lax.rsqrt(var + EPS) * kig_ref[...]
    o += LANES
    wi = h[:, o:o + LANES] * (N_IDX_HEADS ** -0.5)
    return q, k, v, qi, ki, wi


def _attn_in_sample_kernel(*refs):
    q_o, k_o, v_o, qi_o, ki_o, wi_o = refs[10:]
    q, k, v, qi, ki, wi = _attn_project(*refs[:10])
    q_o[...] = q.astype(BF16)
    k_o[...] = k
    v_o[...] = v
    qi_o[...] = qi.astype(BF16)
    ki_o[...] = ki
    wi_o[...] = wi


def _attn_in_prompt_kernel(*refs):
    k_o, v_o, ki_o, qt_o, qit_o, wt_o, kg_o, vt_o, kib_o = refs[10:]
    q, k, v, qi, ki, wi = _attn_project(*refs[:10])
    k_o[...] = k
    v_o[...] = v
    ki_o[...] = ki
    kib_o[...] = ki[:, :IDX_DIM].astype(BF16)
    for g in range(N_KV_HEADS):
        kg_o[g] = k[:, g * HEAD_DIM:(g + 1) * HEAD_DIM].astype(BF16)
    tail = (lax.broadcasted_iota(I32, (VT_ROWS - HEAD_DIM, BLK), 0) == 0).astype(BF16)
    for r in range(q.shape[0] // BLK):
        rows = slice(r * BLK, (r + 1) * BLK)
        for t in range(N_HEADS * HEAD_DIM // LANES):
            tt = q[rows, t * LANES:(t + 1) * LANES].T.astype(BF16)
            for u in range(LANES // HEAD_DIM):
                h = t * (LANES // HEAD_DIM) + u
                g, hh = h // HEADS_PER_KV, h % HEADS_PER_KV
                qt_o[r, g, :, hh * BLK:(hh + 1) * BLK] = tt[u * HEAD_DIM:(u + 1) * HEAD_DIM]
        for t in range(N_IDX_HEADS * IDX_DIM // LANES):
            tt = qi[rows, t * LANES:(t + 1) * LANES].T.astype(BF16)
            for u in range(LANES // IDX_DIM):
                h = t * (LANES // IDX_DIM) + u
                qit_o[r, :, h * BLK:(h + 1) * BLK] = tt[u * IDX_DIM:(u + 1) * IDX_DIM]
        wt_o[r] = wi[rows].T[0:N_IDX_HEADS]
        for t in range(N_KV_HEADS * HEAD_DIM // LANES):
            tt = v[rows, t * LANES:(t + 1) * LANES].T.astype(BF16)
            for u in range(LANES // HEAD_DIM):
                vt_o[r, t * (LANES // HEAD_DIM) + u] = jnp.concatenate(
                    [tt[u * HEAD_DIM:(u + 1) * HEAD_DIM], tail], axis=0)


def _seg_mats(n_heads, hd):
    col = jnp.arange(n_heads * hd) // hd
    gsum = (col[:, None] == jnp.arange(LANES)[None, :]).astype(BF16)
    return gsum, gsum.T


def _attn_in(x, g, w_pad, q_gain, k_gain, ki_gain, prompt):
    rows, d = x.shape
    n = w_pad.shape[1]
    qd, kd, qid = N_HEADS * HEAD_DIM, N_KV_HEADS * HEAD_DIM, N_IDX_HEADS * IDX_DIM
    gw = HEADS_PER_KV * BLK
    tm = _row_tile(rows, 512)
    gq, eq = _seg_mats(N_HEADS, HEAD_DIM)
    gk, ek = _seg_mats(N_KV_HEADS, HEAD_DIM)
    const = lambda shape: pl.BlockSpec(shape, lambda i: (0,) * len(shape))
    rowb = lambda w: pl.BlockSpec((tm, w), lambda i: (i, 0))
    sds = jax.ShapeDtypeStruct
    if prompt:
        assert tm % BLK == 0
        nbk, tb = rows // BLK, tm // BLK
        blkb = lambda *s: pl.BlockSpec((tb,) + s, lambda i: (i,) + (0,) * len(s))
        kern = _attn_in_prompt_kernel
        out_specs = [rowb(kd), rowb(kd), rowb(LANES), blkb(N_KV_HEADS, HEAD_DIM, gw),
                     blkb(IDX_DIM, N_IDX_HEADS * BLK), blkb(N_IDX_HEADS, BLK),
                     pl.BlockSpec((N_KV_HEADS, tm, HEAD_DIM), lambda i: (0, i, 0)),
                     blkb(N_KV_HEADS, VT_ROWS, BLK),
                     rowb(IDX_DIM)]
        out_shape = [sds((rows, kd), F32), sds((rows, kd), F32), sds((rows, LANES), F32),
                     sds((nbk, N_KV_HEADS, HEAD_DIM, gw), BF16),
                     sds((nbk, IDX_DIM, N_IDX_HEADS * BLK), BF16),
                     sds((nbk, N_IDX_HEADS, BLK), F32),
                     sds((N_KV_HEADS, rows, HEAD_DIM), BF16),
                     sds((nbk, N_KV_HEADS, VT_ROWS, BLK), BF16),
                     sds((rows, IDX_DIM), BF16)]
    else:
        kern = _attn_in_sample_kernel
        out_specs = [rowb(qd), rowb(kd), rowb(kd), rowb(qid), rowb(LANES), rowb(LANES)]
        out_shape = [sds((rows, qd), BF16), sds((rows, kd), F32), sds((rows, kd), F32),
                     sds((rows, qid), BF16), sds((rows, LANES), F32), sds((rows, LANES), F32)]
    return pl.pallas_call(
        kern,
        grid=(rows // tm,),
        in_specs=[rowb(d), const((1, d)), const((d, n)), const((1, qd)), const((1, kd)),
                  const((1, LANES)), const((qd, LANES)), const((LANES, qd)),
                  const((kd, LANES)), const((LANES, kd))],
        out_specs=out_specs,
        out_shape=out_shape,
        compiler_params=_cparams(("parallel",)),
        name="attn_in_prompt" if prompt else "attn_in_sample",
    )(x, g.reshape(1, d), w_pad,
      jnp.tile(q_gain, N_HEADS).reshape(1, qd), jnp.tile(k_gain, N_KV_HEADS).reshape(1, kd),
      jnp.pad(ki_gain, (0, LANES - IDX_DIM)).reshape(1, LANES), gq, eq, gk, ek)


def _norm_proj_kernel(x_ref, g_ref, w_ref, ws_ref, oa_ref, ob_ref, os_ref, xn_ref, *, na):
    j = pl.program_id(1)

    @pl.when(j == 0)
    def _():
        xn = _rms(x_ref[...], g_ref[...]).astype(BF16)
        xn_ref[...] = xn
        os_ref[...] = _dot(xn, ws_ref[...])

    r = _dot(xn_ref[...], w_ref[...])

    @pl.when(j < na)
    def _():
        oa_ref[...] = r

    @pl.when(j >= na)
    def _():
        ob_ref[...] = r


def _norm_proj(x, g, w_main, w_side, n_a):
    rows, d = x.shape
    n = w_main.shape[1]
    tm = _row_tile(rows, 1024)
    tn = math.gcd(math.gcd(n_a, n - n_a), 1024)
    na = n_a // tn
    return pl.pallas_call(
        functools.partial(_norm_proj_kernel, na=na),
        grid=(rows // tm, n // tn),
        in_specs=[
            pl.BlockSpec((tm, d), lambda i, j: (i, 0)),
            pl.BlockSpec((1, d), lambda i, j: (0, 0)),
            pl.BlockSpec((d, tn), lambda i, j: (0, j)),
            pl.BlockSpec((d, LANES), lambda i, j: (0, 0)),
        ],
        out_specs=[pl.BlockSpec((tm, tn), lambda i, j: (i, jnp.minimum(j, na - 1))),
                   pl.BlockSpec((tm, tn), lambda i, j: (i, jnp.maximum(j - na, 0))),
                   pl.BlockSpec((tm, LANES), lambda i, j: (i, 0))],
        out_shape=[jax.ShapeDtypeStruct((rows, n_a), F32),
                   jax.ShapeDtypeStruct((rows, n - n_a), F32),
                   jax.ShapeDtypeStruct((rows, LANES), F32)],
        scratch_shapes=[pltpu.VMEM((tm, d), BF16)],
        compiler_params=_cparams(("parallel", "arbitrary")),
        name="norm_proj",
    )(x, g.reshape(1, d), w_main, w_side)


def _out_proj_kernel(x_ref, y_ref, w_ref, o_ref):
    o_ref[...] = x_ref[...] + _dot(y_ref[...], w_ref[...])


def _out_proj(x, y, w):
    rows, d = x.shape
    k = y.shape[1]
    tm = _row_tile(rows, 512)
    return pl.pallas_call(
        _out_proj_kernel,
        grid=(rows // tm,),
        in_specs=[pl.BlockSpec((tm, d), lambda i: (i, 0)),
                  pl.BlockSpec((tm, k), lambda i: (i, 0)),
                  pl.BlockSpec((k, d), lambda i: (0, 0))],
        out_specs=pl.BlockSpec((tm, d), lambda i: (i, 0)),
        out_shape=jax.ShapeDtypeStruct((rows, d), F32),
        compiler_params=_cparams(("parallel",)),
        name="out_proj",
    )(x, y, w)


def _sort_key(score):
    bits = pltpu.bitcast(score, I32)
    return bits ^ ((bits >> 31) & 0x7FFFFFFF)


def _tile_fold(x, op):
    r = x[0:SUBLANES]
    for t in range(1, x.shape[0] // SUBLANES):
        r = op(r, x[t * SUBLANES:(t + 1) * SUBLANES])
    return r


def _dsa_pair_kernel(smax_ref, qita_ref, qitb_ref, wta_ref, wtb_ref, qta_ref, qtb_ref,
                     ki_ref, k_ref, vt_ref, oa_ref, ob_ref,
                     key_ref, bias_ref, acc_ref, qit_s, wt_s, qt_s, *, pad, topk, nb):
    j = pl.program_id(1)
    n_a = j + 1
    blocks = (j, nb - 1 - j)
    ntrip = nb + 1
    ncount = key_ref.shape[0]
    last = ki_ref.shape[0] - 1
    row = lax.broadcasted_iota(I32, (BLK, BLK), 0)
    lane = lax.broadcasted_iota(I32, (BLK, BLK), 1)
    gw = HEADS_PER_KV * BLK

    qit_s[0], qit_s[1] = qita_ref[...], qitb_ref[...]
    wt_s[0], wt_s[1] = wta_ref[...], wtb_ref[...]
    qt_s[0], qt_s[1] = qta_ref[...], qtb_ref[...]

    def trip(t):
        blk = (t >= n_a).astype(I32)
        c = t - blk * n_a
        return blk, c, jnp.minimum(c, last)

    def pick(blk, a, b):
        if isinstance(a, tuple):
            return tuple(pick(blk, x, y) for x, y in zip(a, b))
        return jnp.where(blk == 1, b, a)

    for t in range(ncount):
        if t >= ntrip:
            key_ref[t] = jnp.full((BLK, BLK), INT_MIN, I32)
            continue
        blk, c, cr = trip(t)
        dots = _dot(ki_ref[cr], qit_s[blk])
        sc = jnp.zeros((BLK, BLK), F32)
        for h in range(N_IDX_HEADS):
            sc = sc + wt_s[blk, h:h + 1, :] * jnp.maximum(dots[:, h * BLK:(h + 1) * BLK], 0.0)
        s_pos = c * BLK + row
        t_pos = pick(blk, blocks[0], blocks[1]) * BLK + lane
        valid = (s_pos <= t_pos) & (s_pos >= pad)
        key_ref[t] = jnp.where(valid, _sort_key(sc), INT_MIN)

    def count(pred, arg):
        cnt = [jnp.zeros((SUBLANES, BLK), I32), jnp.zeros((SUBLANES, BLK), I32)]
        for t in range(ncount):
            blk, c, _ = trip(t)
            hit = jnp.where(pred(key_ref[t], pick(blk, arg[0], arg[1]), c * BLK + row), 1, 0)
            part = _tile_fold(hit, jnp.add)
            cnt[0] = cnt[0] + jnp.where(blk == 0, part, 0)
            cnt[1] = cnt[1] + jnp.where(blk == 1, part, 0)
        return tuple(jnp.sum(x, axis=0, keepdims=True) for x in cnt)

    zero = jnp.zeros((1, BLK), I32)
    c0 = count(lambda k, a, s: k >= a, (zero, zero))
    thr0 = tuple(jnp.where(x >= topk, 0, INT_MIN).astype(I32) for x in c0)

    def bit_step(it, st):
        thr, n_ge = st[:2], st[2:]
        cand = tuple(x + (jnp.int32(1) << (30 - it)) for x in thr)
        cnt = count(lambda k, a, s: k >= a, cand)
        ok = tuple(x >= topk for x in cnt)
        return (tuple(jnp.where(ok[b], cand[b], thr[b]) for b in range(2))
                + tuple(jnp.where(ok[b], cnt[b], n_ge[b]) for b in range(2)))

    st = lax.fori_loop(0, 31, bit_step, thr0 + c0)
    thr, n_ge = st[:2], st[2:]
    tied = ((n_ge[0] > topk) & (thr[0] > INT_MIN)) | ((n_ge[1] > topk) & (thr[1] > INT_MIN))
    any_tied = jnp.max(tied.astype(I32))
    nbits = (nb * BLK).bit_length()

    def idx_search():
        n_gt = count(lambda k, a, s: k > a, thr)
        need = tuple(topk - x for x in n_gt)

        def step(it, lo):
            cand = tuple(x + (jnp.int32(1) << (nbits - 1 - it)) for x in lo)
            below = count(lambda k, a, s: (k == a[0]) & (s < a[1]),
                          ((thr[0], cand[0]), (thr[1], cand[1])))
            return tuple(jnp.where(below[b] < need[b], cand[b], lo[b]) for b in range(2))
        return lax.fori_loop(0, nbits, step, (zero, zero))

    big = jnp.full((1, BLK), 2 ** 30, I32)
    jcut = lax.cond(any_tied > 0, idx_search, lambda: (big, big))

    for t in range(ntrip):
        blk, c, _ = trip(t)
        k = key_ref[t]
        th = pick(blk, thr[0], thr[1])
        sel = (k > th) | ((k == th) & (c * BLK + row <= pick(blk, jcut[0], jcut[1])))
        bias_ref[t] = jnp.where(sel & (k > INT_MIN), 0.0, NEG)

    def logits(t):
        blk, _, cr = trip(t)
        bb = jnp.concatenate([bias_ref[t]] * HEADS_PER_KV, axis=1)
        return blk, cr, [_dot(k_ref[g, cr], qt_s[blk, g]) + bb for g in range(N_KV_HEADS)]

    def attend(shift):
        acc_ref[...] = jnp.zeros_like(acc_ref)
        for t in range(ntrip):
            blk, cr, s = logits(t)
            for g in range(N_KV_HEADS):
                x = s[g] if shift is None else s[g] - pick(blk, shift[0][g], shift[1][g])
                acc_ref[blk, g] += _dot(vt_ref[cr, g], jnp.exp(x).astype(BF16))

    safe = smax_ref[0] <= SAFE_LOGIT

    @pl.when(safe)
    def _():
        attend(None)

    @pl.when(jnp.logical_not(safe))
    def _():
        m = [[jnp.full((SUBLANES, gw), NEG, F32) for _ in range(N_KV_HEADS)] for _ in range(2)]
        for t in range(ntrip):
            blk, _, s = logits(t)
            for g in range(N_KV_HEADS):
                part = _tile_fold(s[g], jnp.maximum)
                m[0][g] = jnp.maximum(m[0][g], jnp.where(blk == 0, part, NEG))
                m[1][g] = jnp.maximum(m[1][g], jnp.where(blk == 1, part, NEG))
        attend([[jnp.max(x, axis=0, keepdims=True) for x in mb] for mb in m])

    for b, o_ref in enumerate((oa_ref, ob_ref)):
        q_row = blocks[b] * BLK + lax.broadcasted_iota(I32, (BLK, LANES), 0)
        for g in range(N_KV_HEADS):
            a = acc_ref[b, g]
            res = a[0:HEAD_DIM] * (1.0 / a[HEAD_DIM:HEAD_DIM + 1])
            for t in range(HEADS_PER_KV // 2):
                two = jnp.concatenate([res[:, (2 * t + u) * BLK:(2 * t + u + 1) * BLK] for u in range(2)],
                                      axis=0)
                two = jnp.where(q_row >= pad, two.T, 0.0)
                lo = (g * HEADS_PER_KV + 2 * t) * HEAD_DIM
                o_ref[:, lo:lo + LANES] = two.astype(BF16)


def _dsa_prompt_pairs(smax, qit, wt, kib, qt, kg, vt, pad, topk):
    bsz, nb = qit.shape[:2]
    steps = (nb + 1) // 2
    gw = HEADS_PER_KV * BLK
    qd = N_HEADS * HEAD_DIM
    ncount = nb + 1
    kern = functools.partial(_dsa_pair_kernel, pad=pad, topk=topk, nb=nb)
    first = lambda *s: pl.BlockSpec((None, None) + s, lambda b, j: (b, j) + (0,) * len(s))
    second = lambda *s: pl.BlockSpec((None, None) + s, lambda b, j: (b, nb - 1 - j) + (0,) * len(s))
    oa, ob = pl.pallas_call(
        kern,
        grid=(bsz, steps),
        in_specs=[
            pl.BlockSpec(memory_space=pltpu.SMEM),
            first(IDX_DIM, N_IDX_HEADS * BLK), second(IDX_DIM, N_IDX_HEADS * BLK),
            first(N_IDX_HEADS, BLK), second(N_IDX_HEADS, BLK),
            first(N_KV_HEADS, HEAD_DIM, gw), second(N_KV_HEADS, HEAD_DIM, gw),
            pl.BlockSpec((None, nb, BLK, IDX_DIM), lambda b, j: (b, 0, 0, 0)),
            pl.BlockSpec((N_KV_HEADS, None, nb, BLK, HEAD_DIM), lambda b, j: (0, b, 0, 0, 0)),
            pl.BlockSpec((None, nb, N_KV_HEADS, VT_ROWS, BLK), lambda b, j: (b, 0, 0, 0, 0)),
        ],
        out_specs=[pl.BlockSpec((None, BLK, qd), lambda b, j: (b, j, 0)),
                   pl.BlockSpec((None, BLK, qd), lambda b, j: (b, steps - 1 - j, 0))],
        out_shape=[jax.ShapeDtypeStruct((bsz, steps * BLK, qd), BF16)] * 2,
        scratch_shapes=[pltpu.VMEM((ncount, BLK, BLK), I32),
                        pltpu.VMEM((nb + 1, BLK, BLK), F32),
                        pltpu.VMEM((2, N_KV_HEADS, VT_ROWS, gw), F32),
                        pltpu.VMEM((2, IDX_DIM, N_IDX_HEADS * BLK), BF16),
                        pltpu.VMEM((2, N_IDX_HEADS, BLK), F32),
                        pltpu.VMEM((2, N_KV_HEADS, HEAD_DIM, gw), BF16)],
        compiler_params=_cparams(("parallel", "arbitrary")),
        name="dsa_prompt",
    )(smax, qit, qit, wt, wt, qt, qt, kib, kg, vt)
    return jnp.concatenate([oa[:, :(nb - steps) * BLK], ob], axis=1)


def _smp_scores_kernel(pt_ref, qi_ref, w_ref, *refs, pg):
    page_refs, o_ref = refs[:pg], refs[pg]
    qi = qi_ref[...]
    w = w_ref[...]
    for p in range(pg):
        d = _dot(qi, page_refs[p][...].astype(BF16))
        o_ref[p:p + 1, :] = jnp.sum(w * jnp.maximum(d, 0.0), axis=0, keepdims=True)


def _smp_scores(page_table, qi, wi, kidx_t, pg):
    db, npg = page_table.shape
    kern = functools.partial(_smp_scores_kernel, pg=pg)
    page_spec = lambda p: pl.BlockSpec(
        (None, IDX_DIM, BLK), lambda b, j, pt: (pt[b * npg + j * pg + p], 0, 0))
    return pl.pallas_call(
        kern,
        grid_spec=pltpu.PrefetchScalarGridSpec(
            num_scalar_prefetch=1,
            grid=(db, npg // pg),
            in_specs=[pl.BlockSpec((None, N_IDX_HEADS, IDX_DIM), lambda b, j, pt: (b, 0, 0)),
                      pl.BlockSpec((None, N_IDX_HEADS, 1), lambda b, j, pt: (b, 0, 0))]
                     + [page_spec(p) for p in range(pg)],
            out_specs=pl.BlockSpec((None, pg, BLK), lambda b, j, pt: (b, j, 0)),
        ),
        out_shape=jax.ShapeDtypeStruct((db, npg, BLK), F32),
        compiler_params=_cparams(("parallel", "arbitrary")),
        name="sample_scores",
    )(page_table.reshape(-1), qi, wi, *([kidx_t] * pg))


def _smp_select_kernel(sc_ref, qi_ref, w_ref, kin_ref, bias_ref, nb_ref, *, topk, past):
    db, npg, _ = sc_ref.shape
    key = _sort_key(sc_ref[...])
    qi = qi_ref[...].astype(F32)
    kn = kin_ref[...].astype(BF16).astype(F32)
    d = jnp.sum(qi * kn, axis=-1, keepdims=True)
    s_new = jnp.sum(w_ref[...] * jnp.maximum(d, 0.0), axis=1, keepdims=True)
    key_new = _sort_key(s_new)
    pos = (lax.broadcasted_iota(I32, (db, npg, BLK), 1) * BLK
           + lax.broadcasted_iota(I32, (db, npg, BLK), 2))

    def count(pred_past, pred_new):
        c = jnp.sum(jnp.where(pred_past, 1, 0), axis=2, keepdims=True)
        return jnp.sum(c, axis=1, keepdims=True) + jnp.where(pred_new, 1, 0)

    thr = jnp.where(count(key >= 0, key_new >= 0) >= topk, 0, INT_MIN).astype(I32)

    def bit_step(it, thr):
        cand = thr + (jnp.int32(1) << (30 - it))
        return jnp.where(count(key >= cand, key_new >= cand) >= topk, cand, thr)

    thr = lax.fori_loop(0, 31, bit_step, thr)
    need = topk - count(key > thr, key_new > thr)
    nbits = max(1, past.bit_length())

    def step(it, lo):
        cand = lo + (jnp.int32(1) << (nbits - 1 - it))
        below = count((key == thr) & (pos < cand), (key_new == thr) & (past < cand))
        return jnp.where(below < need, cand, lo)

    jcut = lax.fori_loop(0, nbits, step, jnp.zeros((db, 1, 1), I32))
    sel = (key > thr) | ((key == thr) & (pos <= jcut))
    bias_ref[...] = jnp.where(sel, 0.0, NEG)
    sel_new = (key_new > thr) | ((key_new == thr) & (past <= jcut))
    nb_ref[...] = jnp.broadcast_to(jnp.where(sel_new, 0.0, NEG), nb_ref.shape)


def _smp_select(scores, qi, wi, ki_new, topk):
    db, npg, _ = scores.shape
    kern = functools.partial(_smp_select_kernel, topk=topk, past=npg * BLK)
    full = lambda *s: pl.BlockSpec(s, lambda i: (0,) * len(s))
    return pl.pallas_call(
        kern,
        grid=(1,),
        in_specs=[full(db, npg, BLK), full(db, N_IDX_HEADS, IDX_DIM), full(db, N_IDX_HEADS, 1),
                  full(db, 1, IDX_DIM)],
        out_specs=[full(db, npg, BLK), full(db, SUBLANES, LANES)],
        out_shape=[jax.ShapeDtypeStruct((db, npg, BLK), F32),
                   jax.ShapeDtypeStruct((db, SUBLANES, LANES), F32)],
        compiler_params=_cparams(("arbitrary",)),
        name="sample_select",
    )(scores, qi, wi, ki_new)


def _smp_attend_kernel(pt_ref, q_ref, bias_ref, nb_ref, kn_ref, vn_ref, *refs, pg):
    k_refs, v_refs = refs[:pg], refs[pg:2 * pg]
    o_ref, m_ref, l_ref, acc_ref = refs[2 * pg:]
    j = pl.program_id(1)
    kd = N_KV_HEADS * HEAD_DIM

    @pl.when(j == 0)
    def _():
        m_ref[...] = jnp.full_like(m_ref, NEG)
        l_ref[...] = jnp.zeros_like(l_ref)
        acc_ref[...] = jnp.zeros_like(acc_ref)

    q = q_ref[...]
    s = [_dot(q, k_refs[p][...].astype(BF16)) + bias_ref[p:p + 1, :] for p in range(pg)]
    m_old = m_ref[...]
    m_new = m_old
    for p in range(pg):
        m_new = jnp.maximum(m_new, jnp.max(s[p], axis=-1, keepdims=True))
    alpha = jnp.exp(m_old - m_new)
    l = l_ref[...] * alpha
    acc = acc_ref[...] * alpha
    for p in range(pg):
        e = jnp.exp(s[p] - m_new)
        l = l + jnp.sum(e, axis=-1, keepdims=True)
        acc = acc + _dot_t1(e.astype(BF16), v_refs[p][...].astype(BF16))
    m_ref[...] = m_new
    l_ref[...] = l
    acc_ref[...] = acc

    @pl.when(j == pl.num_programs(1) - 1)
    def _():
        qf = q.astype(F32)
        kn = kn_ref[...].astype(BF16).astype(F32)
        vn = vn_ref[...].astype(BF16).astype(F32)
        s_new = jnp.sum(qf * kn, axis=-1, keepdims=True) + nb_ref[0:1, 0:1]
        m_fin = jnp.maximum(m_new, s_new)
        a2 = jnp.exp(m_new - m_fin)
        e_new = jnp.exp(s_new - m_fin)
        l_fin = l * a2 + e_new
        acc_fin = acc * a2 + e_new.astype(BF16).astype(F32) * vn
        res = acc_fin / l_fin
        hgrp = lax.broadcasted_iota(I32, (N_HEADS, HEAD_DIM), 0) // HEADS_PER_KV
        out = jnp.zeros((N_HEADS, HEAD_DIM), F32)
        for g in range(N_KV_HEADS):
            out = out + jnp.where(hgrp == g, res[:, g * HEAD_DIM:(g + 1) * HEAD_DIM], 0.0)
        o_ref[...] = out


def _smp_attend(page_table, q_bd, bias, nbias, k_new, v_new, k_t, v_t, pg):
    db, npg = page_table.shape
    kd = N_KV_HEADS * HEAD_DIM
    kern = functools.partial(_smp_attend_kernel, pg=pg)
    page_spec = lambda p: pl.BlockSpec(
        (None, kd, BLK), lambda b, j, pt: (pt[b * npg + j * pg + p], 0, 0))
    per_b = lambda shape: pl.BlockSpec((None,) + shape, lambda b, j, pt: (b, 0, 0))
    return pl.pallas_call(
        kern,
        grid_spec=pltpu.PrefetchScalarGridSpec(
            num_scalar_prefetch=1,
            grid=(db, npg // pg),
            in_specs=[per_b((N_HEADS, kd)),
                      pl.BlockSpec((None, pg, BLK), lambda b, j, pt: (b, j, 0)),
                      per_b((SUBLANES, LANES)), per_b((1, kd)), per_b((1, kd))]
                     + [page_spec(p) for p in range(pg)] * 2,
            out_specs=per_b((N_HEADS, HEAD_DIM)),
            scratch_shapes=[pltpu.VMEM((N_HEADS, 1), F32), pltpu.VMEM((N_HEADS, 1), F32),
                            pltpu.VMEM((N_HEADS, kd), F32)],
        ),
        out_shape=jax.ShapeDtypeStruct((db, N_HEADS, HEAD_DIM), F32),
        compiler_params=_cparams(("parallel", "arbitrary")),
        name="sample_attend",
    )(page_table.reshape(-1), q_bd, bias, nbias, k_new, v_new,
      *([k_t] * pg), *([v_t] * pg))


def _ssd_prompt_kernel(x_ref, g_ref, wz_ref, wx_ref, wdt_ref, wo_ref,
                       cw_ref, cb_ref, dtb_ref, a_ref,
                       dtbc_ref, ac_ref, dsk_ref, gn_ref, ltri_ref, utri_ref,
                       o_ref, st_ref, cv_ref, xpad_ref, h_ref, yb_ref, xt_ref, yt_ref, z_ref,
                       *, pad, n_heads):
    c = pl.program_id(1)
    d_in = n_heads * SSD_HEAD_DIM
    gn = SSD_GROUPS * D_STATE
    hpg = n_heads // SSD_GROUPS

    @pl.when(c == 0)
    def _():
        xpad_ref[0:SUBLANES, :] = jnp.zeros((SUBLANES, xpad_ref.shape[1]), F32)
        h_ref[...] = jnp.zeros_like(h_ref)

    xn = _rms(x_ref[...], g_ref[...]).astype(BF16)
    z_ref[...] = _dot(xn, wz_ref[...])
    dt_raw = _dot(xn, wdt_ref[...])
    dt_raw_t = dt_raw.T[0:n_heads]

    xpad_ref[SUBLANES:, :] = _dot(xn, wx_ref[...])
    conv = cb_ref[...] + cw_ref[CONV_W - 1:CONV_W, :] * xpad_ref[SUBLANES:, :]
    for j in range(CONV_W - 1):
        sh = CONV_W - 1 - j
        conv = conv + cw_ref[j:j + 1, :] * xpad_ref[SUBLANES - sh:SUBLANES - sh + BLK, :]
    xpad_ref[0:SUBLANES, :] = xpad_ref[BLK:BLK + SUBLANES, :]
    xbc = _silu(conv)

    live = (c > 0) | (lax.broadcasted_iota(I32, (BLK, LANES), 0) >= pad)
    dt = jnp.where(live, _softplus(dt_raw + dtb_ref[...]), 0.0)
    acum = _dot_sel(dt * a_ref[...], ltri_ref[...], fn=lambda x, m: _dot(m, x))
    live_t = (c > 0) | (lax.broadcasted_iota(I32, (n_heads, BLK), 1) >= pad)
    dtt = jnp.where(live_t, _softplus(dt_raw_t + dtbc_ref[...]), 0.0)
    acum_t = _dot_sel(dtt * ac_ref[...], utri_ref[...])
    for t in range(d_in // LANES):
        xt_ref[t * LANES:(t + 1) * LANES, :] = xbc[:, t * LANES:(t + 1) * LANES].T
    a_last = acum_t[:, BLK - 1:BLK]
    ecol_t = jnp.exp(acum_t)
    decs_t = jnp.exp(a_last - acum_t)
    ea_last = jnp.exp(a_last)
    causal_t = (lax.broadcasted_iota(I32, (BLK, BLK), 0) <= lax.broadcasted_iota(I32, (BLK, BLK), 1))
    hp = SSD_HEAD_DIM

    for g in range(SSD_GROUPS):
        bm = xbc[:, d_in + g * D_STATE:d_in + (g + 1) * D_STATE].astype(BF16)
        ct = xbc[:, d_in + gn + g * D_STATE:d_in + gn + (g + 1) * D_STATE].T.astype(BF16)
        cb_t = _dot(bm, ct)
        hprev = h_ref[g * hpg:(g + 1) * hpg].reshape(hpg * hp, D_STATE)
        y_off = _dot(hprev.astype(BF16), ct)
        ws = []
        for hh in range(hpg):
            h = g * hpg + hh
            rows = slice(h * hp, (h + 1) * hp)
            xh = xt_ref[rows, :]
            xdt = xh * dtt[h:h + 1, :]
            decay_t = jnp.exp(jnp.where(causal_t, acum_t[h:h + 1, :] - acum[:, h:h + 1], NEG))
            y = _dot(xdt.astype(BF16), (cb_t * decay_t).astype(BF16))
            y = y + ecol_t[h:h + 1, :] * y_off[hh * hp:(hh + 1) * hp]
            yt_ref[rows, :] = y + dsk_ref[0:1, h:h + 1] * xh
            ws.append((xdt * decs_t[h:h + 1, :]).astype(BF16))
        upd = _dot(jnp.concatenate(ws, axis=0), bm)
        for hh in range(hpg):
            h = g * hpg + hh
            h_ref[h] = ea_last[h:h + 1, :] * hprev[hh * hp:(hh + 1) * hp] + upd[hh * hp:(hh + 1) * hp]

    for t in range(d_in // LANES):
        yb_ref[:, t * LANES:(t + 1) * LANES] = yt_ref[t * LANES:(t + 1) * LANES, :].T

    yg = yb_ref[...] * _silu(z_ref[...])
    gsz = d_in // SSD_GROUPS
    yn = jnp.concatenate(
        [_rms(yg[:, g * gsz:(g + 1) * gsz], gn_ref[:, g * gsz:(g + 1) * gsz]).astype(BF16)
         for g in range(SSD_GROUPS)], axis=1)
    o_ref[...] = x_ref[...] + _dot(yn, wo_ref[...])

    @pl.when(c == pl.num_programs(1) - 1)
    def _():
        st_ref[...] = h_ref[...]
        cv_ref[...] = xpad_ref[0:SUBLANES, :]


def _ssd_prompt(x, g, w_z, w_xbc, w_dt, w_out, conv_w, conv_b, dt_bias, a_log, d_skip, gate_norm,
                bsz, nb, pad, n_heads):
    d = x.shape[1]
    d_in = n_heads * SSD_HEAD_DIM
    cdim = d_in + 2 * SSD_GROUPS * D_STATE
    assert n_heads <= LANES
    hp = LANES - n_heads
    a = -jnp.exp(a_log.astype(F32))
    ltri = jnp.tril(jnp.ones((BLK, BLK), F32)).astype(BF16)
    kern = functools.partial(_ssd_prompt_kernel, pad=pad, n_heads=n_heads)
    const = lambda shape: pl.BlockSpec(shape, lambda b, c: (0,) * len(shape),
                                       pipeline_mode=pl.Buffered(1))
    return pl.pallas_call(
        kern,
        grid=(bsz, nb),
        in_specs=[
            pl.BlockSpec((BLK, d), lambda b, c: (b * nb + c, 0)),
            const((1, d)), const((d, d_in)), const((d, cdim)), const((d, LANES)), const((d_in, d)),
            const((CONV_W, cdim)), const((1, cdim)), const((1, LANES)), const((1, LANES)),
            const((n_heads, 1)), const((n_heads, 1)), const((1, LANES)), const((1, d_in)),
            const((BLK, BLK)), const((BLK, BLK)),
        ],
        out_specs=[pl.BlockSpec((BLK, d), lambda b, c: (b * nb + c, 0)),
                   pl.BlockSpec((None, n_heads, SSD_HEAD_DIM, D_STATE), lambda b, c: (b, 0, 0, 0)),
                   pl.BlockSpec((None, SUBLANES, cdim), lambda b, c: (b, 0, 0))],
        out_shape=[jax.ShapeDtypeStruct((bsz * nb * BLK, d), F32),
                   jax.ShapeDtypeStruct((bsz, n_heads, SSD_HEAD_DIM, D_STATE), F32),
                   jax.ShapeDtypeStruct((bsz, SUBLANES, cdim), F32)],
        scratch_shapes=[pltpu.VMEM((BLK + SUBLANES, cdim), F32),
                        pltpu.VMEM((n_heads, SSD_HEAD_DIM, D_STATE), F32),
                        pltpu.VMEM((BLK, d_in), F32),
                        pltpu.VMEM((d_in, BLK), F32),
                        pltpu.VMEM((d_in, BLK), F32),
                        pltpu.VMEM((BLK, d_in), F32)],
        compiler_params=_cparams(("parallel", "arbitrary")),
        name="ssd_prompt",
    )(x, g.reshape(1, d), w_z, w_xbc, w_dt, w_out, conv_w, conv_b.reshape(1, cdim),
      jnp.pad(dt_bias, (0, hp)).reshape(1, LANES), jnp.pad(a, (0, hp)).reshape(1, LANES),
      dt_bias.reshape(n_heads, 1), a.reshape(n_heads, 1),
      jnp.pad(d_skip, (0, hp)).reshape(1, LANES), gate_norm.reshape(1, d_in), ltri, ltri.T)


def _ssd_sample_kernel(z_ref, xbc_ref, dt_ref, cst_ref, h0_ref, cw_ref, cb_ref, dtb_ref, a_ref,
                       dsk_ref, gn_ref, exp_ref, y_ref, h_ref, *, n_heads):
    d_in = n_heads * SSD_HEAD_DIM
    gn = SSD_GROUPS * D_STATE
    gsz = d_in // SSD_GROUPS
    hpg = n_heads // SSD_GROUPS
    conv = cb_ref[...] + cw_ref[CONV_W - 1:CONV_W, :] * xbc_ref[...]
    for j in range(CONV_W - 1):
        conv = conv + cw_ref[j:j + 1, :] * cst_ref[j:j + 1, :]
    xbc = _silu(conv)
    dt = _softplus(dt_ref[...] + dtb_ref[...])
    pad8 = lambda r: jnp.concatenate([r, jnp.zeros((SUBLANES - 1, r.shape[1]), F32)], axis=0)
    dt_ch = _dot_sel(pad8(dt), exp_ref[...])[0:1]
    da_ch = jnp.exp(_dot_sel(pad8(dt * a_ref[...]), exp_ref[...])[0:1])
    dsk_ch = _dot_sel(pad8(dsk_ref[...]), exp_ref[...])[0:1]
    xh = xbc[:, :d_in]
    xdt = xh * dt_ch
    ones = jnp.ones((SUBLANES, D_STATE), BF16)
    outs = []
    for g in range(SSD_GROUPS):
        sl = slice(g * gsz, (g + 1) * gsz)
        bm = xbc[:, d_in + g * D_STATE:d_in + (g + 1) * D_STATE]
        cm = xbc[:, d_in + gn + g * D_STATE:d_in + gn + (g + 1) * D_STATE]
        da_col = _dot_sel(pad8(da_ch[:, sl]), ones, fn=_dot_t0)
        xdt_col = _dot_sel(pad8(xdt[:, sl]), ones, fn=_dot_t0)
        h0 = h0_ref[g * hpg:(g + 1) * hpg].reshape(gsz, D_STATE)
        hn = da_col * h0 + xdt_col.astype(BF16).astype(F32) * bm.astype(BF16).astype(F32)
        h_ref[g * hpg:(g + 1) * hpg] = hn.reshape(hpg, SSD_HEAD_DIM, D_STATE)
        y_col = jnp.sum(hn * cm, axis=-1, keepdims=True)
        outs.append(y_col)
    y_cols = jnp.concatenate(outs, axis=0)
    rows = []
    eye = (lax.broadcasted_iota(I32, (LANES, LANES), 0)
           == lax.broadcasted_iota(I32, (LANES, LANES), 1)).astype(F32)
    for t in range(d_in // LANES):
        blk = y_cols[t * LANES:(t + 1) * LANES]
        rows.append(jnp.sum(blk * eye, axis=0, keepdims=True))
    y = jnp.concatenate(rows, axis=1) + dsk_ch * xh
    y = y * _silu(z_ref[...])
    for g in range(SSD_GROUPS):
        sl = slice(g * gsz, (g + 1) * gsz)
        y_ref[:, sl] = _rms(y[:, sl], gn_ref[:, sl]).astype(BF16)


def _ssd_sample(z, xbc_raw, dt_raw, conv_state, h0, conv_w, conv_b, dt_bias, a_log, d_skip,
                gate_norm, n_heads):
    db = z.shape[0]
    d_in = n_heads * SSD_HEAD_DIM
    cdim = d_in + 2 * SSD_GROUPS * D_STATE
    hp = LANES - n_heads
    a = -jnp.exp(a_log.astype(F32))
    expand = (jnp.arange(LANES)[:, None] == (jnp.arange(d_in) // SSD_HEAD_DIM)[None, :]).astype(BF16)
    kern = functools.partial(_ssd_sample_kernel, n_heads=n_heads)
    const = lambda shape: pl.BlockSpec(shape, lambda b: (0,) * len(shape))
    per_b = lambda shape: pl.BlockSpec((None,) + shape, lambda b: (b,) + (0,) * len(shape))
    y, h = pl.pallas_call(
        kern,
        grid=(db,),
        in_specs=[per_b((1, d_in)), per_b((1, cdim)), per_b((1, LANES)), per_b((CONV_W - 1, cdim)),
                  per_b((n_heads, SSD_HEAD_DIM, D_STATE)),
                  const((CONV_W, cdim)), const((1, cdim)), const((1, LANES)), const((1, LANES)),
                  const((1, LANES)), const((1, d_in)), const((LANES, d_in))],
        out_specs=[per_b((1, d_in)), per_b((n_heads, SSD_HEAD_DIM, D_STATE))],
        out_shape=[jax.ShapeDtypeStruct((db, 1, d_in), BF16),
                   jax.ShapeDtypeStruct((db, n_heads, SSD_HEAD_DIM, D_STATE), F32)],
        compiler_params=_cparams(("parallel",)),
        name="ssd_sample",
    )(z.reshape(db, 1, d_in), xbc_raw.reshape(db, 1, cdim), dt_raw.reshape(db, 1, LANES),
      conv_state, h0, conv_w, conv_b.reshape(1, cdim),
      jnp.pad(dt_bias, (0, hp)).reshape(1, LANES), jnp.pad(a, (0, hp)).reshape(1, LANES),
      jnp.pad(d_skip, (0, hp)).reshape(1, LANES), gate_norm.reshape(1, d_in), expand)
    return y.reshape(db, d_in), h


def kernel(x_prompt, x_sample, cache_k, cache_v, cache_kidx, page_table, state_ssm, state_conv,
           meta_tokens, norm_ffn_a, w_ffn_a_in, w_ffn_a_out, norm_mix, norm_ffn_b, w_ffn_b_in,
           w_ffn_b_out, w_attn_in, q_norm, k_norm, kidx_norm, w_attn_out,
           w_ssd_in, conv_w, conv_b, dt_bias, a_log, d_skip, gate_norm, w_ssd_out):
    bsz, seq, d = x_prompt.shape
    db = x_sample.shape[0]
    assert x_sample.shape[1] == 1
    t_real = N_META + seq
    nb = -(-t_real // BLK)
    t_pad = nb * BLK
    pad = t_pad - t_real
    npg = page_table.shape[1]
    past = npg * BLK
    topk_p = min(TOPK_MAX, seq // 4)
    topk_s = min(TOPK_MAX, (past + 1) // 4)
    qd, kd, qid = N_HEADS * HEAD_DIM, N_KV_HEADS * HEAD_DIM, N_IDX_HEADS * IDX_DIM
    d_in = w_ssd_out.shape[1]
    n_heads = d_in // SSD_HEAD_DIM
    cdim = d_in + 2 * SSD_GROUPS * D_STATE
    depth = norm_mix.shape[0]

    meta = jnp.broadcast_to(meta_tokens.astype(F32)[None], (bsz, N_META, d))
    xp = jnp.concatenate([jnp.zeros((bsz, pad, d), F32), meta, x_prompt], axis=1)
    xp = xp.reshape(bsz * t_pad, d)
    xs = x_sample.reshape(db, d)

    outs = {k: [] for k in ("kp", "vp", "kip", "ks", "vs", "kis", "hp", "cp", "hs", "cs")}
    for i in range(depth):
        wts = _ffn_weights(w_ffn_a_in[i], w_ffn_a_out[i])
        xp = _ffn(xp, norm_ffn_a[i], wts)
        xs = _ffn(xs, norm_ffn_a[i], wts)
        j = i // 2
        if i % 2 == 0:
            w = w_attn_in[j]
            o1, o2 = qd + 2 * kd + qid, qd + 2 * kd + qid + IDX_DIM
            w_pad = jnp.concatenate(
                [w[:, :o1], jnp.pad(w[:, o1:o2], ((0, 0), (0, LANES - IDX_DIM))),
                 jnp.pad(w[:, o2:], ((0, 0), (0, LANES - N_IDX_HEADS)))], axis=1).astype(BF16)
            w_out = w_attn_out[j].astype(BF16)

            k, v, ki, qt, qit, wt, kg, vt, kib = _attn_in(
                xp, norm_mix[i], w_pad, q_norm[j], k_norm[j], kidx_norm[j], prompt=True)
            gw = HEADS_PER_KV * BLK
            smax = (1.02 * math.sqrt(HEAD_DIM) * jnp.max(jnp.abs(q_norm[j]))
                    * jnp.max(jnp.abs(k_norm[j]))).reshape(1).astype(F32)
            o = _dsa_prompt_pairs(smax,
                            qit.reshape(bsz, nb, IDX_DIM, N_IDX_HEADS * BLK),
                            wt.reshape(bsz, nb, N_IDX_HEADS, BLK),
                            kib.reshape(bsz, nb, BLK, IDX_DIM),
                            qt.reshape(bsz, nb, N_KV_HEADS, HEAD_DIM, gw),
                            kg.reshape(N_KV_HEADS, bsz, nb, BLK, HEAD_DIM),
                            vt.reshape(bsz, nb, N_KV_HEADS, VT_ROWS, BLK),
                            pad, topk_p)
            xp = _out_proj(xp, o.reshape(bsz * t_pad, qd), w_out)
            outs["kp"].append(k.reshape(bsz, t_pad, N_KV_HEADS, HEAD_DIM)[:, pad:])
            outs["vp"].append(v.reshape(bsz, t_pad, N_KV_HEADS, HEAD_DIM)[:, pad:])
            outs["kip"].append(ki[:, :IDX_DIM].reshape(bsz, t_pad, IDX_DIM)[:, pad:])

            q, k, v, qi, ki, wi = _attn_in(
                xs, norm_mix[i], w_pad, q_norm[j], k_norm[j], kidx_norm[j], prompt=False)
            pg = math.gcd(npg, 32)
            qi3 = qi.reshape(db, N_IDX_HEADS, IDX_DIM)
            wi3 = wi[:, :N_IDX_HEADS].reshape(db, N_IDX_HEADS, 1)
            scores = _smp_scores(page_table, qi3, wi3, cache_kidx[j].transpose(0, 2, 1), pg)
            bias, nbias = _smp_select(scores, qi3, wi3, ki[:, :IDX_DIM].reshape(db, 1, IDX_DIM), topk_s)
            hsel = (jnp.arange(N_HEADS)[:, None] // HEADS_PER_KV == jnp.arange(N_KV_HEADS)[None, :])
            q_bd = (q.reshape(db, N_HEADS, 1, HEAD_DIM) * hsel[None, :, :, None].astype(BF16))
            q_bd = q_bd.reshape(db, N_HEADS, kd)
            npool = cache_k.shape[1]
            o = _smp_attend(page_table, q_bd, bias, nbias, k.reshape(db, 1, kd), v.reshape(db, 1, kd),
                            cache_k[j].transpose(0, 2, 3, 1).reshape(npool, kd, BLK),
                            cache_v[j].transpose(0, 2, 3, 1).reshape(npool, kd, BLK), pg)
            xs = _out_proj(xs, o.reshape(db, qd).astype(BF16), w_out)
            outs["ks"].append(k.reshape(db, 1, N_KV_HEADS, HEAD_DIM))
            outs["vs"].append(v.reshape(db, 1, N_KV_HEADS, HEAD_DIM))
            outs["kis"].append(ki[:, :IDX_DIM].reshape(db, 1, IDX_DIM))
        else:
            w = w_ssd_in[j]
            w_main = w[:, :d_in + cdim].astype(BF16)
            w_dt = jnp.pad(w[:, d_in + cdim:], ((0, 0), (0, LANES - n_heads))).astype(BF16)
            w_out = w_ssd_out[j].astype(BF16)
            sp = (conv_w[j], conv_b[j], dt_bias[j], a_log[j], d_skip[j], gate_norm[j])

            xp, hfin, ctail = _ssd_prompt(xp, norm_mix[i], w_main[:, :d_in], w_main[:, d_in:], w_dt, w_out,
                                          *sp, bsz, nb, pad, n_heads)
            outs["hp"].append(hfin)
            outs["cp"].append(ctail[:, SUBLANES - (CONV_W - 1):])

            z, xbc_raw, dt_raw = _norm_proj(xs, norm_mix[i], w_main, w_dt, d_in)
            y, hnew = _ssd_sample(z, xbc_raw, dt_raw, state_conv[j], state_ssm[j], *sp, n_heads)
            xs = _out_proj(xs, y, w_out)
            outs["hs"].append(hnew)
            outs["cs"].append(jnp.concatenate([state_conv[j][:, 1:], xbc_raw[:, None, :]], axis=1))
        wts = _ffn_weights(w_ffn_b_in[i], w_ffn_b_out[i])
        xp = _ffn(xp, norm_ffn_b[i], wts)
        xs = _ffn(xs, norm_ffn_b[i], wts)

    y_prompt = xp.reshape(bsz, t_pad, d)[:, pad + N_META:]
    y_sample = xs.reshape(db, 1, d)
    st = lambda key: jnp.stack(outs[key])
    return (y_prompt, y_sample, st("kp"), st("vp"), st("kip"), st("ks"), st("vs"), st("kis"),
            st("hp"), st("cp"), st("hs"), st("cs"))
```

```python
import functools
import math

import jax
import jax.numpy as jnp
from jax import lax
from jax.experimental import pallas as pl
from jax.experimental.pallas import tpu as pltpu

F32 = jnp.float32
BF16 = jnp.bfloat16
I32 = jnp.int32

N_META = 16
N_HEADS = 16
HEAD_DIM = 64
N_KV_HEADS = 4
HEADS_PER_KV = N_HEADS // N_KV_HEADS
N_IDX_HEADS = 8
IDX_DIM = 64
TOPK_MAX = 256
SSD_HEAD_DIM = 64
SSD_GROUPS = 4
D_STATE = 128
CONV_W = 4
EPS = 1e-6

LANES = 128
SUBLANES = 8
BLK = 128
NEG = -1e30
SAFE_LOGIT = 40.0
VT_ROWS = 80
INT_MIN = -2 ** 31
VMEM_LIMIT = 56 * 1024 * 1024


def _cparams(sem, vmem=VMEM_LIMIT):
    return pltpu.CompilerParams(dimension_semantics=sem, vmem_limit_bytes=vmem)


def _row_tile(rows, pref):
    best = None
    for d in range(SUBLANES, min(rows, pref) + 1, SUBLANES):
        if rows % d == 0:
            best = d
    assert best is not None, rows
    return best


def _rms(x, g):
    var = jnp.mean(x * x, axis=-1, keepdims=True)
    return x * lax.rsqrt(var + EPS) * g


def _dot(a, b):
    return jnp.dot(a, b, preferred_element_type=F32)


def _dot_t0(a, b):
    return lax.dot_general(a, b, (((0,), (0,)), ((), ())), preferred_element_type=F32)


def _dot_t1(a, b):
    return lax.dot_general(a, b, (((1,), (1,)), ((), ())), preferred_element_type=F32)


def _split3(a):
    a0 = a.astype(BF16)
    r = a - a0.astype(F32)
    a1 = r.astype(BF16)
    a2 = (r - a1.astype(F32)).astype(BF16)
    return a0, a1, a2


def _dot_sel(a, m, fn=_dot):
    a0, a1, a2 = _split3(a)
    return fn(a0, m) + fn(a1, m) + fn(a2, m)


def _silu(x):
    return x * (1.0 / (1.0 + jnp.exp(-x)))


def _softplus(x):
    return jnp.maximum(x, 0.0) + jnp.log(1.0 + jnp.exp(-jnp.abs(x)))


def _ffn_kernel(x_ref, g_ref, wi_ref, wo_ref, o_ref, xn_ref, acc_ref):
    hid = wo_ref.shape[0]
    th = 256 if hid % 256 == 0 else LANES
    xn_ref[...] = _rms(x_ref[...], g_ref[...]).astype(BF16)
    acc_ref[...] = jnp.zeros_like(acc_ref)
    for j in range(hid // th):
        xn = xn_ref[...]
        a = _dot(xn, wi_ref[:, j * th:(j + 1) * th])
        b = _dot(xn, wi_ref[:, hid + j * th:hid + (j + 1) * th])
        h = (_silu(a) * b).astype(BF16)
        acc_ref[...] += _dot(h, wo_ref[j * th:(j + 1) * th, :])
    o_ref[...] = x_ref[...] + 0.5 * acc_ref[...]


def _ffn_weights(w_in, w_out):
    return w_in.astype(BF16), w_out.astype(BF16)


def _ffn(x, g, weights):
    w_in, w_out = weights
    rows, d = x.shape
    hid = w_out.shape[0]
    tm = _row_tile(rows, 1024)
    resident = lambda shape: pl.BlockSpec(shape, lambda i: (0,) * len(shape),
                                          pipeline_mode=pl.Buffered(1))
    return pl.pallas_call(
        _ffn_kernel,
        grid=(rows // tm,),
        in_specs=[
            pl.BlockSpec((tm, d), lambda i: (i, 0)),
            resident((1, d)), resident((d, 2 * hid)), resident((hid, d)),
        ],
        out_specs=pl.BlockSpec((tm, d), lambda i: (i, 0)),
        out_shape=jax.ShapeDtypeStruct((rows, d), F32),
        scratch_shapes=[pltpu.VMEM((tm, d), BF16), pltpu.VMEM((tm, d), F32)],
        compiler_params=_cparams(("parallel",)),
        name="ffn",
    )(x, g.reshape(1, d), w_in, w_out)


def _attn_project(x_ref, g_ref, w_ref, qg_ref, kg_ref, kig_ref, gq_ref, eq_ref, gk_ref, ek_ref):
    qd = N_HEADS * HEAD_DIM
    kd = N_KV_HEADS * HEAD_DIM
    qid = N_IDX_HEADS * IDX_DIM
    xn = _rms(x_ref[...], g_ref[...]).astype(BF16)
    h = _dot(xn, w_ref[...])

    def head_norm(t, gsum_ref, gexp_ref, gain):
        ss = _dot_sel(t * t, gsum_ref[...])
        rs = lax.rsqrt(ss * (1.0 / HEAD_DIM) + EPS)
        return t * _dot_sel(rs, gexp_ref[...]) * gain

    o = 0
    q = head_norm(h[:, o:o + qd], gq_ref, eq_ref, qg_ref[...]) * (HEAD_DIM ** -0.5)
    o += qd
    k = head_norm(h[:, o:o + kd], gk_ref, ek_ref, kg_ref[...])
    o += kd
    v = h[:, o:o + kd]
    o += kd
    qi = h[:, o:o + qid] * (IDX_DIM ** -0.5)
    o += qid
    ki = h[:, o:o + LANES]
    var = jnp.sum(ki * ki, axis=-1, keepdims=True) * (1.0 / IDX_DIM)
    ki = ki * lax.rsqrt(var + EPS) * kig_ref[...]
    o += LANES
    wi = h[:, o:o + LANES] * (N_IDX_HEADS ** -0.5)
    return q, k, v, qi, ki, wi


def _attn_in_sample_kernel(*refs):
    q_o, k_o, v_o, qi_o, ki_o, wi_o = refs[10:]
    q, k, v, qi, ki, wi = _attn_project(*refs[:10])
    q_o[...] = q.astype(BF16)
    k_o[...] = k
    v_o[...] = v
    qi_o[...] = qi.astype(BF16)
    ki_o[...] = ki
    wi_o[...] = wi


def _attn_in_prompt_kernel(*refs):
    k_o, v_o, ki_o, qt_o, qit_o, wt_o, kg_o, vt_o, kib_o = refs[10:]
    q, k, v, qi, ki, wi = _attn_project(*refs[:10])
    k_o[...] = k
    v_o[...] = v
    ki_o[...] = ki
    kib_o[...] = ki[:, :IDX_DIM].astype(BF16)
    for g in range(N_KV_HEADS):
        kg_o[g] = k[:, g * HEAD_DIM:(g + 1) * HEAD_DIM].astype(BF16)
    tail = (lax.broadcasted_iota(I32, (VT_ROWS - HEAD_DIM, BLK), 0) == 0).astype(BF16)
    for r in range(q.shape[0] // BLK):
        rows = slice(r * BLK, (r + 1) * BLK)
        for t in range(N_HEADS * HEAD_DIM // LANES):
            tt = q[rows, t * LANES:(t + 1) * LANES].T.astype(BF16)
            for u in range(LANES // HEAD_DIM):
                h = t * (LANES // HEAD_DIM) + u
                g, hh = h // HEADS_PER_KV, h % HEADS_PER_KV
                qt_o[r, g, :, hh * BLK:(hh + 1) * BLK] = tt[u * HEAD_DIM:(u + 1) * HEAD_DIM]
        for t in range(N_IDX_HEADS * IDX_DIM // LANES):
            tt = qi[rows, t * LANES:(t + 1) * LANES].T.astype(BF16)
            for u in range(LANES // IDX_DIM):
                h = t * (LANES // IDX_DIM) + u
                qit_o[r, :, h * BLK:(h + 1) * BLK] = tt[u * IDX_DIM:(u + 1) * IDX_DIM]
        wt_o[r] = wi[rows].T[0:N_IDX_HEADS]
        for t in range(N_KV_HEADS * HEAD_DIM // LANES):
            tt = v[rows, t * LANES:(t + 1) * LANES].T.astype(BF16)
            for u in range(LANES // HEAD_DIM):
                vt_o[r, t * (LANES // HEAD_DIM) + u] = jnp.concatenate(
                    [tt[u * HEAD_DIM:(u + 1) * HEAD_DIM], tail], axis=0)


def _seg_mats(n_heads, hd):
    col = jnp.arange(n_heads * hd) // hd
    gsum = (col[:, None] == jnp.arange(LANES)[None, :]).astype(BF16)
    return gsum, gsum.T


def _attn_in(x, g, w_pad, q_gain, k_gain, ki_gain, prompt):
    rows, d = x.shape
    n = w_pad.shape[1]
    qd, kd, qid = N_HEADS * HEAD_DIM, N_KV_HEADS * HEAD_DIM, N_IDX_HEADS * IDX_DIM
    gw = HEADS_PER_KV * BLK
    tm = _row_tile(rows, 512)
    gq, eq = _seg_mats(N_HEADS, HEAD_DIM)
    gk, ek = _seg_mats(N_KV_HEADS, HEAD_DIM)
    const = lambda shape: pl.BlockSpec(shape, lambda i: (0,) * len(shape))
    rowb = lambda w: pl.BlockSpec((tm, w), lambda i: (i, 0))
    sds = jax.ShapeDtypeStruct
    if prompt:
        assert tm % BLK == 0
        nbk, tb = rows // BLK, tm // BLK
        blkb = lambda *s: pl.BlockSpec((tb,) + s, lambda i: (i,) + (0,) * len(s))
        kern = _attn_in_prompt_kernel
        out_specs = [rowb(kd), rowb(kd), rowb(LANES), blkb(N_KV_HEADS, HEAD_DIM, gw),
                     blkb(IDX_DIM, N_IDX_HEADS * BLK), blkb(N_IDX_HEADS, BLK),
                     pl.BlockSpec((N_KV_HEADS, tm, HEAD_DIM), lambda i: (0, i, 0)),
                     blkb(N_KV_HEADS, VT_ROWS, BLK),
                     rowb(IDX_DIM)]
        out_shape = [sds((rows, kd), F32), sds((rows, kd), F32), sds((rows, LANES), F32),
                     sds((nbk, N_KV_HEADS, HEAD_DIM, gw), BF16),
                     sds((nbk, IDX_DIM, N_IDX_HEADS * BLK), BF16),
                     sds((nbk, N_IDX_HEADS, BLK), F32),
                     sds((N_KV_HEADS, rows, HEAD_DIM), BF16),
                     sds((nbk, N_KV_HEADS, VT_ROWS, BLK), BF16),
                     sds((rows, IDX_DIM), BF16)]
    else:
        kern = _attn_in_sample_kernel
        out_specs = [rowb(qd), rowb(kd), rowb(kd), rowb(qid), rowb(LANES), rowb(LANES)]
        out_shape = [sds((rows, qd), BF16), sds((rows, kd), F32), sds((rows, kd), F32),
                     sds((rows, qid), BF16), sds((rows, LANES), F32), sds((rows, LANES), F32)]
    return pl.pallas_call(
        kern,
        grid=(rows // tm,),
        in_specs=[rowb(d), const((1, d)), const((d, n)), const((1, qd)), const((1, kd)),
                  const((1, LANES)), const((qd, LANES)), const((LANES, qd)),
                  const((kd, LANES)), const((LANES, kd))],
        out_specs=out_specs,
        out_shape=out_shape,
        compiler_params=_cparams(("parallel",)),
        name="attn_in_prompt" if prompt else "attn_in_sample",
    )(x, g.reshape(1, d), w_pad,
      jnp.tile(q_gain, N_HEADS).reshape(1, qd), jnp.tile(k_gain, N_KV_HEADS).reshape(1, kd),
      jnp.pad(ki_gain, (0, LANES - IDX_DIM)).reshape(1, LANES), gq, eq, gk, ek)


def _norm_proj_kernel(x_ref, g_ref, w_ref, ws_ref, oa_ref, ob_ref, os_ref, xn_ref, *, na):
    j = pl.program_id(1)

    @pl.when(j == 0)
    def _():
        xn = _rms(x_ref[...], g_ref[...]).astype(BF16)
        xn_ref[...] = xn
        os_ref[...] = _dot(xn, ws_ref[...])

    r = _dot(xn_ref[...], w_ref[...])

    @pl.when(j < na)
    def _():
        oa_ref[...] = r

    @pl.when(j >= na)
    def _():
        ob_ref[...] = r


def _norm_proj(x, g, w_main, w_side, n_a):
    rows, d = x.shape
    n = w_main.shape[1]
    tm = _row_tile(rows, 1024)
    tn = math.gcd(math.gcd(n_a, n - n_a), 1024)
    na = n_a // tn
    return pl.pallas_call(
        functools.partial(_norm_proj_kernel, na=na),
        grid=(rows // tm, n // tn),
        in_specs=[
            pl.BlockSpec((tm, d), lambda i, j: (i, 0)),
            pl.BlockSpec((1, d), lambda i, j: (0, 0)),
            pl.BlockSpec((d, tn), lambda i, j: (0, j)),
            pl.BlockSpec((d, LANES), lambda i, j: (0, 0)),
        ],
        out_specs=[pl.BlockSpec((tm, tn), lambda i, j: (i, jnp.minimum(j, na - 1))),
                   pl.BlockSpec((tm, tn), lambda i, j: (i, jnp.maximum(j - na, 0))),
                   pl.BlockSpec((tm, LANES), lambda i, j: (i, 0))],
        out_shape=[jax.ShapeDtypeStruct((rows, n_a), F32),
                   jax.ShapeDtypeStruct((rows, n - n_a), F32),
                   jax.ShapeDtypeStruct((rows, LANES), F32)],
        scratch_shapes=[pltpu.VMEM((tm, d), BF16)],
        compiler_params=_cparams(("parallel", "arbitrary")),
        name="norm_proj",
    )(x, g.reshape(1, d), w_main, w_side)


def _out_proj_kernel(x_ref, y_ref, w_ref, o_ref):
    o_ref[...] = x_ref[...] + _dot(y_ref[...], w_ref[...])


def _out_proj(x, y, w):
    rows, d = x.shape
    k = y.shape[1]
    tm = _row_tile(rows, 512)
    return pl.pallas_call(
        _out_proj_kernel,
        grid=(rows // tm,),
        in_specs=[pl.BlockSpec((tm, d), lambda i: (i, 0)),
                  pl.BlockSpec((tm, k), lambda i: (i, 0)),
                  pl.BlockSpec((k, d), lambda i: (0, 0))],
        out_specs=pl.BlockSpec((tm, d), lambda i: (i, 0)),
        out_shape=jax.ShapeDtypeStruct((rows, d), F32),
        compiler_params=_cparams(("parallel",)),
        name="out_proj",
    )(x, y, w)


def _sort_key(score):
    bits = pltpu.bitcast(score, I32)
    return bits ^ ((bits >> 31) & 0x7FFFFFFF)


def _tile_fold(x, op):
    r = x[0:SUBLANES]
    for t in range(1, x.shape[0] // SUBLANES):
        r = op(r, x[t * SUBLANES:(t + 1) * SUBLANES])
    return r


def _dsa_pair_kernel(smax_ref, qita_ref, qitb_ref, wta_ref, wtb_ref, qta_ref, qtb_ref,
                     ki_ref, k_ref, vt_ref, oa_ref, ob_ref,
                     key_ref, acc_ref, qit_s, wt_s, qt_s, *, pad, topk, nb):
    j = pl.program_id(1)
    n_a = j + 1
    blocks = (j, nb - 1 - j)
    ntrip = nb + 1
    ncount = key_ref.shape[0]
    last = ki_ref.shape[0] - 1
    row = lax.broadcasted_iota(I32, (BLK, BLK), 0)
    lane = lax.broadcasted_iota(I32, (BLK, BLK), 1)
    gw = HEADS_PER_KV * BLK

    qit_s[0], qit_s[1] = qita_ref[...], qitb_ref[...]
    wt_s[0], wt_s[1] = wta_ref[...], wtb_ref[...]
    qt_s[0], qt_s[1] = qta_ref[...], qtb_ref[...]

    def trip(t):
        blk = (t >= n_a).astype(I32)
        c = t - blk * n_a
        return blk, c, jnp.minimum(c, last)

    def pick(blk, a, b):
        if isinstance(a, tuple):
            return tuple(pick(blk, x, y) for x, y in zip(a, b))
        return jnp.where(blk == 1, b, a)

    for t in range(ncount):
        if t >= ntrip:
            key_ref[t] = jnp.full((BLK, BLK), INT_MIN, I32)
            continue
        blk, c, cr = trip(t)
        dots = _dot(ki_ref[cr], qit_s[blk])
        sc = jnp.zeros((BLK, BLK), F32)
        for h in range(N_IDX_HEADS):
            sc = sc + wt_s[blk, h:h + 1, :] * jnp.maximum(dots[:, h * BLK:(h + 1) * BLK], 0.0)
        s_pos = c * BLK + row
        t_pos = pick(blk, blocks[0], blocks[1]) * BLK + lane
        valid = (s_pos <= t_pos) & (s_pos >= pad)
        key_ref[t] = jnp.where(valid, _sort_key(sc), INT_MIN)

    def count(pred, arg):
        cnt = [jnp.zeros((SUBLANES, BLK), I32), jnp.zeros((SUBLANES, BLK), I32)]
        for t in range(ncount):
            blk, c, _ = trip(t)
            hit = jnp.where(pred(key_ref[t], pick(blk, arg[0], arg[1]), c * BLK + row), 1, 0)
            part = _tile_fold(hit, jnp.add)
            cnt[0] = cnt[0] + jnp.where(blk == 0, part, 0)
            cnt[1] = cnt[1] + jnp.where(blk == 1, part, 0)
        return tuple(jnp.sum(x, axis=0, keepdims=True) for x in cnt)

    zero = jnp.zeros((1, BLK), I32)
    c0 = count(lambda k, a, s: k >= a, (zero, zero))
    thr0 = tuple(jnp.where(x >= topk, 0, INT_MIN).astype(I32) for x in c0)

    def bit_step(it, st):
        thr, n_ge = st[:2], st[2:]
        cand = tuple(x + (jnp.int32(1) << (30 - it)) for x in thr)
        cnt = count(lambda k, a, s: k >= a, cand)
        ok = tuple(x >= topk for x in cnt)
        return (tuple(jnp.where(ok[b], cand[b], thr[b]) for b in range(2))
                + tuple(jnp.where(ok[b], cnt[b], n_ge[b]) for b in range(2)))

    st = lax.fori_loop(0, 31, bit_step, thr0 + c0)
    thr, n_ge = st[:2], st[2:]
    tied = ((n_ge[0] > topk) & (thr[0] > INT_MIN)) | ((n_ge[1] > topk) & (thr[1] > INT_MIN))
    any_tied = jnp.max(tied.astype(I32))
    nbits = (nb * BLK).bit_length()

    def idx_search():
        n_gt = count(lambda k, a, s: k > a, thr)
        need = tuple(topk - x for x in n_gt)

        def step(it, lo):
            cand = tuple(x + (jnp.int32(1) << (nbits - 1 - it)) for x in lo)
            below = count(lambda k, a, s: (k == a[0]) & (s < a[1]),
                          ((thr[0], cand[0]), (thr[1], cand[1])))
            return tuple(jnp.where(below[b] < need[b], cand[b], lo[b]) for b in range(2))
        return lax.fori_loop(0, nbits, step, (zero, zero))

    big = jnp.full((1, BLK), 2 ** 30, I32)
    jcut = lax.cond(any_tied > 0, idx_search, lambda: (big, big))

    def logits(t):
        blk, c, cr = trip(t)
        k = key_ref[t]
        th = pick(blk, thr[0], thr[1])
        sel = (k > th) | ((k == th) & (c * BLK + row <= pick(blk, jcut[0], jcut[1])))
        bias = jnp.where(sel & (k > INT_MIN), 0.0, NEG)
        bb = jnp.concatenate([bias] * HEADS_PER_KV, axis=1)
        return blk, cr, [_dot(k_ref[g, cr], qt_s[blk, g]) + bb for g in range(N_KV_HEADS)]

    def attend(shift):
        acc_ref[...] = jnp.zeros_like(acc_ref)
        for t in range(ntrip):
            blk, cr, s = logits(t)
            for g in range(N_KV_HEADS):
                x = s[g] if shift is None else s[g] - pick(blk, shift[0][g], shift[1][g])
                acc_ref[blk, g] += _dot(vt_ref[cr, g], jnp.exp(x).astype(BF16))

    safe = smax_ref[0] <= SAFE_LOGIT

    @pl.when(safe)
    def _():
        attend(None)

    @pl.when(jnp.logical_not(safe))
    def _():
        m = [[jnp.full((SUBLANES, gw), NEG, F32) for _ in range(N_KV_HEADS)] for _ in range(2)]
        for t in range(ntrip):
            blk, _, s = logits(t)
            for g in range(N_KV_HEADS):
                part = _tile_fold(s[g], jnp.maximum)
                m[0][g] = jnp.maximum(m[0][g], jnp.where(blk == 0, part, NEG))
                m[1][g] = jnp.maximum(m[1][g], jnp.where(blk == 1, part, NEG))
        attend([[jnp.max(x, axis=0, keepdims=True) for x in mb] for mb in m])

    for b, o_ref in enumerate((oa_ref, ob_ref)):
        q_row = blocks[b] * BLK + lax.broadcasted_iota(I32, (BLK, LANES), 0)
        for g in range(N_KV_HEADS):
            a = acc_ref[b, g]
            res = a[0:HEAD_DIM] * (1.0 / a[HEAD_DIM:HEAD_DIM + 1])
            for t in range(HEADS_PER_KV // 2):
                two = jnp.concatenate([res[:, (2 * t + u) * BLK:(2 * t + u + 1) * BLK] for u in range(2)],
                                      axis=0)
                two = jnp.where(q_row >= pad, two.T, 0.0)
                lo = (g * HEADS_PER_KV + 2 * t) * HEAD_DIM
                o_ref[:, lo:lo + LANES] = two.astype(BF16)


def _dsa_prompt_pairs(smax, qit, wt, kib, qt, kg, vt, pad, topk):
    bsz, nb = qit.shape[:2]
    steps = (nb + 1) // 2
    gw = HEADS_PER_KV * BLK
    qd = N_HEADS * HEAD_DIM
    ncount = nb + 1
    kern = functools.partial(_dsa_pair_kernel, pad=pad, topk=topk, nb=nb)
    first = lambda *s: pl.BlockSpec((None, None) + s, lambda b, j: (b, j) + (0,) * len(s))
    second = lambda *s: pl.BlockSpec((None, None) + s, lambda b, j: (b, nb - 1 - j) + (0,) * len(s))
    oa, ob = pl.pallas_call(
        kern,
        grid=(bsz, steps),
        in_specs=[
            pl.BlockSpec(memory_space=pltpu.SMEM),
            first(IDX_DIM, N_IDX_HEADS * BLK), second(IDX_DIM, N_IDX_HEADS * BLK),
            first(N_IDX_HEADS, BLK), second(N_IDX_HEADS, BLK),
            first(N_KV_HEADS, HEAD_DIM, gw), second(N_KV_HEADS, HEAD_DIM, gw),
            pl.BlockSpec((None, nb, BLK, IDX_DIM), lambda b, j: (b, 0, 0, 0)),
            pl.BlockSpec((N_KV_HEADS, None, nb, BLK, HEAD_DIM), lambda b, j: (0, b, 0, 0, 0)),
            pl.BlockSpec((None, nb, N_KV_HEADS, VT_ROWS, BLK), lambda b, j: (b, 0, 0, 0, 0)),
        ],
        out_specs=[pl.BlockSpec((None, BLK, qd), lambda b, j: (b, j, 0)),
                   pl.BlockSpec((None, BLK, qd), lambda b, j: (b, steps - 1 - j, 0))],
        out_shape=[jax.ShapeDtypeStruct((bsz, steps * BLK, qd), BF16)] * 2,
        scratch_shapes=[pltpu.VMEM((ncount, BLK, BLK), I32),
                        pltpu.VMEM((2, N_KV_HEADS, VT_ROWS, gw), F32),
                        pltpu.VMEM((2, IDX_DIM, N_IDX_HEADS * BLK), BF16),
                        pltpu.VMEM((2, N_IDX_HEADS, BLK), F32),
                        pltpu.VMEM((2, N_KV_HEADS, HEAD_DIM, gw), BF16)],
        compiler_params=_cparams(("parallel", "arbitrary")),
        name="dsa_prompt",
    )(smax, qit, qit, wt, wt, qt, qt, kib, kg, vt)
    return jnp.concatenate([oa[:, :(nb - steps) * BLK], ob], axis=1)


def _smp_scores_kernel(pt_ref, qi_ref, w_ref, *refs, pg):
    page_refs, o_ref = refs[:pg], refs[pg]
    qi = qi_ref[...]
    w = w_ref[...]
    for p in range(pg):
        d = _dot(qi, page_refs[p][...].astype(BF16))
        o_ref[p:p + 1, :] = jnp.sum(w * jnp.maximum(d, 0.0), axis=0, keepdims=True)


def _smp_scores(page_table, qi, wi, kidx_t, pg):
    db, npg = page_table.shape
    kern = functools.partial(_smp_scores_kernel, pg=pg)
    page_spec = lambda p: pl.BlockSpec(
        (None, IDX_DIM, BLK), lambda b, j, pt: (pt[b * npg + j * pg + p], 0, 0))
    return pl.pallas_call(
        kern,
        grid_spec=pltpu.PrefetchScalarGridSpec(
            num_scalar_prefetch=1,
            grid=(db, npg // pg),
            in_specs=[pl.BlockSpec((None, N_IDX_HEADS, IDX_DIM), lambda b, j, pt: (b, 0, 0)),
                      pl.BlockSpec((None, N_IDX_HEADS, 1), lambda b, j, pt: (b, 0, 0))]
                     + [page_spec(p) for p in range(pg)],
            out_specs=pl.BlockSpec((None, pg, BLK), lambda b, j, pt: (b, j, 0)),
        ),
        out_shape=jax.ShapeDtypeStruct((db, npg, BLK), F32),
        compiler_params=_cparams(("parallel", "arbitrary")),
        name="sample_scores",
    )(page_table.reshape(-1), qi, wi, *([kidx_t] * pg))


def _smp_select_kernel(sc_ref, qi_ref, w_ref, kin_ref, bias_ref, nb_ref, *, topk, past):
    db, npg, _ = sc_ref.shape
    key = _sort_key(sc_ref[...])
    qi = qi_ref[...].astype(F32)
    kn = kin_ref[...].astype(BF16).astype(F32)
    d = jnp.sum(qi * kn, axis=-1, keepdims=True)
    s_new = jnp.sum(w_ref[...] * jnp.maximum(d, 0.0), axis=1, keepdims=True)
    key_new = _sort_key(s_new)
    pos = (lax.broadcasted_iota(I32, (db, npg, BLK), 1) * BLK
           + lax.broadcasted_iota(I32, (db, npg, BLK), 2))

    def count(pred_past, pred_new):
        c = jnp.sum(jnp.where(pred_past, 1, 0), axis=2, keepdims=True)
        return jnp.sum(c, axis=1, keepdims=True) + jnp.where(pred_new, 1, 0)

    thr = jnp.where(count(key >= 0, key_new >= 0) >= topk, 0, INT_MIN).astype(I32)

    def bit_step(it, thr):
        cand = thr + (jnp.int32(1) << (30 - it))
        return jnp.where(count(key >= cand, key_new >= cand) >= topk, cand, thr)

    thr = lax.fori_loop(0, 31, bit_step, thr)
    need = topk - count(key > thr, key_new > thr)
    nbits = max(1, past.bit_length())

    def step(it, lo):
        cand = lo + (jnp.int32(1) << (nbits - 1 - it))
        below = count((key == thr) & (pos < cand), (key_new == thr) & (past < cand))
        return jnp.where(below < need, cand, lo)

    jcut = lax.fori_loop(0, nbits, step, jnp.zeros((db, 1, 1), I32))
    sel = (key > thr) | ((key == thr) & (pos <= jcut))
    bias_ref[...] = jnp.where(sel, 0.0, NEG)
    sel_new = (key_new > thr) | ((key_new == thr) & (past <= jcut))
    nb_ref[...] = jnp.broadcast_to(jnp.where(sel_new, 0.0, NEG), nb_ref.shape)


def _smp_select(scores, qi, wi, ki_new, topk):
    db, npg, _ = scores.shape
    kern = functools.partial(_smp_select_kernel, topk=topk, past=npg * BLK)
    full = lambda *s: pl.BlockSpec(s, lambda i: (0,) * len(s))
    return pl.pallas_call(
        kern,
        grid=(1,),
        in_specs=[full(db, npg, BLK), full(db, N_IDX_HEADS, IDX_DIM), full(db, N_IDX_HEADS, 1),
                  full(db, 1, IDX_DIM)],
        out_specs=[full(db, npg, BLK), full(db, SUBLANES, LANES)],
        out_shape=[jax.ShapeDtypeStruct((db, npg, BLK), F32),
                   jax.ShapeDtypeStruct((db, SUBLANES, LANES), F32)],
        compiler_params=_cparams(("arbitrary",)),
        name="sample_select",
    )(scores, qi, wi, ki_new)


def _smp_attend_kernel(pt_ref, q_ref, bias_ref, nb_ref, kn_ref, vn_ref, *refs, pg):
    k_refs, v_refs = refs[:pg], refs[pg:2 * pg]
    o_ref, m_ref, l_ref, acc_ref = refs[2 * pg:]
    j = pl.program_id(1)
    kd = N_KV_HEADS * HEAD_DIM

    @pl.when(j == 0)
    def _():
        m_ref[...] = jnp.full_like(m_ref, NEG)
        l_ref[...] = jnp.zeros_like(l_ref)
        acc_ref[...] = jnp.zeros_like(acc_ref)

    q = q_ref[...]
    s = [_dot(q, k_refs[p][...].astype(BF16)) + bias_ref[p:p + 1, :] for p in range(pg)]
    m_old = m_ref[...]
    m_new = m_old
    for p in range(pg):
        m_new = jnp.maximum(m_new, jnp.max(s[p], axis=-1, keepdims=True))
    alpha = jnp.exp(m_old - m_new)
    l = l_ref[...] * alpha
    acc = acc_ref[...] * alpha
    for p in range(pg):
        e = jnp.exp(s[p] - m_new)
        l = l + jnp.sum(e, axis=-1, keepdims=True)
        acc = acc + _dot_t1(e.astype(BF16), v_refs[p][...].astype(BF16))
    m_ref[...] = m_new
    l_ref[...] = l
    acc_ref[...] = acc

    @pl.when(j == pl.num_programs(1) - 1)
    def _():
        qf = q.astype(F32)
        kn = kn_ref[...].astype(BF16).astype(F32)
        vn = vn_ref[...].astype(BF16).astype(F32)
        s_new = jnp.sum(qf * kn, axis=-1, keepdims=True) + nb_ref[0:1, 0:1]
        m_fin = jnp.maximum(m_new, s_new)
        a2 = jnp.exp(m_new - m_fin)
        e_new = jnp.exp(s_new - m_fin)
        l_fin = l * a2 + e_new
        acc_fin = acc * a2 + e_new.astype(BF16).astype(F32) * vn
        res = acc_fin / l_fin
        hgrp = lax.broadcasted_iota(I32, (N_HEADS, HEAD_DIM), 0) // HEADS_PER_KV
        out = jnp.zeros((N_HEADS, HEAD_DIM), F32)
        for g in range(N_KV_HEADS):
            out = out + jnp.where(hgrp == g, res[:, g * HEAD_DIM:(g + 1) * HEAD_DIM], 0.0)
        o_ref[...] = out


def _smp_attend(page_table, q_bd, bias, nbias, k_new, v_new, k_t, v_t, pg):
    db, npg = page_table.shape
    kd = N_KV_HEADS * HEAD_DIM
    kern = functools.partial(_smp_attend_kernel, pg=pg)
    page_spec = lambda p: pl.BlockSpec(
        (None, kd, BLK), lambda b, j, pt: (pt[b * npg + j * pg + p], 0, 0))
    per_b = lambda shape: pl.BlockSpec((None,) + shape, lambda b, j, pt: (b, 0, 0))
    return pl.pallas_call(
        kern,
        grid_spec=pltpu.PrefetchScalarGridSpec(
            num_scalar_prefetch=1,
            grid=(db, npg // pg),
            in_specs=[per_b((N_HEADS, kd)),
                      pl.BlockSpec((None, pg, BLK), lambda b, j, pt: (b, j, 0)),
                      per_b((SUBLANES, LANES)), per_b((1, kd)), per_b((1, kd))]
                     + [page_spec(p) for p in range(pg)] * 2,
            out_specs=per_b((N_HEADS, HEAD_DIM)),
            scratch_shapes=[pltpu.VMEM((N_HEADS, 1), F32), pltpu.VMEM((N_HEADS, 1), F32),
                            pltpu.VMEM((N_HEADS, kd), F32)],
        ),
        out_shape=jax.ShapeDtypeStruct((db, N_HEADS, HEAD_DIM), F32),
        compiler_params=_cparams(("parallel", "arbitrary")),
        name="sample_attend",
    )(page_table.reshape(-1), q_bd, bias, nbias, k_new, v_new,
      *([k_t] * pg), *([v_t] * pg))


def _ssd_prompt_kernel(x_ref, g_ref, wz_ref, wx_ref, wdt_ref, wo_ref,
                       cw_ref, cb_ref, dtb_ref, a_ref,
                       dtbc_ref, ac_ref, dsk_ref, gn_ref, ltri_ref, utri_ref,
                       o_ref, st_ref, cv_ref, xpad_ref, h_ref, yb_ref, xt_ref, yt_ref, z_ref,
                       *, pad, n_heads):
    c = pl.program_id(1)
    d_in = n_heads * SSD_HEAD_DIM
    gn = SSD_GROUPS * D_STATE
    hpg = n_heads // SSD_GROUPS

    @pl.when(c == 0)
    def _():
        xpad_ref[0:SUBLANES, :] = jnp.zeros((SUBLANES, xpad_ref.shape[1]), F32)
        h_ref[...] = jnp.zeros_like(h_ref)

    xn = _rms(x_ref[...], g_ref[...]).astype(BF16)
    z_ref[...] = _dot(xn, wz_ref[...])
    dt_raw = _dot(xn, wdt_ref[...])
    dt_raw_t = dt_raw.T[0:n_heads]

    xpad_ref[SUBLANES:, :] = _dot(xn, wx_ref[...])
    conv = cb_ref[...] + cw_ref[CONV_W - 1:CONV_W, :] * xpad_ref[SUBLANES:, :]
    for j in range(CONV_W - 1):
        sh = CONV_W - 1 - j
        conv = conv + cw_ref[j:j + 1, :] * xpad_ref[SUBLANES - sh:SUBLANES - sh + BLK, :]
    xpad_ref[0:SUBLANES, :] = xpad_ref[BLK:BLK + SUBLANES, :]
    xbc = _silu(conv)

    live = (c > 0) | (lax.broadcasted_iota(I32, (BLK, LANES), 0) >= pad)
    dt = jnp.where(live, _softplus(dt_raw + dtb_ref[...]), 0.0)
    acum = _dot_sel(dt * a_ref[...], ltri_ref[...], fn=lambda x, m: _dot(m, x))
    live_t = (c > 0) | (lax.broadcasted_iota(I32, (n_heads, BLK), 1) >= pad)
    dtt = jnp.where(live_t, _softplus(dt_raw_t + dtbc_ref[...]), 0.0)
    acum_t = _dot_sel(dtt * ac_ref[...], utri_ref[...])
    for t in range(d_in // LANES):
        xt_ref[t * LANES:(t + 1) * LANES, :] = xbc[:, t * LANES:(t + 1) * LANES].T
    a_last = acum_t[:, BLK - 1:BLK]
    ecol_t = jnp.exp(acum_t)
    decs_t = jnp.exp(a_last - acum_t)
    ea_last = jnp.exp(a_last)
    causal_t = (lax.broadcasted_iota(I32, (BLK, BLK), 0) <= lax.broadcasted_iota(I32, (BLK, BLK), 1))
    hp = SSD_HEAD_DIM

    for g in range(SSD_GROUPS):
        bm = xbc[:, d_in + g * D_STATE:d_in + (g + 1) * D_STATE].astype(BF16)
        ct = xbc[:, d_in + gn + g * D_STATE:d_in + gn + (g + 1) * D_STATE].T.astype(BF16)
        cb_t = _dot(bm, ct)
        hprev = h_ref[g * hpg:(g + 1) * hpg].reshape(hpg * hp, D_STATE)
        y_off = _dot(hprev.astype(BF16), ct)
        ws = []
        for hh in range(hpg):
            h = g * hpg + hh
            rows = slice(h * hp, (h + 1) * hp)
            xh = xt_ref[rows, :]
            xdt = xh * dtt[h:h + 1, :]
            decay_t = jnp.exp(jnp.where(causal_t, acum_t[h:h + 1, :] - acum[:, h:h + 1], NEG))
            y = _dot(xdt.astype(BF16), (cb_t * decay_t).astype(BF16))
            y = y + ecol_t[h:h + 1, :] * y_off[hh * hp:(hh + 1) * hp]
            yt_ref[rows, :] = y + dsk_ref[0:1, h:h + 1] * xh
            ws.append((xdt * decs_t[h:h + 1, :]).astype(BF16))
        upd = _dot(jnp.concatenate(ws, axis=0), bm)
        for hh in range(hpg):
            h = g * hpg + hh
            h_ref[h] = ea_last[h:h + 1, :] * hprev[hh * hp:(hh + 1) * hp] + upd[hh * hp:(hh + 1) * hp]

    for t in range(d_in // LANES):
        yb_ref[:, t * LANES:(t + 1) * LANES] = yt_ref[t * LANES:(t + 1) * LANES, :].T

    yg = yb_ref[...] * _silu(z_ref[...])
    gsz = d_in // SSD_GROUPS
    yn = jnp.concatenate(
        [_rms(yg[:, g * gsz:(g + 1) * gsz], gn_ref[:, g * gsz:(g + 1) * gsz]).astype(BF16)
         for g in range(SSD_GROUPS)], axis=1)
    o_ref[...] = x_ref[...] + _dot(yn, wo_ref[...])

    @pl.when(c == pl.num_programs(1) - 1)
    def _():
        st_ref[...] = h_ref[...]
        cv_ref[...] = xpad_ref[0:SUBLANES, :]


def _ssd_prompt(x, g, w_z, w_xbc, w_dt, w_out, conv_w, conv_b, dt_bias, a_log, d_skip, gate_norm,
                bsz, nb, pad, n_heads):
    d = x.shape[1]
    d_in = n_heads * SSD_HEAD_DIM
    cdim = d_in + 2 * SSD_GROUPS * D_STATE
    assert n_heads <= LANES
    hp = LANES - n_heads
    a = -jnp.exp(a_log.astype(F32))
    ltri = jnp.tril(jnp.ones((BLK, BLK), F32)).astype(BF16)
    kern = functools.partial(_ssd_prompt_kernel, pad=pad, n_heads=n_heads)
    const = lambda shape: pl.BlockSpec(shape, lambda b, c: (0,) * len(shape),
                                       pipeline_mode=pl.Buffered(1))
    return pl.pallas_call(
        kern,
        grid=(bsz, nb),
        in_specs=[
            pl.BlockSpec((BLK, d), lambda b, c: (b * nb + c, 0)),
            const((1, d)), const((d, d_in)), const((d, cdim)), const((d, LANES)), const((d_in, d)),
            const((CONV_W, cdim)), const((1, cdim)), const((1, LANES)), const((1, LANES)),
            const((n_heads, 1)), const((n_heads, 1)), const((1, LANES)), const((1, d_in)),
            const((BLK, BLK)), const((BLK, BLK)),
        ],
        out_specs=[pl.BlockSpec((BLK, d), lambda b, c: (b * nb + c, 0)),
                   pl.BlockSpec((None, n_heads, SSD_HEAD_DIM, D_STATE), lambda b, c: (b, 0, 0, 0)),
                   pl.BlockSpec((None, SUBLANES, cdim), lambda b, c: (b, 0, 0))],
        out_shape=[jax.ShapeDtypeStruct((bsz * nb * BLK, d), F32),
                   jax.ShapeDtypeStruct((bsz, n_heads, SSD_HEAD_DIM, D_STATE), F32),
                   jax.ShapeDtypeStruct((bsz, SUBLANES, cdim), F32)],
        scratch_shapes=[pltpu.VMEM((BLK + SUBLANES, cdim), F32),
                        pltpu.VMEM((n_heads, SSD_HEAD_DIM, D_STATE), F32),
                        pltpu.VMEM((BLK, d_in), F32),
                        pltpu.VMEM((d_in, BLK), F32),
                        pltpu.VMEM((d_in, BLK), F32),
                        pltpu.VMEM((BLK, d_in), F32)],
        compiler_params=_cparams(("parallel", "arbitrary")),
        name="ssd_prompt",
    )(x, g.reshape(1, d), w_z, w_xbc, w_dt, w_out, conv_w, conv_b.reshape(1, cdim),
      jnp.pad(dt_bias, (0, hp)).reshape(1, LANES), jnp.pad(a, (0, hp)).reshape(1, LANES),
      dt_bias.reshape(n_heads, 1), a.reshape(n_heads, 1),
      jnp.pad(d_skip, (0, hp)).reshape(1, LANES), gate_norm.reshape(1, d_in), ltri, ltri.T)


def _ssd_sample_kernel(z_ref, xbc_ref, dt_ref, cst_ref, h0_ref, cw_ref, cb_ref, dtb_ref, a_ref,
                       dsk_ref, gn_ref, exp_ref, y_ref, h_ref, *, n_heads):
    d_in = n_heads * SSD_HEAD_DIM
    gn = SSD_GROUPS * D_STATE
    gsz = d_in // SSD_GROUPS
    hpg = n_heads // SSD_GROUPS
    conv = cb_ref[...] + cw_ref[CONV_W - 1:CONV_W, :] * xbc_ref[...]
    for j in range(CONV_W - 1):
        conv = conv + cw_ref[j:j + 1, :] * cst_ref[j:j + 1, :]
    xbc = _silu(conv)
    dt = _softplus(dt_ref[...] + dtb_ref[...])
    pad8 = lambda r: jnp.concatenate([r, jnp.zeros((SUBLANES - 1, r.shape[1]), F32)], axis=0)
    dt_ch = _dot_sel(pad8(dt), exp_ref[...])[0:1]
    da_ch = jnp.exp(_dot_sel(pad8(dt * a_ref[...]), exp_ref[...])[0:1])
    dsk_ch = _dot_sel(pad8(dsk_ref[...]), exp_ref[...])[0:1]
    xh = xbc[:, :d_in]
    xdt = xh * dt_ch
    ones = jnp.ones((SUBLANES, D_STATE), BF16)
    outs = []
    for g in range(SSD_GROUPS):
        sl = slice(g * gsz, (g + 1) * gsz)
        bm = xbc[:, d_in + g * D_STATE:d_in + (g + 1) * D_STATE]
        cm = xbc[:, d_in + gn + g * D_STATE:d_in + gn + (g + 1) * D_STATE]
        da_col = _dot_sel(pad8(da_ch[:, sl]), ones, fn=_dot_t0)
        xdt_col = _dot_sel(pad8(xdt[:, sl]), ones, fn=_dot_t0)
        h0 = h0_ref[g * hpg:(g + 1) * hpg].reshape(gsz, D_STATE)
        hn = da_col * h0 + xdt_col.astype(BF16).astype(F32) * bm.astype(BF16).astype(F32)
        h_ref[g * hpg:(g + 1) * hpg] = hn.reshape(hpg, SSD_HEAD_DIM, D_STATE)
        y_col = jnp.sum(hn * cm, axis=-1, keepdims=True)
        outs.append(y_col)
    y_cols = jnp.concatenate(outs, axis=0)
    rows = []
    eye = (lax.broadcasted_iota(I32, (LANES, LANES), 0)
           == lax.broadcasted_iota(I32, (LANES, LANES), 1)).astype(F32)
    for t in range(d_in // LANES):
        blk = y_cols[t * LANES:(t + 1) * LANES]
        rows.append(jnp.sum(blk * eye, axis=0, keepdims=True))
    y = jnp.concatenate(rows, axis=1) + dsk_ch * xh
    y = y * _silu(z_ref[...])
    for g in range(SSD_GROUPS):
        sl = slice(g * gsz, (g + 1) * gsz)
        y_ref[:, sl] = _rms(y[:, sl], gn_ref[:, sl]).astype(BF16)


def _ssd_sample(z, xbc_raw, dt_raw, conv_state, h0, conv_w, conv_b, dt_bias, a_log, d_skip,
                gate_norm, n_heads):
    db = z.shape[0]
    d_in = n_heads * SSD_HEAD_DIM
    cdim = d_in + 2 * SSD_GROUPS * D_STATE
    hp = LANES - n_heads
    a = -jnp.exp(a_log.astype(F32))
    expand = (jnp.arange(LANES)[:, None] == (jnp.arange(d_in) // SSD_HEAD_DIM)[None, :]).astype(BF16)
    kern = functools.partial(_ssd_sample_kernel, n_heads=n_heads)
    const = lambda shape: pl.BlockSpec(shape, lambda b: (0,) * len(shape))
    per_b = lambda shape: pl.BlockSpec((None,) + shape, lambda b: (b,) + (0,) * len(shape))
    y, h = pl.pallas_call(
        kern,
        grid=(db,),
        in_specs=[per_b((1, d_in)), per_b((1, cdim)), per_b((1, LANES)), per_b((CONV_W - 1, cdim)),
                  per_b((n_heads, SSD_HEAD_DIM, D_STATE)),
                  const((CONV_W, cdim)), const((1, cdim)), const((1, LANES)), const((1, LANES)),
                  const((1, LANES)), const((1, d_in)), const((LANES, d_in))],
        out_specs=[per_b((1, d_in)), per_b((n_heads, SSD_HEAD_DIM, D_STATE))],
        out_shape=[jax.ShapeDtypeStruct((db, 1, d_in), BF16),
                   jax.ShapeDtypeStruct((db, n_heads, SSD_HEAD_DIM, D_STATE), F32)],
        compiler_params=_cparams(("parallel",)),
        name="ssd_sample",
    )(z.reshape(db, 1, d_in), xbc_raw.reshape(db, 1, cdim), dt_raw.reshape(db, 1, LANES),
      conv_state, h0, conv_w, conv_b.reshape(1, cdim),
      jnp.pad(dt_bias, (0, hp)).reshape(1, LANES), jnp.pad(a, (0, hp)).reshape(1, LANES),
      jnp.pad(d_skip, (0, hp)).reshape(1, LANES), gate_norm.reshape(1, d_in), expand)
    return y.reshape(db, d_in), h


def kernel(x_prompt, x_sample, cache_k, cache_v, cache_kidx, page_table, state_ssm, state_conv,
           meta_tokens, norm_ffn_a, w_ffn_a_in, w_ffn_a_out, norm_mix, norm_ffn_b, w_ffn_b_in,
           w_ffn_b_out, w_attn_in, q_norm, k_norm, kidx_norm, w_attn_out,
           w_ssd_in, conv_w, conv_b, dt_bias, a_log, d_skip, gate_norm, w_ssd_out):
    bsz, seq, d = x_prompt.shape
    db = x_sample.shape[0]
    assert x_sample.shape[1] == 1
    t_real = N_META + seq
    nb = -(-t_real // BLK)
    t_pad = nb * BLK
    pad = t_pad - t_real
    npg = page_table.shape[1]
    past = npg * BLK
    topk_p = min(TOPK_MAX, seq // 4)
    topk_s = min(TOPK_MAX, (past + 1) // 4)
    qd, kd, qid = N_HEADS * HEAD_DIM, N_KV_HEADS * HEAD_DIM, N_IDX_HEADS * IDX_DIM
    d_in = w_ssd_out.shape[1]
    n_heads = d_in // SSD_HEAD_DIM
    cdim = d_in + 2 * SSD_GROUPS * D_STATE
    depth = norm_mix.shape[0]

    meta = jnp.broadcast_to(meta_tokens.astype(F32)[None], (bsz, N_META, d))
    xp = jnp.concatenate([jnp.zeros((bsz, pad, d), F32), meta, x_prompt], axis=1)
    xp = xp.reshape(bsz * t_pad, d)
    xs = x_sample.reshape(db, d)

    outs = {k: [] for k in ("kp", "vp", "kip", "ks", "vs", "kis", "hp", "cp", "hs", "cs")}
    for i in range(depth):
        wts = _ffn_weights(w_ffn_a_in[i], w_ffn_a_out[i])
        xp = _ffn(xp, norm_ffn_a[i], wts)
        xs = _ffn(xs, norm_ffn_a[i], wts)
        j = i // 2
        if i % 2 == 0:
            w = w_attn_in[j]
            o1, o2 = qd + 2 * kd + qid, qd + 2 * kd + qid + IDX_DIM
            w_pad = jnp.concatenate(
                [w[:, :o1], jnp.pad(w[:, o1:o2], ((0, 0), (0, LANES - IDX_DIM))),
                 jnp.pad(w[:, o2:], ((0, 0), (0, LANES - N_IDX_HEADS)))], axis=1).astype(BF16)
            w_out = w_attn_out[j].astype(BF16)

            k, v, ki, qt, qit, wt, kg, vt, kib = _attn_in(
                xp, norm_mix[i], w_pad, q_norm[j], k_norm[j], kidx_norm[j], prompt=True)
            gw = HEADS_PER_KV * BLK
            smax = (1.02 * math.sqrt(HEAD_DIM) * jnp.max(jnp.abs(q_norm[j]))
                    * jnp.max(jnp.abs(k_norm[j]))).reshape(1).astype(F32)
            o = _dsa_prompt_pairs(smax,
                            qit.reshape(bsz, nb, IDX_DIM, N_IDX_HEADS * BLK),
                            wt.reshape(bsz, nb, N_IDX_HEADS, BLK),
                            kib.reshape(bsz, nb, BLK, IDX_DIM),
                            qt.reshape(bsz, nb, N_KV_HEADS, HEAD_DIM, gw),
                            kg.reshape(N_KV_HEADS, bsz, nb, BLK, HEAD_DIM),
                            vt.reshape(bsz, nb, N_KV_HEADS, VT_ROWS, BLK),
                            pad, topk_p)
            xp = _out_proj(xp, o.reshape(bsz * t_pad, qd), w_out)
            outs["kp"].append(k.reshape(bsz, t_pad, N_KV_HEADS, HEAD_DIM)[:, pad:])
            outs["vp"].append(v.reshape(bsz, t_pad, N_KV_HEADS, HEAD_DIM)[:, pad:])
            outs["kip"].append(ki[:, :IDX_DIM].reshape(bsz, t_pad, IDX_DIM)[:, pad:])

            q, k, v, qi, ki, wi = _attn_in(
                xs, norm_mix[i], w_pad, q_norm[j], k_norm[j], kidx_norm[j], prompt=False)
            pg = math.gcd(npg, 32)
            qi3 = qi.reshape(db, N_IDX_HEADS, IDX_DIM)
            wi3 = wi[:, :N_IDX_HEADS].reshape(db, N_IDX_HEADS, 1)
            scores = _smp_scores(page_table, qi3, wi3, cache_kidx[j].transpose(0, 2, 1), pg)
            bias, nbias = _smp_select(scores, qi3, wi3, ki[:, :IDX_DIM].reshape(db, 1, IDX_DIM), topk_s)
            hsel = (jnp.arange(N_HEADS)[:, None] // HEADS_PER_KV == jnp.arange(N_KV_HEADS)[None, :])
            q_bd = (q.reshape(db, N_HEADS, 1, HEAD_DIM) * hsel[None, :, :, None].astype(BF16))
            q_bd = q_bd.reshape(db, N_HEADS, kd)
            npool = cache_k.shape[1]
            o = _smp_attend(page_table, q_bd, bias, nbias, k.reshape(db, 1, kd), v.reshape(db, 1, kd),
                            cache_k[j].transpose(0, 2, 3, 1).reshape(npool, kd, BLK),
                            cache_v[j].transpose(0, 2, 3, 1).reshape(npool, kd, BLK), pg)
            xs = _out_proj(xs, o.reshape(db, qd).astype(BF16), w_out)
            outs["ks"].append(k.reshape(db, 1, N_KV_HEADS, HEAD_DIM))
            outs["vs"].append(v.reshape(db, 1, N_KV_HEADS, HEAD_DIM))
            outs["kis"].append(ki[:, :IDX_DIM].reshape(db, 1, IDX_DIM))
        else:
            w = w_ssd_in[j]
            w_main = w[:, :d_in + cdim].astype(BF16)
            w_dt = jnp.pad(w[:, d_in + cdim:], ((0, 0), (0, LANES - n_heads))).astype(BF16)
            w_out = w_ssd_out[j].astype(BF16)
            sp = (conv_w[j], conv_b[j], dt_bias[j], a_log[j], d_skip[j], gate_norm[j])

            xp, hfin, ctail = _ssd_prompt(xp, norm_mix[i], w_main[:, :d_in], w_main[:, d_in:], w_dt, w_out,
                                          *sp, bsz, nb, pad, n_heads)
            outs["hp"].append(hfin)
            outs["cp"].append(ctail[:, SUBLANES - (CONV_W - 1):])

            z, xbc_raw, dt_raw = _norm_proj(xs, norm_mix[i], w_main, w_dt, d_in)
            y, hnew = _ssd_sample(z, xbc_raw, dt_raw, state_conv[j], state_ssm[j], *sp, n_heads)
            xs = _out_proj(xs, y, w_out)
            outs["hs"].append(hnew)
            outs["cs"].append(jnp.concatenate([state_conv[j][:, 1:], xbc_raw[:, None, :]], axis=1))
        wts = _ffn_weights(w_ffn_b_in[i], w_ffn_b_out[i])
        xp = _ffn(xp, norm_ffn_b[i], wts)
        xs = _ffn(xs, norm_ffn_b[i], wts)

    y_prompt = xp.reshape(bsz, t_pad, d)[:, pad + N_META:]
    y_sample = xs.reshape(db, 1, d)
    st = lambda key: jnp.stack(outs[key])
    return (y_prompt, y_sample, st("kp"), st("vp"), st("kip"), st("ks"), st("vs"), st("kis"),
            st("hp"), st("cp"), st("hs"), st("cs"))
```

```python
import functools
import math

import jax
import jax.numpy as jnp
from jax import lax
from jax.experimental import pallas as pl
from jax.experimental.pallas import tpu as pltpu

F32 = jnp.float32
BF16 = jnp.bfloat16
I32 = jnp.int32

N_META = 16
N_HEADS = 16
HEAD_DIM = 64
N_KV_HEADS = 4
HEADS_PER_KV = N_HEADS // N_KV_HEADS
N_IDX_HEADS = 8
IDX_DIM = 64
TOPK_MAX = 256
SSD_HEAD_DIM = 64
SSD_GROUPS = 4
D_STATE = 128
CONV_W = 4
EPS = 1e-6

LANES = 128
SUBLANES = 8
BLK = 128
NEG = -1e30
SAFE_LOGIT = 40.0
VT_ROWS = 80
INT_MIN = -2 ** 31
VMEM_LIMIT = 56 * 1024 * 1024


def _cparams(sem, vmem=VMEM_LIMIT):
    return pltpu.CompilerParams(dimension_semantics=sem, vmem_limit_bytes=vmem)


def _row_tile(rows, pref):
    best = None
    for d in range(SUBLANES, min(rows, pref) + 1, SUBLANES):
        if rows % d == 0:
            best = d
    assert best is not None, rows
    return best


def _rms(x, g):
    var = jnp.mean(x * x, axis=-1, keepdims=True)
    return x * lax.rsqrt(var + EPS) * g


def _dot(a, b):
    return jnp.dot(a, b, preferred_element_type=F32)


def _dot_t0(a, b):
    return lax.dot_general(a, b, (((0,), (0,)), ((), ())), preferred_element_type=F32)


def _dot_t1(a, b):
    return lax.dot_general(a, b, (((1,), (1,)), ((), ())), preferred_element_type=F32)


def _split3(a):
    a0 = a.astype(BF16)
    r = a - a0.astype(F32)
    a1 = r.astype(BF16)
    a2 = (r - a1.astype(F32)).astype(BF16)
    return a0, a1, a2


def _dot_sel(a, m, fn=_dot):
    a0, a1, a2 = _split3(a)
    return fn(a0, m) + fn(a1, m) + fn(a2, m)


def _silu(x):
    return x * (1.0 / (1.0 + jnp.exp(-x)))


def _softplus(x):
    return jnp.maximum(x, 0.0) + jnp.log(1.0 + jnp.exp(-jnp.abs(x)))


def _ffn_kernel(x_ref, g_ref, wi_ref, wo_ref, o_ref, xn_ref, acc_ref):
    hid = wo_ref.shape[0]
    th = 256 if hid % 256 == 0 else LANES
    xn_ref[...] = _rms(x_ref[...], g_ref[...]).astype(BF16)
    acc_ref[...] = jnp.zeros_like(acc_ref)
    for j in range(hid // th):
        xn = xn_ref[...]
        a = _dot(xn, wi_ref[:, j * th:(j + 1) * th])
        b = _dot(xn, wi_ref[:, hid + j * th:hid + (j + 1) * th])
        h = (_silu(a) * b).astype(BF16)
        acc_ref[...] += _dot(h, wo_ref[j * th:(j + 1) * th, :])
    o_ref[...] = x_ref[...] + 0.5 * acc_ref[...]


def _ffn_weights(w_in, w_out):
    return w_in.astype(BF16), w_out.astype(BF16)


def _ffn(x, g, weights):
    w_in, w_out = weights
    rows, d = x.shape
    hid = w_out.shape[0]
    tm = _row_tile(rows, 1024)
    resident = lambda shape: pl.BlockSpec(shape, lambda i: (0,) * len(shape),
                                          pipeline_mode=pl.Buffered(1))
    return pl.pallas_call(
        _ffn_kernel,
        grid=(rows // tm,),
        in_specs=[
            pl.BlockSpec((tm, d), lambda i: (i, 0)),
            resident((1, d)), resident((d, 2 * hid)), resident((hid, d)),
        ],
        out_specs=pl.BlockSpec((tm, d), lambda i: (i, 0)),
        out_shape=jax.ShapeDtypeStruct((rows, d), F32),
        scratch_shapes=[pltpu.VMEM((tm, d), BF16), pltpu.VMEM((tm, d), F32)],
        compiler_params=_cparams(("parallel",)),
        name="ffn",
    )(x, g.reshape(1, d), w_in, w_out)


def _attn_project(x_ref, g_ref, w_ref, qg_ref, kg_ref, kig_ref, gq_ref, eq_ref, gk_ref, ek_ref):
    qd = N_HEADS * HEAD_DIM
    kd = N_KV_HEADS * HEAD_DIM
    qid = N_IDX_HEADS * IDX_DIM
    xn = _rms(x_ref[...], g_ref[...]).astype(BF16)
    h = _dot(xn, w_ref[...])

    def head_norm(t, gsum_ref, gexp_ref, gain):
        ss = _dot_sel(t * t, gsum_ref[...])
        rs = lax.rsqrt(ss * (1.0 / HEAD_DIM) + EPS)
        return t * _dot_sel(rs, gexp_ref[...]) * gain

    o = 0
    q = head_norm(h[:, o:o + qd], gq_ref, eq_ref, qg_ref[...]) * (HEAD_DIM ** -0.5)
    o += qd
    k = head_norm(h[:, o:o + kd], gk_ref, ek_ref, kg_ref[...])
    o += kd
    v = h[:, o:o + kd]
    o += kd
    qi = h[:, o:o + qid] * (IDX_DIM ** -0.5)
    o += qid
    ki = h[:, o:o + LANES]
    var = jnp.sum(ki * ki, axis=-1, keepdims=True) * (1.0 / IDX_DIM)
    ki = ki * lax.rsqrt(var + EPS) * kig_ref[...]
    o += LANES
    wi = h[:, o:o + LANES] * (N_IDX_HEADS ** -0.5)
    return q, k, v, qi, ki, wi


def _attn_in_sample_kernel(*refs):
    q_o, k_o, v_o, qi_o, ki_o, wi_o = refs[10:]
    q, k, v, qi, ki, wi = _attn_project(*refs[:10])
    q_o[...] = q.astype(BF16)
    k_o[...] = k
    v_o[...] = v
    qi_o[...] = qi.astype(BF16)
    ki_o[...] = ki
    wi_o[...] = wi


def _attn_in_prompt_kernel(*refs):
    k_o, v_o, ki_o, qt_o, qit_o, wt_o, kg_o, vt_o, kib_o = refs[10:]
    q, k, v, qi, ki, wi = _attn_project(*refs[:10])
    k_o[...] = k
    v_o[...] = v
    ki_o[...] = ki
    kib_o[...] = ki[:, :IDX_DIM].astype(BF16)
    for g in range(N_KV_HEADS):
        kg_o[g] = k[:, g * HEAD_DIM:(g + 1) * HEAD_DIM].astype(BF16)
    tail = (lax.broadcasted_iota(I32, (VT_ROWS - HEAD_DIM, BLK), 0) == 0).astype(BF16)
    for r in range(q.shape[0] // BLK):
        rows = slice(r * BLK, (r + 1) * BLK)
        for t in range(N_HEADS * HEAD_DIM // LANES):
            tt = q[rows, t * LANES:(t + 1) * LANES].T.astype(BF16)
            for u in range(LANES // HEAD_DIM):
                h = t * (LANES // HEAD_DIM) + u
                g, hh = h // HEADS_PER_KV, h % HEADS_PER_KV
                qt_o[r, g, :, hh * BLK:(hh + 1) * BLK] = tt[u * HEAD_DIM:(u + 1) * HEAD_DIM]
        for t in range(N_IDX_HEADS * IDX_DIM // LANES):
            tt = qi[rows, t * LANES:(t + 1) * LANES].T.astype(BF16)
            for u in range(LANES // IDX_DIM):
                h = t * (LANES // IDX_DIM) + u
                qit_o[r, :, h * BLK:(h + 1) * BLK] = tt[u * IDX_DIM:(u + 1) * IDX_DIM]
        wt_o[r] = wi[rows].T[0:N_IDX_HEADS]
        for t in range(N_KV_HEADS * HEAD_DIM // LANES):
            tt = v[rows, t * LANES:(t + 1) * LANES].T.astype(BF16)
            for u in range(LANES // HEAD_DIM):
                vt_o[r, t * (LANES // HEAD_DIM) + u] = jnp.concatenate(
                    [tt[u * HEAD_DIM:(u + 1) * HEAD_DIM], tail], axis=0)


def _seg_mats(n_heads, hd):
    col = jnp.arange(n_heads * hd) // hd
    gsum = (col[:, None] == jnp.arange(LANES)[None, :]).astype(BF16)
    return gsum, gsum.T


def _attn_in(x, g, w_pad, q_gain, k_gain, ki_gain, prompt):
    rows, d = x.shape
    n = w_pad.shape[1]
    qd, kd, qid = N_HEADS * HEAD_DIM, N_KV_HEADS * HEAD_DIM, N_IDX_HEADS * IDX_DIM
    gw = HEADS_PER_KV * BLK
    tm = _row_tile(rows, 512)
    gq, eq = _seg_mats(N_HEADS, HEAD_DIM)
    gk, ek = _seg_mats(N_KV_HEADS, HEAD_DIM)
    const = lambda shape: pl.BlockSpec(shape, lambda i: (0,) * len(shape))
    rowb = lambda w: pl.BlockSpec((tm, w), lambda i: (i, 0))
    sds = jax.ShapeDtypeStruct
    if prompt:
        assert tm % BLK == 0
        nbk, tb = rows // BLK, tm // BLK
        blkb = lambda *s: pl.BlockSpec((tb,) + s, lambda i: (i,) + (0,) * len(s))
        kern = _attn_in_prompt_kernel
        out_specs = [rowb(kd), rowb(kd), rowb(LANES), blkb(N_KV_HEADS, HEAD_DIM, gw),
                     blkb(IDX_DIM, N_IDX_HEADS * BLK), blkb(N_IDX_HEADS, BLK),
                     pl.BlockSpec((N_KV_HEADS, tm, HEAD_DIM), lambda i: (0, i, 0)),
                     blkb(N_KV_HEADS, VT_ROWS, BLK),
                     rowb(IDX_DIM)]
        out_shape = [sds((rows, kd), F32), sds((rows, kd), F32), sds((rows, LANES), F32),
                     sds((nbk, N_KV_HEADS, HEAD_DIM, gw), BF16),
                     sds((nbk, IDX_DIM, N_IDX_HEADS * BLK), BF16),
                     sds((nbk, N_IDX_HEADS, BLK), F32),
                     sds((N_KV_HEADS, rows, HEAD_DIM), BF16),
                     sds((nbk, N_KV_HEADS, VT_ROWS, BLK), BF16),
                     sds((rows, IDX_DIM), BF16)]
    else:
        kern = _attn_in_sample_kernel
        out_specs = [rowb(qd), rowb(kd), rowb(kd), rowb(qid), rowb(LANES), rowb(LANES)]
        out_shape = [sds((rows, qd), BF16), sds((rows, kd), F32), sds((rows, kd), F32),
                     sds((rows, qid), BF16), sds((rows, LANES), F32), sds((rows, LANES), F32)]
    return pl.pallas_call(
        kern,
        grid=(rows // tm,),
        in_specs=[rowb(d), const((1, d)), const((d, n)), const((1, qd)), const((1, kd)),
                  const((1, LANES)), const((qd, LANES)), const((LANES, qd)),
                  const((kd, LANES)), const((LANES, kd))],
        out_specs=out_specs,
        out_shape=out_shape,
        compiler_params=_cparams(("parallel",)),
        name="attn_in_prompt" if prompt else "attn_in_sample",
    )(x, g.reshape(1, d), w_pad,
      jnp.tile(q_gain, N_HEADS).reshape(1, qd), jnp.tile(k_gain, N_KV_HEADS).reshape(1, kd),
      jnp.pad(ki_gain, (0, LANES - IDX_DIM)).reshape(1, LANES), gq, eq, gk, ek)


def _norm_proj_kernel(x_ref, g_ref, w_ref, ws_ref, oa_ref, ob_ref, os_ref, xn_ref, *, na):
    j = pl.program_id(1)

    @pl.when(j == 0)
    def _():
        xn = _rms(x_ref[...], g_ref[...]).astype(BF16)
        xn_ref[...] = xn
        os_ref[...] = _dot(xn, ws_ref[...])

    r = _dot(xn_ref[...], w_ref[...])

    @pl.when(j < na)
    def _():
        oa_ref[...] = r

    @pl.when(j >= na)
    def _():
        ob_ref[...] = r


def _norm_proj(x, g, w_main, w_side, n_a):
    rows, d = x.shape
    n = w_main.shape[1]
    tm = _row_tile(rows, 1024)
    tn = math.gcd(math.gcd(n_a, n - n_a), 1024)
    na = n_a // tn
    return pl.pallas_call(
        functools.partial(_norm_proj_kernel, na=na),
        grid=(rows // tm, n // tn),
        in_specs=[
            pl.BlockSpec((tm, d), lambda i, j: (i, 0)),
            pl.BlockSpec((1, d), lambda i, j: (0, 0)),
            pl.BlockSpec((d, tn), lambda i, j: (0, j)),
            pl.BlockSpec((d, LANES), lambda i, j: (0, 0)),
        ],
        out_specs=[pl.BlockSpec((tm, tn), lambda i, j: (i, jnp.minimum(j, na - 1))),
                   pl.BlockSpec((tm, tn), lambda i, j: (i, jnp.maximum(j - na, 0))),
                   pl.BlockSpec((tm, LANES), lambda i, j: (i, 0))],
        out_shape=[jax.ShapeDtypeStruct((rows, n_a), F32),
                   jax.ShapeDtypeStruct((rows, n - n_a), F32),
                   jax.ShapeDtypeStruct((rows, LANES), F32)],
        scratch_shapes=[pltpu.VMEM((tm, d), BF16)],
        compiler_params=_cparams(("parallel", "arbitrary")),
        name="norm_proj",
    )(x, g.reshape(1, d), w_main, w_side)


def _out_proj_kernel(x_ref, y_ref, w_ref, o_ref):
    o_ref[...] = x_ref[...] + _dot(y_ref[...], w_ref[...])


def _out_proj(x, y, w):
    rows, d = x.shape
    k = y.shape[1]
    tm = _row_tile(rows, 512)
    return pl.pallas_call(
        _out_proj_kernel,
        grid=(rows // tm,),
        in_specs=[pl.BlockSpec((tm, d), lambda i: (i, 0)),
                  pl.BlockSpec((tm, k), lambda i: (i, 0)),
                  pl.BlockSpec((k, d), lambda i: (0, 0))],
        out_specs=pl.BlockSpec((tm, d), lambda i: (i, 0)),
        out_shape=jax.ShapeDtypeStruct((rows, d), F32),
        compiler_params=_cparams(("parallel",)),
        name="out_proj",
    )(x, y, w)


def _sort_key(score):
    bits = pltpu.bitcast(score, I32)
    return bits ^ ((bits >> 31) & 0x7FFFFFFF)


def _tile_fold(x, op):
    r = x[0:SUBLANES]
    for t in range(1, x.shape[0] // SUBLANES):
        r = op(r, x[t * SUBLANES:(t + 1) * SUBLANES])
    return r


def _dsa_pair_kernel(smax_ref, qita_ref, qitb_ref, wta_ref, wtb_ref, qta_ref, qtb_ref,
                     ki_ref, k_ref, vt_ref, oa_ref, ob_ref,
                     key_ref, acc_ref, qit_s, wt_s, qt_s, *, pad, topk, nb):
    j = pl.program_id(1)
    n_a = j + 1
    blocks = (j, nb - 1 - j)
    ntrip = nb + 1
    ncount = key_ref.shape[0]
    last = ki_ref.shape[0] - 1
    row = lax.broadcasted_iota(I32, (BLK, BLK), 0)
    lane = lax.broadcasted_iota(I32, (BLK, BLK), 1)
    gw = HEADS_PER_KV * BLK

    qit_s[0], qit_s[1] = qita_ref[...], qitb_ref[...]
    wt_s[0], wt_s[1] = wta_ref[...], wtb_ref[...]
    qt_s[0], qt_s[1] = qta_ref[...], qtb_ref[...]

    def trip(t):
        blk = (t >= n_a).astype(I32)
        c = t - blk * n_a
        return blk, c, jnp.minimum(c, last)

    def pick(blk, a, b):
        if isinstance(a, tuple):
            return tuple(pick(blk, x, y) for x, y in zip(a, b))
        return jnp.where(blk == 1, b, a)

    for t in range(ncount):
        if t >= ntrip:
            key_ref[t] = jnp.full((BLK, BLK), INT_MIN, I32)
            continue
        blk, c, cr = trip(t)
        dots = _dot(ki_ref[cr], qit_s[blk])
        sc = jnp.zeros((BLK, BLK), F32)
        for h in range(N_IDX_HEADS):
            sc = sc + wt_s[blk, h:h + 1, :] * jnp.maximum(dots[:, h * BLK:(h + 1) * BLK], 0.0)
        s_pos = c * BLK + row
        t_pos = pick(blk, blocks[0], blocks[1]) * BLK + lane
        valid = (s_pos <= t_pos) & (s_pos >= pad)
        key_ref[t] = jnp.where(valid, _sort_key(sc), INT_MIN)

    def count(pred, arg):
        cnt = [jnp.zeros((SUBLANES, BLK), I32), jnp.zeros((SUBLANES, BLK), I32)]
        for t in range(ncount):
            blk, c, _ = trip(t)
            hit = jnp.where(pred(key_ref[t], pick(blk, arg[0], arg[1]), c * BLK + row), 1, 0)
            part = _tile_fold(hit, jnp.add)
            cnt[0] = cnt[0] + jnp.where(blk == 0, part, 0)
            cnt[1] = cnt[1] + jnp.where(blk == 1, part, 0)
        return tuple(jnp.sum(x, axis=0, keepdims=True) for x in cnt)

    zero = jnp.zeros((1, BLK), I32)
    c0 = count(lambda k, a, s: k >= a, (zero, zero))
    thr0 = tuple(jnp.where(x >= topk, 0, INT_MIN).astype(I32) for x in c0)

    def bit_step(it, st):
        thr, n_ge = st[:2], st[2:]
        cand = tuple(x + (jnp.int32(1) << (30 - it)) for x in thr)
        cnt = count(lambda k, a, s: k >= a, cand)
        ok = tuple(x >= topk for x in cnt)
        return (tuple(jnp.where(ok[b], cand[b], thr[b]) for b in range(2))
                + tuple(jnp.where(ok[b], cnt[b], n_ge[b]) for b in range(2)))

    st = lax.fori_loop(0, 31, bit_step, thr0 + c0)
    thr, n_ge = st[:2], st[2:]
    tied = ((n_ge[0] > topk) & (thr[0] > INT_MIN)) | ((n_ge[1] > topk) & (thr[1] > INT_MIN))
    any_tied = jnp.max(tied.astype(I32))
    nbits = (nb * BLK).bit_length()

    def idx_search():
        n_gt = count(lambda k, a, s: k > a, thr)
        need = tuple(topk - x for x in n_gt)

        def step(it, lo):
            cand = tuple(x + (jnp.int32(1) << (nbits - 1 - it)) for x in lo)
            below = count(lambda k, a, s: (k == a[0]) & (s < a[1]),
                          ((thr[0], cand[0]), (thr[1], cand[1])))
            return tuple(jnp.where(below[b] < need[b], cand[b], lo[b]) for b in range(2))
        return lax.fori_loop(0, nbits, step, (zero, zero))

    big = jnp.full((1, BLK), 2 ** 30, I32)
    jcut = lax.cond(any_tied > 0, idx_search, lambda: (big, big))

    def logits(t):
        blk, c, cr = trip(t)
        k = key_ref[t]
        th = pick(blk, thr[0], thr[1])
        sel = (k > th) | ((k == th) & (c * BLK + row <= pick(blk, jcut[0], jcut[1])))
        bias = jnp.where(sel & (k > INT_MIN), 0.0, NEG)
        bb = jnp.concatenate([bias] * HEADS_PER_KV, axis=1)
        return blk, cr, [_dot(k_ref[g, cr], qt_s[blk, g]) + bb for g in range(N_KV_HEADS)]

    def attend(shift):
        acc_ref[...] = jnp.zeros_like(acc_ref)

        def probs(t):
            blk, cr, s = logits(t)
            if shift is not None:
                s = [s[g] - pick(blk, shift[0][g], shift[1][g]) for g in range(N_KV_HEADS)]
            return blk, cr, [jnp.exp(x).astype(BF16) for x in s]

        n_first_max = (nb + 1) // 2
        t = 0
        while t < ntrip:
            if t >= n_first_max and t + 1 < ntrip:
                _, cr0, p0 = probs(t)
                _, cr1, p1 = probs(t + 1)
                for g in range(N_KV_HEADS):
                    vt = jnp.concatenate([vt_ref[cr0, g], vt_ref[cr1, g]], axis=1)
                    acc_ref[1, g] += _dot(vt, jnp.concatenate([p0[g], p1[g]], axis=0))
                t += 2
            else:
                blk, cr, p = probs(t)
                for g in range(N_KV_HEADS):
                    acc_ref[blk, g] += _dot(vt_ref[cr, g], p[g])
                t += 1

    safe = smax_ref[0] <= SAFE_LOGIT

    @pl.when(safe)
    def _():
        attend(None)

    @pl.when(jnp.logical_not(safe))
    def _():
        m = [[jnp.full((SUBLANES, gw), NEG, F32) for _ in range(N_KV_HEADS)] for _ in range(2)]
        for t in range(ntrip):
            blk, _, s = logits(t)
            for g in range(N_KV_HEADS):
                part = _tile_fold(s[g], jnp.maximum)
                m[0][g] = jnp.maximum(m[0][g], jnp.where(blk == 0, part, NEG))
                m[1][g] = jnp.maximum(m[1][g], jnp.where(blk == 1, part, NEG))
        attend([[jnp.max(x, axis=0, keepdims=True) for x in mb] for mb in m])

    for b, o_ref in enumerate((oa_ref, ob_ref)):
        q_row = blocks[b] * BLK + lax.broadcasted_iota(I32, (BLK, LANES), 0)
        for g in range(N_KV_HEADS):
            a = acc_ref[b, g]
            res = a[0:HEAD_DIM] * (1.0 / a[HEAD_DIM:HEAD_DIM + 1])
            for t in range(HEADS_PER_KV // 2):
                two = jnp.concatenate([res[:, (2 * t + u) * BLK:(2 * t + u + 1) * BLK] for u in range(2)],
                                      axis=0)
                two = jnp.where(q_row >= pad, two.T, 0.0)
                lo = (g * HEADS_PER_KV + 2 * t) * HEAD_DIM
                o_ref[:, lo:lo + LANES] = two.astype(BF16)


def _dsa_prompt_pairs(smax, qit, wt, kib, qt, kg, vt, pad, topk):
    bsz, nb = qit.shape[:2]
    steps = (nb + 1) // 2
    gw = HEADS_PER_KV * BLK
    qd = N_HEADS * HEAD_DIM
    ncount = nb + 1
    kern = functools.partial(_dsa_pair_kernel, pad=pad, topk=topk, nb=nb)
    first = lambda *s: pl.BlockSpec((None, None) + s, lambda b, j: (b, j) + (0,) * len(s))
    second = lambda *s: pl.BlockSpec((None, None) + s, lambda b, j: (b, nb - 1 - j) + (0,) * len(s))
    oa, ob = pl.pallas_call(
        kern,
        grid=(bsz, steps),
        in_specs=[
            pl.BlockSpec(memory_space=pltpu.SMEM),
            first(IDX_DIM, N_IDX_HEADS * BLK), second(IDX_DIM, N_IDX_HEADS * BLK),
            first(N_IDX_HEADS, BLK), second(N_IDX_HEADS, BLK),
            first(N_KV_HEADS, HEAD_DIM, gw), second(N_KV_HEADS, HEAD_DIM, gw),
            pl.BlockSpec((None, nb, BLK, IDX_DIM), lambda b, j: (b, 0, 0, 0)),
            pl.BlockSpec((N_KV_HEADS, None, nb, BLK, HEAD_DIM), lambda b, j: (0, b, 0, 0, 0)),
            pl.BlockSpec((None, nb, N_KV_HEADS, VT_ROWS, BLK), lambda b, j: (b, 0, 0, 0, 0)),
        ],
        out_specs=[pl.BlockSpec((None, BLK, qd), lambda b, j: (b, j, 0)),
                   pl.BlockSpec((None, BLK, qd), lambda b, j: (b, steps - 1 - j, 0))],
        out_shape=[jax.ShapeDtypeStruct((bsz, steps * BLK, qd), BF16)] * 2,
        scratch_shapes=[pltpu.VMEM((ncount, BLK, BLK), I32),
                        pltpu.VMEM((2, N_KV_HEADS, VT_ROWS, gw), F32),
                        pltpu.VMEM((2, IDX_DIM, N_IDX_HEADS * BLK), BF16),
                        pltpu.VMEM((2, N_IDX_HEADS, BLK), F32),
                        pltpu.VMEM((2, N_KV_HEADS, HEAD_DIM, gw), BF16)],
        compiler_params=_cparams(("parallel", "arbitrary")),
        name="dsa_prompt",
    )(smax, qit, qit, wt, wt, qt, qt, kib, kg, vt)
    return jnp.concatenate([oa[:, :(nb - steps) * BLK], ob], axis=1)


def _smp_scores_kernel(pt_ref, qi_ref, w_ref, *refs, pg):
    page_refs, o_ref = refs[:pg], refs[pg]
    qi = qi_ref[...]
    w = w_ref[...]
    for p in range(pg):
        d = _dot(qi, page_refs[p][...].astype(BF16))
        o_ref[p:p + 1, :] = jnp.sum(w * jnp.maximum(d, 0.0), axis=0, keepdims=True)


def _smp_scores(page_table, qi, wi, kidx_t, pg):
    db, npg = page_table.shape
    kern = functools.partial(_smp_scores_kernel, pg=pg)
    page_spec = lambda p: pl.BlockSpec(
        (None, IDX_DIM, BLK), lambda b, j, pt: (pt[b * npg + j * pg + p], 0, 0))
    return pl.pallas_call(
        kern,
        grid_spec=pltpu.PrefetchScalarGridSpec(
            num_scalar_prefetch=1,
            grid=(db, npg // pg),
            in_specs=[pl.BlockSpec((None, N_IDX_HEADS, IDX_DIM), lambda b, j, pt: (b, 0, 0)),
                      pl.BlockSpec((None, N_IDX_HEADS, 1), lambda b, j, pt: (b, 0, 0))]
                     + [page_spec(p) for p in range(pg)],
            out_specs=pl.BlockSpec((None, pg, BLK), lambda b, j, pt: (b, j, 0)),
        ),
        out_shape=jax.ShapeDtypeStruct((db, npg, BLK), F32),
        compiler_params=_cparams(("parallel", "arbitrary")),
        name="sample_scores",
    )(page_table.reshape(-1), qi, wi, *([kidx_t] * pg))


def _smp_select_kernel(sc_ref, qi_ref, w_ref, kin_ref, bias_ref, nb_ref, *, topk, past):
    db, npg, _ = sc_ref.shape
    key = _sort_key(sc_ref[...])
    qi = qi_ref[...].astype(F32)
    kn = kin_ref[...].astype(BF16).astype(F32)
    d = jnp.sum(qi * kn, axis=-1, keepdims=True)
    s_new = jnp.sum(w_ref[...] * jnp.maximum(d, 0.0), axis=1, keepdims=True)
    key_new = _sort_key(s_new)
    pos = (lax.broadcasted_iota(I32, (db, npg, BLK), 1) * BLK
           + lax.broadcasted_iota(I32, (db, npg, BLK), 2))

    def count(pred_past, pred_new):
        c = jnp.sum(jnp.where(pred_past, 1, 0), axis=2, keepdims=True)
        return jnp.sum(c, axis=1, keepdims=True) + jnp.where(pred_new, 1, 0)

    thr = jnp.where(count(key >= 0, key_new >= 0) >= topk, 0, INT_MIN).astype(I32)

    def bit_step(it, thr):
        cand = thr + (jnp.int32(1) << (30 - it))
        return jnp.where(count(key >= cand, key_new >= cand) >= topk, cand, thr)

    thr = lax.fori_loop(0, 31, bit_step, thr)
    need = topk - count(key > thr, key_new > thr)
    nbits = max(1, past.bit_length())

    def step(it, lo):
        cand = lo + (jnp.int32(1) << (nbits - 1 - it))
        below = count((key == thr) & (pos < cand), (key_new == thr) & (past < cand))
        return jnp.where(below < need, cand, lo)

    jcut = lax.fori_loop(0, nbits, step, jnp.zeros((db, 1, 1), I32))
    sel = (key > thr) | ((key == thr) & (pos <= jcut))
    bias_ref[...] = jnp.where(sel, 0.0, NEG)
    sel_new = (key_new > thr) | ((key_new == thr) & (past <= jcut))
    nb_ref[...] = jnp.broadcast_to(jnp.where(sel_new, 0.0, NEG), nb_ref.shape)


def _smp_select(scores, qi, wi, ki_new, topk):
    db, npg, _ = scores.shape
    kern = functools.partial(_smp_select_kernel, topk=topk, past=npg * BLK)
    full = lambda *s: pl.BlockSpec(s, lambda i: (0,) * len(s))
    return pl.pallas_call(
        kern,
        grid=(1,),
        in_specs=[full(db, npg, BLK), full(db, N_IDX_HEADS, IDX_DIM), full(db, N_IDX_HEADS, 1),
                  full(db, 1, IDX_DIM)],
        out_specs=[full(db, npg, BLK), full(db, SUBLANES, LANES)],
        out_shape=[jax.ShapeDtypeStruct((db, npg, BLK), F32),
                   jax.ShapeDtypeStruct((db, SUBLANES, LANES), F32)],
        compiler_params=_cparams(("arbitrary",)),
        name="sample_select",
    )(scores, qi, wi, ki_new)


def _smp_attend_kernel(pt_ref, q_ref, bias_ref, nb_ref, kn_ref, vn_ref, *refs, pg):
    k_refs, v_refs = refs[:pg], refs[pg:2 * pg]
    o_ref, m_ref, l_ref, acc_ref = refs[2 * pg:]
    j = pl.program_id(1)
    kd = N_KV_HEADS * HEAD_DIM

    @pl.when(j == 0)
    def _():
        m_ref[...] = jnp.full_like(m_ref, NEG)
        l_ref[...] = jnp.zeros_like(l_ref)
        acc_ref[...] = jnp.zeros_like(acc_ref)

    q = q_ref[...]
    s = [_dot(q, k_refs[p][...].astype(BF16)) + bias_ref[p:p + 1, :] for p in range(pg)]
    m_old = m_ref[...]
    m_new = m_old
    for p in range(pg):
        m_new = jnp.maximum(m_new, jnp.max(s[p], axis=-1, keepdims=True))
    alpha = jnp.exp(m_old - m_new)
    l = l_ref[...] * alpha
    acc = acc_ref[...] * alpha
    for p in range(pg):
        e = jnp.exp(s[p] - m_new)
        l = l + jnp.sum(e, axis=-1, keepdims=True)
        acc = acc + _dot_t1(e.astype(BF16), v_refs[p][...].astype(BF16))
    m_ref[...] = m_new
    l_ref[...] = l
    acc_ref[...] = acc

    @pl.when(j == pl.num_programs(1) - 1)
    def _():
        qf = q.astype(F32)
        kn = kn_ref[...].astype(BF16).astype(F32)
        vn = vn_ref[...].astype(BF16).astype(F32)
        s_new = jnp.sum(qf * kn, axis=-1, keepdims=True) + nb_ref[0:1, 0:1]
        m_fin = jnp.maximum(m_new, s_new)
        a2 = jnp.exp(m_new - m_fin)
        e_new = jnp.exp(s_new - m_fin)
        l_fin = l * a2 + e_new
        acc_fin = acc * a2 + e_new.astype(BF16).astype(F32) * vn
        res = acc_fin / l_fin
        hgrp = lax.broadcasted_iota(I32, (N_HEADS, HEAD_DIM), 0) // HEADS_PER_KV
        out = jnp.zeros((N_HEADS, HEAD_DIM), F32)
        for g in range(N_KV_HEADS):
            out = out + jnp.where(hgrp == g, res[:, g * HEAD_DIM:(g + 1) * HEAD_DIM], 0.0)
        o_ref[...] = out


def _smp_attend(page_table, q_bd, bias, nbias, k_new, v_new, k_t, v_t, pg):
    db, npg = page_table.shape
    kd = N_KV_HEADS * HEAD_DIM
    kern = functools.partial(_smp_attend_kernel, pg=pg)
    page_spec = lambda p: pl.BlockSpec(
        (None, kd, BLK), lambda b, j, pt: (pt[b * npg + j * pg + p], 0, 0))
    per_b = lambda shape: pl.BlockSpec((None,) + shape, lambda b, j, pt: (b, 0, 0))
    return pl.pallas_call(
        kern,
        grid_spec=pltpu.PrefetchScalarGridSpec(
            num_scalar_prefetch=1,
            grid=(db, npg // pg),
            in_specs=[per_b((N_HEADS, kd)),
                      pl.BlockSpec((None, pg, BLK), lambda b, j, pt: (b, j, 0)),
                      per_b((SUBLANES, LANES)), per_b((1, kd)), per_b((1, kd))]
                     + [page_spec(p) for p in range(pg)] * 2,
            out_specs=per_b((N_HEADS, HEAD_DIM)),
            scratch_shapes=[pltpu.VMEM((N_HEADS, 1), F32), pltpu.VMEM((N_HEADS, 1), F32),
                            pltpu.VMEM((N_HEADS, kd), F32)],
        ),
        out_shape=jax.ShapeDtypeStruct((db, N_HEADS, HEAD_DIM), F32),
        compiler_params=_cparams(("parallel", "arbitrary")),
        name="sample_attend",
    )(page_table.reshape(-1), q_bd, bias, nbias, k_new, v_new,
      *([k_t] * pg), *([v_t] * pg))


def _ssd_prompt_kernel(x_ref, g_ref, wz_ref, wx_ref, wdt_ref, wo_ref,
                       cw_ref, cb_ref, dtb_ref, a_ref,
                       dtbc_ref, ac_ref, dsk_ref, gn_ref, ltri_ref, utri_ref,
                       o_ref, st_ref, cv_ref, xpad_ref, h_ref, yb_ref, xt_ref, yt_ref, z_ref,
                       *, pad, n_heads):
    c = pl.program_id(1)
    d_in = n_heads * SSD_HEAD_DIM
    gn = SSD_GROUPS * D_STATE
    hpg = n_heads // SSD_GROUPS

    @pl.when(c == 0)
    def _():
        xpad_ref[0:SUBLANES, :] = jnp.zeros((SUBLANES, xpad_ref.shape[1]), F32)
        h_ref[...] = jnp.zeros_like(h_ref)

    xn = _rms(x_ref[...], g_ref[...]).astype(BF16)
    z_ref[...] = _dot(xn, wz_ref[...])
    dt_raw = _dot(xn, wdt_ref[...])
    dt_raw_t = dt_raw.T[0:n_heads]

    xpad_ref[SUBLANES:, :] = _dot(xn, wx_ref[...])
    conv = cb_ref[...] + cw_ref[CONV_W - 1:CONV_W, :] * xpad_ref[SUBLANES:, :]
    for j in range(CONV_W - 1):
        sh = CONV_W - 1 - j
        conv = conv + cw_ref[j:j + 1, :] * xpad_ref[SUBLANES - sh:SUBLANES - sh + BLK, :]
    xpad_ref[0:SUBLANES, :] = xpad_ref[BLK:BLK + SUBLANES, :]
    xbc = _silu(conv)

    live = (c > 0) | (lax.broadcasted_iota(I32, (BLK, LANES), 0) >= pad)
    dt = jnp.where(live, _softplus(dt_raw + dtb_ref[...]), 0.0)
    acum = _dot_sel(dt * a_ref[...], ltri_ref[...], fn=lambda x, m: _dot(m, x))
    live_t = (c > 0) | (lax.broadcasted_iota(I32, (n_heads, BLK), 1) >= pad)
    dtt = jnp.where(live_t, _softplus(dt_raw_t + dtbc_ref[...]), 0.0)
    acum_t = _dot_sel(dtt * ac_ref[...], utri_ref[...])
    for t in range(d_in // LANES):
        xt_ref[t * LANES:(t + 1) * LANES, :] = xbc[:, t * LANES:(t + 1) * LANES].T
    a_last = acum_t[:, BLK - 1:BLK]
    ecol_t = jnp.exp(acum_t)
    decs_t = jnp.exp(a_last - acum_t)
    ea_last = jnp.exp(a_last)
    causal_t = (lax.broadcasted_iota(I32, (BLK, BLK), 0) <= lax.broadcasted_iota(I32, (BLK, BLK), 1))
    hp = SSD_HEAD_DIM

    for g in range(SSD_GROUPS):
        bm = xbc[:, d_in + g * D_STATE:d_in + (g + 1) * D_STATE].astype(BF16)
        ct = xbc[:, d_in + gn + g * D_STATE:d_in + gn + (g + 1) * D_STATE].T.astype(BF16)
        cb_t = _dot(bm, ct)
        hprev = h_ref[g * hpg:(g + 1) * hpg].reshape(hpg * hp, D_STATE)
        y_off = _dot(hprev.astype(BF16), ct)
        ws = []
        for hh in range(hpg):
            h = g * hpg + hh
            rows = slice(h * hp, (h + 1) * hp)
            xh = xt_ref[rows, :]
            xdt = xh * dtt[h:h + 1, :]
            decay_t = jnp.exp(jnp.where(causal_t, acum_t[h:h + 1, :] - acum[:, h:h + 1], NEG))
            y = _dot(xdt.astype(BF16), (cb_t * decay_t).astype(BF16))
            y = y + ecol_t[h:h + 1, :] * y_off[hh * hp:(hh + 1) * hp]
            yt_ref[rows, :] = y + dsk_ref[0:1, h:h + 1] * xh
            ws.append((xdt * decs_t[h:h + 1, :]).astype(BF16))
        upd = _dot(jnp.concatenate(ws, axis=0), bm)
        for hh in range(hpg):
            h = g * hpg + hh
            h_ref[h] = ea_last[h:h + 1, :] * hprev[hh * hp:(hh + 1) * hp] + upd[hh * hp:(hh + 1) * hp]

    for t in range(d_in // LANES):
        yb_ref[:, t * LANES:(t + 1) * LANES] = yt_ref[t * LANES:(t + 1) * LANES, :].T

    yg = yb_ref[...] * _silu(z_ref[...])
    gsz = d_in // SSD_GROUPS
    yn = jnp.concatenate(
        [_rms(yg[:, g * gsz:(g + 1) * gsz], gn_ref[:, g * gsz:(g + 1) * gsz]).astype(BF16)
         for g in range(SSD_GROUPS)], axis=1)
    o_ref[...] = x_ref[...] + _dot(yn, wo_ref[...])

    @pl.when(c == pl.num_programs(1) - 1)
    def _():
        st_ref[...] = h_ref[...]
        cv_ref[...] = xpad_ref[0:SUBLANES, :]


def _ssd_prompt(x, g, w_z, w_xbc, w_dt, w_out, conv_w, conv_b, dt_bias, a_log, d_skip, gate_norm,
                bsz, nb, pad, n_heads):
    d = x.shape[1]
    d_in = n_heads * SSD_HEAD_DIM
    cdim = d_in + 2 * SSD_GROUPS * D_STATE
    assert n_heads <= LANES
    hp = LANES - n_heads
    a = -jnp.exp(a_log.astype(F32))
    ltri = jnp.tril(jnp.ones((BLK, BLK), F32)).astype(BF16)
    kern = functools.partial(_ssd_prompt_kernel, pad=pad, n_heads=n_heads)
    const = lambda shape: pl.BlockSpec(shape, lambda b, c: (0,) * len(shape),
                                       pipeline_mode=pl.Buffered(1))
    return pl.pallas_call(
        kern,
        grid=(bsz, nb),
        in_specs=[
            pl.BlockSpec((BLK, d), lambda b, c: (b * nb + c, 0)),
            const((1, d)), const((d, d_in)), const((d, cdim)), const((d, LANES)), const((d_in, d)),
            const((CONV_W, cdim)), const((1, cdim)), const((1, LANES)), const((1, LANES)),
            const((n_heads, 1)), const((n_heads, 1)), const((1, LANES)), const((1, d_in)),
            const((BLK, BLK)), const((BLK, BLK)),
        ],
        out_specs=[pl.BlockSpec((BLK, d), lambda b, c: (b * nb + c, 0)),
                   pl.BlockSpec((None, n_heads, SSD_HEAD_DIM, D_STATE), lambda b, c: (b, 0, 0, 0)),
                   pl.BlockSpec((None, SUBLANES, cdim), lambda b, c: (b, 0, 0))],
        out_shape=[jax.ShapeDtypeStruct((bsz * nb * BLK, d), F32),
                   jax.ShapeDtypeStruct((bsz, n_heads, SSD_HEAD_DIM, D_STATE), F32),
                   jax.ShapeDtypeStruct((bsz, SUBLANES, cdim), F32)],
        scratch_shapes=[pltpu.VMEM((BLK + SUBLANES, cdim), F32),
                        pltpu.VMEM((n_heads, SSD_HEAD_DIM, D_STATE), F32),
                        pltpu.VMEM((BLK, d_in), F32),
                        pltpu.VMEM((d_in, BLK), F32),
                        pltpu.VMEM((d_in, BLK), F32),
                        pltpu.VMEM((BLK, d_in), F32)],
        compiler_params=_cparams(("parallel", "arbitrary")),
        name="ssd_prompt",
    )(x, g.reshape(1, d), w_z, w_xbc, w_dt, w_out, conv_w, conv_b.reshape(1, cdim),
      jnp.pad(dt_bias, (0, hp)).reshape(1, LANES), jnp.pad(a, (0, hp)).reshape(1, LANES),
      dt_bias.reshape(n_heads, 1), a.reshape(n_heads, 1),
      jnp.pad(d_skip, (0, hp)).reshape(1, LANES), gate_norm.reshape(1, d_in), ltri, ltri.T)


def _ssd_sample_kernel(z_ref, xbc_ref, dt_ref, cst_ref, h0_ref, cw_ref, cb_ref, dtb_ref, a_ref,
                       dsk_ref, gn_ref, exp_ref, y_ref, h_ref, *, n_heads):
    d_in = n_heads * SSD_HEAD_DIM
    gn = SSD_GROUPS * D_STATE
    gsz = d_in // SSD_GROUPS
    hpg = n_heads // SSD_GROUPS
    conv = cb_ref[...] + cw_ref[CONV_W - 1:CONV_W, :] * xbc_ref[...]
    for j in range(CONV_W - 1):
        conv = conv + cw_ref[j:j + 1, :] * cst_ref[j:j + 1, :]
    xbc = _silu(conv)
    dt = _softplus(dt_ref[...] + dtb_ref[...])
    pad8 = lambda r: jnp.concatenate([r, jnp.zeros((SUBLANES - 1, r.shape[1]), F32)], axis=0)
    dt_ch = _dot_sel(pad8(dt), exp_ref[...])[0:1]
    da_ch = jnp.exp(_dot_sel(pad8(dt * a_ref[...]), exp_ref[...])[0:1])
    dsk_ch = _dot_sel(pad8(dsk_ref[...]), exp_ref[...])[0:1]
    xh = xbc[:, :d_in]
    xdt = xh * dt_ch
    ones = jnp.ones((SUBLANES, D_STATE), BF16)
    outs = []
    for g in range(SSD_GROUPS):
        sl = slice(g * gsz, (g + 1) * gsz)
        bm = xbc[:, d_in + g * D_STATE:d_in + (g + 1) * D_STATE]
        cm = xbc[:, d_in + gn + g * D_STATE:d_in + gn + (g + 1) * D_STATE]
        da_col = _dot_sel(pad8(da_ch[:, sl]), ones, fn=_dot_t0)
        xdt_col = _dot_sel(pad8(xdt[:, sl]), ones, fn=_dot_t0)
        h0 = h0_ref[g * hpg:(g + 1) * hpg].reshape(gsz, D_STATE)
        hn = da_col * h0 + xdt_col.astype(BF16).astype(F32) * bm.astype(BF16).astype(F32)
        h_ref[g * hpg:(g + 1) * hpg] = hn.reshape(hpg, SSD_HEAD_DIM, D_STATE)
        y_col = jnp.sum(hn * cm, axis=-1, keepdims=True)
        outs.append(y_col)
    y_cols = jnp.concatenate(outs, axis=0)
    rows = []
    eye = (lax.broadcasted_iota(I32, (LANES, LANES), 0)
           == lax.broadcasted_iota(I32, (LANES, LANES), 1)).astype(F32)
    for t in range(d_in // LANES):
        blk = y_cols[t * LANES:(t + 1) * LANES]
        rows.append(jnp.sum(blk * eye, axis=0, keepdims=True))
    y = jnp.concatenate(rows, axis=1) + dsk_ch * xh
    y = y * _silu(z_ref[...])
    for g in range(SSD_GROUPS):
        sl = slice(g * gsz, (g + 1) * gsz)
        y_ref[:, sl] = _rms(y[:, sl], gn_ref[:, sl]).astype(BF16)


def _ssd_sample(z, xbc_raw, dt_raw, conv_state, h0, conv_w, conv_b, dt_bias, a_log, d_skip,
                gate_norm, n_heads):
    db = z.shape[0]
    d_in = n_heads * SSD_HEAD_DIM
    cdim = d_in + 2 * SSD_GROUPS * D_STATE
    hp = LANES - n_heads
    a = -jnp.exp(a_log.astype(F32))
    expand = (jnp.arange(LANES)[:, None] == (jnp.arange(d_in) // SSD_HEAD_DIM)[None, :]).astype(BF16)
    kern = functools.partial(_ssd_sample_kernel, n_heads=n_heads)
    const = lambda shape: pl.BlockSpec(shape, lambda b: (0,) * len(shape))
    per_b = lambda shape: pl.BlockSpec((None,) + shape, lambda b: (b,) + (0,) * len(shape))
    y, h = pl.pallas_call(
        kern,
        grid=(db,),
        in_specs=[per_b((1, d_in)), per_b((1, cdim)), per_b((1, LANES)), per_b((CONV_W - 1, cdim)),
                  per_b((n_heads, SSD_HEAD_DIM, D_STATE)),
                  const((CONV_W, cdim)), const((1, cdim)), const((1, LANES)), const((1, LANES)),
                  const((1, LANES)), const((1, d_in)), const((LANES, d_in))],
        out_specs=[per_b((1, d_in)), per_b((n_heads, SSD_HEAD_DIM, D_STATE))],
        out_shape=[jax.ShapeDtypeStruct((db, 1, d_in), BF16),
                   jax.ShapeDtypeStruct((db, n_heads, SSD_HEAD_DIM, D_STATE), F32)],
        compiler_params=_cparams(("parallel",)),
        name="ssd_sample",
    )(z.reshape(db, 1, d_in), xbc_raw.reshape(db, 1, cdim), dt_raw.reshape(db, 1, LANES),
      conv_state, h0, conv_w, conv_b.reshape(1, cdim),
      jnp.pad(dt_bias, (0, hp)).reshape(1, LANES), jnp.pad(a, (0, hp)).reshape(1, LANES),
      jnp.pad(d_skip, (0, hp)).reshape(1, LANES), gate_norm.reshape(1, d_in), expand)
    return y.reshape(db, d_in), h


def kernel(x_prompt, x_sample, cache_k, cache_v, cache_kidx, page_table, state_ssm, state_conv,
           meta_tokens, norm_ffn_a, w_ffn_a_in, w_ffn_a_out, norm_mix, norm_ffn_b, w_ffn_b_in,
           w_ffn_b_out, w_attn_in, q_norm, k_norm, kidx_norm, w_attn_out,
           w_ssd_in, conv_w, conv_b, dt_bias, a_log, d_skip, gate_norm, w_ssd_out):
    bsz, seq, d = x_prompt.shape
    db = x_sample.shape[0]
    assert x_sample.shape[1] == 1
    t_real = N_META + seq
    nb = -(-t_real // BLK)
    t_pad = nb * BLK
    pad = t_pad - t_real
    npg = page_table.shape[1]
    past = npg * BLK
    topk_p = min(TOPK_MAX, seq // 4)
    topk_s = min(TOPK_MAX, (past + 1) // 4)
    qd, kd, qid = N_HEADS * HEAD_DIM, N_KV_HEADS * HEAD_DIM, N_IDX_HEADS * IDX_DIM
    d_in = w_ssd_out.shape[1]
    n_heads = d_in // SSD_HEAD_DIM
    cdim = d_in + 2 * SSD_GROUPS * D_STATE
    depth = norm_mix.shape[0]

    meta = jnp.broadcast_to(meta_tokens.astype(F32)[None], (bsz, N_META, d))
    xp = jnp.concatenate([jnp.zeros((bsz, pad, d), F32), meta, x_prompt], axis=1)
    xp = xp.reshape(bsz * t_pad, d)
    xs = x_sample.reshape(db, d)

    outs = {k: [] for k in ("kp", "vp", "kip", "ks", "vs", "kis", "hp", "cp", "hs", "cs")}
    for i in range(depth):
        wts = _ffn_weights(w_ffn_a_in[i], w_ffn_a_out[i])
        xp = _ffn(xp, norm_ffn_a[i], wts)
        xs = _ffn(xs, norm_ffn_a[i], wts)
        j = i // 2
        if i % 2 == 0:
            w = w_attn_in[j]
            o1, o2 = qd + 2 * kd + qid, qd + 2 * kd + qid + IDX_DIM
            w_pad = jnp.concatenate(
                [w[:, :o1], jnp.pad(w[:, o1:o2], ((0, 0), (0, LANES - IDX_DIM))),
                 jnp.pad(w[:, o2:], ((0, 0), (0, LANES - N_IDX_HEADS)))], axis=1).astype(BF16)
            w_out = w_attn_out[j].astype(BF16)

            k, v, ki, qt, qit, wt, kg, vt, kib = _attn_in(
                xp, norm_mix[i], w_pad, q_norm[j], k_norm[j], kidx_norm[j], prompt=True)
            gw = HEADS_PER_KV * BLK
            smax = (1.02 * math.sqrt(HEAD_DIM) * jnp.max(jnp.abs(q_norm[j]))
                    * jnp.max(jnp.abs(k_norm[j]))).reshape(1).astype(F32)
            o = _dsa_prompt_pairs(smax,
                            qit.reshape(bsz, nb, IDX_DIM, N_IDX_HEADS * BLK),
                            wt.reshape(bsz, nb, N_IDX_HEADS, BLK),
                            kib.reshape(bsz, nb, BLK, IDX_DIM),
                            qt.reshape(bsz, nb, N_KV_HEADS, HEAD_DIM, gw),
                            kg.reshape(N_KV_HEADS, bsz, nb, BLK, HEAD_DIM),
                            vt.reshape(bsz, nb, N_KV_HEADS, VT_ROWS, BLK),
                            pad, topk_p)
            xp = _out_proj(xp, o.reshape(bsz * t_pad, qd), w_out)
            outs["kp"].append(k.reshape(bsz, t_pad, N_KV_HEADS, HEAD_DIM)[:, pad:])
            outs["vp"].append(v.reshape(bsz, t_pad, N_KV_HEADS, HEAD_DIM)[:, pad:])
            outs["kip"].append(ki[:, :IDX_DIM].reshape(bsz, t_pad, IDX_DIM)[:, pad:])

            q, k, v, qi, ki, wi = _attn_in(
                xs, norm_mix[i], w_pad, q_norm[j], k_norm[j], kidx_norm[j], prompt=False)
            pg = math.gcd(npg, 32)
            qi3 = qi.reshape(db, N_IDX_HEADS, IDX_DIM)
            wi3 = wi[:, :N_IDX_HEADS].reshape(db, N_IDX_HEADS, 1)
            scores = _smp_scores(page_table, qi3, wi3, cache_kidx[j].transpose(0, 2, 1), pg)
            bias, nbias = _smp_select(scores, qi3, wi3, ki[:, :IDX_DIM].reshape(db, 1, IDX_DIM), topk_s)
            hsel = (jnp.arange(N_HEADS)[:, None] // HEADS_PER_KV == jnp.arange(N_KV_HEADS)[None, :])
            q_bd = (q.reshape(db, N_HEADS, 1, HEAD_DIM) * hsel[None, :, :, None].astype(BF16))
            q_bd = q_bd.reshape(db, N_HEADS, kd)
            npool = cache_k.shape[1]
            o = _smp_attend(page_table, q_bd, bias, nbias, k.reshape(db, 1, kd), v.reshape(db, 1, kd),
                            cache_k[j].transpose(0, 2, 3, 1).reshape(npool, kd, BLK),
                            cache_v[j].transpose(0, 2, 3, 1).reshape(npool, kd, BLK), pg)
            xs = _out_proj(xs, o.reshape(db, qd).astype(BF16), w_out)
            outs["ks"].append(k.reshape(db, 1, N_KV_HEADS, HEAD_DIM))
            outs["vs"].append(v.reshape(db, 1, N_KV_HEADS, HEAD_DIM))
            outs["kis"].append(ki[:, :IDX_DIM].reshape(db, 1, IDX_DIM))
        else:
            w = w_ssd_in[j]
            w_main = w[:, :d_in + cdim].astype(BF16)
            w_dt = jnp.pad(w[:, d_in + cdim:], ((0, 0), (0, LANES - n_heads))).astype(BF16)
            w_out = w_ssd_out[j].astype(BF16)
            sp = (conv_w[j], conv_b[j], dt_bias[j], a_log[j], d_skip[j], gate_norm[j])

            xp, hfin, ctail = _ssd_prompt(xp, norm_mix[i], w_main[:, :d_in], w_main[:, d_in:], w_dt, w_out,
                                          *sp, bsz, nb, pad, n_heads)
            outs["hp"].append(hfin)
            outs["cp"].append(ctail[:, SUBLANES - (CONV_W - 1):])

            z, xbc_raw, dt_raw = _norm_proj(xs, norm_mix[i], w_main, w_dt, d_in)
            y, hnew = _ssd_sample(z, xbc_raw, dt_raw, state_conv[j], state_ssm[j], *sp, n_heads)
            xs = _out_proj(xs, y, w_out)
            outs["hs"].append(hnew)
            outs["cs"].append(jnp.concatenate([state_conv[j][:, 1:], xbc_raw[:, None, :]], axis=1))
        wts = _ffn_weights(w_ffn_b_in[i], w_ffn_b_out[i])
        xp = _ffn(xp, norm_ffn_b[i], wts)
        xs = _ffn(xs, norm_ffn_b[i], wts)

    y_prompt = xp.reshape(bsz, t_pad, d)[:, pad + N_META:]
    y_sample = xs.reshape(db, 1, d)
    st = lambda key: jnp.stack(outs[key])
    return (y_prompt, y_sample, st("kp"), st("vp"), st("kip"), st("ks"), st("vs"), st("kis"),
            st("hp"), st("cp"), st("hs"), st("cs"))
```

```python
import functools
import math

import jax
import jax.numpy as jnp
from jax import lax
from jax.experimental import pallas as pl
from jax.experimental.pallas import tpu as pltpu

F32 = jnp.float32
BF16 = jnp.bfloat16
I32 = jnp.int32

N_META = 16
N_HEADS = 16
HEAD_DIM = 64
N_KV_HEADS = 4
HEADS_PER_KV = N_HEADS // N_KV_HEADS
N_IDX_HEADS = 8
IDX_DIM = 64
TOPK_MAX = 256
SSD_HEAD_DIM = 64
SSD_GROUPS = 4
D_STATE = 128
CONV_W = 4
EPS = 1e-6

LANES = 128
SUBLANES = 8
BLK = 128
NEG = -1e30
SAFE_LOGIT = 40.0
VT_ROWS = 80
INT_MIN = -2 ** 31
VMEM_LIMIT = 56 * 1024 * 1024


def _cparams(sem, vmem=VMEM_LIMIT):
    return pltpu.CompilerParams(dimension_semantics=sem, vmem_limit_bytes=vmem)


def _row_tile(rows, pref):
    best = None
    for d in range(SUBLANES, min(rows, pref) + 1, SUBLANES):
        if rows % d == 0:
            best = d
    assert best is not None, rows
    return best


def _rms(x, g):
    var = jnp.mean(x * x, axis=-1, keepdims=True)
    return x * lax.rsqrt(var + EPS) * g


def _dot(a, b):
    return jnp.dot(a, b, preferred_element_type=F32)


def _dot_t0(a, b):
    return lax.dot_general(a, b, (((0,), (0,)), ((), ())), preferred_element_type=F32)


def _dot_t1(a, b):
    return lax.dot_general(a, b, (((1,), (1,)), ((), ())), preferred_element_type=F32)


def _split3(a):
    a0 = a.astype(BF16)
    r = a - a0.astype(F32)
    a1 = r.astype(BF16)
    a2 = (r - a1.astype(F32)).astype(BF16)
    return a0, a1, a2


def _dot_sel(a, m, fn=_dot):
    a0, a1, a2 = _split3(a)
    return fn(a0, m) + fn(a1, m) + fn(a2, m)


def _silu(x):
    return x * (1.0 / (1.0 + jnp.exp(-x)))


def _softplus(x):
    return jnp.maximum(x, 0.0) + jnp.log(1.0 + jnp.exp(-jnp.abs(x)))


def _ffn_kernel(x_ref, g_ref, wi_ref, wo_ref, o_ref, xn_ref, acc_ref):
    hid = wo_ref.shape[0]
    th = 256 if hid % 256 == 0 else LANES
    xn_ref[...] = _rms(x_ref[...], g_ref[...]).astype(BF16)
    acc_ref[...] = jnp.zeros_like(acc_ref)
    for j in range(hid // th):
        xn = xn_ref[...]
        a = _dot(xn, wi_ref[:, j * th:(j + 1) * th])
        b = _dot(xn, wi_ref[:, hid + j * th:hid + (j + 1) * th])
        h = (_silu(a) * b).astype(BF16)
        acc_ref[...] += _dot(h, wo_ref[j * th:(j + 1) * th, :])
    o_ref[...] = x_ref[...] + 0.5 * acc_ref[...]


def _ffn_weights(w_in, w_out):
    return w_in.astype(BF16), w_out.astype(BF16)


def _ffn(x, g, weights):
    w_in, w_out = weights
    rows, d = x.shape
    hid = w_out.shape[0]
    tm = _row_tile(rows, 1024)
    resident = lambda shape: pl.BlockSpec(shape, lambda i: (0,) * len(shape),
                                          pipeline_mode=pl.Buffered(1))
    return pl.pallas_call(
        _ffn_kernel,
        grid=(rows // tm,),
        in_specs=[
            pl.BlockSpec((tm, d), lambda i: (i, 0)),
            resident((1, d)), resident((d, 2 * hid)), resident((hid, d)),
        ],
        out_specs=pl.BlockSpec((tm, d), lambda i: (i, 0)),
        out_shape=jax.ShapeDtypeStruct((rows, d), F32),
        scratch_shapes=[pltpu.VMEM((tm, d), BF16), pltpu.VMEM((tm, d), F32)],
        compiler_params=_cparams(("parallel",)),
        name="ffn",
    )(x, g.reshape(1, d), w_in, w_out)


def _attn_project(x_ref, g_ref, w_ref, qg_ref, kg_ref, kig_ref, gq_ref, eq_ref, gk_ref, ek_ref):
    qd = N_HEADS * HEAD_DIM
    kd = N_KV_HEADS * HEAD_DIM
    qid = N_IDX_HEADS * IDX_DIM
    xn = _rms(x_ref[...], g_ref[...]).astype(BF16)
    h = _dot(xn, w_ref[...])

    def head_norm(t, gsum_ref, gexp_ref, gain):
        ss = _dot_sel(t * t, gsum_ref[...])
        rs = lax.rsqrt(ss * (1.0 / HEAD_DIM) + EPS)
        return t * _dot_sel(rs, gexp_ref[...]) * gain

    o = 0
    q = head_norm(h[:, o:o + qd], gq_ref, eq_ref, qg_ref[...]) * (HEAD_DIM ** -0.5)
    o += qd
    k = head_norm(h[:, o:o + kd], gk_ref, ek_ref, kg_ref[...])
    o += kd
    v = h[:, o:o + kd]
    o += kd
    qi = h[:, o:o + qid] * (IDX_DIM ** -0.5)
    o += qid
    ki = h[:, o:o + LANES]
    var = jnp.sum(ki * ki, axis=-1, keepdims=True) * (1.0 / IDX_DIM)
    ki = ki * lax.rsqrt(var + EPS) * kig_ref[...]
    o += LANES
    wi = h[:, o:o + LANES] * (N_IDX_HEADS ** -0.5)
    return q, k, v, qi, ki, wi


def _attn_in_sample_kernel(*refs):
    q_o, k_o, v_o, qi_o, ki_o, wi_o = refs[10:]
    q, k, v, qi, ki, wi = _attn_project(*refs[:10])
    q_o[...] = q.astype(BF16)
    k_o[...] = k
    v_o[...] = v
    qi_o[...] = qi.astype(BF16)
    ki_o[...] = ki
    wi_o[...] = wi


def _attn_in_prompt_kernel(*refs):
    k_o, v_o, ki_o, qt_o, qit_o, wt_o, kg_o, vt_o, kib_o = refs[10:]
    q, k, v, qi, ki, wi = _attn_project(*refs[:10])
    k_o[...] = k
    v_o[...] = v
    ki_o[...] = ki
    kib_o[...] = ki[:, :IDX_DIM].astype(BF16)
    for g in range(N_KV_HEADS):
        kg_o[g] = k[:, g * HEAD_DIM:(g + 1) * HEAD_DIM].astype(BF16)
    tail = (lax.broadcasted_iota(I32, (VT_ROWS - HEAD_DIM, BLK), 0) == 0).astype(BF16)
    for r in range(q.shape[0] // BLK):
        rows = slice(r * BLK, (r + 1) * BLK)
        for t in range(N_HEADS * HEAD_DIM // LANES):
            tt = q[rows, t * LANES:(t + 1) * LANES].T.astype(BF16)
            for u in range(LANES // HEAD_DIM):
                h = t * (LANES // HEAD_DIM) + u
                g, hh = h // HEADS_PER_KV, h % HEADS_PER_KV
                qt_o[r, g, :, hh * BLK:(hh + 1) * BLK] = tt[u * HEAD_DIM:(u + 1) * HEAD_DIM]
        for t in range(N_IDX_HEADS * IDX_DIM // LANES):
            tt = qi[rows, t * LANES:(t + 1) * LANES].T.astype(BF16)
            for u in range(LANES // IDX_DIM):
                h = t * (LANES // IDX_DIM) + u
                qit_o[r, :, h * BLK:(h + 1) * BLK] = tt[u * IDX_DIM:(u + 1) * IDX_DIM]
        wt_o[r] = wi[rows].T[0:N_IDX_HEADS]
        for t in range(N_KV_HEADS * HEAD_DIM // LANES):
            tt = v[rows, t * LANES:(t + 1) * LANES].T.astype(BF16)
            for u in range(LANES // HEAD_DIM):
                vt_o[r, t * (LANES // HEAD_DIM) + u] = jnp.concatenate(
                    [tt[u * HEAD_DIM:(u + 1) * HEAD_DIM], tail], axis=0)


def _seg_mats(n_heads, hd):
    col = jnp.arange(n_heads * hd) // hd
    gsum = (col[:, None] == jnp.arange(LANES)[None, :]).astype(BF16)
    return gsum, gsum.T


def _attn_in(x, g, w_pad, q_gain, k_gain, ki_gain, prompt):
    rows, d = x.shape
    n = w_pad.shape[1]
    qd, kd, qid = N_HEADS * HEAD_DIM, N_KV_HEADS * HEAD_DIM, N_IDX_HEADS * IDX_DIM
    gw = HEADS_PER_KV * BLK
    tm = _row_tile(rows, 512)
    gq, eq = _seg_mats(N_HEADS, HEAD_DIM)
    gk, ek = _seg_mats(N_KV_HEADS, HEAD_DIM)
    const = lambda shape: pl.BlockSpec(shape, lambda i: (0,) * len(shape))
    rowb = lambda w: pl.BlockSpec((tm, w), lambda i: (i, 0))
    sds = jax.ShapeDtypeStruct
    if prompt:
        assert tm % BLK == 0
        nbk, tb = rows // BLK, tm // BLK
        blkb = lambda *s: pl.BlockSpec((tb,) + s, lambda i: (i,) + (0,) * len(s))
        kern = _attn_in_prompt_kernel
        out_specs = [rowb(kd), rowb(kd), rowb(LANES), blkb(N_KV_HEADS, HEAD_DIM, gw),
                     blkb(IDX_DIM, N_IDX_HEADS * BLK), blkb(N_IDX_HEADS, BLK),
                     pl.BlockSpec((N_KV_HEADS, tm, HEAD_DIM), lambda i: (0, i, 0)),
                     blkb(N_KV_HEADS, VT_ROWS, BLK),
                     rowb(IDX_DIM)]
        out_shape = [sds((rows, kd), F32), sds((rows, kd), F32), sds((rows, LANES), F32),
                     sds((nbk, N_KV_HEADS, HEAD_DIM, gw), BF16),
                     sds((nbk, IDX_DIM, N_IDX_HEADS * BLK), BF16),
                     sds((nbk, N_IDX_HEADS, BLK), F32),
                     sds((N_KV_HEADS, rows, HEAD_DIM), BF16),
                     sds((nbk, N_KV_HEADS, VT_ROWS, BLK), BF16),
                     sds((rows, IDX_DIM), BF16)]
    else:
        kern = _attn_in_sample_kernel
        out_specs = [rowb(qd), rowb(kd), rowb(kd), rowb(qid), rowb(LANES), rowb(LANES)]
        out_shape = [sds((rows, qd), BF16), sds((rows, kd), F32), sds((rows, kd), F32),
                     sds((rows, qid), BF16), sds((rows, LANES), F32), sds((rows, LANES), F32)]
    return pl.pallas_call(
        kern,
        grid=(rows // tm,),
        in_specs=[rowb(d), const((1, d)), const((d, n)), const((1, qd)), const((1, kd)),
                  const((1, LANES)), const((qd, LANES)), const((LANES, qd)),
                  const((kd, LANES)), const((LANES, kd))],
        out_specs=out_specs,
        out_shape=out_shape,
        compiler_params=_cparams(("parallel",)),
        name="attn_in_prompt" if prompt else "attn_in_sample",
    )(x, g.reshape(1, d), w_pad,
      jnp.tile(q_gain, N_HEADS).reshape(1, qd), jnp.tile(k_gain, N_KV_HEADS).reshape(1, kd),
      jnp.pad(ki_gain, (0, LANES - IDX_DIM)).reshape(1, LANES), gq, eq, gk, ek)


def _norm_proj_kernel(x_ref, g_ref, w_ref, ws_ref, oa_ref, ob_ref, os_ref, xn_ref, *, na):
    j = pl.program_id(1)

    @pl.when(j == 0)
    def _():
        xn = _rms(x_ref[...], g_ref[...]).astype(BF16)
        xn_ref[...] = xn
        os_ref[...] = _dot(xn, ws_ref[...])

    r = _dot(xn_ref[...], w_ref[...])

    @pl.when(j < na)
    def _():
        oa_ref[...] = r

    @pl.when(j >= na)
    def _():
        ob_ref[...] = r


def _norm_proj(x, g, w_main, w_side, n_a):
    rows, d = x.shape
    n = w_main.shape[1]
    tm = _row_tile(rows, 1024)
    tn = math.gcd(math.gcd(n_a, n - n_a), 1024)
    na = n_a // tn
    return pl.pallas_call(
        functools.partial(_norm_proj_kernel, na=na),
        grid=(rows // tm, n // tn),
        in_specs=[
            pl.BlockSpec((tm, d), lambda i, j: (i, 0)),
            pl.BlockSpec((1, d), lambda i, j: (0, 0)),
            pl.BlockSpec((d, tn), lambda i, j: (0, j)),
            pl.BlockSpec((d, LANES), lambda i, j: (0, 0)),
        ],
        out_specs=[pl.BlockSpec((tm, tn), lambda i, j: (i, jnp.minimum(j, na - 1))),
                   pl.BlockSpec((tm, tn), lambda i, j: (i, jnp.maximum(j - na, 0))),
                   pl.BlockSpec((tm, LANES), lambda i, j: (i, 0))],
        out_shape=[jax.ShapeDtypeStruct((rows, n_a), F32),
                   jax.ShapeDtypeStruct((rows, n - n_a), F32),
                   jax.ShapeDtypeStruct((rows, LANES), F32)],
        scratch_shapes=[pltpu.VMEM((tm, d), BF16)],
        compiler_params=_cparams(("parallel", "arbitrary")),
        name="norm_proj",
    )(x, g.reshape(1, d), w_main, w_side)


def _out_proj_kernel(x_ref, y_ref, w_ref, o_ref):
    o_ref[...] = x_ref[...] + _dot(y_ref[...], w_ref[...])


def _out_proj(x, y, w):
    rows, d = x.shape
    k = y.shape[1]
    tm = _row_tile(rows, 512)
    return pl.pallas_call(
        _out_proj_kernel,
        grid=(rows // tm,),
        in_specs=[pl.BlockSpec((tm, d), lambda i: (i, 0)),
                  pl.BlockSpec((tm, k), lambda i: (i, 0)),
                  pl.BlockSpec((k, d), lambda i: (0, 0))],
        out_specs=pl.BlockSpec((tm, d), lambda i: (i, 0)),
        out_shape=jax.ShapeDtypeStruct((rows, d), F32),
        compiler_params=_cparams(("parallel",)),
        name="out_proj",
    )(x, y, w)


def _sort_key(score):
    bits = pltpu.bitcast(score, I32)
    return bits ^ ((bits >> 31) & 0x7FFFFFFF)


def _tile_fold(x, op):
    r = x[0:SUBLANES]
    for t in range(1, x.shape[0] // SUBLANES):
        r = op(r, x[t * SUBLANES:(t + 1) * SUBLANES])
    return r


def _dsa_pair_kernel(smax_ref, qita_ref, qitb_ref, wta_ref, wtb_ref, qta_ref, qtb_ref,
                     ki_ref, k_ref, vt_ref, oa_ref, ob_ref,
                     key_ref, acc_ref, qit_s, wt_s, qt_s, *, pad, topk, nb):
    j = pl.program_id(1)
    n_a = j + 1
    blocks = (j, nb - 1 - j)
    ntrip = nb + 1
    ncount = key_ref.shape[0]
    last = ki_ref.shape[0] - 1
    row = lax.broadcasted_iota(I32, (BLK, BLK), 0)
    lane = lax.broadcasted_iota(I32, (BLK, BLK), 1)
    gw = HEADS_PER_KV * BLK

    qit_s[0], qit_s[1] = qita_ref[...], qitb_ref[...]
    wt_s[0], wt_s[1] = wta_ref[...], wtb_ref[...]
    qt_s[0], qt_s[1] = qta_ref[...], qtb_ref[...]

    n_first_max = (nb + 1) // 2

    def trip(t):
        if t >= n_first_max:
            return 1, t - n_a, jnp.minimum(t - n_a, last)
        blk = (t >= n_a).astype(I32)
        c = t - blk * n_a
        return blk, c, jnp.minimum(c, last)

    def pick(blk, a, b):
        if isinstance(blk, int):
            return b if blk else a
        if isinstance(a, tuple):
            return tuple(pick(blk, x, y) for x, y in zip(a, b))
        return jnp.where(blk == 1, b, a)

    for t in range(ncount):
        if t >= ntrip:
            key_ref[t] = jnp.full((BLK, BLK), INT_MIN, I32)
            continue
        blk, c, cr = trip(t)
        dots = _dot(ki_ref[cr], qit_s[blk])
        sc = jnp.zeros((BLK, BLK), F32)
        for h in range(N_IDX_HEADS):
            sc = sc + wt_s[blk, h:h + 1, :] * jnp.maximum(dots[:, h * BLK:(h + 1) * BLK], 0.0)
        s_pos = c * BLK + row
        t_pos = pick(blk, blocks[0], blocks[1]) * BLK + lane
        valid = (s_pos <= t_pos) & (s_pos >= pad)
        key_ref[t] = jnp.where(valid, _sort_key(sc), INT_MIN)

    def count(pred, arg):
        cnt = [jnp.zeros((SUBLANES, BLK), I32), jnp.zeros((SUBLANES, BLK), I32)]
        for t in range(ncount):
            blk, c, _ = trip(t)
            hit = jnp.where(pred(key_ref[t], pick(blk, arg[0], arg[1]), c * BLK + row), 1, 0)
            part = _tile_fold(hit, jnp.add)
            if isinstance(blk, int):
                cnt[blk] = cnt[blk] + part
            else:
                cnt[0] = cnt[0] + jnp.where(blk == 0, part, 0)
                cnt[1] = cnt[1] + jnp.where(blk == 1, part, 0)
        return tuple(jnp.sum(x, axis=0, keepdims=True) for x in cnt)

    zero = jnp.zeros((1, BLK), I32)
    c0 = count(lambda k, a, s: k >= a, (zero, zero))
    thr0 = tuple(jnp.where(x >= topk, 0, INT_MIN).astype(I32) for x in c0)

    def bit_step(it, st):
        thr, n_ge = st[:2], st[2:]
        cand = tuple(x + (jnp.int32(1) << (30 - it)) for x in thr)
        cnt = count(lambda k, a, s: k >= a, cand)
        ok = tuple(x >= topk for x in cnt)
        return (tuple(jnp.where(ok[b], cand[b], thr[b]) for b in range(2))
                + tuple(jnp.where(ok[b], cnt[b], n_ge[b]) for b in range(2)))

    st = lax.fori_loop(0, 31, bit_step, thr0 + c0)
    thr, n_ge = st[:2], st[2:]
    tied = ((n_ge[0] > topk) & (thr[0] > INT_MIN)) | ((n_ge[1] > topk) & (thr[1] > INT_MIN))
    any_tied = jnp.max(tied.astype(I32))
    nbits = (nb * BLK).bit_length()

    def idx_search():
        n_gt = count(lambda k, a, s: k > a, thr)
        need = tuple(topk - x for x in n_gt)

        def step(it, lo):
            cand = tuple(x + (jnp.int32(1) << (nbits - 1 - it)) for x in lo)
            below = count(lambda k, a, s: (k == a[0]) & (s < a[1]),
                          ((thr[0], cand[0]), (thr[1], cand[1])))
            return tuple(jnp.where(below[b] < need[b], cand[b], lo[b]) for b in range(2))
        return lax.fori_loop(0, nbits, step, (zero, zero))

    big = jnp.full((1, BLK), 2 ** 30, I32)
    jcut = lax.cond(any_tied > 0, idx_search, lambda: (big, big))

    def logits(t):
        blk, c, cr = trip(t)
        k = key_ref[t]
        th = pick(blk, thr[0], thr[1])
        sel = (k > th) | ((k == th) & (c * BLK + row <= pick(blk, jcut[0], jcut[1])))
        bias = jnp.where(sel & (k > INT_MIN), 0.0, NEG)
        bb = jnp.concatenate([bias] * HEADS_PER_KV, axis=1)
        return blk, cr, [_dot(k_ref[g, cr], qt_s[blk, g]) + bb for g in range(N_KV_HEADS)]

    def attend(shift):
        acc_ref[...] = jnp.zeros_like(acc_ref)

        def probs(t):
            blk, cr, s = logits(t)
            if shift is not None:
                s = [s[g] - pick(blk, shift[0][g], shift[1][g]) for g in range(N_KV_HEADS)]
            return blk, cr, [jnp.exp(x).astype(BF16) for x in s]

        t = 0
        while t < ntrip:
            if t >= n_first_max and t + 1 < ntrip:
                _, cr0, p0 = probs(t)
                _, cr1, p1 = probs(t + 1)
                for g in range(N_KV_HEADS):
                    vt = jnp.concatenate([vt_ref[cr0, g], vt_ref[cr1, g]], axis=1)
                    acc_ref[1, g] += _dot(vt, jnp.concatenate([p0[g], p1[g]], axis=0))
                t += 2
            else:
                blk, cr, p = probs(t)
                for g in range(N_KV_HEADS):
                    acc_ref[blk, g] += _dot(vt_ref[cr, g], p[g])
                t += 1

    safe = smax_ref[0] <= SAFE_LOGIT

    @pl.when(safe)
    def _():
        attend(None)

    @pl.when(jnp.logical_not(safe))
    def _():
        m = [[jnp.full((SUBLANES, gw), NEG, F32) for _ in range(N_KV_HEADS)] for _ in range(2)]
        for t in range(ntrip):
            blk, _, s = logits(t)
            for g in range(N_KV_HEADS):
                part = _tile_fold(s[g], jnp.maximum)
                m[0][g] = jnp.maximum(m[0][g], jnp.where(blk == 0, part, NEG))
                m[1][g] = jnp.maximum(m[1][g], jnp.where(blk == 1, part, NEG))
        attend([[jnp.max(x, axis=0, keepdims=True) for x in mb] for mb in m])

    for b, o_ref in enumerate((oa_ref, ob_ref)):
        q_row = blocks[b] * BLK + lax.broadcasted_iota(I32, (BLK, LANES), 0)
        for g in range(N_KV_HEADS):
            a = acc_ref[b, g]
            res = a[0:HEAD_DIM] * (1.0 / a[HEAD_DIM:HEAD_DIM + 1])
            for t in range(HEADS_PER_KV // 2):
                two = jnp.concatenate([res[:, (2 * t + u) * BLK:(2 * t + u + 1) * BLK] for u in range(2)],
                                      axis=0)
                two = jnp.where(q_row >= pad, two.T, 0.0)
                lo = (g * HEADS_PER_KV + 2 * t) * HEAD_DIM
                o_ref[:, lo:lo + LANES] = two.astype(BF16)


def _dsa_prompt_pairs(smax, qit, wt, kib, qt, kg, vt, pad, topk):
    bsz, nb = qit.shape[:2]
    steps = (nb + 1) // 2
    gw = HEADS_PER_KV * BLK
    qd = N_HEADS * HEAD_DIM
    ncount = nb + 1
    kern = functools.partial(_dsa_pair_kernel, pad=pad, topk=topk, nb=nb)
    first = lambda *s: pl.BlockSpec((None, None) + s, lambda b, j: (b, j) + (0,) * len(s))
    second = lambda *s: pl.BlockSpec((None, None) + s, lambda b, j: (b, nb - 1 - j) + (0,) * len(s))
    oa, ob = pl.pallas_call(
        kern,
        grid=(bsz, steps),
        in_specs=[
            pl.BlockSpec(memory_space=pltpu.SMEM),
            first(IDX_DIM, N_IDX_HEADS * BLK), second(IDX_DIM, N_IDX_HEADS * BLK),
            first(N_IDX_HEADS, BLK), second(N_IDX_HEADS, BLK),
            first(N_KV_HEADS, HEAD_DIM, gw), second(N_KV_HEADS, HEAD_DIM, gw),
            pl.BlockSpec((None, nb, BLK, IDX_DIM), lambda b, j: (b, 0, 0, 0)),
            pl.BlockSpec((N_KV_HEADS, None, nb, BLK, HEAD_DIM), lambda b, j: (0, b, 0, 0, 0)),
            pl.BlockSpec((None, nb, N_KV_HEADS, VT_ROWS, BLK), lambda b, j: (b, 0, 0, 0, 0)),
        ],
        out_specs=[pl.BlockSpec((None, BLK, qd), lambda b, j: (b, j, 0)),
                   pl.BlockSpec((None, BLK, qd), lambda b, j: (b, steps - 1 - j, 0))],
        out_shape=[jax.ShapeDtypeStruct((bsz, steps * BLK, qd), BF16)] * 2,
        scratch_shapes=[pltpu.VMEM((ncount, BLK, BLK), I32),
                        pltpu.VMEM((2, N_KV_HEADS, VT_ROWS, gw), F32),
                        pltpu.VMEM((2, IDX_DIM, N_IDX_HEADS * BLK), BF16),
                        pltpu.VMEM((2, N_IDX_HEADS, BLK), F32),
                        pltpu.VMEM((2, N_KV_HEADS, HEAD_DIM, gw), BF16)],
        compiler_params=_cparams(("parallel", "arbitrary")),
        name="dsa_prompt",
    )(smax, qit, qit, wt, wt, qt, qt, kib, kg, vt)
    return jnp.concatenate([oa[:, :(nb - steps) * BLK], ob], axis=1)


def _smp_scores_kernel(pt_ref, qi_ref, w_ref, *refs, pg):
    page_refs, o_ref = refs[:pg], refs[pg]
    qi = qi_ref[...]
    w = w_ref[...]
    for p in range(pg):
        d = _dot(qi, page_refs[p][...].astype(BF16))
        o_ref[p:p + 1, :] = jnp.sum(w * jnp.maximum(d, 0.0), axis=0, keepdims=True)


def _smp_scores(page_table, qi, wi, kidx_t, pg):
    db, npg = page_table.shape
    kern = functools.partial(_smp_scores_kernel, pg=pg)
    page_spec = lambda p: pl.BlockSpec(
        (None, IDX_DIM, BLK), lambda b, j, pt: (pt[b * npg + j * pg + p], 0, 0))
    return pl.pallas_call(
        kern,
        grid_spec=pltpu.PrefetchScalarGridSpec(
            num_scalar_prefetch=1,
            grid=(db, npg // pg),
            in_specs=[pl.BlockSpec((None, N_IDX_HEADS, IDX_DIM), lambda b, j, pt: (b, 0, 0)),
                      pl.BlockSpec((None, N_IDX_HEADS, 1), lambda b, j, pt: (b, 0, 0))]
                     + [page_spec(p) for p in range(pg)],
            out_specs=pl.BlockSpec((None, pg, BLK), lambda b, j, pt: (b, j, 0)),
        ),
        out_shape=jax.ShapeDtypeStruct((db, npg, BLK), F32),
        compiler_params=_cparams(("parallel", "arbitrary")),
        name="sample_scores",
    )(page_table.reshape(-1), qi, wi, *([kidx_t] * pg))


def _smp_select_kernel(sc_ref, qi_ref, w_ref, kin_ref, bias_ref, nb_ref, *, topk, past):
    db, npg, _ = sc_ref.shape
    key = _sort_key(sc_ref[...])
    qi = qi_ref[...].astype(F32)
    kn = kin_ref[...].astype(BF16).astype(F32)
    d = jnp.sum(qi * kn, axis=-1, keepdims=True)
    s_new = jnp.sum(w_ref[...] * jnp.maximum(d, 0.0), axis=1, keepdims=True)
    key_new = _sort_key(s_new)
    pos = (lax.broadcasted_iota(I32, (db, npg, BLK), 1) * BLK
           + lax.broadcasted_iota(I32, (db, npg, BLK), 2))

    def count(pred_past, pred_new):
        c = jnp.sum(jnp.where(pred_past, 1, 0), axis=2, keepdims=True)
        return jnp.sum(c, axis=1, keepdims=True) + jnp.where(pred_new, 1, 0)

    thr = jnp.where(count(key >= 0, key_new >= 0) >= topk, 0, INT_MIN).astype(I32)

    def bit_step(it, thr):
        cand = thr + (jnp.int32(1) << (30 - it))
        return jnp.where(count(key >= cand, key_new >= cand) >= topk, cand, thr)

    thr = lax.fori_loop(0, 31, bit_step, thr)
    need = topk - count(key > thr, key_new > thr)
    nbits = max(1, past.bit_length())

    def step(it, lo):
        cand = lo + (jnp.int32(1) << (nbits - 1 - it))
        below = count((key == thr) & (pos < cand), (key_new == thr) & (past < cand))
        return jnp.where(below < need, cand, lo)

    jcut = lax.fori_loop(0, nbits, step, jnp.zeros((db, 1, 1), I32))
    sel = (key > thr) | ((key == thr) & (pos <= jcut))
    bias_ref[...] = jnp.where(sel, 0.0, NEG)
    sel_new = (key_new > thr) | ((key_new == thr) & (past <= jcut))
    nb_ref[...] = jnp.broadcast_to(jnp.where(sel_new, 0.0, NEG), nb_ref.shape)


def _smp_select(scores, qi, wi, ki_new, topk):
    db, npg, _ = scores.shape
    kern = functools.partial(_smp_select_kernel, topk=topk, past=npg * BLK)
    full = lambda *s: pl.BlockSpec(s, lambda i: (0,) * len(s))
    return pl.pallas_call(
        kern,
        grid=(1,),
        in_specs=[full(db, npg, BLK), full(db, N_IDX_HEADS, IDX_DIM), full(db, N_IDX_HEADS, 1),
                  full(db, 1, IDX_DIM)],
        out_specs=[full(db, npg, BLK), full(db, SUBLANES, LANES)],
        out_shape=[jax.ShapeDtypeStruct((db, npg, BLK), F32),
                   jax.ShapeDtypeStruct((db, SUBLANES, LANES), F32)],
        compiler_params=_cparams(("arbitrary",)),
        name="sample_select",
    )(scores, qi, wi, ki_new)


def _smp_attend_kernel(pt_ref, q_ref, bias_ref, nb_ref, kn_ref, vn_ref, *refs, pg):
    k_refs, v_refs = refs[:pg], refs[pg:2 * pg]
    o_ref, m_ref, l_ref, acc_ref = refs[2 * pg:]
    j = pl.program_id(1)
    kd = N_KV_HEADS * HEAD_DIM

    @pl.when(j == 0)
    def _():
        m_ref[...] = jnp.full_like(m_ref, NEG)
        l_ref[...] = jnp.zeros_like(l_ref)
        acc_ref[...] = jnp.zeros_like(acc_ref)

    q = q_ref[...]
    s = [_dot(q, k_refs[p][...].astype(BF16)) + bias_ref[p:p + 1, :] for p in range(pg)]
    m_old = m_ref[...]
    m_new = m_old
    for p in range(pg):
        m_new = jnp.maximum(m_new, jnp.max(s[p], axis=-1, keepdims=True))
    alpha = jnp.exp(m_old - m_new)
    l = l_ref[...] * alpha
    acc = acc_ref[...] * alpha
    for p in range(pg):
        e = jnp.exp(s[p] - m_new)
        l = l + jnp.sum(e, axis=-1, keepdims=True)
        acc = acc + _dot_t1(e.astype(BF16), v_refs[p][...].astype(BF16))
    m_ref[...] = m_new
    l_ref[...] = l
    acc_ref[...] = acc

    @pl.when(j == pl.num_programs(1) - 1)
    def _():
        qf = q.astype(F32)
        kn = kn_ref[...].astype(BF16).astype(F32)
        vn = vn_ref[...].astype(BF16).astype(F32)
        s_new = jnp.sum(qf * kn, axis=-1, keepdims=True) + nb_ref[0:1, 0:1]
        m_fin = jnp.maximum(m_new, s_new)
        a2 = jnp.exp(m_new - m_fin)
        e_new = jnp.exp(s_new - m_fin)
        l_fin = l * a2 + e_new
        acc_fin = acc * a2 + e_new.astype(BF16).astype(F32) * vn
        res = acc_fin / l_fin
        hgrp = lax.broadcasted_iota(I32, (N_HEADS, HEAD_DIM), 0) // HEADS_PER_KV
        out = jnp.zeros((N_HEADS, HEAD_DIM), F32)
        for g in range(N_KV_HEADS):
            out = out + jnp.where(hgrp == g, res[:, g * HEAD_DIM:(g + 1) * HEAD_DIM], 0.0)
        o_ref[...] = out


def _smp_attend(page_table, q_bd, bias, nbias, k_new, v_new, k_t, v_t, pg):
    db, npg = page_table.shape
    kd = N_KV_HEADS * HEAD_DIM
    kern = functools.partial(_smp_attend_kernel, pg=pg)
    page_spec = lambda p: pl.BlockSpec(
        (None, kd, BLK), lambda b, j, pt: (pt[b * npg + j * pg + p], 0, 0))
    per_b = lambda shape: pl.BlockSpec((None,) + shape, lambda b, j, pt: (b, 0, 0))
    return pl.pallas_call(
        kern,
        grid_spec=pltpu.PrefetchScalarGridSpec(
            num_scalar_prefetch=1,
            grid=(db, npg // pg),
            in_specs=[per_b((N_HEADS, kd)),
                      pl.BlockSpec((None, pg, BLK), lambda b, j, pt: (b, j, 0)),
                      per_b((SUBLANES, LANES)), per_b((1, kd)), per_b((1, kd))]
                     + [page_spec(p) for p in range(pg)] * 2,
            out_specs=per_b((N_HEADS, HEAD_DIM)),
            scratch_shapes=[pltpu.VMEM((N_HEADS, 1), F32), pltpu.VMEM((N_HEADS, 1), F32),
                            pltpu.VMEM((N_HEADS, kd), F32)],
        ),
        out_shape=jax.ShapeDtypeStruct((db, N_HEADS, HEAD_DIM), F32),
        compiler_params=_cparams(("parallel", "arbitrary")),
        name="sample_attend",
    )(page_table.reshape(-1), q_bd, bias, nbias, k_new, v_new,
      *([k_t] * pg), *([v_t] * pg))


def _ssd_prompt_kernel(x_ref, g_ref, wz_ref, wx_ref, wdt_ref, wo_ref,
                       cw_ref, cb_ref, dtb_ref, a_ref,
                       dtbc_ref, ac_ref, dsk_ref, gn_ref, ltri_ref, utri_ref,
                       o_ref, st_ref, cv_ref, xpad_ref, h_ref, yb_ref, xt_ref, yt_ref, z_ref,
                       *, pad, n_heads):
    c = pl.program_id(1)
    d_in = n_heads * SSD_HEAD_DIM
    gn = SSD_GROUPS * D_STATE
    hpg = n_heads // SSD_GROUPS

    @pl.when(c == 0)
    def _():
        xpad_ref[0:SUBLANES, :] = jnp.zeros((SUBLANES, xpad_ref.shape[1]), F32)
        h_ref[...] = jnp.zeros_like(h_ref)

    xn = _rms(x_ref[...], g_ref[...]).astype(BF16)
    z_ref[...] = _dot(xn, wz_ref[...])
    dt_raw = _dot(xn, wdt_ref[...])
    dt_raw_t = dt_raw.T[0:n_heads]

    xpad_ref[SUBLANES:, :] = _dot(xn, wx_ref[...])
    conv = cb_ref[...] + cw_ref[CONV_W - 1:CONV_W, :] * xpad_ref[SUBLANES:, :]
    for j in range(CONV_W - 1):
        sh = CONV_W - 1 - j
        conv = conv + cw_ref[j:j + 1, :] * xpad_ref[SUBLANES - sh:SUBLANES - sh + BLK, :]
    xpad_ref[0:SUBLANES, :] = xpad_ref[BLK:BLK + SUBLANES, :]
    xbc = _silu(conv)

    live = (c > 0) | (lax.broadcasted_iota(I32, (BLK, LANES), 0) >= pad)
    dt = jnp.where(live, _softplus(dt_raw + dtb_ref[...]), 0.0)
    acum = _dot_sel(dt * a_ref[...], ltri_ref[...], fn=lambda x, m: _dot(m, x))
    live_t = (c > 0) | (lax.broadcasted_iota(I32, (n_heads, BLK), 1) >= pad)
    dtt = jnp.where(live_t, _softplus(dt_raw_t + dtbc_ref[...]), 0.0)
    acum_t = _dot_sel(dtt * ac_ref[...], utri_ref[...])
    for t in range(d_in // LANES):
        xt_ref[t * LANES:(t + 1) * LANES, :] = xbc[:, t * LANES:(t + 1) * LANES].T
    a_last = acum_t[:, BLK - 1:BLK]
    ecol_t = jnp.exp(acum_t)
    decs_t = jnp.exp(a_last - acum_t)
    ea_last = jnp.exp(a_last)
    causal_t = (lax.broadcasted_iota(I32, (BLK, BLK), 0) <= lax.broadcasted_iota(I32, (BLK, BLK), 1))
    hp = SSD_HEAD_DIM

    for g in range(SSD_GROUPS):
        bm = xbc[:, d_in + g * D_STATE:d_in + (g + 1) * D_STATE].astype(BF16)
        ct = xbc[:, d_in + gn + g * D_STATE:d_in + gn + (g + 1) * D_STATE].T.astype(BF16)
        cb_t = _dot(bm, ct)
        hprev = h_ref[g * hpg:(g + 1) * hpg].reshape(hpg * hp, D_STATE)
        y_off = _dot(hprev.astype(BF16), ct)
        ws = []
        for hh in range(hpg):
            h = g * hpg + hh
            rows = slice(h * hp, (h + 1) * hp)
            xh = xt_ref[rows, :]
            xdt = xh * dtt[h:h + 1, :]
            decay_t = jnp.exp(jnp.where(causal_t, acum_t[h:h + 1, :] - acum[:, h:h + 1], NEG))
            y = _dot(xdt.astype(BF16), (cb_t * decay_t).astype(BF16))
            y = y + ecol_t[h:h + 1, :] * y_off[hh * hp:(hh + 1) * hp]
            yt_ref[rows, :] = y + dsk_ref[0:1, h:h + 1] * xh
            ws.append((xdt * decs_t[h:h + 1, :]).astype(BF16))
        upd = _dot(jnp.concatenate(ws, axis=0), bm)
        for hh in range(hpg):
            h = g * hpg + hh
            h_ref[h] = ea_last[h:h + 1, :] * hprev[hh * hp:(hh + 1) * hp] + upd[hh * hp:(hh + 1) * hp]

    for t in range(d_in // LANES):
        yb_ref[:, t * LANES:(t + 1) * LANES] = yt_ref[t * LANES:(t + 1) * LANES, :].T

    yg = yb_ref[...] * _silu(z_ref[...])
    gsz = d_in // SSD_GROUPS
    yn = jnp.concatenate(
        [_rms(yg[:, g * gsz:(g + 1) * gsz], gn_ref[:, g * gsz:(g + 1) * gsz]).astype(BF16)
         for g in range(SSD_GROUPS)], axis=1)
    o_ref[...] = x_ref[...] + _dot(yn, wo_ref[...])

    @pl.when(c == pl.num_programs(1) - 1)
    def _():
        st_ref[...] = h_ref[...]
        cv_ref[...] = xpad_ref[0:SUBLANES, :]


def _ssd_prompt(x, g, w_z, w_xbc, w_dt, w_out, conv_w, conv_b, dt_bias, a_log, d_skip, gate_norm,
                bsz, nb, pad, n_heads):
    d = x.shape[1]
    d_in = n_heads * SSD_HEAD_DIM
    cdim = d_in + 2 * SSD_GROUPS * D_STATE
    assert n_heads <= LANES
    hp = LANES - n_heads
    a = -jnp.exp(a_log.astype(F32))
    ltri = jnp.tril(jnp.ones((BLK, BLK), F32)).astype(BF16)
    kern = functools.partial(_ssd_prompt_kernel, pad=pad, n_heads=n_heads)
    const = lambda shape: pl.BlockSpec(shape, lambda b, c: (0,) * len(shape),
                                       pipeline_mode=pl.Buffered(1))
    return pl.pallas_call(
        kern,
        grid=(bsz, nb),
        in_specs=[
            pl.BlockSpec((BLK, d), lambda b, c: (b * nb + c, 0)),
            const((1, d)), const((d, d_in)), const((d, cdim)), const((d, LANES)), const((d_in, d)),
            const((CONV_W, cdim)), const((1, cdim)), const((1, LANES)), const((1, LANES)),
            const((n_heads, 1)), const((n_heads, 1)), const((1, LANES)), const((1, d_in)),
            const((BLK, BLK)), const((BLK, BLK)),
        ],
        out_specs=[pl.BlockSpec((BLK, d), lambda b, c: (b * nb + c, 0)),
                   pl.BlockSpec((None, n_heads, SSD_HEAD_DIM, D_STATE), lambda b, c: (b, 0, 0, 0)),
                   pl.BlockSpec((None, SUBLANES, cdim), lambda b, c: (b, 0, 0))],
        out_shape=[jax.ShapeDtypeStruct((bsz * nb * BLK, d), F32),
                   jax.ShapeDtypeStruct((bsz, n_heads, SSD_HEAD_DIM, D_STATE), F32),
                   jax.ShapeDtypeStruct((bsz, SUBLANES, cdim), F32)],
        scratch_shapes=[pltpu.VMEM((BLK + SUBLANES, cdim), F32),
                        pltpu.VMEM((n_heads, SSD_HEAD_DIM, D_STATE), F32),
                        pltpu.VMEM((BLK, d_in), F32),
                        pltpu.VMEM((d_in, BLK), F32),
                        pltpu.VMEM((d_in, BLK), F32),
                        pltpu.VMEM((BLK, d_in), F32)],
        compiler_params=_cparams(("parallel", "arbitrary")),
        name="ssd_prompt",
    )(x, g.reshape(1, d), w_z, w_xbc, w_dt, w_out, conv_w, conv_b.reshape(1, cdim),
      jnp.pad(dt_bias, (0, hp)).reshape(1, LANES), jnp.pad(a, (0, hp)).reshape(1, LANES),
      dt_bias.reshape(n_heads, 1), a.reshape(n_heads, 1),
      jnp.pad(d_skip, (0, hp)).reshape(1, LANES), gate_norm.reshape(1, d_in), ltri, ltri.T)


def _ssd_sample_kernel(z_ref, xbc_ref, dt_ref, cst_ref, h0_ref, cw_ref, cb_ref, dtb_ref, a_ref,
                       dsk_ref, gn_ref, exp_ref, y_ref, h_ref, *, n_heads):
    d_in = n_heads * SSD_HEAD_DIM
    gn = SSD_GROUPS * D_STATE
    gsz = d_in // SSD_GROUPS
    hpg = n_heads // SSD_GROUPS
    conv = cb_ref[...] + cw_ref[CONV_W - 1:CONV_W, :] * xbc_ref[...]
    for j in range(CONV_W - 1):
        conv = conv + cw_ref[j:j + 1, :] * cst_ref[j:j + 1, :]
    xbc = _silu(conv)
    dt = _softplus(dt_ref[...] + dtb_ref[...])
    pad8 = lambda r: jnp.concatenate([r, jnp.zeros((SUBLANES - 1, r.shape[1]), F32)], axis=0)
    dt_ch = _dot_sel(pad8(dt), exp_ref[...])[0:1]
    da_ch = jnp.exp(_dot_sel(pad8(dt * a_ref[...]), exp_ref[...])[0:1])
    dsk_ch = _dot_sel(pad8(dsk_ref[...]), exp_ref[...])[0:1]
    xh = xbc[:, :d_in]
    xdt = xh * dt_ch
    ones = jnp.ones((SUBLANES, D_STATE), BF16)
    outs = []
    for g in range(SSD_GROUPS):
        sl = slice(g * gsz, (g + 1) * gsz)
        bm = xbc[:, d_in + g * D_STATE:d_in + (g + 1) * D_STATE]
        cm = xbc[:, d_in + gn + g * D_STATE:d_in + gn + (g + 1) * D_STATE]
        da_col = _dot_sel(pad8(da_ch[:, sl]), ones, fn=_dot_t0)
        xdt_col = _dot_sel(pad8(xdt[:, sl]), ones, fn=_dot_t0)
        h0 = h0_ref[g * hpg:(g + 1) * hpg].reshape(gsz, D_STATE)
        hn = da_col * h0 + xdt_col.astype(BF16).astype(F32) * bm.astype(BF16).astype(F32)
        h_ref[g * hpg:(g + 1) * hpg] = hn.reshape(hpg, SSD_HEAD_DIM, D_STATE)
        y_col = jnp.sum(hn * cm, axis=-1, keepdims=True)
        outs.append(y_col)
    y_cols = jnp.concatenate(outs, axis=0)
    rows = []
    eye = (lax.broadcasted_iota(I32, (LANES, LANES), 0)
           == lax.broadcasted_iota(I32, (LANES, LANES), 1)).astype(F32)
    for t in range(d_in // LANES):
        blk = y_cols[t * LANES:(t + 1) * LANES]
        rows.append(jnp.sum(blk * eye, axis=0, keepdims=True))
    y = jnp.concatenate(rows, axis=1) + dsk_ch * xh
    y = y * _silu(z_ref[...])
    for g in range(SSD_GROUPS):
        sl = slice(g * gsz, (g + 1) * gsz)
        y_ref[:, sl] = _rms(y[:, sl], gn_ref[:, sl]).astype(BF16)


def _ssd_sample(z, xbc_raw, dt_raw, conv_state, h0, conv_w, conv_b, dt_bias, a_log, d_skip,
                gate_norm, n_heads):
    db = z.shape[0]
    d_in = n_heads * SSD_HEAD_DIM
    cdim = d_in + 2 * SSD_GROUPS * D_STATE
    hp = LANES - n_heads
    a = -jnp.exp(a_log.astype(F32))
    expand = (jnp.arange(LANES)[:, None] == (jnp.arange(d_in) // SSD_HEAD_DIM)[None, :]).astype(BF16)
    kern = functools.partial(_ssd_sample_kernel, n_heads=n_heads)
    const = lambda shape: pl.BlockSpec(shape, lambda b: (0,) * len(shape))
    per_b = lambda shape: pl.BlockSpec((None,) + shape, lambda b: (b,) + (0,) * len(shape))
    y, h = pl.pallas_call(
        kern,
        grid=(db,),
        in_specs=[per_b((1, d_in)), per_b((1, cdim)), per_b((1, LANES)), per_b((CONV_W - 1, cdim)),
                  per_b((n_heads, SSD_HEAD_DIM, D_STATE)),
                  const((CONV_W, cdim)), const((1, cdim)), const((1, LANES)), const((1, LANES)),
                  const((1, LANES)), const((1, d_in)), const((LANES, d_in))],
        out_specs=[per_b((1, d_in)), per_b((n_heads, SSD_HEAD_DIM, D_STATE))],
        out_shape=[jax.ShapeDtypeStruct((db, 1, d_in), BF16),
                   jax.ShapeDtypeStruct((db, n_heads, SSD_HEAD_DIM, D_STATE), F32)],
        compiler_params=_cparams(("parallel",)),
        name="ssd_sample",
    )(z.reshape(db, 1, d_in), xbc_raw.reshape(db, 1, cdim), dt_raw.reshape(db, 1, LANES),
      conv_state, h0, conv_w, conv_b.reshape(1, cdim),
      jnp.pad(dt_bias, (0, hp)).reshape(1, LANES), jnp.pad(a, (0, hp)).reshape(1, LANES),
      jnp.pad(d_skip, (0, hp)).reshape(1, LANES), gate_norm.reshape(1, d_in), expand)
    return y.reshape(db, d_in), h


def kernel(x_prompt, x_sample, cache_k, cache_v, cache_kidx, page_table, state_ssm, state_conv,
           meta_tokens, norm_ffn_a, w_ffn_a_in, w_ffn_a_out, norm_mix, norm_ffn_b, w_ffn_b_in,
           w_ffn_b_out, w_attn_in, q_norm, k_norm, kidx_norm, w_attn_out,
           w_ssd_in, conv_w, conv_b, dt_bias, a_log, d_skip, gate_norm, w_ssd_out):
    bsz, seq, d = x_prompt.shape
    db = x_sample.shape[0]
    assert x_sample.shape[1] == 1
    t_real = N_META + seq
    nb = -(-t_real // BLK)
    t_pad = nb * BLK
    pad = t_pad - t_real
    npg = page_table.shape[1]
    past = npg * BLK
    topk_p = min(TOPK_MAX, seq // 4)
    topk_s = min(TOPK_MAX, (past + 1) // 4)
    qd, kd, qid = N_HEADS * HEAD_DIM, N_KV_HEADS * HEAD_DIM, N_IDX_HEADS * IDX_DIM
    d_in = w_ssd_out.shape[1]
    n_heads = d_in // SSD_HEAD_DIM
    cdim = d_in + 2 * SSD_GROUPS * D_STATE
    depth = norm_mix.shape[0]

    meta = jnp.broadcast_to(meta_tokens.astype(F32)[None], (bsz, N_META, d))
    xp = jnp.concatenate([jnp.zeros((bsz, pad, d), F32), meta, x_prompt], axis=1)
    xp = xp.reshape(bsz * t_pad, d)
    xs = x_sample.reshape(db, d)

    outs = {k: [] for k in ("kp", "vp", "kip", "ks", "vs", "kis", "hp", "cp", "hs", "cs")}
    for i in range(depth):
        wts = _ffn_weights(w_ffn_a_in[i], w_ffn_a_out[i])
        xp = _ffn(xp, norm_ffn_a[i], wts)
        xs = _ffn(xs, norm_ffn_a[i], wts)
        j = i // 2
        if i % 2 == 0:
            w = w_attn_in[j]
            o1, o2 = qd + 2 * kd + qid, qd + 2 * kd + qid + IDX_DIM
            w_pad = jnp.concatenate(
                [w[:, :o1], jnp.pad(w[:, o1:o2], ((0, 0), (0, LANES - IDX_DIM))),
                 jnp.pad(w[:, o2:], ((0, 0), (0, LANES - N_IDX_HEADS)))], axis=1).astype(BF16)
            w_out = w_attn_out[j].astype(BF16)

            k, v, ki, qt, qit, wt, kg, vt, kib = _attn_in(
                xp, norm_mix[i], w_pad, q_norm[j], k_norm[j], kidx_norm[j], prompt=True)
            gw = HEADS_PER_KV * BLK
            smax = (1.02 * math.sqrt(HEAD_DIM) * jnp.max(jnp.abs(q_norm[j]))
                    * jnp.max(jnp.abs(k_norm[j]))).reshape(1).astype(F32)
            o = _dsa_prompt_pairs(smax,
                            qit.reshape(bsz, nb, IDX_DIM, N_IDX_HEADS * BLK),
                            wt.reshape(bsz, nb, N_IDX_HEADS, BLK),
                            kib.reshape(bsz, nb, BLK, IDX_DIM),
                            qt.reshape(bsz, nb, N_KV_HEADS, HEAD_DIM, gw),
                            kg.reshape(N_KV_HEADS, bsz, nb, BLK, HEAD_DIM),
                            vt.reshape(bsz, nb, N_KV_HEADS, VT_ROWS, BLK),
                            pad, topk_p)
            xp = _out_proj(xp, o.reshape(bsz * t_pad, qd), w_out)
            outs["kp"].append(k.reshape(bsz, t_pad, N_KV_HEADS, HEAD_DIM)[:, pad:])
            outs["vp"].append(v.reshape(bsz, t_pad, N_KV_HEADS, HEAD_DIM)[:, pad:])
            outs["kip"].append(ki[:, :IDX_DIM].reshape(bsz, t_pad, IDX_DIM)[:, pad:])

            q, k, v, qi, ki, wi = _attn_in(
                xs, norm_mix[i], w_pad, q_norm[j], k_norm[j], kidx_norm[j], prompt=False)
            pg = math.gcd(npg, 32)
            qi3 = qi.reshape(db, N_IDX_HEADS, IDX_DIM)
            wi3 = wi[:, :N_IDX_HEADS].reshape(db, N_IDX_HEADS, 1)
            scores = _smp_scores(page_table, qi3, wi3, cache_kidx[j].transpose(0, 2, 1), pg)
            bias, nbias = _smp_select(scores, qi3, wi3, ki[:, :IDX_DIM].reshape(db, 1, IDX_DIM), topk_s)
            hsel = (jnp.arange(N_HEADS)[:, None] // HEADS_PER_KV == jnp.arange(N_KV_HEADS)[None, :])
            q_bd = (q.reshape(db, N_HEADS, 1, HEAD_DIM) * hsel[None, :, :, None].astype(BF16))
            q_bd = q_bd.reshape(db, N_HEADS, kd)
            npool = cache_k.shape[1]
            o = _smp_attend(page_table, q_bd, bias, nbias, k.reshape(db, 1, kd), v.reshape(db, 1, kd),
                            cache_k[j].transpose(0, 2, 3, 1).reshape(npool, kd, BLK),
                            cache_v[j].transpose(0, 2, 3, 1).reshape(npool, kd, BLK), pg)
            xs = _out_proj(xs, o.reshape(db, qd).astype(BF16), w_out)
            outs["ks"].append(k.reshape(db, 1, N_KV_HEADS, HEAD_DIM))
            outs["vs"].append(v.reshape(db, 1, N_KV_HEADS, HEAD_DIM))
            outs["kis"].append(ki[:, :IDX_DIM].reshape(db, 1, IDX_DIM))
        else:
            w = w_ssd_in[j]
            w_main = w[:, :d_in + cdim].astype(BF16)
            w_dt = jnp.pad(w[:, d_in + cdim:], ((0, 0), (0, LANES - n_heads))).astype(BF16)
            w_out = w_ssd_out[j].astype(BF16)
            sp = (conv_w[j], conv_b[j], dt_bias[j], a_log[j], d_skip[j], gate_norm[j])

            xp, hfin, ctail = _ssd_prompt(xp, norm_mix[i], w_main[:, :d_in], w_main[:, d_in:], w_dt, w_out,
                                          *sp, bsz, nb, pad, n_heads)
            outs["hp"].append(hfin)
            outs["cp"].append(ctail[:, SUBLANES - (CONV_W - 1):])

            z, xbc_raw, dt_raw = _norm_proj(xs, norm_mix[i], w_main, w_dt, d_in)
            y, hnew = _ssd_sample(z, xbc_raw, dt_raw, state_conv[j], state_ssm[j], *sp, n_heads)
            xs = _out_proj(xs, y, w_out)
            outs["hs"].append(hnew)
            outs["cs"].append(jnp.concatenate([state_conv[j][:, 1:], xbc_raw[:, None, :]], axis=1))
        wts = _ffn_weights(w_ffn_b_in[i], w_ffn_b_out[i])
        xp = _ffn(xp, norm_ffn_b[i], wts)
        xs = _ffn(xs, norm_ffn_b[i], wts)

    y_prompt = xp.reshape(bsz, t_pad, d)[:, pad + N_META:]
    y_sample = xs.reshape(db, 1, d)
    st = lambda key: jnp.stack(outs[key])
    return (y_prompt, y_sample, st("kp"), st("vp"), st("kip"), st("ks"), st("vs"), st("kis"),
            st("hp"), st("cp"), st("hs"), st("cs"))
```

```python
import functools
import math

import jax
import jax.numpy as jnp
from jax import lax
from jax.experimental import pallas as pl
from jax.experimental.pallas import tpu as pltpu

F32 = jnp.float32
BF16 = jnp.bfloat16
I32 = jnp.int32

N_META = 16
N_HEADS = 16
HEAD_DIM = 64
N_KV_HEADS = 4
HEADS_PER_KV = N_HEADS // N_KV_HEADS
N_IDX_HEADS = 8
IDX_DIM = 64
TOPK_MAX = 256
SSD_HEAD_DIM = 64
SSD_GROUPS = 4
D_STATE = 128
CONV_W = 4
EPS = 1e-6

LANES = 128
SUBLANES = 8
BLK = 128
NEG = -1e30
SAFE_LOGIT = 40.0
VT_ROWS = 80
INT_MIN = -2 ** 31
VMEM_LIMIT = 56 * 1024 * 1024


def _cparams(sem, vmem=VMEM_LIMIT):
    return pltpu.CompilerParams(dimension_semantics=sem, vmem_limit_bytes=vmem)


def _row_tile(rows, pref):
    best = None
    for d in range(SUBLANES, min(rows, pref) + 1, SUBLANES):
        if rows % d == 0:
            best = d
    assert best is not None, rows
    return best


def _rms(x, g):
    var = jnp.mean(x * x, axis=-1, keepdims=True)
    return x * lax.rsqrt(var + EPS) * g


def _dot(a, b):
    return jnp.dot(a, b, preferred_element_type=F32)


def _dot_t0(a, b):
    return lax.dot_general(a, b, (((0,), (0,)), ((), ())), preferred_element_type=F32)


def _dot_t1(a, b):
    return lax.dot_general(a, b, (((1,), (1,)), ((), ())), preferred_element_type=F32)


def _split3(a):
    a0 = a.astype(BF16)
    r = a - a0.astype(F32)
    a1 = r.astype(BF16)
    a2 = (r - a1.astype(F32)).astype(BF16)
    return a0, a1, a2


def _dot_sel(a, m, fn=_dot):
    a0, a1, a2 = _split3(a)
    return fn(a0, m) + fn(a1, m) + fn(a2, m)


def _silu(x):
    return x * (1.0 / (1.0 + jnp.exp(-x)))


def _softplus(x):
    return jnp.maximum(x, 0.0) + jnp.log(1.0 + jnp.exp(-jnp.abs(x)))


def _ffn_kernel(x_ref, g_ref, wi_ref, wo_ref, o_ref, xn_ref, acc_ref):
    hid = wo_ref.shape[0]
    th = 256 if hid % 256 == 0 else LANES
    xn_ref[...] = _rms(x_ref[...], g_ref[...]).astype(BF16)
    acc_ref[...] = jnp.zeros_like(acc_ref)
    for j in range(hid // th):
        xn = xn_ref[...]
        a = _dot(xn, wi_ref[:, j * th:(j + 1) * th])
        b = _dot(xn, wi_ref[:, hid + j * th:hid + (j + 1) * th])
        h = (_silu(a) * b).astype(BF16)
        acc_ref[...] += _dot(h, wo_ref[j * th:(j + 1) * th, :])
    o_ref[...] = x_ref[...] + 0.5 * acc_ref[...]


def _ffn_weights(w_in, w_out):
    return w_in.astype(BF16), w_out.astype(BF16)


def _ffn(x, g, weights):
    w_in, w_out = weights
    rows, d = x.shape
    hid = w_out.shape[0]
    tm = _row_tile(rows, 1024)
    resident = lambda shape: pl.BlockSpec(shape, lambda i: (0,) * len(shape),
                                          pipeline_mode=pl.Buffered(1))
    return pl.pallas_call(
        _ffn_kernel,
        grid=(rows // tm,),
        in_specs=[
            pl.BlockSpec((tm, d), lambda i: (i, 0)),
            resident((1, d)), resident((d, 2 * hid)), resident((hid, d)),
        ],
        out_specs=pl.BlockSpec((tm, d), lambda i: (i, 0)),
        out_shape=jax.ShapeDtypeStruct((rows, d), F32),
        scratch_shapes=[pltpu.VMEM((tm, d), BF16), pltpu.VMEM((tm, d), F32)],
        compiler_params=_cparams(("parallel",)),
        name="ffn",
    )(x, g.reshape(1, d), w_in, w_out)


def _attn_project(x_ref, g_ref, w_ref, qg_ref, kg_ref, kig_ref, gq_ref, eq_ref, gk_ref, ek_ref):
    qd = N_HEADS * HEAD_DIM
    kd = N_KV_HEADS * HEAD_DIM
    qid = N_IDX_HEADS * IDX_DIM
    xn = _rms(x_ref[...], g_ref[...]).astype(BF16)
    h = _dot(xn, w_ref[...])

    def head_norm(t, gsum_ref, gexp_ref, gain):
        ss = _dot_sel(t * t, gsum_ref[...])
        rs = lax.rsqrt(ss * (1.0 / HEAD_DIM) + EPS)
        return t * _dot_sel(rs, gexp_ref[...]) * gain

    o = 0
    q = head_norm(h[:, o:o + qd], gq_ref, eq_ref, qg_ref[...]) * (HEAD_DIM ** -0.5)
    o += qd
    k = head_norm(h[:, o:o + kd], gk_ref, ek_ref, kg_ref[...])
    o += kd
    v = h[:, o:o + kd]
    o += kd
    qi = h[:, o:o + qid] * (IDX_DIM ** -0.5)
    o += qid
    ki = h[:, o:o + LANES]
    var = jnp.sum(ki * ki, axis=-1, keepdims=True) * (1.0 / IDX_DIM)
    ki = ki * lax.rsqrt(var + EPS) * kig_ref[...]
    o += LANES
    wi = h[:, o:o + LANES] * (N_IDX_HEADS ** -0.5)
    return q, k, v, qi, ki, wi


def _attn_in_sample_kernel(*refs):
    q_o, k_o, v_o, qi_o, ki_o, wi_o = refs[10:]
    q, k, v, qi, ki, wi = _attn_project(*refs[:10])
    q_o[...] = q.astype(BF16)
    k_o[...] = k
    v_o[...] = v
    qi_o[...] = qi.astype(BF16)
    ki_o[...] = ki
    wi_o[...] = wi


def _attn_in_prompt_kernel(*refs):
    k_o, v_o, ki_o, qt_o, qit_o, wt_o, kg_o, vt_o, kib_o = refs[10:]
    q, k, v, qi, ki, wi = _attn_project(*refs[:10])
    k_o[...] = k
    v_o[...] = v
    ki_o[...] = ki
    kib_o[...] = ki[:, :IDX_DIM].astype(BF16)
    for g in range(N_KV_HEADS):
        kg_o[g] = k[:, g * HEAD_DIM:(g + 1) * HEAD_DIM].astype(BF16)
    tail = (lax.broadcasted_iota(I32, (VT_ROWS - HEAD_DIM, BLK), 0) == 0).astype(BF16)
    for r in range(q.shape[0] // BLK):
        rows = slice(r * BLK, (r + 1) * BLK)
        for t in range(N_HEADS * HEAD_DIM // LANES):
            tt = q[rows, t * LANES:(t + 1) * LANES].T.astype(BF16)
            for u in range(LANES // HEAD_DIM):
                h = t * (LANES // HEAD_DIM) + u
                g, hh = h // HEADS_PER_KV, h % HEADS_PER_KV
                qt_o[r, g, :, hh * BLK:(hh + 1) * BLK] = tt[u * HEAD_DIM:(u + 1) * HEAD_DIM]
        for t in range(N_IDX_HEADS * IDX_DIM // LANES):
            tt = qi[rows, t * LANES:(t + 1) * LANES].T.astype(BF16)
            for u in range(LANES // IDX_DIM):
                h = t * (LANES // IDX_DIM) + u
                qit_o[r, :, h * BLK:(h + 1) * BLK] = tt[u * IDX_DIM:(u + 1) * IDX_DIM]
        wt_o[r] = wi[rows].T[0:N_IDX_HEADS]
        for t in range(N_KV_HEADS * HEAD_DIM // LANES):
            tt = v[rows, t * LANES:(t + 1) * LANES].T.astype(BF16)
            for u in range(LANES // HEAD_DIM):
                vt_o[r, t * (LANES // HEAD_DIM) + u] = jnp.concatenate(
                    [tt[u * HEAD_DIM:(u + 1) * HEAD_DIM], tail], axis=0)


def _seg_mats(n_heads, hd):
    col = jnp.arange(n_heads * hd) // hd
    gsum = (col[:, None] == jnp.arange(LANES)[None, :]).astype(BF16)
    return gsum, gsum.T


def _attn_in(x, g, w_pad, q_gain, k_gain, ki_gain, prompt):
    rows, d = x.shape
    n = w_pad.shape[1]
    qd, kd, qid = N_HEADS * HEAD_DIM, N_KV_HEADS * HEAD_DIM, N_IDX_HEADS * IDX_DIM
    gw = HEADS_PER_KV * BLK
    tm = _row_tile(rows, 512)
    gq, eq = _seg_mats(N_HEADS, HEAD_DIM)
    gk, ek = _seg_mats(N_KV_HEADS, HEAD_DIM)
    const = lambda shape: pl.BlockSpec(shape, lambda i: (0,) * len(shape))
    rowb = lambda w: pl.BlockSpec((tm, w), lambda i: (i, 0))
    sds = jax.ShapeDtypeStruct
    if prompt:
        assert tm % BLK == 0
        nbk, tb = rows // BLK, tm // BLK
        blkb = lambda *s: pl.BlockSpec((tb,) + s, lambda i: (i,) + (0,) * len(s))
        kern = _attn_in_prompt_kernel
        out_specs = [rowb(kd), rowb(kd), rowb(LANES), blkb(N_KV_HEADS, HEAD_DIM, gw),
                     blkb(IDX_DIM, N_IDX_HEADS * BLK), blkb(N_IDX_HEADS, BLK),
                     pl.BlockSpec((N_KV_HEADS, tm, HEAD_DIM), lambda i: (0, i, 0)),
                     blkb(N_KV_HEADS, VT_ROWS, BLK),
                     rowb(IDX_DIM)]
        out_shape = [sds((rows, kd), F32), sds((rows, kd), F32), sds((rows, LANES), F32),
                     sds((nbk, N_KV_HEADS, HEAD_DIM, gw), BF16),
                     sds((nbk, IDX_DIM, N_IDX_HEADS * BLK), BF16),
                     sds((nbk, N_IDX_HEADS, BLK), F32),
                     sds((N_KV_HEADS, rows, HEAD_DIM), BF16),
                     sds((nbk, N_KV_HEADS, VT_ROWS, BLK), BF16),
                     sds((rows, IDX_DIM), BF16)]
    else:
        kern = _attn_in_sample_kernel
        out_specs = [rowb(qd), rowb(kd), rowb(kd), rowb(qid), rowb(LANES), rowb(LANES)]
        out_shape = [sds((rows, qd), BF16), sds((rows, kd), F32), sds((rows, kd), F32),
                     sds((rows, qid), BF16), sds((rows, LANES), F32), sds((rows, LANES), F32)]
    return pl.pallas_call(
        kern,
        grid=(rows // tm,),
        in_specs=[rowb(d), const((1, d)), const((d, n)), const((1, qd)), const((1, kd)),
                  const((1, LANES)), const((qd, LANES)), const((LANES, qd)),
                  const((kd, LANES)), const((LANES, kd))],
        out_specs=out_specs,
        out_shape=out_shape,
        compiler_params=_cparams(("parallel",)),
        name="attn_in_prompt" if prompt else "attn_in_sample",
    )(x, g.reshape(1, d), w_pad,
      jnp.tile(q_gain, N_HEADS).reshape(1, qd), jnp.tile(k_gain, N_KV_HEADS).reshape(1, kd),
      jnp.pad(ki_gain, (0, LANES - IDX_DIM)).reshape(1, LANES), gq, eq, gk, ek)


def _norm_proj_kernel(x_ref, g_ref, w_ref, ws_ref, oa_ref, ob_ref, os_ref, xn_ref, *, na):
    j = pl.program_id(1)

    @pl.when(j == 0)
    def _():
        xn = _rms(x_ref[...], g_ref[...]).astype(BF16)
        xn_ref[...] = xn
        os_ref[...] = _dot(xn, ws_ref[...])

    r = _dot(xn_ref[...], w_ref[...])

    @pl.when(j < na)
    def _():
        oa_ref[...] = r

    @pl.when(j >= na)
    def _():
        ob_ref[...] = r


def _norm_proj(x, g, w_main, w_side, n_a):
    rows, d = x.shape
    n = w_main.shape[1]
    tm = _row_tile(rows, 1024)
    tn = math.gcd(math.gcd(n_a, n - n_a), 1024)
    na = n_a // tn
    return pl.pallas_call(
        functools.partial(_norm_proj_kernel, na=na),
        grid=(rows // tm, n // tn),
        in_specs=[
            pl.BlockSpec((tm, d), lambda i, j: (i, 0)),
            pl.BlockSpec((1, d), lambda i, j: (0, 0)),
            pl.BlockSpec((d, tn), lambda i, j: (0, j)),
            pl.BlockSpec((d, LANES), lambda i, j: (0, 0)),
        ],
        out_specs=[pl.BlockSpec((tm, tn), lambda i, j: (i, jnp.minimum(j, na - 1))),
                   pl.BlockSpec((tm, tn), lambda i, j: (i, jnp.maximum(j - na, 0))),
                   pl.BlockSpec((tm, LANES), lambda i, j: (i, 0))],
        out_shape=[jax.ShapeDtypeStruct((rows, n_a), F32),
                   jax.ShapeDtypeStruct((rows, n - n_a), F32),
                   jax.ShapeDtypeStruct((rows, LANES), F32)],
        scratch_shapes=[pltpu.VMEM((tm, d), BF16)],
        compiler_params=_cparams(("parallel", "arbitrary")),
        name="norm_proj",
    )(x, g.reshape(1, d), w_main, w_side)


def _out_proj_kernel(x_ref, y_ref, w_ref, o_ref):
    o_ref[...] = x_ref[...] + _dot(y_ref[...], w_ref[...])


def _out_proj(x, y, w):
    rows, d = x.shape
    k = y.shape[1]
    tm = _row_tile(rows, 512)
    return pl.pallas_call(
        _out_proj_kernel,
        grid=(rows // tm,),
        in_specs=[pl.BlockSpec((tm, d), lambda i: (i, 0)),
                  pl.BlockSpec((tm, k), lambda i: (i, 0)),
                  pl.BlockSpec((k, d), lambda i: (0, 0))],
        out_specs=pl.BlockSpec((tm, d), lambda i: (i, 0)),
        out_shape=jax.ShapeDtypeStruct((rows, d), F32),
        compiler_params=_cparams(("parallel",)),
        name="out_proj",
    )(x, y, w)


def _sort_key(score):
    bits = pltpu.bitcast(score, I32)
    return bits ^ ((bits >> 31) & 0x7FFFFFFF)


def _tile_fold(x, op):
    r = x[0:SUBLANES]
    for t in range(1, x.shape[0] // SUBLANES):
        r = op(r, x[t * SUBLANES:(t + 1) * SUBLANES])
    return r


def _dsa_pair_kernel(smax_ref, qita_ref, qitb_ref, wta_ref, wtb_ref, qta_ref, qtb_ref,
                     ki_ref, k_ref, vt_ref, oa_ref, ob_ref,
                     key_ref, acc_ref, qit_s, wt_s, qt_s, *, pad, topk, nb):
    j = pl.program_id(1)
    n_a = j + 1
    blocks = (j, nb - 1 - j)
    ntrip = nb + 1
    ncount = key_ref.shape[0]
    last = ki_ref.shape[0] - 1
    row = lax.broadcasted_iota(I32, (BLK, BLK), 0)
    lane = lax.broadcasted_iota(I32, (BLK, BLK), 1)
    gw = HEADS_PER_KV * BLK

    qit_s[0], qit_s[1] = qita_ref[...], qitb_ref[...]
    wt_s[0], wt_s[1] = wta_ref[...], wtb_ref[...]
    qt_s[0], qt_s[1] = qta_ref[...], qtb_ref[...]

    n_first_max = (nb + 1) // 2

    def trip(t):
        if t >= n_first_max:
            return 1, t - n_a, jnp.minimum(t - n_a, last)
        blk = (t >= n_a).astype(I32)
        c = t - blk * n_a
        return blk, c, jnp.minimum(c, last)

    def pick(blk, a, b):
        if isinstance(blk, int):
            return b if blk else a
        if isinstance(a, tuple):
            return tuple(pick(blk, x, y) for x, y in zip(a, b))
        return jnp.where(blk == 1, b, a)

    for t in range(ncount):
        if t >= ntrip:
            key_ref[t] = jnp.full((BLK, BLK), INT_MIN, I32)
            continue
        blk, c, cr = trip(t)
        dots = _dot(ki_ref[cr], qit_s[blk])
        sc = jnp.zeros((BLK, BLK), F32)
        for h in range(N_IDX_HEADS):
            sc = sc + wt_s[blk, h:h + 1, :] * jnp.maximum(dots[:, h * BLK:(h + 1) * BLK], 0.0)
        s_pos = c * BLK + row
        t_pos = pick(blk, blocks[0], blocks[1]) * BLK + lane
        valid = (s_pos <= t_pos) & (s_pos >= pad)
        key_ref[t] = jnp.where(valid, _sort_key(sc), INT_MIN)

    def count(pred, arg):
        cnt = [jnp.zeros((SUBLANES, BLK), I32), jnp.zeros((SUBLANES, BLK), I32)]
        for t in range(ncount):
            blk, c, _ = trip(t)
            hit = jnp.where(pred(key_ref[t], pick(blk, arg[0], arg[1]), c * BLK + row), 1, 0)
            part = _tile_fold(hit, jnp.add)
            if isinstance(blk, int):
                cnt[blk] = cnt[blk] + part
            else:
                cnt[0] = cnt[0] + jnp.where(blk == 0, part, 0)
                cnt[1] = cnt[1] + jnp.where(blk == 1, part, 0)
        return tuple(jnp.sum(x, axis=0, keepdims=True) for x in cnt)

    zero = jnp.zeros((1, BLK), I32)
    c0 = count(lambda k, a, s: k >= a, (zero, zero))
    thr0 = tuple(jnp.where(x >= topk, 0, INT_MIN).astype(I32) for x in c0)

    def bit_step(it, st):
        thr, n_ge = st[:2], st[2:]
        cand = tuple(x + (jnp.int32(1) << (30 - it)) for x in thr)
        cnt = count(lambda k, a, s: k >= a, cand)
        ok = tuple(x >= topk for x in cnt)
        return (tuple(jnp.where(ok[b], cand[b], thr[b]) for b in range(2))
                + tuple(jnp.where(ok[b], cnt[b], n_ge[b]) for b in range(2)))

    st = lax.fori_loop(0, 31, bit_step, thr0 + c0)
    thr, n_ge = st[:2], st[2:]
    tied = ((n_ge[0] > topk) & (thr[0] > INT_MIN)) | ((n_ge[1] > topk) & (thr[1] > INT_MIN))
    any_tied = jnp.max(tied.astype(I32))
    nbits = (nb * BLK).bit_length()

    def idx_search():
        n_gt = count(lambda k, a, s: k > a, thr)
        need = tuple(topk - x for x in n_gt)

        def step(it, lo):
            cand = tuple(x + (jnp.int32(1) << (nbits - 1 - it)) for x in lo)
            below = count(lambda k, a, s: (k == a[0]) & (s < a[1]),
                          ((thr[0], cand[0]), (thr[1], cand[1])))
            return tuple(jnp.where(below[b] < need[b], cand[b], lo[b]) for b in range(2))
        return lax.fori_loop(0, nbits, step, (zero, zero))

    big = jnp.full((1, BLK), 2 ** 30, I32)
    jcut = lax.cond(any_tied > 0, idx_search, lambda: (big, big))

    def logits(t):
        blk, c, cr = trip(t)
        k = key_ref[t]
        th = pick(blk, thr[0], thr[1])
        sel = (k > th) | ((k == th) & (c * BLK + row <= pick(blk, jcut[0], jcut[1])))
        bias = jnp.where(sel & (k > INT_MIN), 0.0, NEG)
        bb = jnp.concatenate([bias] * HEADS_PER_KV, axis=1)
        return blk, cr, [_dot(k_ref[g, cr], qt_s[blk, g]) + bb for g in range(N_KV_HEADS)]

    def attend(shift):
        acc_ref[...] = jnp.zeros_like(acc_ref)

        def probs(t):
            blk, cr, s = logits(t)
            if shift is not None:
                s = [s[g] - pick(blk, shift[0][g], shift[1][g]) for g in range(N_KV_HEADS)]
            return blk, cr, [jnp.exp(x).astype(BF16) for x in s]

        t = 0
        while t < ntrip:
            if t >= n_first_max and t + 1 < ntrip:
                _, cr0, p0 = probs(t)
                _, cr1, p1 = probs(t + 1)
                for g in range(N_KV_HEADS):
                    vt = jnp.concatenate([vt_ref[cr0, g], vt_ref[cr1, g]], axis=1)
                    acc_ref[1, g] += _dot(vt, jnp.concatenate([p0[g], p1[g]], axis=0))
                t += 2
            else:
                blk, cr, p = probs(t)
                for g in range(N_KV_HEADS):
                    acc_ref[blk, g] += _dot(vt_ref[cr, g], p[g])
                t += 1

    safe = smax_ref[0] <= SAFE_LOGIT

    @pl.when(safe)
    def _():
        attend(None)

    @pl.when(jnp.logical_not(safe))
    def _():
        m = [[jnp.full((SUBLANES, gw), NEG, F32) for _ in range(N_KV_HEADS)] for _ in range(2)]
        for t in range(ntrip):
            blk, _, s = logits(t)
            for g in range(N_KV_HEADS):
                part = _tile_fold(s[g], jnp.maximum)
                m[0][g] = jnp.maximum(m[0][g], jnp.where(blk == 0, part, NEG))
                m[1][g] = jnp.maximum(m[1][g], jnp.where(blk == 1, part, NEG))
        attend([[jnp.max(x, axis=0, keepdims=True) for x in mb] for mb in m])

    for b, o_ref in enumerate((oa_ref, ob_ref)):
        q_row = blocks[b] * BLK + lax.broadcasted_iota(I32, (BLK, LANES), 0)
        for g in range(N_KV_HEADS):
            a = acc_ref[b, g]
            res = a[0:HEAD_DIM] * (1.0 / a[HEAD_DIM:HEAD_DIM + 1])
            for t in range(HEADS_PER_KV // 2):
                two = jnp.concatenate([res[:, (2 * t + u) * BLK:(2 * t + u + 1) * BLK] for u in range(2)],
                                      axis=0)
                two = jnp.where(q_row >= pad, two.T, 0.0)
                lo = (g * HEADS_PER_KV + 2 * t) * HEAD_DIM
                o_ref[:, lo:lo + LANES] = two.astype(BF16)


def _dsa_prompt_pairs(smax, qit, wt, kib, qt, kg, vt, pad, topk):
    bsz, nb = qit.shape[:2]
    steps = (nb + 1) // 2
    gw = HEADS_PER_KV * BLK
    qd = N_HEADS * HEAD_DIM
    ncount = nb + 1
    kern = functools.partial(_dsa_pair_kernel, pad=pad, topk=topk, nb=nb)
    first = lambda *s: pl.BlockSpec((None, None) + s, lambda b, j: (b, j) + (0,) * len(s))
    second = lambda *s: pl.BlockSpec((None, None) + s, lambda b, j: (b, nb - 1 - j) + (0,) * len(s))
    oa, ob = pl.pallas_call(
        kern,
        grid=(bsz, steps),
        in_specs=[
            pl.BlockSpec(memory_space=pltpu.SMEM),
            first(IDX_DIM, N_IDX_HEADS * BLK), second(IDX_DIM, N_IDX_HEADS * BLK),
            first(N_IDX_HEADS, BLK), second(N_IDX_HEADS, BLK),
            first(N_KV_HEADS, HEAD_DIM, gw), second(N_KV_HEADS, HEAD_DIM, gw),
            pl.BlockSpec((None, nb, BLK, IDX_DIM), lambda b, j: (b, 0, 0, 0)),
            pl.BlockSpec((N_KV_HEADS, None, nb, BLK, HEAD_DIM), lambda b, j: (0, b, 0, 0, 0)),
            pl.BlockSpec((None, nb, N_KV_HEADS, VT_ROWS, BLK), lambda b, j: (b, 0, 0, 0, 0)),
        ],
        out_specs=[pl.BlockSpec((None, BLK, qd), lambda b, j: (b, j, 0)),
                   pl.BlockSpec((None, BLK, qd), lambda b, j: (b, steps - 1 - j, 0))],
        out_shape=[jax.ShapeDtypeStruct((bsz, steps * BLK, qd), BF16)] * 2,
        scratch_shapes=[pltpu.VMEM((ncount, BLK, BLK), I32),
                        pltpu.VMEM((2, N_KV_HEADS, VT_ROWS, gw), F32),
                        pltpu.VMEM((2, IDX_DIM, N_IDX_HEADS * BLK), BF16),
                        pltpu.VMEM((2, N_IDX_HEADS, BLK), F32),
                        pltpu.VMEM((2, N_KV_HEADS, HEAD_DIM, gw), BF16)],
        compiler_params=_cparams(("parallel", "arbitrary")),
        name="dsa_prompt",
    )(smax, qit, qit, wt, wt, qt, qt, kib, kg, vt)
    return jnp.concatenate([oa[:, :(nb - steps) * BLK], ob], axis=1)


def _smp_scores_kernel(pt_ref, qi_ref, w_ref, *refs, pg):
    page_refs, o_ref = refs[:pg], refs[pg]
    qi = qi_ref[...]
    w = w_ref[...]
    for p in range(pg):
        d = _dot(qi, page_refs[p][...].astype(BF16))
        o_ref[p:p + 1, :] = jnp.sum(w * jnp.maximum(d, 0.0), axis=0, keepdims=True)


def _smp_scores(page_table, qi, wi, kidx_t, pg):
    db, npg = page_table.shape
    kern = functools.partial(_smp_scores_kernel, pg=pg)
    page_spec = lambda p: pl.BlockSpec(
        (None, IDX_DIM, BLK), lambda b, j, pt: (pt[b * npg + j * pg + p], 0, 0))
    return pl.pallas_call(
        kern,
        grid_spec=pltpu.PrefetchScalarGridSpec(
            num_scalar_prefetch=1,
            grid=(db, npg // pg),
            in_specs=[pl.BlockSpec((None, N_IDX_HEADS, IDX_DIM), lambda b, j, pt: (b, 0, 0)),
                      pl.BlockSpec((None, N_IDX_HEADS, 1), lambda b, j, pt: (b, 0, 0))]
                     + [page_spec(p) for p in range(pg)],
            out_specs=pl.BlockSpec((None, pg, BLK), lambda b, j, pt: (b, j, 0)),
        ),
        out_shape=jax.ShapeDtypeStruct((db, npg, BLK), F32),
        compiler_params=_cparams(("parallel", "arbitrary")),
        name="sample_scores",
    )(page_table.reshape(-1), qi, wi, *([kidx_t] * pg))


def _smp_select_kernel(sc_ref, qi_ref, w_ref, kin_ref, bias_ref, nb_ref, *, topk, past):
    db, npg, _ = sc_ref.shape
    key = _sort_key(sc_ref[...])
    qi = qi_ref[...].astype(F32)
    kn = kin_ref[...].astype(BF16).astype(F32)
    d = jnp.sum(qi * kn, axis=-1, keepdims=True)
    s_new = jnp.sum(w_ref[...] * jnp.maximum(d, 0.0), axis=1, keepdims=True)
    key_new = _sort_key(s_new)
    pos = (lax.broadcasted_iota(I32, (db, npg, BLK), 1) * BLK
           + lax.broadcasted_iota(I32, (db, npg, BLK), 2))

    def count(pred_past, pred_new):
        c = jnp.sum(jnp.where(pred_past, 1, 0), axis=2, keepdims=True)
        return jnp.sum(c, axis=1, keepdims=True) + jnp.where(pred_new, 1, 0)

    thr = jnp.where(count(key >= 0, key_new >= 0) >= topk, 0, INT_MIN).astype(I32)

    def bit_step(it, thr):
        cand = thr + (jnp.int32(1) << (30 - it))
        return jnp.where(count(key >= cand, key_new >= cand) >= topk, cand, thr)

    thr = lax.fori_loop(0, 31, bit_step, thr)
    need = topk - count(key > thr, key_new > thr)
    nbits = max(1, past.bit_length())

    def step(it, lo):
        cand = lo + (jnp.int32(1) << (nbits - 1 - it))
        below = count((key == thr) & (pos < cand), (key_new == thr) & (past < cand))
        return jnp.where(below < need, cand, lo)

    jcut = lax.fori_loop(0, nbits, step, jnp.zeros((db, 1, 1), I32))
    sel = (key > thr) | ((key == thr) & (pos <= jcut))
    bias_ref[...] = jnp.where(sel, 0.0, NEG)
    sel_new = (key_new > thr) | ((key_new == thr) & (past <= jcut))
    nb_ref[...] = jnp.broadcast_to(jnp.where(sel_new, 0.0, NEG), nb_ref.shape)


def _smp_select(scores, qi, wi, ki_new, topk):
    db, npg, _ = scores.shape
    kern = functools.partial(_smp_select_kernel, topk=topk, past=npg * BLK)
    full = lambda *s: pl.BlockSpec(s, lambda i: (0,) * len(s))
    return pl.pallas_call(
        kern,
        grid=(1,),
        in_specs=[full(db, npg, BLK), full(db, N_IDX_HEADS, IDX_DIM), full(db, N_IDX_HEADS, 1),
                  full(db, 1, IDX_DIM)],
        out_specs=[full(db, npg, BLK), full(db, SUBLANES, LANES)],
        out_shape=[jax.ShapeDtypeStruct((db, npg, BLK), F32),
                   jax.ShapeDtypeStruct((db, SUBLANES, LANES), F32)],
        compiler_params=_cparams(("arbitrary",)),
        name="sample_select",
    )(scores, qi, wi, ki_new)


def _smp_attend_kernel(pt_ref, q_ref, bias_ref, nb_ref, kn_ref, vn_ref, *refs, pg):
    k_refs, v_refs = refs[:pg], refs[pg:2 * pg]
    o_ref, m_ref, l_ref, acc_ref = refs[2 * pg:]
    j = pl.program_id(1)
    kd = N_KV_HEADS * HEAD_DIM

    @pl.when(j == 0)
    def _():
        m_ref[...] = jnp.full_like(m_ref, NEG)
        l_ref[...] = jnp.zeros_like(l_ref)
        acc_ref[...] = jnp.zeros_like(acc_ref)

    q = q_ref[...]
    s = [_dot(q, k_refs[p][...].astype(BF16)) + bias_ref[p:p + 1, :] for p in range(pg)]
    m_old = m_ref[...]
    m_new = m_old
    for p in range(pg):
        m_new = jnp.maximum(m_new, jnp.max(s[p], axis=-1, keepdims=True))
    alpha = jnp.exp(m_old - m_new)
    l = l_ref[...] * alpha
    acc = acc_ref[...] * alpha
    for p in range(pg):
        e = jnp.exp(s[p] - m_new)
        l = l + jnp.sum(e, axis=-1, keepdims=True)
        acc = acc + _dot_t1(e.astype(BF16), v_refs[p][...].astype(BF16))
    m_ref[...] = m_new
    l_ref[...] = l
    acc_ref[...] = acc

    @pl.when(j == pl.num_programs(1) - 1)
    def _():
        qf = q.astype(F32)
        kn = kn_ref[...].astype(BF16).astype(F32)
        vn = vn_ref[...].astype(BF16).astype(F32)
        s_new = jnp.sum(qf * kn, axis=-1, keepdims=True) + nb_ref[0:1, 0:1]
        m_fin = jnp.maximum(m_new, s_new)
        a2 = jnp.exp(m_new - m_fin)
        e_new = jnp.exp(s_new - m_fin)
        l_fin = l * a2 + e_new
        acc_fin = acc * a2 + e_new.astype(BF16).astype(F32) * vn
        res = acc_fin / l_fin
        hgrp = lax.broadcasted_iota(I32, (N_HEADS, HEAD_DIM), 0) // HEADS_PER_KV
        out = jnp.zeros((N_HEADS, HEAD_DIM), F32)
        for g in range(N_KV_HEADS):
            out = out + jnp.where(hgrp == g, res[:, g * HEAD_DIM:(g + 1) * HEAD_DIM], 0.0)
        o_ref[...] = out


def _smp_attend(page_table, q_bd, bias, nbias, k_new, v_new, k_t, v_t, pg):
    db, npg = page_table.shape
    kd = N_KV_HEADS * HEAD_DIM
    kern = functools.partial(_smp_attend_kernel, pg=pg)
    page_spec = lambda p: pl.BlockSpec(
        (None, kd, BLK), lambda b, j, pt: (pt[b * npg + j * pg + p], 0, 0))
    per_b = lambda shape: pl.BlockSpec((None,) + shape, lambda b, j, pt: (b, 0, 0))
    return pl.pallas_call(
        kern,
        grid_spec=pltpu.PrefetchScalarGridSpec(
            num_scalar_prefetch=1,
            grid=(db, npg // pg),
            in_specs=[per_b((N_HEADS, kd)),
                      pl.BlockSpec((None, pg, BLK), lambda b, j, pt: (b, j, 0)),
                      per_b((SUBLANES, LANES)), per_b((1, kd)), per_b((1, kd))]
                     + [page_spec(p) for p in range(pg)] * 2,
            out_specs=per_b((N_HEADS, HEAD_DIM)),
            scratch_shapes=[pltpu.VMEM((N_HEADS, 1), F32), pltpu.VMEM((N_HEADS, 1), F32),
                            pltpu.VMEM((N_HEADS, kd), F32)],
        ),
        out_shape=jax.ShapeDtypeStruct((db, N_HEADS, HEAD_DIM), F32),
        compiler_params=_cparams(("parallel", "arbitrary")),
        name="sample_attend",
    )(page_table.reshape(-1), q_bd, bias, nbias, k_new, v_new,
      *([k_t] * pg), *([v_t] * pg))


def _ssd_prompt_kernel(x_ref, g_ref, wz_ref, wx_ref, wdt_ref, wo_ref,
                       cw_ref, cb_ref, dtb_ref, a_ref,
                       dtbc_ref, ac_ref, dsk_ref, gn_ref, ltri_ref, utri_ref,
                       o_ref, st_ref, cv_ref, xpad_ref, h_ref, yb_ref, xt_ref, yt_ref, z_ref,
                       *, pad, n_heads):
    c = pl.program_id(1)
    d_in = n_heads * SSD_HEAD_DIM
    gn = SSD_GROUPS * D_STATE
    hpg = n_heads // SSD_GROUPS

    @pl.when(c == 0)
    def _():
        xpad_ref[0:SUBLANES, :] = jnp.zeros((SUBLANES, xpad_ref.shape[1]), F32)
        h_ref[...] = jnp.zeros_like(h_ref)

    xn = _rms(x_ref[...], g_ref[...]).astype(BF16)
    z_ref[...] = _dot(xn, wz_ref[...])
    dt_raw = _dot(xn, wdt_ref[...])
    dt_raw_t = dt_raw.T[0:n_heads]

    xpad_ref[SUBLANES:, :] = _dot(xn, wx_ref[...])
    conv = cb_ref[...] + cw_ref[CONV_W - 1:CONV_W, :] * xpad_ref[SUBLANES:, :]
    for j in range(CONV_W - 1):
        sh = CONV_W - 1 - j
        conv = conv + cw_ref[j:j + 1, :] * xpad_ref[SUBLANES - sh:SUBLANES - sh + BLK, :]
    xpad_ref[0:SUBLANES, :] = xpad_ref[BLK:BLK + SUBLANES, :]
    xbc = _silu(conv)

    live = (c > 0) | (lax.broadcasted_iota(I32, (BLK, LANES), 0) >= pad)
    dt = jnp.where(live, _softplus(dt_raw + dtb_ref[...]), 0.0)
    acum = _dot_sel(dt * a_ref[...], ltri_ref[...], fn=lambda x, m: _dot(m, x))
    live_t = (c > 0) | (lax.broadcasted_iota(I32, (n_heads, BLK), 1) >= pad)
    dtt = jnp.where(live_t, _softplus(dt_raw_t + dtbc_ref[...]), 0.0)
    acum_t = _dot_sel(dtt * ac_ref[...], utri_ref[...])
    for t in range(d_in // LANES):
        xt_ref[t * LANES:(t + 1) * LANES, :] = xbc[:, t * LANES:(t + 1) * LANES].T
    a_last = acum_t[:, BLK - 1:BLK]
    ecol_t = jnp.exp(acum_t)
    decs_t = jnp.exp(a_last - acum_t)
    ea_last = jnp.exp(a_last)
    causal_t = (lax.broadcasted_iota(I32, (BLK, BLK), 0) <= lax.broadcasted_iota(I32, (BLK, BLK), 1))
    hp = SSD_HEAD_DIM

    for g in range(SSD_GROUPS):
        bm = xbc[:, d_in + g * D_STATE:d_in + (g + 1) * D_STATE].astype(BF16)
        ct = xbc[:, d_in + gn + g * D_STATE:d_in + gn + (g + 1) * D_STATE].T.astype(BF16)
        cb_t = _dot(bm, ct)
        hprev = h_ref[g * hpg:(g + 1) * hpg].reshape(hpg * hp, D_STATE)
        y_off = _dot(hprev.astype(BF16), ct)
        ws = []
        for hh in range(hpg):
            h = g * hpg + hh
            rows = slice(h * hp, (h + 1) * hp)
            xh = xt_ref[rows, :]
            xdt = xh * dtt[h:h + 1, :]
            decay_t = jnp.exp(jnp.where(causal_t, acum_t[h:h + 1, :] - acum[:, h:h + 1], NEG))
            y = _dot(xdt.astype(BF16), (cb_t * decay_t).astype(BF16))
            y = y + ecol_t[h:h + 1, :] * y_off[hh * hp:(hh + 1) * hp]
            yt_ref[rows, :] = y + dsk_ref[0:1, h:h + 1] * xh
            ws.append((xdt * decs_t[h:h + 1, :]).astype(BF16))
        upd = _dot(jnp.concatenate(ws, axis=0), bm)
        for hh in range(hpg):
            h = g * hpg + hh
            h_ref[h] = ea_last[h:h + 1, :] * hprev[hh * hp:(hh + 1) * hp] + upd[hh * hp:(hh + 1) * hp]

    for t in range(d_in // LANES):
        yb_ref[:, t * LANES:(t + 1) * LANES] = yt_ref[t * LANES:(t + 1) * LANES, :].T

    yg = yb_ref[...] * _silu(z_ref[...])
    gsz = d_in // SSD_GROUPS
    yn = jnp.concatenate(
        [_rms(yg[:, g * gsz:(g + 1) * gsz], gn_ref[:, g * gsz:(g + 1) * gsz]).astype(BF16)
         for g in range(SSD_GROUPS)], axis=1)
    o_ref[...] = x_ref[...] + _dot(yn, wo_ref[...])

    @pl.when(c == pl.num_programs(1) - 1)
    def _():
        st_ref[...] = h_ref[...]
        cv_ref[...] = xpad_ref[0:SUBLANES, :]


def _ssd_prompt(x, g, w_z, w_xbc, w_dt, w_out, conv_w, conv_b, dt_bias, a_log, d_skip, gate_norm,
                bsz, nb, pad, n_heads):
    d = x.shape[1]
    d_in = n_heads * SSD_HEAD_DIM
    cdim = d_in + 2 * SSD_GROUPS * D_STATE
    assert n_heads <= LANES
    hp = LANES - n_heads
    a = -jnp.exp(a_log.astype(F32))
    ltri = jnp.tril(jnp.ones((BLK, BLK), F32)).astype(BF16)
    kern = functools.partial(_ssd_prompt_kernel, pad=pad, n_heads=n_heads)
    const = lambda shape: pl.BlockSpec(shape, lambda b, c: (0,) * len(shape),
                                       pipeline_mode=pl.Buffered(1))
    return pl.pallas_call(
        kern,
        grid=(bsz, nb),
        in_specs=[
            pl.BlockSpec((BLK, d), lambda b, c: (b * nb + c, 0)),
            const((1, d)), const((d, d_in)), const((d, cdim)), const((d, LANES)), const((d_in, d)),
            const((CONV_W, cdim)), const((1, cdim)), const((1, LANES)), const((1, LANES)),
            const((n_heads, 1)), const((n_heads, 1)), const((1, LANES)), const((1, d_in)),
            const((BLK, BLK)), const((BLK, BLK)),
        ],
        out_specs=[pl.BlockSpec((BLK, d), lambda b, c: (b * nb + c, 0)),
                   pl.BlockSpec((None, n_heads, SSD_HEAD_DIM, D_STATE), lambda b, c: (b, 0, 0, 0)),
                   pl.BlockSpec((None, SUBLANES, cdim), lambda b, c: (b, 0, 0))],
        out_shape=[jax.ShapeDtypeStruct((bsz * nb * BLK, d), F32),
                   jax.ShapeDtypeStruct((bsz, n_heads, SSD_HEAD_DIM, D_STATE), F32),
                   jax.ShapeDtypeStruct((bsz, SUBLANES, cdim), F32)],
        scratch_shapes=[pltpu.VMEM((BLK + SUBLANES, cdim), F32),
                        pltpu.VMEM((n_heads, SSD_HEAD_DIM, D_STATE), F32),
                        pltpu.VMEM((BLK, d_in), F32),
                        pltpu.VMEM((d_in, BLK), F32),
                        pltpu.VMEM((d_in, BLK), F32),
                        pltpu.VMEM((BLK, d_in), F32)],
        compiler_params=_cparams(("parallel", "arbitrary")),
        name="ssd_prompt",
    )(x, g.reshape(1, d), w_z, w_xbc, w_dt, w_out, conv_w, conv_b.reshape(1, cdim),
      jnp.pad(dt_bias, (0, hp)).reshape(1, LANES), jnp.pad(a, (0, hp)).reshape(1, LANES),
      dt_bias.reshape(n_heads, 1), a.reshape(n_heads, 1),
      jnp.pad(d_skip, (0, hp)).reshape(1, LANES), gate_norm.reshape(1, d_in), ltri, ltri.T)


def _ssd_sample_kernel(z_ref, xbc_ref, dt_ref, cst_ref, h0_ref, cw_ref, cb_ref, dtb_ref, a_ref,
                       dsk_ref, gn_ref, exp_ref, y_ref, h_ref, *, n_heads):
    d_in = n_heads * SSD_HEAD_DIM
    gn = SSD_GROUPS * D_STATE
    gsz = d_in // SSD_GROUPS
    hpg = n_heads // SSD_GROUPS
    conv = cb_ref[...] + cw_ref[CONV_W - 1:CONV_W, :] * xbc_ref[...]
    for j in range(CONV_W - 1):
        conv = conv + cw_ref[j:j + 1, :] * cst_ref[j:j + 1, :]
    xbc = _silu(conv)
    dt = _softplus(dt_ref[...] + dtb_ref[...])
    pad8 = lambda r: jnp.concatenate([r, jnp.zeros((SUBLANES - 1, r.shape[1]), F32)], axis=0)
    dt_ch = _dot_sel(pad8(dt), exp_ref[...])[0:1]
    da_ch = jnp.exp(_dot_sel(pad8(dt * a_ref[...]), exp_ref[...])[0:1])
    dsk_ch = _dot_sel(pad8(dsk_ref[...]), exp_ref[...])[0:1]
    xh = xbc[:, :d_in]
    xdt = xh * dt_ch
    ones = jnp.ones((SUBLANES, D_STATE), BF16)
    outs = []
    for g in range(SSD_GROUPS):
        sl = slice(g * gsz, (g + 1) * gsz)
        bm = xbc[:, d_in + g * D_STATE:d_in + (g + 1) * D_STATE]
        cm = xbc[:, d_in + gn + g * D_STATE:d_in + gn + (g + 1) * D_STATE]
        da_col = _dot_sel(pad8(da_ch[:, sl]), ones, fn=_dot_t0)
        xdt_col = _dot_sel(pad8(xdt[:, sl]), ones, fn=_dot_t0)
        h0 = h0_ref[g * hpg:(g + 1) * hpg].reshape(gsz, D_STATE)
        hn = da_col * h0 + xdt_col.astype(BF16).astype(F32) * bm.astype(BF16).astype(F32)
        h_ref[g * hpg:(g + 1) * hpg] = hn.reshape(hpg, SSD_HEAD_DIM, D_STATE)
        y_col = jnp.sum(hn * cm, axis=-1, keepdims=True)
        outs.append(y_col)
    y_cols = jnp.concatenate(outs, axis=0)
    rows = []
    eye = (lax.broadcasted_iota(I32, (LANES, LANES), 0)
           == lax.broadcasted_iota(I32, (LANES, LANES), 1)).astype(F32)
    for t in range(d_in // LANES):
        blk = y_cols[t * LANES:(t + 1) * LANES]
        rows.append(jnp.sum(blk * eye, axis=0, keepdims=True))
    y = jnp.concatenate(rows, axis=1) + dsk_ch * xh
    y = y * _silu(z_ref[...])
    for g in range(SSD_GROUPS):
        sl = slice(g * gsz, (g + 1) * gsz)
        y_ref[:, sl] = _rms(y[:, sl], gn_ref[:, sl]).astype(BF16)


def _ssd_sample(z, xbc_raw, dt_raw, conv_state, h0, conv_w, conv_b, dt_bias, a_log, d_skip,
                gate_norm, n_heads):
    db = z.shape[0]
    d_in = n_heads * SSD_HEAD_DIM
    cdim = d_in + 2 * SSD_GROUPS * D_STATE
    hp = LANES - n_heads
    a = -jnp.exp(a_log.astype(F32))
    expand = (jnp.arange(LANES)[:, None] == (jnp.arange(d_in) // SSD_HEAD_DIM)[None, :]).astype(BF16)
    kern = functools.partial(_ssd_sample_kernel, n_heads=n_heads)
    const = lambda shape: pl.BlockSpec(shape, lambda b: (0,) * len(shape))
    per_b = lambda shape: pl.BlockSpec((None,) + shape, lambda b: (b,) + (0,) * len(shape))
    y, h = pl.pallas_call(
        kern,
        grid=(db,),
        in_specs=[per_b((1, d_in)), per_b((1, cdim)), per_b((1, LANES)), per_b((CONV_W - 1, cdim)),
                  per_b((n_heads, SSD_HEAD_DIM, D_STATE)),
                  const((CONV_W, cdim)), const((1, cdim)), const((1, LANES)), const((1, LANES)),
                  const((1, LANES)), const((1, d_in)), const((LANES, d_in))],
        out_specs=[per_b((1, d_in)), per_b((n_heads, SSD_HEAD_DIM, D_STATE))],
        out_shape=[jax.ShapeDtypeStruct((db, 1, d_in), BF16),
                   jax.ShapeDtypeStruct((db, n_heads, SSD_HEAD_DIM, D_STATE), F32)],
        compiler_params=_cparams(("parallel",)),
        name="ssd_sample",
    )(z.reshape(db, 1, d_in), xbc_raw.reshape(db, 1, cdim), dt_raw.reshape(db, 1, LANES),
      conv_state, h0, conv_w, conv_b.reshape(1, cdim),
      jnp.pad(dt_bias, (0, hp)).reshape(1, LANES), jnp.pad(a, (0, hp)).reshape(1, LANES),
      jnp.pad(d_skip, (0, hp)).reshape(1, LANES), gate_norm.reshape(1, d_in), expand)
    return y.reshape(db, d_in), h


def kernel(x_prompt, x_sample, cache_k, cache_v, cache_kidx, page_table, state_ssm, state_conv,
           meta_tokens, norm_ffn_a, w_ffn_a_in, w_ffn_a_out, norm_mix, norm_ffn_b, w_ffn_b_in,
           w_ffn_b_out, w_attn_in, q_norm, k_norm, kidx_norm, w_attn_out,
           w_ssd_in, conv_w, conv_b, dt_bias, a_log, d_skip, gate_norm, w_ssd_out):
    bsz, seq, d = x_prompt.shape
    db = x_sample.shape[0]
    assert x_sample.shape[1] == 1
    t_real = N_META + seq
    nb = -(-t_real // BLK)
    t_pad = nb * BLK
    pad = t_pad - t_real
    npg = page_table.shape[1]
    past = npg * BLK
    topk_p = min(TOPK_MAX, seq // 4)
    topk_s = min(TOPK_MAX, (past + 1) // 4)
    qd, kd, qid = N_HEADS * HEAD_DIM, N_KV_HEADS * HEAD_DIM, N_IDX_HEADS * IDX_DIM
    d_in = w_ssd_out.shape[1]
    n_heads = d_in // SSD_HEAD_DIM
    cdim = d_in + 2 * SSD_GROUPS * D_STATE
    depth = norm_mix.shape[0]

    meta = jnp.broadcast_to(meta_tokens.astype(F32)[None], (bsz, N_META, d))
    xp = jnp.concatenate([jnp.zeros((bsz, pad, d), F32), meta, x_prompt], axis=1)
    xp = xp.reshape(bsz * t_pad, d)
    xs = x_sample.reshape(db, d)

    outs = {k: [] for k in ("kp", "vp", "kip", "ks", "vs", "kis", "hp", "cp", "hs", "cs")}
    for i in range(depth):
        wts = _ffn_weights(w_ffn_a_in[i], w_ffn_a_out[i])
        xp = _ffn(xp, norm_ffn_a[i], wts)
        xs = _ffn(xs, norm_ffn_a[i], wts)
        j = i // 2
        if i % 2 == 0:
            w = w_attn_in[j]
            o1, o2 = qd + 2 * kd + qid, qd + 2 * kd + qid + IDX_DIM
            w_pad = jnp.concatenate(
                [w[:, :o1], jnp.pad(w[:, o1:o2], ((0, 0), (0, LANES - IDX_DIM))),
                 jnp.pad(w[:, o2:], ((0, 0), (0, LANES - N_IDX_HEADS)))], axis=1).astype(BF16)
            w_out = w_attn_out[j].astype(BF16)

            k, v, ki, qt, qit, wt, kg, vt, kib = _attn_in(
                xp, norm_mix[i], w_pad, q_norm[j], k_norm[j], kidx_norm[j], prompt=True)
            gw = HEADS_PER_KV * BLK
            smax = (1.02 * math.sqrt(HEAD_DIM) * jnp.max(jnp.abs(q_norm[j]))
                    * jnp.max(jnp.abs(k_norm[j]))).reshape(1).astype(F32)
            o = _dsa_prompt_pairs(smax,
                            qit.reshape(bsz, nb, IDX_DIM, N_IDX_HEADS * BLK),
                            wt.reshape(bsz, nb, N_IDX_HEADS, BLK),
                            kib.reshape(bsz, nb, BLK, IDX_DIM),
                            qt.reshape(bsz, nb, N_KV_HEADS, HEAD_DIM, gw),
                            kg.reshape(N_KV_HEADS, bsz, nb, BLK, HEAD_DIM),
                            vt.reshape(bsz, nb, N_KV_HEADS, VT_ROWS, BLK),
                            pad, topk_p)
            xp = _out_proj(xp, o.reshape(bsz * t_pad, qd), w_out)
            outs["kp"].append(k.reshape(bsz, t_pad, N_KV_HEADS, HEAD_DIM)[:, pad:])
            outs["vp"].append(v.reshape(bsz, t_pad, N_KV_HEADS, HEAD_DIM)[:, pad:])
            outs["kip"].append(ki[:, :IDX_DIM].reshape(bsz, t_pad, IDX_DIM)[:, pad:])

            q, k, v, qi, ki, wi = _attn_in(
                xs, norm_mix[i], w_pad, q_norm[j], k_norm[j], kidx_norm[j], prompt=False)
            pg = math.gcd(npg, 64)
            qi3 = qi.reshape(db, N_IDX_HEADS, IDX_DIM)
            wi3 = wi[:, :N_IDX_HEADS].reshape(db, N_IDX_HEADS, 1)
            scores = _smp_scores(page_table, qi3, wi3, cache_kidx[j].transpose(0, 2, 1), pg)
            bias, nbias = _smp_select(scores, qi3, wi3, ki[:, :IDX_DIM].reshape(db, 1, IDX_DIM), topk_s)
            hsel = (jnp.arange(N_HEADS)[:, None] // HEADS_PER_KV == jnp.arange(N_KV_HEADS)[None, :])
            q_bd = (q.reshape(db, N_HEADS, 1, HEAD_DIM) * hsel[None, :, :, None].astype(BF16))
            q_bd = q_bd.reshape(db, N_HEADS, kd)
            npool = cache_k.shape[1]
            o = _smp_attend(page_table, q_bd, bias, nbias, k.reshape(db, 1, kd), v.reshape(db, 1, kd),
                            cache_k[j].transpose(0, 2, 3, 1).reshape(npool, kd, BLK),
                            cache_v[j].transpose(0, 2, 3, 1).reshape(npool, kd, BLK), pg)
            xs = _out_proj(xs, o.reshape(db, qd).astype(BF16), w_out)
            outs["ks"].append(k.reshape(db, 1, N_KV_HEADS, HEAD_DIM))
            outs["vs"].append(v.reshape(db, 1, N_KV_HEADS, HEAD_DIM))
            outs["kis"].append(ki[:, :IDX_DIM].reshape(db, 1, IDX_DIM))
        else:
            w = w_ssd_in[j]
            w_main = w[:, :d_in + cdim].astype(BF16)
            w_dt = jnp.pad(w[:, d_in + cdim:], ((0, 0), (0, LANES - n_heads))).astype(BF16)
            w_out = w_ssd_out[j].astype(BF16)
            sp = (conv_w[j], conv_b[j], dt_bias[j], a_log[j], d_skip[j], gate_norm[j])

            xp, hfin, ctail = _ssd_prompt(xp, norm_mix[i], w_main[:, :d_in], w_main[:, d_in:], w_dt, w_out,
                                          *sp, bsz, nb, pad, n_heads)
            outs["hp"].append(hfin)
            outs["cp"].append(ctail[:, SUBLANES - (CONV_W - 1):])

            z, xbc_raw, dt_raw = _norm_proj(xs, norm_mix[i], w_main, w_dt, d_in)
            y, hnew = _ssd_sample(z, xbc_raw, dt_raw, state_conv[j], state_ssm[j], *sp, n_heads)
            xs = _out_proj(xs, y, w_out)
            outs["hs"].append(hnew)
            outs["cs"].append(jnp.concatenate([state_conv[j][:, 1:], xbc_raw[:, None, :]], axis=1))
        wts = _ffn_weights(w_ffn_b_in[i], w_ffn_b_out[i])
        xp = _ffn(xp, norm_ffn_b[i], wts)
        xs = _ffn(xs, norm_ffn_b[i], wts)

    y_prompt = xp.reshape(bsz, t_pad, d)[:, pad + N_META:]
    y_sample = xs.reshape(db, 1, d)
    st = lambda key: jnp.stack(outs[key])
    return (y_prompt, y_sample, st("kp"), st("vp"), st("kip"), st("ks"), st("vs"), st("kis"),
            st("hp"), st("cp"), st("hs"), st("cs"))
```
